```python
import jax
import jax.numpy as jnp
from jax import lax
import numpy as np

D_MODEL = 1024
BATCH = 8
SEQ = 8192
DEPTH = 1

D_RNN = 1024
RNN_BLOCKS = 8
RNN_BW = D_RNN // RNN_BLOCKS
CONV_WIDTH = 4
LRU_C = 8.0
HEAD_DIM = 128
HEADS_PER_GROUP = 4
ATTN_GROUPS = ((128, 1), (512, 4), (2048, 16))
N_GROUPS = 3
ATTN_WIDTH = HEADS_PER_GROUP * HEAD_DIM
ROPE_THETA = 500000.0
ROPE_DIM = HEAD_DIM // 4
ATTN_BLK = 64
NEG_INF = -1e30
D_IN = 2 * D_RNN + 3 * N_GROUPS * ATTN_WIDTH + 2 * D_MODEL
N_EXPERTS = 16
CAPACITY_FACTOR = 2
D_FF = 2048
RMS_EPS = 1e-6

kernel_name = 'hybrid_rglru_dilated_attn_ec_moe'


def _rmsnorm(x, g):
    xf = x.astype(jnp.float32)
    y = xf * lax.rsqrt(jnp.mean(xf * xf, axis=-1, keepdims=True) + RMS_EPS)
    return (y * g.astype(jnp.float32)).astype(x.dtype)


def _split_points():
    widths = [D_RNN, D_RNN] + [ATTN_WIDTH] * (3 * N_GROUPS) + [D_MODEL, D_MODEL]
    pts, acc = [], 0
    for w in widths[:-1]:
        acc += w
        pts.append(acc)
    return pts


def _centred_depthwise_conv(x, w, b):
    S = x.shape[1]
    left = CONV_WIDTH // 2
    right = CONV_WIDTH - 1 - left
    xp = jnp.pad(x, ((0, 0), (left, right), (0, 0)))
    y = b
    for tap in range(CONV_WIDTH):
        y = y + w[tap] * xp[:, tap:tap + S]
    return y


def _linear_scan_combine(earlier, later):
    a1, b1 = earlier
    a2, b2 = later
    return a1 * a2, a2 * b1 + b2


def _rg_lru(x, w_gates, b_gates, lam, reverse):
    B, S, C = x.shape
    xb = x.reshape(B, S, RNN_BLOCKS, RNN_BW)
    gates = jnp.einsum('bsni,gnij->gbsnj', xb, w_gates).reshape(2, B, S, C)
    gates = gates.astype(jnp.float32) + b_gates.astype(jnp.float32)[:, None, None, :]
    r = jax.nn.sigmoid(gates[0])
    i = jax.nn.sigmoid(gates[1])
    log_a = -LRU_C * r * jax.nn.softplus(-lam.astype(jnp.float32))
    a = jnp.exp(log_a)
    u = x.astype(jnp.float32) * i * jnp.sqrt(-jnp.expm1(2.0 * log_a))
    _, h = lax.associative_scan(_linear_scan_combine, (a, u), axis=1, reverse=reverse)
    return h


def _rotary_tables(S):
    pos = jnp.arange(S, dtype=jnp.float32)
    inv = ROPE_THETA ** (-jnp.arange(0, ROPE_DIM, 2, dtype=jnp.float32) / ROPE_DIM)
    ang = pos[:, None] * inv[None, :]
    return jnp.cos(ang), jnp.sin(ang)


def _partial_rotary(t, cos, sin):
    half = ROPE_DIM // 2
    tf = t.astype(jnp.float32)
    x1, x2, rest = tf[..., :half], tf[..., half:ROPE_DIM], tf[..., ROPE_DIM:]
    c = cos[None, :, None, :]
    s = sin[None, :, None, :]
    out = jnp.concatenate([x1 * c - x2 * s, x2 * c + x1 * s, rest], axis=-1)
    return out.astype(t.dtype)


def _dilated_window_attention(q, k, v, radius, dilation):
    B, S, H, Dh = q.shape
    L = S // dilation
    nblk = -(-L // ATTN_BLK)
    Lp = nblk * ATTN_BLK

    def split(t):
        return t.reshape(B, L, dilation, H, Dh).transpose(0, 2, 3, 1, 4)

    qb = jnp.pad(split(q), ((0, 0), (0, 0), (0, 0), (0, Lp - L), (0, 0)))
    qb = qb.reshape(B, dilation, H, nblk, ATTN_BLK, Dh)

    def windows(t):
        tp = jnp.pad(split(t), ((0, 0), (0, 0), (0, 0), (ATTN_BLK, ATTN_BLK + Lp - L), (0, 0)))
        tp = tp.reshape(B, dilation, H, nblk + 2, ATTN_BLK, Dh)
        return jnp.concatenate([tp[:, :, :, :-2], tp[:, :, :, 1:-1], tp[:, :, :, 2:]], axis=4)

    kw = windows(k)
    vw = windows(v)
    s = jnp.einsum('brhnqd,brhnkd->brhnqk', qb, kw).astype(jnp.float32) * (Dh ** -0.5)
    jq = jnp.arange(nblk)[:, None] * ATTN_BLK + jnp.arange(ATTN_BLK)[None, :]
    jk = (jnp.arange(nblk)[:, None] - 1) * ATTN_BLK + jnp.arange(3 * ATTN_BLK)[None, :]
    rel = jk[:, None, :] - jq[:, :, None]
    valid = (jnp.abs(rel) <= radius) & (jk[:, None, :] >= 0) & (jk[:, None, :] < L)
    s = jnp.where(valid, s, NEG_INF)
    m = jnp.max(s, axis=-1, keepdims=True)
    p = jnp.exp(s - m)
    l = jnp.sum(p, axis=-1, keepdims=True)
    o = jnp.einsum('brhnqk,brhnkd->brhnqd', p, vw.astype(jnp.float32)) / l
    lse = (m + jnp.log(l))[..., 0]

    def merge(t):
        t = t.reshape((B, dilation, H, Lp) + t.shape[5:])[:, :, :, :L]
        t = jnp.moveaxis(t, 3, 1)
        return t.reshape((B, S, H) + t.shape[4:])

    return merge(o), merge(lse)


def _mixer_sublayer(x, cos, sin, norm_g, w_in, b_in, conv_w, conv_b, rg_w, rg_b, rg_lambda,
                    p_rnn, p_attn, w_out):
    B, S, _ = x.shape
    h = _rmsnorm(x, norm_g)
    proj = jnp.einsum('bsd,de->bse', h, w_in) + b_in
    parts = jnp.split(proj, _split_points(), axis=-1)
    xr, gr = parts[0], parts[1]
    q_parts = parts[2:2 + N_GROUPS]
    k_parts = parts[2 + N_GROUPS:2 + 2 * N_GROUPS]
    v_parts = parts[2 + 2 * N_GROUPS:2 + 3 * N_GROUPS]
    gate_a, gate_b = parts[-2], parts[-1]

    xc = _centred_depthwise_conv(xr, conv_w, conv_b)
    h_rnn = (_rg_lru(xc, rg_w[0], rg_b[0], rg_lambda[0], False)
             + _rg_lru(xc, rg_w[1], rg_b[1], rg_lambda[1], True))
    y_rnn = (jax.nn.gelu(gr.astype(jnp.float32)) * h_rnn).astype(x.dtype)
    branch_a = jnp.einsum('bsc,cd->bsd', y_rnn, p_rnn)

    outs, lses = [], []
    for g, (window, dilation) in enumerate(ATTN_GROUPS):
        shp = (B, S, HEADS_PER_GROUP, HEAD_DIM)
        q = _partial_rotary(q_parts[g].reshape(shp), cos, sin)
        k = _partial_rotary(k_parts[g].reshape(shp), cos, sin)
        v = v_parts[g].reshape(shp)
        o, lse = _dilated_window_attention(q, k, v, window // (2 * dilation), dilation)
        outs.append(o)
        lses.append(lse)
    wts = jax.nn.softmax(jnp.stack(lses, axis=0), axis=0)
    o = jnp.sum(wts[..., None] * jnp.stack(outs, axis=0), axis=0)
    y_attn = o.reshape(B, S, ATTN_WIDTH).astype(x.dtype)
    branch_b = jnp.einsum('bsc,cd->bsd', y_attn, p_attn)

    merged = jax.nn.sigmoid(gate_a) * branch_a + jax.nn.sigmoid(gate_b) * branch_b
    return x + jnp.einsum('bsd,de->bse', merged, w_out)


def _expert_choice_sublayer(x, norm_g, w_router, b_router, w_gate, w_up, w_down):
    B, S, _ = x.shape
    h = _rmsnorm(x, norm_g)
    logits = jnp.einsum('bsd,de->bse', h, w_router).astype(jnp.float32) + b_router.astype(jnp.float32)
    aff = jax.nn.softmax(logits, axis=-1)
    cap = CAPACITY_FACTOR * S // N_EXPERTS
    gates, idx = lax.top_k(jnp.swapaxes(aff, 1, 2), cap)
    bidx = jnp.arange(B)[:, None, None]
    xg = h[bidx, idx]
    hid = jax.nn.silu(jnp.einsum('becd,edf->becf', xg, w_gate)) * jnp.einsum('becd,edf->becf', xg, w_up)
    eo = jnp.einsum('becf,efd->becd', hid, w_down) * gates[..., None].astype(x.dtype)
    y = jnp.zeros_like(x).at[bidx, idx].add(eo)
    return x + y


def setup_inputs(seed: int = 0) -> dict:
    key = jax.random.key(seed)
    ks = jax.random.split(key, 20)
    f32 = jnp.float32
    nrm = lambda k, shape, scale: jax.random.normal(k, shape, f32) * scale
    u = jax.random.uniform(ks[8], (DEPTH, 2, D_RNN), f32, 0.9, 0.999)
    s_init = u ** (1.0 / LRU_C)
    return {
        'x': jax.random.normal(ks[0], (BATCH, SEQ, D_MODEL), f32),
        'norm_mix': 1.0 + nrm(ks[1], (DEPTH, D_MODEL), 0.05),
        'w_in': nrm(ks[2], (DEPTH, D_MODEL, D_IN), D_MODEL ** -0.5),
        'b_in': nrm(ks[3], (DEPTH, D_IN), 0.02),
        'conv_w': nrm(ks[4], (DEPTH, CONV_WIDTH, D_RNN), CONV_WIDTH ** -0.5),
        'conv_b': nrm(ks[5], (DEPTH, D_RNN), 0.02),
        'rg_w': nrm(ks[6], (DEPTH, 2, 2, RNN_BLOCKS, RNN_BW, RNN_BW), RNN_BW ** -0.5),
        'rg_b': nrm(ks[7], (DEPTH, 2, 2, D_RNN), 0.02),
        'rg_lambda': jnp.log(s_init) - jnp.log1p(-s_init),
        'p_rnn': nrm(ks[9], (DEPTH, D_RNN, D_MODEL), D_RNN ** -0.5),
        'p_attn': nrm(ks[10], (DEPTH, ATTN_WIDTH, D_MODEL), ATTN_WIDTH ** -0.5),
        'w_out': nrm(ks[11], (DEPTH, D_MODEL, D_MODEL), D_MODEL ** -0.5),
        'norm_ffn': 1.0 + nrm(ks[12], (DEPTH, D_MODEL), 0.05),
        'w_router': nrm(ks[13], (DEPTH, D_MODEL, N_EXPERTS), D_MODEL ** -0.5),
        'b_router': nrm(ks[14], (DEPTH, N_EXPERTS), 0.01),
        'w_gate': nrm(ks[15], (DEPTH, N_EXPERTS, D_MODEL, D_FF), D_MODEL ** -0.5),
        'w_up': nrm(ks[16], (DEPTH, N_EXPERTS, D_MODEL, D_FF), D_MODEL ** -0.5),
        'w_down': nrm(ks[17], (DEPTH, N_EXPERTS, D_FF, D_MODEL), D_FF ** -0.5),
        'norm_final': 1.0 + nrm(ks[18], (D_MODEL,), 0.05),
    }


def reference(x, norm_mix, w_in, b_in, conv_w, conv_b, rg_w, rg_b, rg_lambda, p_rnn, p_attn,
              w_out, norm_ffn, w_router, b_router, w_gate, w_up, w_down, norm_final):
    cos, sin = _rotary_tables(x.shape[1])
    for layer in range(DEPTH):
        x = _mixer_sublayer(x, cos, sin, norm_mix[layer], w_in[layer], b_in[layer],
                            conv_w[layer], conv_b[layer], rg_w[layer], rg_b[layer],
                            rg_lambda[layer], p_rnn[layer], p_attn[layer], w_out[layer])
        x = _expert_choice_sublayer(x, norm_ffn[layer], w_router[layer], b_router[layer],
                                    w_gate[layer], w_up[layer], w_down[layer])
    return _rmsnorm(x, norm_final)
```

```python
import functools

import jax
import jax.numpy as jnp
from jax import lax
from jax.experimental import pallas as pl
from jax.experimental.pallas import tpu as pltpu

F32 = jnp.float32
BF16 = jnp.bfloat16

D_RNN = 1024
RNN_BW = 128
CONV_WIDTH = 4
LRU_C = 8.0
HEAD_DIM = 128
HEADS = 4
ATTN_WIDTH = HEADS * HEAD_DIM
ATTN_GROUPS = ((128, 1), (512, 4), (2048, 16))
N_GROUPS = 3
ROPE_THETA = 500000.0
ROPE_DIM = HEAD_DIM // 4
RADIUS = 64
NEG_INF = -1e30
N_EXPERTS = 16
CAPACITY_FACTOR = 2
RMS_EPS = 1e-6

LANES = 128
SUBLANES = 8
VMEM_LIMIT = 56 * 1024 * 1024

ROW_TILE = 512
SCAN_SEG = 128
SCAN_PITCH = 136
SCAN_CHUNK = 512
ATT_QBLK = 1024
ATT_SUB = 128
TOK_BLK = 256
WIN = TOK_BLK + 16


def _cparams(sem):
    return pltpu.CompilerParams(dimension_semantics=sem, vmem_limit_bytes=VMEM_LIMIT)


def _rms(x, g):
    ms = jnp.mean(x * x, axis=-1, keepdims=True)
    return x * lax.rsqrt(ms + RMS_EPS) * g


def _proj_f32_kernel(x_ref, g_ref, w_ref, b_ref, o_ref, *, n_chunk):
    hb = _rms(x_ref[...], g_ref[...]).astype(BF16)
    for c in range(o_ref.shape[1] // n_chunk):
        sl = slice(c * n_chunk, (c + 1) * n_chunk)
        o_ref[:, sl] = jnp.dot(hb, w_ref[:, sl], preferred_element_type=F32) + b_ref[:, sl]


def _proj_f32(x2d, g, w, b):
    T, D = x2d.shape
    N = w.shape[1]
    return pl.pallas_call(
        functools.partial(_proj_f32_kernel, n_chunk=512),
        out_shape=jax.ShapeDtypeStruct((T, N), F32),
        grid=(T // ROW_TILE,),
        in_specs=[
            pl.BlockSpec((ROW_TILE, D), lambda i: (i, 0)),
            pl.BlockSpec((1, D), lambda i: (0, 0)),
            pl.BlockSpec((D, N), lambda i: (0, 0)),
            pl.BlockSpec((1, N), lambda i: (0, 0)),
        ],
        out_specs=pl.BlockSpec((ROW_TILE, N), lambda i: (i, 0)),
        compiler_params=_cparams(("parallel",)),
        name="proj_rnn",
    )(x2d, g, w, b)


def _proj_qkv_kernel(x_ref, g_ref, w_ref, b_ref, cos_ref, sin_ref, o_ref, *, n_rot):
    hb = _rms(x_ref[...], g_ref[...]).astype(BF16)
    cosv = cos_ref[...]
    sinv = sin_ref[...]
    lane = lax.broadcasted_iota(jnp.int32, (1, HEAD_DIM), 1)
    first_half = lane < (ROPE_DIM // 2)
    for c in range(o_ref.shape[1] // HEAD_DIM):
        sl = slice(c * HEAD_DIM, (c + 1) * HEAD_DIM)
        t = jnp.dot(hb, w_ref[:, sl], preferred_element_type=F32) + b_ref[:, sl]
        if c < n_rot:
            up = pltpu.roll(t, HEAD_DIM - ROPE_DIM // 2, 1)
            down = pltpu.roll(t, ROPE_DIM // 2, 1)
            t = t * cosv + jnp.where(first_half, up, down) * sinv
        o_ref[:, sl] = t.astype(BF16)


def _proj_qkv(x2d, g, w, b, cos_t, sin_t, seq):
    T, D = x2d.shape
    N = w.shape[1]
    nseq = seq // ROW_TILE
    return pl.pallas_call(
        functools.partial(_proj_qkv_kernel, n_rot=2 * N_GROUPS * HEADS),
        out_shape=jax.ShapeDtypeStruct((T, N), BF16),
        grid=(T // ROW_TILE,),
        in_specs=[
            pl.BlockSpec((ROW_TILE, D), lambda i: (i, 0)),
            pl.BlockSpec((1, D), lambda i: (0, 0)),
            pl.BlockSpec((D, N), lambda i: (0, 0)),
            pl.BlockSpec((1, N), lambda i: (0, 0)),
            pl.BlockSpec((ROW_TILE, HEAD_DIM), lambda i: (i % nseq, 0)),
            pl.BlockSpec((ROW_TILE, HEAD_DIM), lambda i: (i % nseq, 0)),
        ],
        out_specs=pl.BlockSpec((ROW_TILE, N), lambda i: (i, 0)),
        compiler_params=_cparams(("parallel",)),
        name="proj_qkv",
    )(x2d, g, w, b, cos_t, sin_t)


def _rope_tables(seq):
    pos = jnp.arange(seq, dtype=F32)
    inv = ROPE_THETA ** (-jnp.arange(0, ROPE_DIM, 2, dtype=F32) / ROPE_DIM)
    ang = pos[:, None] * inv[None, :]
    cos, sin = jnp.cos(ang), jnp.sin(ang)
    half = ROPE_DIM // 2
    pad = HEAD_DIM - ROPE_DIM
    cos_t = jnp.concatenate([cos, cos, jnp.ones((seq, pad), F32)], axis=1)
    sin_t = jnp.concatenate([-sin, sin, jnp.zeros((seq, pad), F32)], axis=1)
    del half
    return cos_t, sin_t


def _rglru_kernel(xr_ref, gr_ref, cw_ref, cb_ref, gw_ref, gb_ref, lam_ref, o_ref,
                  af, uf, ab, ub, cf, cbk, *, seq):
    nseg = seq // SCAN_SEG
    nchunk = seq // SCAN_CHUNK
    seg_per_chunk = SCAN_CHUNK // SCAN_SEG
    nv = nseg // SUBLANES

    cw = cw_ref[...]
    cbias = cb_ref[...]
    gbias = gb_ref[...]
    z = -lam_ref[...]
    sp = jnp.maximum(z, 0.0) + jnp.log1p(jnp.exp(-jnp.abs(z)))

    def stage1(c, carry):
        t0 = pl.multiple_of(c * SCAN_CHUNK, SCAN_CHUNK)
        main = xr_ref[0, pl.ds(t0, SCAN_CHUNK), :]
        pstart = pl.multiple_of(jnp.maximum(t0 - SUBLANES, 0), SUBLANES)
        nstart = pl.multiple_of(jnp.minimum(t0 + SCAN_CHUNK, seq - SUBLANES), SUBLANES)
        prev = xr_ref[0, pl.ds(pstart, SUBLANES), :] * (c > 0).astype(F32)
        nxt = xr_ref[0, pl.ds(nstart, SUBLANES), :] * (c < nchunk - 1).astype(F32)
        ext = jnp.concatenate([prev, main, nxt], axis=0)
        xc = cbias
        for tap in range(CONV_WIDTH):
            lo = SUBLANES - CONV_WIDTH // 2 + tap
            xc = xc + cw[tap:tap + 1, :] * ext[lo:lo + SCAN_CHUNK, :]
        gates = jnp.dot(xc.astype(BF16), gw_ref[...], preferred_element_type=F32) + gbias
        for d, (a_s, u_s) in enumerate(((af, uf), (ab, ub))):
            r = jax.nn.sigmoid(gates[:, (2 * d) * LANES:(2 * d + 1) * LANES])
            ig = jax.nn.sigmoid(gates[:, (2 * d + 1) * LANES:(2 * d + 2) * LANES])
            log_a = (-LRU_C) * r * sp[d:d + 1, :]
            a = jnp.exp(log_a)
            u = xc * ig * jnp.sqrt(jnp.tanh(-log_a) * (1.0 + a * a))
            for k in range(seg_per_chunk):
                row = pl.multiple_of((c * seg_per_chunk + k) * SCAN_PITCH, SUBLANES)
                a_s[pl.ds(row, SCAN_SEG), :] = a[k * SCAN_SEG:(k + 1) * SCAN_SEG, :]
                u_s[pl.ds(row, SCAN_SEG), :] = u[k * SCAN_SEG:(k + 1) * SCAN_SEG, :]
        return carry

    lax.fori_loop(0, nchunk, stage1, 0)

    def scan_dir(a_s, u_s, reverse):
        def body(it, carry):
            i = (SCAN_SEG - 1 - it) if reverse else it
            hs, cum = carry
            new_h, new_c = [], []
            for k in range(nv):
                idx = pl.ds(i + k * SUBLANES * SCAN_PITCH, SUBLANES, stride=SCAN_PITCH)
                a = a_s[idx, :]
                u = u_s[idx, :]
                h = a * hs[k] + u
                cp = a * cum[k]
                u_s[idx, :] = h
                a_s[idx, :] = cp
                new_h.append(h)
                new_c.append(cp)
            return tuple(new_h), tuple(new_c)

        zeros = tuple(jnp.zeros((SUBLANES, LANES), F32) for _ in range(nv))
        ones = tuple(jnp.ones((SUBLANES, LANES), F32) for _ in range(nv))
        lax.fori_loop(0, SCAN_SEG, body, (zeros, ones))

    scan_dir(af, uf, False)
    scan_dir(ab, ub, True)

    end_h = uf[pl.ds(SCAN_SEG - 1, nseg, stride=SCAN_PITCH), :]
    end_a = af[pl.ds(SCAN_SEG - 1, nseg, stride=SCAN_PITCH), :]
    c = jnp.zeros((1, LANES), F32)
    for s in range(nseg):
        cf[s:s + 1, :] = c
        c = end_h[s:s + 1, :] + end_a[s:s + 1, :] * c
    beg_h = ub[pl.ds(0, nseg, stride=SCAN_PITCH), :]
    beg_a = ab[pl.ds(0, nseg, stride=SCAN_PITCH), :]
    c = jnp.zeros((1, LANES), F32)
    for s in range(nseg - 1, -1, -1):
        cbk[s:s + 1, :] = c
        c = beg_h[s:s + 1, :] + beg_a[s:s + 1, :] * c

    def finish(s, carry):
        row = pl.multiple_of(s * SCAN_PITCH, SUBLANES)
        t0 = pl.multiple_of(s * SCAN_SEG, SCAN_SEG)
        hf = uf[pl.ds(row, SCAN_SEG), :] + af[pl.ds(row, SCAN_SEG), :] * cf[pl.ds(s, 1), :]
        hb = ub[pl.ds(row, SCAN_SEG), :] + ab[pl.ds(row, SCAN_SEG), :] * cbk[pl.ds(s, 1), :]
        g = jax.nn.gelu(gr_ref[0, pl.ds(t0, SCAN_SEG), :])
        o_ref[0, pl.ds(t0, SCAN_SEG), :] = (g * (hf + hb)).astype(BF16)
        return carry

    lax.fori_loop(0, nseg, finish, 0)


def _rglru(xrgr, conv_w, conv_b, gate_w, gate_b, lam):
    B, S, _ = xrgr.shape
    nblk = D_RNN // RNN_BW
    nseg = S // SCAN_SEG
    scr = pltpu.VMEM((nseg * SCAN_PITCH, LANES), F32)
    return pl.pallas_call(
        functools.partial(_rglru_kernel, seq=S),
        out_shape=jax.ShapeDtypeStruct((B, S, D_RNN), BF16),
        grid=(B, nblk),
        in_specs=[
            pl.BlockSpec((1, S, RNN_BW), lambda b, c: (b, 0, c)),
            pl.BlockSpec((1, S, RNN_BW), lambda b, c: (b, 0, nblk + c)),
            pl.BlockSpec((CONV_WIDTH, RNN_BW), lambda b, c: (0, c)),
            pl.BlockSpec((1, RNN_BW), lambda b, c: (0, c)),
            pl.BlockSpec((None, RNN_BW, 4 * RNN_BW), lambda b, c: (c, 0, 0)),
            pl.BlockSpec((None, 1, 4 * RNN_BW), lambda b, c: (c, 0, 0)),
            pl.BlockSpec((None, 2, RNN_BW), lambda b, c: (c, 0, 0)),
        ],
        out_specs=pl.BlockSpec((1, S, RNN_BW), lambda b, c: (b, 0, c)),
        scratch_shapes=[scr, scr, scr, scr,
                        pltpu.VMEM((nseg, LANES), F32), pltpu.VMEM((nseg, LANES), F32)],
        compiler_params=_cparams(("parallel", "parallel")),
        name="rglru",
    )(xrgr, xrgr, conv_w, conv_b, gate_w, gate_b, lam)


def _attn_kernel(q_ref, k_ref, kp_ref, kn_ref, v_ref, vp_ref, vn_ref, o_ref, lse_ref,
                 kext, vext, *, qblk, length):
    j = pl.program_id(2)
    kext[0:RADIUS, :] = kp_ref[0]
    kext[RADIUS:RADIUS + qblk, :] = k_ref[0]
    kext[RADIUS + qblk:RADIUS + qblk + RADIUS, :] = kn_ref[0]
    vext[0:RADIUS, :] = vp_ref[0]
    vext[RADIUS:RADIUS + qblk, :] = v_ref[0]
    vext[RADIUS + qblk:RADIUS + qblk + RADIUS, :] = vn_ref[0]

    nkey = ATT_SUB + 2 * RADIUS
    qi = lax.broadcasted_iota(jnp.int32, (ATT_SUB, nkey), 0)
    ki = lax.broadcasted_iota(jnp.int32, (ATT_SUB, nkey), 1)
    band = jnp.abs(ki - RADIUS - qi) <= RADIUS
    lane = lax.broadcasted_iota(jnp.int32, (ATT_SUB, LANES), 1)
    scale = HEAD_DIM ** -0.5

    def sub(i, carry):
        q0 = pl.multiple_of(i * ATT_SUB, ATT_SUB)
        kglob = j * qblk + q0 - RADIUS + ki
        valid = band & (kglob >= 0) & (kglob < length)
        lse_tile = jnp.zeros((ATT_SUB, LANES), F32)
        for h in range(HEADS):
            hs = slice(h * HEAD_DIM, (h + 1) * HEAD_DIM)
            q = q_ref[0, pl.ds(q0, ATT_SUB), hs]
            ks = kext[pl.ds(q0, nkey), hs]
            vs = vext[pl.ds(q0, nkey), hs]
            s = lax.dot_general(q, ks, (((1,), (1,)), ((), ())),
                                preferred_element_type=F32) * scale
            s = jnp.where(valid, s, NEG_INF)
            m = jnp.max(s, axis=-1, keepdims=True)
            p = jnp.exp(s - m)
            l = jnp.sum(p, axis=-1, keepdims=True)
            o = jnp.dot(p.astype(BF16), vs, preferred_element_type=F32) / l
            o_ref[0, pl.ds(q0, ATT_SUB), hs] = o
            lse_tile = jnp.where(lane == h, m + jnp.log(l), lse_tile)
        lse_ref[0, pl.ds(q0, ATT_SUB), :] = lse_tile
        return carry

    lax.fori_loop(0, qblk // ATT_SUB, sub, 0)


def _attention_group(qkv, g, dilation):
    B, S, W = qkv.shape
    L = S // dilation
    qblk = min(L, ATT_QBLK)
    nblk = L // qblk
    nparts = W // ATTN_WIDTH
    view = qkv.reshape(B, L, dilation * W)
    halo_per_blk = qblk // RADIUS
    last_halo = L // RADIUS - 1

    def main_map(part):
        return lambda b, r, j: (b, j, r * nparts + part)

    def prev_map(part):
        return lambda b, r, j: (b, jnp.maximum(j * halo_per_blk - 1, 0), r * nparts + part)

    def next_map(part):
        return lambda b, r, j: (b, jnp.minimum((j + 1) * halo_per_blk, last_halo),
                                r * nparts + part)

    qp, kp, vp = g, N_GROUPS + g, 2 * N_GROUPS + g
    main = lambda part: pl.BlockSpec((1, qblk, ATTN_WIDTH), main_map(part))
    prev = lambda part: pl.BlockSpec((1, RADIUS, ATTN_WIDTH), prev_map(part))
    nxt = lambda part: pl.BlockSpec((1, RADIUS, ATTN_WIDTH), next_map(part))
    o, lse = pl.pallas_call(
        functools.partial(_attn_kernel, qblk=qblk, length=L),
        out_shape=(jax.ShapeDtypeStruct((B, L, dilation * ATTN_WIDTH), F32),
                   jax.ShapeDtypeStruct((B, L, dilation * LANES), F32)),
        grid=(B, dilation, nblk),
        in_specs=[main(qp), main(kp), prev(kp), nxt(kp), main(vp), prev(vp), nxt(vp)],
        out_specs=(pl.BlockSpec((1, qblk, ATTN_WIDTH), lambda b, r, j: (b, j, r)),
                   pl.BlockSpec((1, qblk, LANES), lambda b, r, j: (b, j, r))),
        scratch_shapes=[pltpu.VMEM((qblk + 2 * RADIUS, ATTN_WIDTH), BF16),
                        pltpu.VMEM((qblk + 2 * RADIUS, ATTN_WIDTH), BF16)],
        compiler_params=_cparams(("parallel", "parallel", "parallel")),
        name=f"attn_g{g}",
    )(view, view, view, view, view, view, view)
    return o.reshape(B, S, ATTN_WIDTH), lse.reshape(B, S, LANES)


def _merge_kernel(x_ref, yr_ref, o0_ref, o1_ref, o2_ref, l0_ref, l1_ref, l2_ref,
                  gmix_ref, wg_ref, bg_ref, prnn_ref, pattn_ref, wout_ref,
                  gffn_ref, wr_ref, wrt_ref, br_ref, brt_ref,
                  x2_ref, h2_ref, aff_ref, afft_ref, yattn):
    d = x_ref.shape[1]
    x = x_ref[...]
    hb = _rms(x, gmix_ref[...]).astype(BF16)

    o_refs = (o0_ref, o1_ref, o2_ref)
    lses = [r[...] for r in (l0_ref, l1_ref, l2_ref)]
    for h in range(HEADS):
        hs = slice(h * HEAD_DIM, (h + 1) * HEAD_DIM)
        cols = [l[:, h:h + 1] for l in lses]
        m = jnp.maximum(jnp.maximum(cols[0], cols[1]), cols[2])
        ws = [jnp.exp(c - m) for c in cols]
        den = ws[0] + ws[1] + ws[2]
        acc = (ws[0] / den) * o_refs[0][:, hs]
        acc = acc + (ws[1] / den) * o_refs[1][:, hs]
        acc = acc + (ws[2] / den) * o_refs[2][:, hs]
        yattn[:, hs] = acc.astype(BF16)

    branch_a = jnp.dot(yr_ref[...], prnn_ref[...], preferred_element_type=F32)
    branch_b = jnp.dot(yattn[...], pattn_ref[...], preferred_element_type=F32)
    ga = jnp.dot(hb, wg_ref[:, 0:d], preferred_element_type=F32) + bg_ref[:, 0:d]
    gb = jnp.dot(hb, wg_ref[:, d:2 * d], preferred_element_type=F32) + bg_ref[:, d:2 * d]
    merged = jax.nn.sigmoid(ga) * branch_a + jax.nn.sigmoid(gb) * branch_b
    x2 = x + jnp.dot(merged.astype(BF16), wout_ref[...], preferred_element_type=F32)
    x2_ref[...] = x2

    h2 = _rms(x2, gffn_ref[...]).astype(BF16)
    h2_ref[...] = h2
    logits = jnp.dot(h2, wr_ref[...], preferred_element_type=F32) + br_ref[...]
    mx = jnp.max(logits, axis=-1, keepdims=True)
    e = jnp.exp(logits - mx)
    aff_ref[...] = e / jnp.sum(e, axis=-1, keepdims=True)
    logits_t = lax.dot_general(wrt_ref[...], h2, (((1,), (1,)), ((), ())),
                               preferred_element_type=F32) + brt_ref[...]
    mxt = jnp.max(logits_t, axis=0, keepdims=True)
    et = jnp.exp(logits_t - mxt)
    afft_ref[0] = et / jnp.sum(et, axis=0, keepdims=True)


def _merge(x2d, y_rnn, outs, lses, gmix, wg, bg, p_rnn, p_attn, w_out, gffn, wr, br, batch):
    T, D = x2d.shape
    S = T // batch
    nseq = S // ROW_TILE
    E = wr.shape[1]
    row = lambda w: pl.BlockSpec((ROW_TILE, w), lambda i: (i, 0))
    full = lambda a: pl.BlockSpec(a.shape, lambda i: (0,) * a.ndim)
    wrt = wr.T
    brt = br.reshape(E, 1)
    args = [x2d, y_rnn, *outs, *lses, gmix, wg, bg, p_rnn, p_attn, w_out, gffn, wr, wrt, br, brt]
    in_specs = [row(D), row(D_RNN), row(ATTN_WIDTH), row(ATTN_WIDTH), row(ATTN_WIDTH),
                row(LANES), row(LANES), row(LANES)] + [full(a) for a in args[8:]]
    return pl.pallas_call(
        _merge_kernel,
        out_shape=(jax.ShapeDtypeStruct((T, D), F32),
                   jax.ShapeDtypeStruct((T, D), BF16),
                   jax.ShapeDtypeStruct((T, E), F32),
                   jax.ShapeDtypeStruct((batch, E, S), F32)),
        grid=(T // ROW_TILE,),
        in_specs=in_specs,
        out_specs=(row(D), row(D), row(E),
                   pl.BlockSpec((1, E, ROW_TILE), lambda i: (i // nseq, 0, i % nseq))),
        scratch_shapes=[pltpu.VMEM((ROW_TILE, ATTN_WIDTH), BF16)],
        compiler_params=_cparams(("parallel",)),
        name="merge",
    )(*args)


def _route_kernel(aff_ref, pos_ref, off_ref, *, cap):
    a = aff_ref[0]
    E, R, _ = a.shape
    bits = pltpu.bitcast(a, jnp.int32)

    def count(mask):
        c = jnp.sum(mask.astype(F32), axis=2, keepdims=True)
        return jnp.sum(c, axis=1, keepdims=True)

    def bs(_, carry):
        lo, hi = carry
        mid = lo + ((hi - lo + 1) >> 1)
        ok = count(bits >= mid) >= cap
        return jnp.where(ok, mid, lo), jnp.where(ok, hi, mid - 1)

    lo0 = jnp.zeros((E, 1, 1), jnp.int32)
    hi0 = jnp.full((E, 1, 1), 0x7F7FFFFF, jnp.int32)
    thr, _ = lax.fori_loop(0, 31, bs, (lo0, hi0))

    gt = bits > thr
    eq = bits == thr
    need = cap - count(gt)

    ki = lax.broadcasted_iota(jnp.int32, (LANES, LANES), 0)
    li = lax.broadcasted_iota(jnp.int32, (LANES, LANES), 1)
    upper = (ki <= li).astype(BF16)
    allones = jnp.ones((LANES, LANES), BF16)
    ri = lax.broadcasted_iota(jnp.int32, (R, R), 0)
    ci = lax.broadcasted_iota(jnp.int32, (R, R), 1)
    strict = (ci < ri).astype(BF16)

    def prefix(mask):
        mb = mask.astype(BF16).reshape(E * R, LANES)
        within = jnp.dot(mb, upper, preferred_element_type=F32)
        totals = jnp.dot(mb, allones, preferred_element_type=F32).astype(BF16)
        offs = []
        for e in range(E):
            offs.append(jnp.dot(strict, totals[e * R:(e + 1) * R, :],
                                preferred_element_type=F32))
        rowoff = jnp.concatenate(offs, axis=0)
        excl = within - mb.astype(F32) + rowoff
        return excl.reshape(E, R, LANES), rowoff.reshape(E, R, LANES)

    eq_rank, _ = prefix(eq)
    sel = gt | (eq & (eq_rank < need.astype(F32)))
    slot, rowoff = prefix(sel)
    pos_ref[0] = jnp.where(sel, slot.astype(jnp.int32), -1)
    off_ref[0] = rowoff.astype(jnp.int32)


def _route(aff_t, cap):
    B, E, S = aff_t.shape
    R = S // LANES
    a4 = aff_t.reshape(B, E, R, LANES)
    blk = pl.BlockSpec((1, E, R, LANES), lambda b: (b, 0, 0, 0))
    pos, off = pl.pallas_call(
        functools.partial(_route_kernel, cap=cap),
        out_shape=(jax.ShapeDtypeStruct((B, E, R, LANES), jnp.int32),
                   jax.ShapeDtypeStruct((B, E, R, LANES), jnp.int32)),
        grid=(B,),
        in_specs=[blk],
        out_specs=(blk, blk),
        compiler_params=_cparams(("parallel",)),
        name="route",
    )(a4)
    return pos, off


def _gather_kernel(off_ref, pos_ref, h_ref, o_ref, acc, *, cap, ntb):
    b = pl.program_id(0)
    e = pl.program_id(1)
    acc[...] = jnp.zeros_like(acc)
    ji = lax.broadcasted_iota(jnp.int32, (WIN, TOK_BLK), 0)

    def body(tb, carry):
        off = off_ref[(b * pl.num_programs(1) + e) * ntb + tb]
        start = pl.multiple_of((off >> 4) << 4, 16)
        prow = pos_ref[0, 0, pl.ds(tb, 1), :]
        onehot = ((ji + start) == prow).astype(BF16)
        t0 = pl.multiple_of(tb * TOK_BLK, TOK_BLK)
        g = jnp.dot(onehot, h_ref[0, pl.ds(t0, TOK_BLK), :], preferred_element_type=F32)
        acc[pl.ds(start, WIN), :] += g
        return carry

    lax.fori_loop(0, ntb, body, 0)
    o_ref[0, 0] = acc[0:cap, :].astype(BF16)


def _gather(offs, pos, h2, cap):
    B, E, S = pos.shape
    D = h2.shape[-1]
    ntb = S // TOK_BLK
    pos4 = pos.reshape(B, E, ntb, TOK_BLK)
    return pl.pallas_call(
        functools.partial(_gather_kernel, cap=cap, ntb=ntb),
        out_shape=jax.ShapeDtypeStruct((B, E, cap, D), BF16),
        grid_spec=pltpu.PrefetchScalarGridSpec(
            num_scalar_prefetch=1,
            grid=(B, E),
            in_specs=[
                pl.BlockSpec((1, 1, ntb, TOK_BLK), lambda b, e, off: (b, e, 0, 0)),
                pl.BlockSpec((1, S, D), lambda b, e, off: (b, 0, 0),
                             pipeline_mode=pl.Buffered(1)),
            ],
            out_specs=pl.BlockSpec((1, 1, cap, D), lambda b, e, off: (b, e, 0, 0)),
            scratch_shapes=[pltpu.VMEM((cap + WIN, D), F32)],
        ),
        compiler_params=_cparams(("arbitrary", "arbitrary")),
        name="gather",
    )(offs, pos4, h2)


def _ffn_kernel(x_ref, wg_ref, wu_ref, wd_ref, o_ref, acc, *, f_chunk):
    x = x_ref[0, 0]
    nf = wg_ref.shape[2] // f_chunk
    for f in range(nf):
        fs = slice(f * f_chunk, (f + 1) * f_chunk)
        g = jnp.dot(x, wg_ref[0, :, fs], preferred_element_type=F32)
        u = jnp.dot(x, wu_ref[0, :, fs], preferred_element_type=F32)
        hid = (g * jax.nn.sigmoid(g) * u).astype(BF16)
        part = jnp.dot(hid, wd_ref[0, fs, :], preferred_element_type=F32)
        if f == 0:
            acc[...] = part
        else:
            acc[...] += part
    o_ref[0, 0] = acc[...].astype(BF16)


def _ffn(xg, w_gate, w_up, w_down):
    B, E, C, D = xg.shape
    F = w_gate.shape[2]
    return pl.pallas_call(
        functools.partial(_ffn_kernel, f_chunk=512),
        out_shape=jax.ShapeDtypeStruct((B, E, C, D), BF16),
        grid=(E, B),
        in_specs=[
            pl.BlockSpec((1, 1, C, D), lambda e, b: (b, e, 0, 0)),
            pl.BlockSpec((1, D, F), lambda e, b: (e, 0, 0)),
            pl.BlockSpec((1, D, F), lambda e, b: (e, 0, 0)),
            pl.BlockSpec((1, F, D), lambda e, b: (e, 0, 0)),
        ],
        out_specs=pl.BlockSpec((1, 1, C, D), lambda e, b: (b, e, 0, 0)),
        scratch_shapes=[pltpu.VMEM((C, D), F32)],
        compiler_params=_cparams(("arbitrary", "arbitrary")),
        name="ffn",
    )(xg, w_gate, w_up, w_down)


def _combine_kernel(off_ref, x_ref, pos_ref, aff_ref, eo_ref, g_ref, o_ref, *, cap, ntb):
    b = pl.program_id(0)
    tb = pl.program_id(1)
    E = pos_ref.shape[2]
    pos = pos_ref[0]
    aff = aff_ref[0]
    ji = lax.broadcasted_iota(jnp.int32, (TOK_BLK, WIN), 1)
    y = x_ref[0]
    for e in range(E):
        off = off_ref[(b * E + e) * ntb + tb]
        start = pl.multiple_of(jnp.minimum((off >> 4) << 4, cap - WIN), 16)
        pcol = pos[:, e:e + 1]
        onehot = ((ji + start) == pcol).astype(BF16)
        win = eo_ref[0, e, pl.ds(start, WIN), :]
        contrib = jnp.dot(onehot, win, preferred_element_type=F32)
        gate = jnp.where(pcol >= 0, aff[:, e:e + 1], 0.0)
        y = y + gate * contrib
    o_ref[0] = _rms(y, g_ref[...])


def _combine(offs, x2, pos_t, aff, eo, g_final, cap):
    B, S, D = x2.shape
    E = pos_t.shape[2]
    ntb = S // TOK_BLK
    return pl.pallas_call(
        functools.partial(_combine_kernel, cap=cap, ntb=ntb),
        out_shape=jax.ShapeDtypeStruct((B, S, D), F32),
        grid_spec=pltpu.PrefetchScalarGridSpec(
            num_scalar_prefetch=1,
            grid=(B, ntb),
            in_specs=[
                pl.BlockSpec((1, TOK_BLK, D), lambda b, t, off: (b, t, 0)),
                pl.BlockSpec((1, TOK_BLK, E), lambda b, t, off: (b, t, 0)),
                pl.BlockSpec((1, TOK_BLK, E), lambda b, t, off: (b, t, 0)),
                pl.BlockSpec((1, E, cap, D), lambda b, t, off: (b, 0, 0, 0),
                             pipeline_mode=pl.Buffered(1)),
                pl.BlockSpec((1, D), lambda b, t, off: (0, 0)),
            ],
            out_specs=pl.BlockSpec((1, TOK_BLK, D), lambda b, t, off: (b, t, 0)),
        ),
        compiler_params=_cparams(("arbitrary", "arbitrary")),
        name="combine",
    )(offs, x2, pos_t, aff, eo, g_final)


def _layer(x, norm_mix, w_in, b_in, conv_w, conv_b, rg_w, rg_b, rg_lambda, p_rnn, p_attn,
           w_out, norm_ffn, w_router, b_router, w_gate, w_up, w_down, norm_out, cos_t, sin_t):
    B, S, D = x.shape
    T = B * S
    x2d = x.reshape(T, D)
    n_rnn = 2 * D_RNN
    n_qkv = 3 * N_GROUPS * ATTN_WIDTH
    gmix = norm_mix.reshape(1, D)
    w_in_b = w_in.astype(BF16)
    b_in2 = b_in.reshape(1, -1)

    xrgr = _proj_f32(x2d, gmix, w_in_b[:, :n_rnn], b_in2[:, :n_rnn])
    qkv = _proj_qkv(x2d, gmix, w_in_b[:, n_rnn:n_rnn + n_qkv], b_in2[:, n_rnn:n_rnn + n_qkv],
                    cos_t, sin_t, S)

    nblk = D_RNN // RNN_BW
    gate_w = jnp.transpose(rg_w, (2, 3, 0, 1, 4)).reshape(nblk, RNN_BW, 4 * RNN_BW).astype(BF16)
    gate_b = jnp.transpose(rg_b.reshape(2, 2, nblk, RNN_BW), (2, 0, 1, 3)).reshape(nblk, 1, 4 * RNN_BW)
    lam = jnp.transpose(rg_lambda.reshape(2, nblk, RNN_BW), (1, 0, 2))
    y_rnn = _rglru(xrgr.reshape(B, S, n_rnn), conv_w, conv_b.reshape(1, D_RNN), gate_w, gate_b, lam)

    qkv3 = qkv.reshape(B, S, n_qkv)
    outs, lses = [], []
    for g, (_, dilation) in enumerate(ATTN_GROUPS):
        o, lse = _attention_group(qkv3, g, dilation)
        outs.append(o.reshape(T, ATTN_WIDTH))
        lses.append(lse.reshape(T, LANES))

    x2, h2, aff, aff_t = _merge(
        x2d, y_rnn.reshape(T, D_RNN), outs, lses, gmix,
        w_in_b[:, n_rnn + n_qkv:], b_in2[:, n_rnn + n_qkv:],
        p_rnn.astype(BF16), p_attn.astype(BF16), w_out.astype(BF16),
        norm_ffn.reshape(1, D), w_router.astype(BF16), b_router.reshape(1, -1), B)

    E = w_router.shape[1]
    cap = CAPACITY_FACTOR * S // E
    pos4, off4 = _route(aff_t, cap)
    pos = pos4.reshape(B, E, S)
    rows_per_blk = TOK_BLK // LANES
    offs = off4[:, :, ::rows_per_blk, 0].reshape(-1)
    xg = _gather(offs, pos, h2.reshape(B, S, D), cap)
    eo = _ffn(xg, w_gate.astype(BF16), w_up.astype(BF16), w_down.astype(BF16))
    pos_t = jnp.transpose(pos, (0, 2, 1))
    return _combine(offs, x2.reshape(B, S, D), pos_t, aff.reshape(B, S, E), eo,
                    norm_out.reshape(1, D), cap)


def kernel(x, norm_mix, w_in, b_in, conv_w, conv_b, rg_w, rg_b, rg_lambda, p_rnn, p_attn, w_out,
           norm_ffn, w_router, b_router, w_gate, w_up, w_down, norm_final):
    depth = w_in.shape[0]
    assert depth == 1, "final rmsnorm is fused into the single layer's combine step"
    cos_t, sin_t = _rope_tables(x.shape[1])
    return _layer(x, norm_mix[0], w_in[0], b_in[0], conv_w[0], conv_b[0], rg_w[0], rg_b[0],
                  rg_lambda[0], p_rnn[0], p_attn[0], w_out[0], norm_ffn[0], w_router[0],
                  b_router[0], w_gate[0], w_up[0], w_down[0], norm_final, cos_t, sin_t)
```

```python
import functools

import jax
import jax.numpy as jnp
from jax import lax
from jax.experimental import pallas as pl
from jax.experimental.pallas import tpu as pltpu

F32 = jnp.float32
BF16 = jnp.bfloat16

D_RNN = 1024
RNN_BW = 128
CONV_WIDTH = 4
LRU_C = 8.0
HEAD_DIM = 128
HEADS = 4
ATTN_WIDTH = HEADS * HEAD_DIM
ATTN_GROUPS = ((128, 1), (512, 4), (2048, 16))
N_GROUPS = 3
ROPE_THETA = 500000.0
ROPE_DIM = HEAD_DIM // 4
RADIUS = 64
NEG_INF = -1e30
N_EXPERTS = 16
CAPACITY_FACTOR = 2
RMS_EPS = 1e-6

LANES = 128
SUBLANES = 8
BF16_ROWS = 16
MXU_DIM = 256
VMEM_LIMIT = 56 * 1024 * 1024
F32_MIN_NORMAL = 1.1754944e-38

ROW_TILE = 512
SCAN_SEG = 128
SCAN_CHUNK = 512
ATT_QBLK = 1024
ATT_SUB = 128
TOK_BLK = 256
SLOT_SHIFT = 6
SLOT_WIN = 1 << SLOT_SHIFT
EXPERT_SPLIT = 2


def _cparams(sem):
    return pltpu.CompilerParams(dimension_semantics=sem, vmem_limit_bytes=VMEM_LIMIT)


def _rms(x, g):
    ms = jnp.mean(x * x, axis=-1, keepdims=True)
    return x * lax.rsqrt(ms + RMS_EPS) * g


def _pitch(d):
    return d if d <= 4 else d + 4


def _deinterleave(t, stage, d):
    n = t.shape[0]
    p = _pitch(d)
    if p == d:
        stage[0:n, :] = t
    else:
        for j in range(n // d):
            stage[j * p:j * p + d, :] = t[j * d:(j + 1) * d, :]
    return [stage[pl.ds(r, n // d, stride=p), :] for r in range(d)]


def _interleave(parts, stage, d):
    m = parts[0].shape[0]
    p = _pitch(d)
    for r in range(d):
        stage[pl.ds(r, m, stride=p), :] = parts[r]
    if p == d:
        return stage[0:m * d, :]
    return jnp.concatenate([stage[j * p:j * p + d, :] for j in range(m)], axis=0)


def _proj_f32_kernel(x_ref, g_ref, w_ref, b_ref, o_ref, *, n_chunk):
    hb = _rms(x_ref[...], g_ref[...]).astype(BF16)
    for c in range(o_ref.shape[1] // n_chunk):
        sl = slice(c * n_chunk, (c + 1) * n_chunk)
        o_ref[:, sl] = jnp.dot(hb, w_ref[:, sl], preferred_element_type=F32) + b_ref[:, sl]


def _proj_f32(x2d, g, w, b):
    T, D = x2d.shape
    N = w.shape[1]
    return pl.pallas_call(
        functools.partial(_proj_f32_kernel, n_chunk=512),
        out_shape=jax.ShapeDtypeStruct((T, N), F32),
        grid=(T // ROW_TILE,),
        in_specs=[
            pl.BlockSpec((ROW_TILE, D), lambda i: (i, 0)),
            pl.BlockSpec((1, D), lambda i: (0, 0)),
            pl.BlockSpec((D, N), lambda i: (0, 0)),
            pl.BlockSpec((1, N), lambda i: (0, 0)),
        ],
        out_specs=pl.BlockSpec((ROW_TILE, N), lambda i: (i, 0)),
        compiler_params=_cparams(("parallel",)),
        name="proj_rnn",
    )(x2d, g, w, b)


def _proj_qkv_kernel(x_ref, g_ref, w0_ref, w1_ref, w2_ref, b_ref, cos_ref, sin_ref,
                     o0_ref, o1_ref, o2_ref, stage):
    hb = _rms(x_ref[...], g_ref[...]).astype(BF16)
    cosv = cos_ref[...]
    sinv = sin_ref[...]
    lane = lax.broadcasted_iota(jnp.int32, (1, HEAD_DIM), 1)
    first_half = lane < (ROPE_DIM // 2)
    gw = 3 * ATTN_WIDTH
    for g, (w_ref, o_ref) in enumerate(((w0_ref, o0_ref), (w1_ref, o1_ref), (w2_ref, o2_ref))):
        d = ATTN_GROUPS[g][1]
        for c in range(gw // MXU_DIM):
            sl = slice(c * MXU_DIM, (c + 1) * MXU_DIM)
            t2 = jnp.dot(hb, w_ref[:, sl], preferred_element_type=F32)
            t2 = t2 + b_ref[:, g * gw + c * MXU_DIM:g * gw + (c + 1) * MXU_DIM]
            for hh in range(MXU_DIM // HEAD_DIM):
                t = t2[:, hh * HEAD_DIM:(hh + 1) * HEAD_DIM]
                col = c * MXU_DIM + hh * HEAD_DIM
                if col < 2 * ATTN_WIDTH:
                    up = pltpu.roll(t, HEAD_DIM - ROPE_DIM // 2, 1)
                    down = pltpu.roll(t, ROPE_DIM // 2, 1)
                    t = t * cosv + jnp.where(first_half, up, down) * sinv
                cs = slice(col, col + HEAD_DIM)
                if d == 1:
                    o_ref[:, cs] = t.astype(BF16)
                else:
                    for r, part in enumerate(_deinterleave(t, stage, d)):
                        o_ref[0, r, :, cs] = part.astype(BF16)


def _proj_qkv(x2d, g, ws, b, cos_t, sin_t, batch):
    T, D = x2d.shape
    S = T // batch
    nseq = S // ROW_TILE
    gw = 3 * ATTN_WIDTH
    d1, d2 = ATTN_GROUPS[1][1], ATTN_GROUPS[2][1]
    full = lambda a: pl.BlockSpec(a.shape, lambda i: (0,) * a.ndim)
    dil_spec = lambda d: pl.BlockSpec((1, d, ROW_TILE // d, gw),
                                      lambda i: (i // nseq, 0, i % nseq, 0))
    return pl.pallas_call(
        _proj_qkv_kernel,
        out_shape=(jax.ShapeDtypeStruct((T, gw), BF16),
                   jax.ShapeDtypeStruct((batch, d1, S // d1, gw), BF16),
                   jax.ShapeDtypeStruct((batch, d2, S // d2, gw), BF16)),
        grid=(T // ROW_TILE,),
        in_specs=[
            pl.BlockSpec((ROW_TILE, D), lambda i: (i, 0)),
            full(g), full(ws[0]), full(ws[1]), full(ws[2]), full(b),
            pl.BlockSpec((ROW_TILE, HEAD_DIM), lambda i: (i % nseq, 0)),
            pl.BlockSpec((ROW_TILE, HEAD_DIM), lambda i: (i % nseq, 0)),
        ],
        out_specs=(pl.BlockSpec((ROW_TILE, gw), lambda i: (i, 0)), dil_spec(d1), dil_spec(d2)),
        scratch_shapes=[pltpu.VMEM((ROW_TILE // d2 * _pitch(d2), LANES), F32)],
        compiler_params=_cparams(("parallel",)),
        name="proj_qkv",
    )(x2d, g, ws[0], ws[1], ws[2], b, cos_t, sin_t)


def _rope_tables(seq):
    pos = jnp.arange(seq, dtype=F32)
    inv = ROPE_THETA ** (-jnp.arange(0, ROPE_DIM, 2, dtype=F32) / ROPE_DIM)
    ang = pos[:, None] * inv[None, :]
    cos, sin = jnp.cos(ang), jnp.sin(ang)
    pad = HEAD_DIM - ROPE_DIM
    cos_t = jnp.concatenate([cos, cos, jnp.ones((seq, pad), F32)], axis=1)
    sin_t = jnp.concatenate([-sin, sin, jnp.zeros((seq, pad), F32)], axis=1)
    return cos_t, sin_t


def _sigmoid(x):
    return 0.5 * jnp.tanh(0.5 * x) + 0.5


def _rglru_kernel(xr_ref, gr_ref, cw_ref, cb_ref, gw_ref, gb_ref, lam_ref, o_ref,
                  xpad, af, uf, ab, ub, cf, cbk, *, seq):
    nseg = seq // SCAN_SEG
    nchunk = seq // SCAN_CHUNK
    seg_per_chunk = SCAN_CHUNK // SCAN_SEG
    pitch = nseg + 4

    cw = cw_ref[...]
    cbias = cb_ref[...]
    gbias = gb_ref[...]
    z = -lam_ref[...]
    sp = jnp.maximum(z, 0.0) + jnp.log1p(jnp.exp(-jnp.abs(z)))
    neg_c_sp = (-LRU_C) * sp

    zero_rows = jnp.zeros((SUBLANES, LANES), F32)
    xpad[0:SUBLANES, :] = zero_rows
    xpad[seq + SUBLANES:seq + 2 * SUBLANES, :] = zero_rows

    def copy_in(c, carry):
        t0 = pl.multiple_of(c * SCAN_CHUNK, SCAN_CHUNK)
        xpad[pl.ds(t0 + SUBLANES, SCAN_CHUNK), :] = xr_ref[0, pl.ds(t0, SCAN_CHUNK), :]
        return carry

    lax.fori_loop(0, nchunk, copy_in, 0)

    def stage1(c, carry):
        t0 = c * SCAN_CHUNK
        xc = cbias
        for tap in range(CONV_WIDTH):
            lo = t0 + SUBLANES - CONV_WIDTH // 2 + tap
            xc = xc + cw[tap:tap + 1, :] * xpad[pl.ds(lo, SCAN_CHUNK), :]
        gates = jnp.dot(xc.astype(BF16), gw_ref[...], preferred_element_type=F32) + gbias
        for d, (a_s, u_s) in enumerate(((af, uf), (ab, ub))):
            r = _sigmoid(gates[:, (2 * d) * LANES:(2 * d + 1) * LANES])
            ig = _sigmoid(gates[:, (2 * d + 1) * LANES:(2 * d + 2) * LANES])
            log_a = r * neg_c_sp[d:d + 1, :]
            a = jnp.exp(log_a)
            v = jnp.tanh(-log_a) * (1.0 + a * a)
            u = xc * ig * (v * lax.rsqrt(jnp.maximum(v, F32_MIN_NORMAL)))
            for k in range(seg_per_chunk):
                seg = c * seg_per_chunk + k
                a_s[pl.ds(seg, SCAN_SEG, stride=pitch), :] = a[k * SCAN_SEG:(k + 1) * SCAN_SEG, :]
                u_s[pl.ds(seg, SCAN_SEG, stride=pitch), :] = u[k * SCAN_SEG:(k + 1) * SCAN_SEG, :]
        return carry

    lax.fori_loop(0, nchunk, stage1, 0)

    def scan_dir(a_s, u_s, reverse):
        def body(it, carry):
            i = (SCAN_SEG - 1 - it) if reverse else it
            h, cum = carry
            rows = pl.ds(i * pitch, nseg)
            a = a_s[rows, :]
            u = u_s[rows, :]
            h = a * h + u
            cum = a * cum
            u_s[rows, :] = h
            a_s[rows, :] = cum
            return h, cum

        return lax.fori_loop(0, SCAN_SEG, body,
                             (jnp.zeros((nseg, LANES), F32), jnp.ones((nseg, LANES), F32)))

    end_h, end_a = scan_dir(af, uf, False)
    beg_h, beg_a = scan_dir(ab, ub, True)

    c = jnp.zeros((1, LANES), F32)
    for s in range(nseg):
        cf[s:s + 1, :] = c
        c = end_h[s:s + 1, :] + end_a[s:s + 1, :] * c
    c = jnp.zeros((1, LANES), F32)
    for s in range(nseg - 1, -1, -1):
        cbk[s:s + 1, :] = c
        c = beg_h[s:s + 1, :] + beg_a[s:s + 1, :] * c

    def finish(s, carry):
        t0 = pl.multiple_of(s * SCAN_SEG, SCAN_SEG)
        rows = pl.ds(s, SCAN_SEG, stride=pitch)
        hf = uf[rows, :] + af[rows, :] * cf[pl.ds(s, 1), :]
        hb = ub[rows, :] + ab[rows, :] * cbk[pl.ds(s, 1), :]
        g = jax.nn.gelu(gr_ref[0, pl.ds(t0, SCAN_SEG), :])
        o_ref[0, pl.ds(t0, SCAN_SEG), :] = (g * (hf + hb)).astype(BF16)
        return carry

    lax.fori_loop(0, nseg, finish, 0)


def _rglru(xrgr, conv_w, conv_b, gate_w, gate_b, lam):
    B, S, _ = xrgr.shape
    nblk = D_RNN // RNN_BW
    nseg = S // SCAN_SEG
    scr = pltpu.VMEM((SCAN_SEG * (nseg + 4), LANES), F32)
    return pl.pallas_call(
        functools.partial(_rglru_kernel, seq=S),
        out_shape=jax.ShapeDtypeStruct((B, S, D_RNN), BF16),
        grid=(B, nblk),
        in_specs=[
            pl.BlockSpec((1, S, RNN_BW), lambda b, c: (b, 0, c)),
            pl.BlockSpec((1, S, RNN_BW), lambda b, c: (b, 0, nblk + c)),
            pl.BlockSpec((CONV_WIDTH, RNN_BW), lambda b, c: (0, c)),
            pl.BlockSpec((1, RNN_BW), lambda b, c: (0, c)),
            pl.BlockSpec((None, RNN_BW, 4 * RNN_BW), lambda b, c: (c, 0, 0)),
            pl.BlockSpec((None, 1, 4 * RNN_BW), lambda b, c: (c, 0, 0)),
            pl.BlockSpec((None, 2, RNN_BW), lambda b, c: (c, 0, 0)),
        ],
        out_specs=pl.BlockSpec((1, S, RNN_BW), lambda b, c: (b, 0, c)),
        scratch_shapes=[pltpu.VMEM((S + 2 * SUBLANES, LANES), F32), scr, scr, scr, scr,
                        pltpu.VMEM((nseg, LANES), F32), pltpu.VMEM((nseg, LANES), F32)],
        compiler_params=_cparams(("parallel", "parallel")),
        name="rglru",
    )(xrgr, xrgr, conv_w, conv_b, gate_w, gate_b, lam)


def _attn_kernel(q_ref, k_ref, kp_ref, kn_ref, v_ref, vp_ref, vn_ref, o_ref, lse_ref,
                 kext, vext, *, qblk, length):
    j = pl.program_id(2)
    kext[0:RADIUS, :] = kp_ref[0, 0]
    kext[RADIUS:RADIUS + qblk, :] = k_ref[0, 0]
    kext[RADIUS + qblk:RADIUS + qblk + RADIUS, :] = kn_ref[0, 0]
    vext[0:RADIUS, :] = vp_ref[0, 0]
    vext[RADIUS:RADIUS + qblk, :] = v_ref[0, 0]
    vext[RADIUS + qblk:RADIUS + qblk + RADIUS, :] = vn_ref[0, 0]

    nkey = ATT_SUB + 2 * RADIUS
    qi = lax.broadcasted_iota(jnp.int32, (ATT_SUB, nkey), 0)
    ki = lax.broadcasted_iota(jnp.int32, (ATT_SUB, nkey), 1)
    band = jnp.abs(ki - RADIUS - qi) <= RADIUS
    lane = lax.broadcasted_iota(jnp.int32, (ATT_SUB, LANES), 1)
    scale = HEAD_DIM ** -0.5

    def sub(i, carry):
        q0 = pl.multiple_of(i * ATT_SUB, ATT_SUB)
        kglob = j * qblk + q0 - RADIUS + ki
        valid = band & (kglob >= 0) & (kglob < length)
        lse_tile = jnp.zeros((ATT_SUB, LANES), F32)
        for h in range(HEADS):
            hs = slice(h * HEAD_DIM, (h + 1) * HEAD_DIM)
            q = q_ref[0, 0, pl.ds(q0, ATT_SUB), hs]
            ks = kext[pl.ds(q0, nkey), hs]
            vs = vext[pl.ds(q0, nkey), hs]
            s = lax.dot_general(q, ks, (((1,), (1,)), ((), ())),
                                preferred_element_type=F32) * scale
            s = jnp.where(valid, s, NEG_INF)
            m = jnp.max(s, axis=-1, keepdims=True)
            p = jnp.exp(s - m)
            l = jnp.sum(p, axis=-1, keepdims=True)
            o = jnp.dot(p.astype(BF16), vs, preferred_element_type=F32) / l
            o_ref[0, 0, pl.ds(q0, ATT_SUB), hs] = o
            lse_tile = jnp.where(lane == h, m + jnp.log(l), lse_tile)
        lse_ref[0, 0, pl.ds(q0, ATT_SUB), :] = lse_tile
        return carry

    lax.fori_loop(0, qblk // ATT_SUB, sub, 0)


def _attention_group(qkv_g):
    B, d, L, _ = qkv_g.shape
    qblk = min(L, ATT_QBLK)
    nblk = L // qblk
    halo_per_blk = qblk // RADIUS
    last_halo = L // RADIUS - 1

    main = lambda part: pl.BlockSpec((1, 1, qblk, ATTN_WIDTH), lambda b, r, j: (b, r, j, part))
    prev = lambda part: pl.BlockSpec(
        (1, 1, RADIUS, ATTN_WIDTH),
        lambda b, r, j: (b, r, jnp.maximum(j * halo_per_blk - 1, 0), part))
    nxt = lambda part: pl.BlockSpec(
        (1, 1, RADIUS, ATTN_WIDTH),
        lambda b, r, j: (b, r, jnp.minimum((j + 1) * halo_per_blk, last_halo), part))
    return pl.pallas_call(
        functools.partial(_attn_kernel, qblk=qblk, length=L),
        out_shape=(jax.ShapeDtypeStruct((B, d, L, ATTN_WIDTH), F32),
                   jax.ShapeDtypeStruct((B, d, L, LANES), F32)),
        grid=(B, d, nblk),
        in_specs=[main(0), main(1), prev(1), nxt(1), main(2), prev(2), nxt(2)],
        out_specs=(pl.BlockSpec((1, 1, qblk, ATTN_WIDTH), lambda b, r, j: (b, r, j, 0)),
                   pl.BlockSpec((1, 1, qblk, LANES), lambda b, r, j: (b, r, j, 0))),
        scratch_shapes=[pltpu.VMEM((qblk + 2 * RADIUS, ATTN_WIDTH), BF16),
                        pltpu.VMEM((qblk + 2 * RADIUS, ATTN_WIDTH), BF16)],
        compiler_params=_cparams(("parallel", "parallel", "parallel")),
        name=f"attn_d{d}",
    )(qkv_g, qkv_g, qkv_g, qkv_g, qkv_g, qkv_g, qkv_g)


def _merge_kernel(x_ref, yr_ref, o0_ref, o1_ref, o2_ref, l0_ref, l1_ref, l2_ref,
                  gmix_ref, wg_ref, bg_ref, prnn_ref, pattn_ref, wout_ref,
                  gffn_ref, wr_ref, wrt_ref, br_ref, brt_ref,
                  x2_ref, h2_ref, aff_ref, afft_ref, yattn, stage):
    d = x_ref.shape[1]
    x = x_ref[...]
    hb = _rms(x, gmix_ref[...]).astype(BF16)

    def natural(ref, g, cs):
        dil = ATTN_GROUPS[g][1]
        if dil == 1:
            return ref[:, cs]
        return _interleave([ref[0, r, :, cs] for r in range(dil)], stage, dil)

    o_refs = (o0_ref, o1_ref, o2_ref)
    all_lanes = slice(0, LANES)
    lses = [natural(r, g, all_lanes) for g, r in enumerate((l0_ref, l1_ref, l2_ref))]
    for h in range(HEADS):
        hs = slice(h * HEAD_DIM, (h + 1) * HEAD_DIM)
        cols = [l[:, h:h + 1] for l in lses]
        m = jnp.maximum(jnp.maximum(cols[0], cols[1]), cols[2])
        ws = [jnp.exp(c - m) for c in cols]
        den = ws[0] + ws[1] + ws[2]
        acc = (ws[0] / den) * natural(o_refs[0], 0, hs)
        acc = acc + (ws[1] / den) * natural(o_refs[1], 1, hs)
        acc = acc + (ws[2] / den) * natural(o_refs[2], 2, hs)
        yattn[:, hs] = acc.astype(BF16)

    branch_a = jnp.dot(yr_ref[...], prnn_ref[...], preferred_element_type=F32)
    branch_b = jnp.dot(yattn[...], pattn_ref[...], preferred_element_type=F32)
    ga = jnp.dot(hb, wg_ref[:, 0:d], preferred_element_type=F32) + bg_ref[:, 0:d]
    gb = jnp.dot(hb, wg_ref[:, d:2 * d], preferred_element_type=F32) + bg_ref[:, d:2 * d]
    merged = _sigmoid(ga) * branch_a + _sigmoid(gb) * branch_b
    x2 = x + jnp.dot(merged.astype(BF16), wout_ref[...], preferred_element_type=F32)
    x2_ref[...] = x2

    h2 = _rms(x2, gffn_ref[...]).astype(BF16)
    h2_ref[...] = h2
    logits = jnp.dot(h2, wr_ref[...], preferred_element_type=F32) + br_ref[...]
    mx = jnp.max(logits, axis=-1, keepdims=True)
    e = jnp.exp(logits - mx)
    aff_ref[...] = e / jnp.sum(e, axis=-1, keepdims=True)
    logits_t = lax.dot_general(wrt_ref[...], h2, (((1,), (1,)), ((), ())),
                               preferred_element_type=F32) + brt_ref[...]
    mxt = jnp.max(logits_t, axis=0, keepdims=True)
    et = jnp.exp(logits_t - mxt)
    afft_ref[0] = et / jnp.sum(et, axis=0, keepdims=True)


def _merge(x2d, y_rnn, outs, lses, gmix, wg, bg, p_rnn, p_attn, w_out, gffn, wr, br, batch):
    T, D = x2d.shape
    S = T // batch
    nseq = S // ROW_TILE
    E = wr.shape[1]
    row = lambda w: pl.BlockSpec((ROW_TILE, w), lambda i: (i, 0))
    full = lambda a: pl.BlockSpec(a.shape, lambda i: (0,) * a.ndim)

    def attn_spec(g, w):
        dil = ATTN_GROUPS[g][1]
        if dil == 1:
            return row(w)
        return pl.BlockSpec((1, dil, ROW_TILE // dil, w), lambda i: (i // nseq, 0, i % nseq, 0))

    wrt = wr.T
    brt = br.reshape(E, 1)
    consts = [gmix, wg, bg, p_rnn, p_attn, w_out, gffn, wr, wrt, br, brt]
    args = [x2d, y_rnn, *outs, *lses, *consts]
    in_specs = ([row(D), row(D_RNN)] + [attn_spec(g, ATTN_WIDTH) for g in range(N_GROUPS)]
                + [attn_spec(g, LANES) for g in range(N_GROUPS)] + [full(a) for a in consts])
    dmax = max(dil for _, dil in ATTN_GROUPS)
    return pl.pallas_call(
        _merge_kernel,
        out_shape=(jax.ShapeDtypeStruct((T, D), F32),
                   jax.ShapeDtypeStruct((T, D), BF16),
                   jax.ShapeDtypeStruct((T, E), F32),
                   jax.ShapeDtypeStruct((batch, E, S), F32)),
        grid=(T // ROW_TILE,),
        in_specs=in_specs,
        out_specs=(row(D), row(D), row(E),
                   pl.BlockSpec((1, E, ROW_TILE), lambda i: (i // nseq, 0, i % nseq))),
        scratch_shapes=[pltpu.VMEM((ROW_TILE, ATTN_WIDTH), BF16),
                        pltpu.VMEM((ROW_TILE // dmax * _pitch(dmax), LANES), F32)],
        compiler_params=_cparams(("parallel",)),
        name="merge",
    )(*args)


def _route_kernel(aff_ref, pos_ref, off_ref, *, cap):
    a = aff_ref[0]
    E, R, _ = a.shape
    bits = pltpu.bitcast(a, jnp.int32)

    def count(mask):
        c = jnp.sum(mask.astype(F32), axis=2, keepdims=True)
        return jnp.sum(c, axis=1, keepdims=True)

    def bs(_, carry):
        lo, hi = carry
        mid = lo + ((hi - lo + 1) >> 1)
        ok = count(bits >= mid) >= cap
        return jnp.where(ok, mid, lo), jnp.where(ok, hi, mid - 1)

    lo0 = jnp.zeros((E, 1, 1), jnp.int32)
    hi0 = jnp.full((E, 1, 1), 0x7F7FFFFF, jnp.int32)
    thr, _ = lax.fori_loop(0, 31, bs, (lo0, hi0))

    gt = bits > thr
    eq = bits == thr
    need = cap - count(gt)

    ki = lax.broadcasted_iota(jnp.int32, (LANES, LANES), 0)
    li = lax.broadcasted_iota(jnp.int32, (LANES, LANES), 1)
    upper = (ki <= li).astype(BF16)
    allones = jnp.ones((LANES, LANES), BF16)
    ri = lax.broadcasted_iota(jnp.int32, (R, R), 0)
    ci = lax.broadcasted_iota(jnp.int32, (R, R), 1)
    strict = (ci < ri).astype(BF16)

    def prefix(mask):
        mb = mask.astype(BF16).reshape(E * R, LANES)
        within = jnp.dot(mb, upper, preferred_element_type=F32)
        totals = jnp.dot(mb, allones, preferred_element_type=F32).astype(BF16)
        offs = []
        for e in range(E):
            offs.append(jnp.dot(strict, totals[e * R:(e + 1) * R, :],
                                preferred_element_type=F32))
        rowoff = jnp.concatenate(offs, axis=0)
        excl = within - mb.astype(F32) + rowoff
        return excl.reshape(E, R, LANES), rowoff.reshape(E, R, LANES)

    eq_rank, _ = prefix(eq)
    sel = gt | (eq & (eq_rank < need.astype(F32)))
    slot, rowoff = prefix(sel)
    pos_ref[0] = jnp.where(sel, slot.astype(jnp.int32), -1)
    off_ref[0] = rowoff.astype(jnp.int32)


def _route(aff_t, cap):
    B, E, S = aff_t.shape
    R = S // LANES
    a4 = aff_t.reshape(B, E, R, LANES)
    blk = pl.BlockSpec((1, E, R, LANES), lambda b: (b, 0, 0, 0))
    pos, off = pl.pallas_call(
        functools.partial(_route_kernel, cap=cap),
        out_shape=(jax.ShapeDtypeStruct((B, E, R, LANES), jnp.int32),
                   jax.ShapeDtypeStruct((B, E, R, LANES), jnp.int32)),
        grid=(B,),
        in_specs=[blk],
        out_specs=(blk, blk),
        compiler_params=_cparams(("parallel",)),
        name="route",
    )(a4)
    return pos, off


def _slot_windows(off_ref, base, n_exp, ntb, tb):
    starts = []
    rounds = jnp.int32(0)
    for e in range(n_exp):
        idx = (base + e) * (ntb + 1) + tb
        start = (off_ref[idx] >> 4) << 4
        starts.append(start)
        rounds = jnp.maximum(rounds, (off_ref[idx + 1] - start + SLOT_WIN - 1) >> SLOT_SHIFT)
    return starts, rounds


def _gather_kernel(off_ref, pos_ref, h_ref, o_ref, *, cap, ntb, n_exp):
    b = pl.program_id(0)
    grp = pl.program_id(1)
    tb = pl.program_id(2)

    @pl.when(tb == 0)
    def _():
        o_ref[...] = jnp.zeros_like(o_ref)

    base = (b * pl.num_programs(1) + grp) * n_exp
    starts, rounds = _slot_windows(off_ref, base, n_exp, ntb, tb)
    ji = lax.broadcasted_iota(jnp.int32, (SLOT_WIN, TOK_BLK), 0)
    hblk = h_ref[0]

    def one_round(k, carry):
        pieces, rows = [], []
        for e in range(n_exp):
            first = starts[e] + k * SLOT_WIN
            row0 = pl.multiple_of(jnp.minimum(first, cap - SLOT_WIN), BF16_ROWS)
            slot = ji + row0
            hit = (slot == pos_ref[0, 0, e:e + 1, :]) & (slot >= first)
            pieces.append(jnp.where(hit, 1.0, 0.0).astype(BF16))
            rows.append(row0)
        onehot = jnp.concatenate(pieces, axis=0)
        g = jnp.dot(onehot, hblk, preferred_element_type=F32)
        for e in range(n_exp):
            o_ref[0, e, pl.ds(rows[e], SLOT_WIN), :] += (
                g[e * SLOT_WIN:(e + 1) * SLOT_WIN, :].astype(BF16))
        return carry

    lax.fori_loop(0, rounds, one_round, 0)


def _gather(offs, pos, h2, cap):
    B, E, S = pos.shape
    D = h2.shape[-1]
    ntb = S // TOK_BLK
    n_exp = E // EXPERT_SPLIT
    pos4 = pos.reshape(B, EXPERT_SPLIT, n_exp, S)
    return pl.pallas_call(
        functools.partial(_gather_kernel, cap=cap, ntb=ntb, n_exp=n_exp),
        out_shape=jax.ShapeDtypeStruct((B, E, cap, D), BF16),
        grid_spec=pltpu.PrefetchScalarGridSpec(
            num_scalar_prefetch=1,
            grid=(B, EXPERT_SPLIT, ntb),
            in_specs=[
                pl.BlockSpec((1, 1, n_exp, TOK_BLK), lambda b, g, t, off: (b, g, 0, t)),
                pl.BlockSpec((1, TOK_BLK, D), lambda b, g, t, off: (b, t, 0)),
            ],
            out_specs=pl.BlockSpec((1, n_exp, cap, D), lambda b, g, t, off: (b, g, 0, 0)),
        ),
        compiler_params=_cparams(("parallel", "parallel", "arbitrary")),
        name="gather",
    )(offs, pos4, h2)


def _ffn_kernel(x_ref, wg_ref, wu_ref, wd_ref, o_ref, acc, *, f_chunk):
    x = x_ref[0, 0]
    nf = wg_ref.shape[2] // f_chunk
    for f in range(nf):
        fs = slice(f * f_chunk, (f + 1) * f_chunk)
        g = jnp.dot(x, wg_ref[0, :, fs], preferred_element_type=F32)
        u = jnp.dot(x, wu_ref[0, :, fs], preferred_element_type=F32)
        hid = (g * _sigmoid(g) * u).astype(BF16)
        part = jnp.dot(hid, wd_ref[0, fs, :], preferred_element_type=F32)
        if f == 0:
            acc[...] = part
        else:
            acc[...] += part
    o_ref[0, 0] = acc[...].astype(BF16)


def _ffn(xg, w_gate, w_up, w_down):
    B, E, C, D = xg.shape
    F = w_gate.shape[2]
    return pl.pallas_call(
        functools.partial(_ffn_kernel, f_chunk=512),
        out_shape=jax.ShapeDtypeStruct((B, E, C, D), BF16),
        grid=(E, B),
        in_specs=[
            pl.BlockSpec((1, 1, C, D), lambda e, b: (b, e, 0, 0)),
            pl.BlockSpec((1, D, F), lambda e, b: (e, 0, 0)),
            pl.BlockSpec((1, D, F), lambda e, b: (e, 0, 0)),
            pl.BlockSpec((1, F, D), lambda e, b: (e, 0, 0)),
        ],
        out_specs=pl.BlockSpec((1, 1, C, D), lambda e, b: (b, e, 0, 0)),
        scratch_shapes=[pltpu.VMEM((C, D), F32)],
        compiler_params=_cparams(("arbitrary", "arbitrary")),
        name="ffn",
    )(xg, w_gate, w_up, w_down)


def _combine_kernel(off_ref, x_ref, pos_ref, aff_ref, eo_ref, g_ref, o_ref, wstack, acc,
                    *, cap, ntb):
    b = pl.program_id(0)
    tb = pl.program_id(1)
    E = pos_ref.shape[2]
    pos = pos_ref[0]
    aff = aff_ref[0]
    starts, rounds = _slot_windows(off_ref, b * E, E, ntb, tb)
    lane = lax.broadcasted_iota(jnp.int32, (TOK_BLK, LANES), 1)
    low = lane < SLOT_WIN
    acc[...] = x_ref[0]

    def one_round(k, carry):
        q_hi, q_lo = [], []
        for p in range(E // 2):
            firsts, row0s = [], []
            for e in (2 * p, 2 * p + 1):
                first = starts[e] + k * SLOT_WIN
                row0 = pl.multiple_of(jnp.minimum(first, cap - SLOT_WIN), BF16_ROWS)
                wstack[e * SLOT_WIN:(e + 1) * SLOT_WIN, :] = eo_ref[0, e, pl.ds(row0, SLOT_WIN), :]
                firsts.append(first)
                row0s.append(row0)
            slot = lane + jnp.where(low, row0s[0], row0s[1] - SLOT_WIN)
            first = jnp.where(low, firsts[0], firsts[1])
            pcol = jnp.where(low, pos[:, 2 * p:2 * p + 1], pos[:, 2 * p + 1:2 * p + 2])
            gate = jnp.where(low, aff[:, 2 * p:2 * p + 1], aff[:, 2 * p + 1:2 * p + 2])
            gate = jnp.where((slot == pcol) & (slot >= first), gate, 0.0)
            hi = gate.astype(BF16)
            q_hi.append(hi)
            q_lo.append((gate - hi.astype(F32)).astype(BF16))
        w = wstack[...]
        y = jnp.dot(jnp.concatenate(q_hi, axis=1), w, preferred_element_type=F32)
        y = y + jnp.dot(jnp.concatenate(q_lo, axis=1), w, preferred_element_type=F32)
        acc[...] += y
        return carry

    lax.fori_loop(0, rounds, one_round, 0)
    o_ref[0] = _rms(acc[...], g_ref[...])


def _combine(offs, x2, pos_t, aff, eo, g_final, cap):
    B, S, D = x2.shape
    E = pos_t.shape[2]
    ntb = S // TOK_BLK
    return pl.pallas_call(
        functools.partial(_combine_kernel, cap=cap, ntb=ntb),
        out_shape=jax.ShapeDtypeStruct((B, S, D), F32),
        grid_spec=pltpu.PrefetchScalarGridSpec(
            num_scalar_prefetch=1,
            grid=(B, ntb),
            in_specs=[
                pl.BlockSpec((1, TOK_BLK, D), lambda b, t, off: (b, t, 0)),
                pl.BlockSpec((1, TOK_BLK, E), lambda b, t, off: (b, t, 0)),
                pl.BlockSpec((1, TOK_BLK, E), lambda b, t, off: (b, t, 0)),
                pl.BlockSpec((1, E, cap, D), lambda b, t, off: (b, 0, 0, 0),
                             pipeline_mode=pl.Buffered(1)),
                pl.BlockSpec((1, D), lambda b, t, off: (0, 0)),
            ],
            out_specs=pl.BlockSpec((1, TOK_BLK, D), lambda b, t, off: (b, t, 0)),
            scratch_shapes=[pltpu.VMEM((E * SLOT_WIN, D), BF16), pltpu.VMEM((TOK_BLK, D), F32)],
        ),
        compiler_params=_cparams(("parallel", "arbitrary")),
        name="combine",
    )(offs, x2, pos_t, aff, eo, g_final)


def _layer(x, norm_mix, w_in, b_in, conv_w, conv_b, rg_w, rg_b, rg_lambda, p_rnn, p_attn,
           w_out, norm_ffn, w_router, b_router, w_gate, w_up, w_down, norm_out, cos_t, sin_t):
    B, S, D = x.shape
    T = B * S
    x2d = x.reshape(T, D)
    n_rnn = 2 * D_RNN
    n_qkv = 3 * N_GROUPS * ATTN_WIDTH
    gmix = norm_mix.reshape(1, D)
    w_in_b = w_in.astype(BF16)
    b_in2 = b_in.reshape(1, -1)

    xrgr = _proj_f32(x2d, gmix, w_in_b[:, :n_rnn], b_in2[:, :n_rnn])

    def group_cols(a, g):
        part = lambda p: a[:, n_rnn + (p * N_GROUPS + g) * ATTN_WIDTH:
                           n_rnn + (p * N_GROUPS + g + 1) * ATTN_WIDTH]
        return jnp.concatenate([part(0), part(1), part(2)], axis=1)

    ws = [group_cols(w_in_b, g) for g in range(N_GROUPS)]
    b_qkv = jnp.concatenate([group_cols(b_in2, g) for g in range(N_GROUPS)], axis=1)
    qkv = _proj_qkv(x2d, gmix, ws, b_qkv, cos_t, sin_t, B)

    nblk = D_RNN // RNN_BW
    gate_w = jnp.transpose(rg_w, (2, 3, 0, 1, 4)).reshape(nblk, RNN_BW, 4 * RNN_BW).astype(BF16)
    gate_b = jnp.transpose(rg_b.reshape(2, 2, nblk, RNN_BW), (2, 0, 1, 3)).reshape(nblk, 1, 4 * RNN_BW)
    lam = jnp.transpose(rg_lambda.reshape(2, nblk, RNN_BW), (1, 0, 2))
    y_rnn = _rglru(xrgr.reshape(B, S, n_rnn), conv_w, conv_b.reshape(1, D_RNN), gate_w, gate_b, lam)

    outs, lses = [], []
    for g, (_, dilation) in enumerate(ATTN_GROUPS):
        qkv_g = qkv[g].reshape(B, 1, S, 3 * ATTN_WIDTH) if dilation == 1 else qkv[g]
        o, lse = _attention_group(qkv_g)
        if dilation == 1:
            o, lse = o.reshape(T, ATTN_WIDTH), lse.reshape(T, LANES)
        outs.append(o)
        lses.append(lse)

    x2, h2, aff, aff_t = _merge(
        x2d, y_rnn.reshape(T, D_RNN), outs, lses, gmix,
        w_in_b[:, n_rnn + n_qkv:], b_in2[:, n_rnn + n_qkv:],
        p_rnn.astype(BF16), p_attn.astype(BF16), w_out.astype(BF16),
        norm_ffn.reshape(1, D), w_router.astype(BF16), b_router.reshape(1, -1), B)

    E = w_router.shape[1]
    cap = CAPACITY_FACTOR * S // E
    pos4, off4 = _route(aff_t, cap)
    pos = pos4.reshape(B, E, S)
    blk_off = off4[:, :, ::TOK_BLK // LANES, 0]
    offs = jnp.concatenate([blk_off, jnp.full((B, E, 1), cap, jnp.int32)], axis=2).reshape(-1)
    xg = _gather(offs, pos, h2.reshape(B, S, D), cap)
    eo = _ffn(xg, w_gate.astype(BF16), w_up.astype(BF16), w_down.astype(BF16))
    pos_t = jnp.transpose(pos, (0, 2, 1))
    return _combine(offs, x2.reshape(B, S, D), pos_t, aff.reshape(B, S, E), eo,
                    norm_out.reshape(1, D), cap)


def kernel(x, norm_mix, w_in, b_in, conv_w, conv_b, rg_w, rg_b, rg_lambda, p_rnn, p_attn, w_out,
           norm_ffn, w_router, b_router, w_gate, w_up, w_down, norm_final):
    depth = w_in.shape[0]
    assert depth == 1, "final rmsnorm is fused into the single layer's combine step"
    cos_t, sin_t = _rope_tables(x.shape[1])
    return _layer(x, norm_mix[0], w_in[0], b_in[0], conv_w[0], conv_b[0], rg_w[0], rg_b[0],
                  rg_lambda[0], p_rnn[0], p_attn[0], w_out[0], norm_ffn[0], w_router[0],
                  b_router[0], w_gate[0], w_up[0], w_down[0], norm_final, cos_t, sin_t)
```

```python
import functools

import jax
import jax.numpy as jnp
from jax import lax
from jax.experimental import pallas as pl
from jax.experimental.pallas import tpu as pltpu

F32 = jnp.float32
BF16 = jnp.bfloat16

D_RNN = 1024
RNN_BW = 128
CONV_WIDTH = 4
LRU_C = 8.0
HEAD_DIM = 128
HEADS = 4
ATTN_WIDTH = HEADS * HEAD_DIM
ATTN_GROUPS = ((128, 1), (512, 4), (2048, 16))
N_GROUPS = 3
ROPE_THETA = 500000.0
ROPE_DIM = HEAD_DIM // 4
RADIUS = 64
NEG_INF = -1e30
N_EXPERTS = 16
CAPACITY_FACTOR = 2
RMS_EPS = 1e-6

LANES = 128
SUBLANES = 8
BF16_ROWS = 16
MXU_DIM = 256
VMEM_LIMIT = 56 * 1024 * 1024
F32_MIN_NORMAL = 1.1754944e-38

ROW_TILE = 512
SCAN_SEG = 128
SCAN_CHUNK = 512
ATT_QBLK = 1024
ATT_SUB = 128
TOK_BLK = 256
SLOT_SHIFT = 6
SLOT_WIN = 1 << SLOT_SHIFT
EXPERT_SPLIT = 2


def _cparams(sem):
    return pltpu.CompilerParams(dimension_semantics=sem, vmem_limit_bytes=VMEM_LIMIT)


def _rms(x, g):
    ms = jnp.mean(x * x, axis=-1, keepdims=True)
    return x * lax.rsqrt(ms + RMS_EPS) * g


def _pitch(d):
    return d if d <= 4 else d + 4


def _deinterleave(t, stage, d):
    n = t.shape[0]
    p = _pitch(d)
    if p == d:
        stage[0:n, :] = t
    else:
        for j in range(n // d):
            stage[j * p:j * p + d, :] = t[j * d:(j + 1) * d, :]
    return [stage[pl.ds(r, n // d, stride=p), :] for r in range(d)]


def _interleave(parts, stage, d):
    m = parts[0].shape[0]
    p = _pitch(d)
    for r in range(d):
        stage[pl.ds(r, m, stride=p), :] = parts[r]
    if p == d:
        return stage[0:m * d, :]
    return jnp.concatenate([stage[j * p:j * p + d, :] for j in range(m)], axis=0)


def _proj_f32_kernel(x_ref, g_ref, w_ref, b_ref, o_ref, *, n_chunk):
    hb = _rms(x_ref[...], g_ref[...]).astype(BF16)
    for c in range(o_ref.shape[1] // n_chunk):
        sl = slice(c * n_chunk, (c + 1) * n_chunk)
        o_ref[:, sl] = jnp.dot(hb, w_ref[:, sl], preferred_element_type=F32) + b_ref[:, sl]


def _proj_f32(x2d, g, w, b):
    T, D = x2d.shape
    N = w.shape[1]
    return pl.pallas_call(
        functools.partial(_proj_f32_kernel, n_chunk=512),
        out_shape=jax.ShapeDtypeStruct((T, N), F32),
        grid=(T // ROW_TILE,),
        in_specs=[
            pl.BlockSpec((ROW_TILE, D), lambda i: (i, 0)),
            pl.BlockSpec((1, D), lambda i: (0, 0)),
            pl.BlockSpec((D, N), lambda i: (0, 0)),
            pl.BlockSpec((1, N), lambda i: (0, 0)),
        ],
        out_specs=pl.BlockSpec((ROW_TILE, N), lambda i: (i, 0)),
        compiler_params=_cparams(("parallel",)),
        name="proj_rnn",
    )(x2d, g, w, b)


def _proj_qkv_kernel(x_ref, g_ref, w0_ref, w1_ref, w2_ref, b_ref, cos_ref, sin_ref,
                     o0_ref, o1_ref, o2_ref, stage):
    hb = _rms(x_ref[...], g_ref[...]).astype(BF16)
    cosv = cos_ref[...]
    sinv = sin_ref[...]
    lane = lax.broadcasted_iota(jnp.int32, (1, HEAD_DIM), 1)
    first_half = lane < (ROPE_DIM // 2)
    gw = 3 * ATTN_WIDTH
    for g, (w_ref, o_ref) in enumerate(((w0_ref, o0_ref), (w1_ref, o1_ref), (w2_ref, o2_ref))):
        d = ATTN_GROUPS[g][1]
        for c in range(gw // MXU_DIM):
            sl = slice(c * MXU_DIM, (c + 1) * MXU_DIM)
            t2 = jnp.dot(hb, w_ref[:, sl], preferred_element_type=F32)
            t2 = t2 + b_ref[:, g * gw + c * MXU_DIM:g * gw + (c + 1) * MXU_DIM]
            for hh in range(MXU_DIM // HEAD_DIM):
                t = t2[:, hh * HEAD_DIM:(hh + 1) * HEAD_DIM]
                col = c * MXU_DIM + hh * HEAD_DIM
                if col < 2 * ATTN_WIDTH:
                    up = pltpu.roll(t, HEAD_DIM - ROPE_DIM // 2, 1)
                    down = pltpu.roll(t, ROPE_DIM // 2, 1)
                    t = t * cosv + jnp.where(first_half, up, down) * sinv
                cs = slice(col, col + HEAD_DIM)
                if d == 1:
                    o_ref[:, cs] = t.astype(BF16)
                else:
                    for r, part in enumerate(_deinterleave(t, stage, d)):
                        o_ref[0, r, :, cs] = part.astype(BF16)


def _proj_qkv(x2d, g, ws, b, cos_t, sin_t, batch):
    T, D = x2d.shape
    S = T // batch
    nseq = S // ROW_TILE
    gw = 3 * ATTN_WIDTH
    d1, d2 = ATTN_GROUPS[1][1], ATTN_GROUPS[2][1]
    full = lambda a: pl.BlockSpec(a.shape, lambda i: (0,) * a.ndim)
    dil_spec = lambda d: pl.BlockSpec((1, d, ROW_TILE // d, gw),
                                      lambda i: (i // nseq, 0, i % nseq, 0))
    return pl.pallas_call(
        _proj_qkv_kernel,
        out_shape=(jax.ShapeDtypeStruct((T, gw), BF16),
                   jax.ShapeDtypeStruct((batch, d1, S // d1, gw), BF16),
                   jax.ShapeDtypeStruct((batch, d2, S // d2, gw), BF16)),
        grid=(T // ROW_TILE,),
        in_specs=[
            pl.BlockSpec((ROW_TILE, D), lambda i: (i, 0)),
            full(g), full(ws[0]), full(ws[1]), full(ws[2]), full(b),
            pl.BlockSpec((ROW_TILE, HEAD_DIM), lambda i: (i % nseq, 0)),
            pl.BlockSpec((ROW_TILE, HEAD_DIM), lambda i: (i % nseq, 0)),
        ],
        out_specs=(pl.BlockSpec((ROW_TILE, gw), lambda i: (i, 0)), dil_spec(d1), dil_spec(d2)),
        scratch_shapes=[pltpu.VMEM((ROW_TILE // d2 * _pitch(d2), LANES), F32)],
        compiler_params=_cparams(("parallel",)),
        name="proj_qkv",
    )(x2d, g, ws[0], ws[1], ws[2], b, cos_t, sin_t)


def _rope_tables(seq):
    pos = jnp.arange(seq, dtype=F32)
    inv = ROPE_THETA ** (-jnp.arange(0, ROPE_DIM, 2, dtype=F32) / ROPE_DIM)
    ang = pos[:, None] * inv[None, :]
    cos, sin = jnp.cos(ang), jnp.sin(ang)
    pad = HEAD_DIM - ROPE_DIM
    cos_t = jnp.concatenate([cos, cos, jnp.ones((seq, pad), F32)], axis=1)
    sin_t = jnp.concatenate([-sin, sin, jnp.zeros((seq, pad), F32)], axis=1)
    return cos_t, sin_t


def _sigmoid(x):
    return 0.5 * jnp.tanh(0.5 * x) + 0.5


def _rglru_kernel(xr_ref, gr_ref, cw_ref, cb_ref, gw_ref, gb_ref, lam_ref, o_ref,
                  xpad, af, uf, ab, ub, cf, cbk, *, seq):
    nseg = seq // SCAN_SEG
    nchunk = seq // SCAN_CHUNK
    seg_per_chunk = SCAN_CHUNK // SCAN_SEG
    pitch = nseg + 4

    cw = cw_ref[...]
    cbias = cb_ref[...]
    gbias = gb_ref[...]
    z = -lam_ref[...]
    sp = jnp.maximum(z, 0.0) + jnp.log1p(jnp.exp(-jnp.abs(z)))
    half_neg_c_sp = (-0.5 * LRU_C) * sp

    zero_rows = jnp.zeros((SUBLANES, LANES), F32)
    xpad[0:SUBLANES, :] = zero_rows
    xpad[seq + SUBLANES:seq + 2 * SUBLANES, :] = zero_rows

    def copy_in(c, carry):
        t0 = pl.multiple_of(c * SCAN_CHUNK, SCAN_CHUNK)
        xpad[pl.ds(t0 + SUBLANES, SCAN_CHUNK), :] = xr_ref[0, pl.ds(t0, SCAN_CHUNK), :]
        return carry

    lax.fori_loop(0, nchunk, copy_in, 0)

    def stage1(c, carry):
        t0 = c * SCAN_CHUNK
        xc = cbias
        for tap in range(CONV_WIDTH):
            lo = t0 + SUBLANES - CONV_WIDTH // 2 + tap
            xc = xc + cw[tap:tap + 1, :] * xpad[pl.ds(lo, SCAN_CHUNK), :]
        th = jnp.tanh(jnp.dot(xc.astype(BF16), gw_ref[...], preferred_element_type=F32) + gbias)
        xc_half = 0.5 * xc
        for d, (a_s, u_s) in enumerate(((af, uf), (ab, ub))):
            t_r = th[:, (2 * d) * LANES:(2 * d + 1) * LANES]
            t_i = th[:, (2 * d + 1) * LANES:(2 * d + 2) * LANES]
            half_k = half_neg_c_sp[d:d + 1, :]
            log_a = t_r * half_k + half_k
            a = jnp.exp(log_a)
            v = jnp.tanh(log_a) * (-1.0 - a * a)
            u = (xc_half * (v * lax.rsqrt(jnp.maximum(v, F32_MIN_NORMAL)))) * (t_i + 1.0)
            for k in range(seg_per_chunk):
                seg = c * seg_per_chunk + k
                a_s[pl.ds(seg, SCAN_SEG, stride=pitch), :] = a[k * SCAN_SEG:(k + 1) * SCAN_SEG, :]
                u_s[pl.ds(seg, SCAN_SEG, stride=pitch), :] = u[k * SCAN_SEG:(k + 1) * SCAN_SEG, :]
        return carry

    lax.fori_loop(0, nchunk, stage1, 0)

    def scan_dir(a_s, u_s, reverse):
        def body(it, carry):
            i = (SCAN_SEG - 1 - it) if reverse else it
            h, cum = carry
            rows = pl.ds(i * pitch, nseg)
            a = a_s[rows, :]
            u = u_s[rows, :]
            h = a * h + u
            cum = a * cum
            u_s[rows, :] = h
            a_s[rows, :] = cum
            return h, cum

        return lax.fori_loop(0, SCAN_SEG, body,
                             (jnp.zeros((nseg, LANES), F32), jnp.ones((nseg, LANES), F32)))

    end_h, end_a = scan_dir(af, uf, False)
    beg_h, beg_a = scan_dir(ab, ub, True)

    c = jnp.zeros((1, LANES), F32)
    for s in range(nseg):
        cf[s:s + 1, :] = c
        c = end_h[s:s + 1, :] + end_a[s:s + 1, :] * c
    c = jnp.zeros((1, LANES), F32)
    for s in range(nseg - 1, -1, -1):
        cbk[s:s + 1, :] = c
        c = beg_h[s:s + 1, :] + beg_a[s:s + 1, :] * c

    def finish(s, carry):
        t0 = pl.multiple_of(s * SCAN_SEG, SCAN_SEG)
        rows = pl.ds(s, SCAN_SEG, stride=pitch)
        hf = uf[rows, :] + af[rows, :] * cf[pl.ds(s, 1), :]
        hb = ub[rows, :] + ab[rows, :] * cbk[pl.ds(s, 1), :]
        g = jax.nn.gelu(gr_ref[0, pl.ds(t0, SCAN_SEG), :])
        o_ref[0, pl.ds(t0, SCAN_SEG), :] = (g * (hf + hb)).astype(BF16)
        return carry

    lax.fori_loop(0, nseg, finish, 0)


def _rglru(xrgr, conv_w, conv_b, gate_w, gate_b, lam):
    B, S, _ = xrgr.shape
    nblk = D_RNN // RNN_BW
    nseg = S // SCAN_SEG
    scr = pltpu.VMEM((SCAN_SEG * (nseg + 4), LANES), F32)
    return pl.pallas_call(
        functools.partial(_rglru_kernel, seq=S),
        out_shape=jax.ShapeDtypeStruct((B, S, D_RNN), BF16),
        grid=(B, nblk),
        in_specs=[
            pl.BlockSpec((1, S, RNN_BW), lambda b, c: (b, 0, c)),
            pl.BlockSpec((1, S, RNN_BW), lambda b, c: (b, 0, nblk + c)),
            pl.BlockSpec((CONV_WIDTH, RNN_BW), lambda b, c: (0, c)),
            pl.BlockSpec((1, RNN_BW), lambda b, c: (0, c)),
            pl.BlockSpec((None, RNN_BW, 4 * RNN_BW), lambda b, c: (c, 0, 0)),
            pl.BlockSpec((None, 1, 4 * RNN_BW), lambda b, c: (c, 0, 0)),
            pl.BlockSpec((None, 2, RNN_BW), lambda b, c: (c, 0, 0)),
        ],
        out_specs=pl.BlockSpec((1, S, RNN_BW), lambda b, c: (b, 0, c)),
        scratch_shapes=[pltpu.VMEM((S + 2 * SUBLANES, LANES), F32), scr, scr, scr, scr,
                        pltpu.VMEM((nseg, LANES), F32), pltpu.VMEM((nseg, LANES), F32)],
        compiler_params=_cparams(("parallel", "parallel")),
        name="rglru",
    )(xrgr, xrgr, conv_w, conv_b, gate_w, gate_b, lam)


def _attn_kernel(q_ref, k_ref, kp_ref, kn_ref, v_ref, vp_ref, vn_ref, o_ref, lse_ref,
                 kext, vext, *, qblk, length):
    j = pl.program_id(2)
    kext[0:RADIUS, :] = kp_ref[0, 0]
    kext[RADIUS:RADIUS + qblk, :] = k_ref[0, 0]
    kext[RADIUS + qblk:RADIUS + qblk + RADIUS, :] = kn_ref[0, 0]
    for h in range(HEADS):
        src = slice(h * HEAD_DIM, (h + 1) * HEAD_DIM)
        dst = slice(2 * h * HEAD_DIM, (2 * h + 1) * HEAD_DIM)
        vext[0:RADIUS, dst] = vp_ref[0, 0, :, src]
        vext[RADIUS:RADIUS + qblk, dst] = v_ref[0, 0, :, src]
        vext[RADIUS + qblk:RADIUS + qblk + RADIUS, dst] = vn_ref[0, 0, :, src]
        vext[:, (2 * h + 1) * HEAD_DIM:(2 * h + 2) * HEAD_DIM] = jnp.ones(
            (qblk + 2 * RADIUS, HEAD_DIM), BF16)

    nkey = ATT_SUB + 2 * RADIUS
    qi = lax.broadcasted_iota(jnp.int32, (ATT_SUB, nkey), 0)
    ki = lax.broadcasted_iota(jnp.int32, (ATT_SUB, nkey), 1)
    band = jnp.abs(ki - RADIUS - qi) <= RADIUS
    lane = lax.broadcasted_iota(jnp.int32, (ATT_SUB, LANES), 1)
    scale = HEAD_DIM ** -0.5

    def sub(i, carry):
        q0 = pl.multiple_of(i * ATT_SUB, ATT_SUB)
        kglob = j * qblk + q0 - RADIUS + ki
        valid = band & (kglob >= 0) & (kglob < length)
        lse_tile = jnp.zeros((ATT_SUB, LANES), F32)
        for h in range(HEADS):
            hs = slice(h * HEAD_DIM, (h + 1) * HEAD_DIM)
            q = q_ref[0, 0, pl.ds(q0, ATT_SUB), hs]
            ks = kext[pl.ds(q0, nkey), hs]
            vs = vext[pl.ds(q0, nkey), 2 * h * HEAD_DIM:(2 * h + 2) * HEAD_DIM]
            s = lax.dot_general(q, ks, (((1,), (1,)), ((), ())),
                                preferred_element_type=F32) * scale
            s = jnp.where(valid, s, NEG_INF)
            m = jnp.max(s, axis=-1, keepdims=True)
            p = jnp.exp(s - m)
            pv = jnp.dot(p.astype(BF16), vs, preferred_element_type=F32)
            l = pv[:, HEAD_DIM:HEAD_DIM + 1]
            o_ref[0, 0, pl.ds(q0, ATT_SUB), hs] = pv[:, 0:HEAD_DIM] / l
            lse_tile = jnp.where(lane == h, m + jnp.log(l), lse_tile)
        lse_ref[0, 0, pl.ds(q0, ATT_SUB), :] = lse_tile
        return carry

    lax.fori_loop(0, qblk // ATT_SUB, sub, 0)


def _attention_group(qkv_g):
    B, d, L, _ = qkv_g.shape
    qblk = min(L, ATT_QBLK)
    nblk = L // qblk
    halo_per_blk = qblk // RADIUS
    last_halo = L // RADIUS - 1

    main = lambda part: pl.BlockSpec((1, 1, qblk, ATTN_WIDTH), lambda b, r, j: (b, r, j, part))
    prev = lambda part: pl.BlockSpec(
        (1, 1, RADIUS, ATTN_WIDTH),
        lambda b, r, j: (b, r, jnp.maximum(j * halo_per_blk - 1, 0), part))
    nxt = lambda part: pl.BlockSpec(
        (1, 1, RADIUS, ATTN_WIDTH),
        lambda b, r, j: (b, r, jnp.minimum((j + 1) * halo_per_blk, last_halo), part))
    return pl.pallas_call(
        functools.partial(_attn_kernel, qblk=qblk, length=L),
        out_shape=(jax.ShapeDtypeStruct((B, d, L, ATTN_WIDTH), F32),
                   jax.ShapeDtypeStruct((B, d, L, LANES), F32)),
        grid=(B, d, nblk),
        in_specs=[main(0), main(1), prev(1), nxt(1), main(2), prev(2), nxt(2)],
        out_specs=(pl.BlockSpec((1, 1, qblk, ATTN_WIDTH), lambda b, r, j: (b, r, j, 0)),
                   pl.BlockSpec((1, 1, qblk, LANES), lambda b, r, j: (b, r, j, 0))),
        scratch_shapes=[pltpu.VMEM((qblk + 2 * RADIUS, ATTN_WIDTH), BF16),
                        pltpu.VMEM((qblk + 2 * RADIUS, 2 * ATTN_WIDTH), BF16)],
        compiler_params=_cparams(("parallel", "parallel", "parallel")),
        name=f"attn_d{d}",
    )(qkv_g, qkv_g, qkv_g, qkv_g, qkv_g, qkv_g, qkv_g)


def _merge_kernel(x_ref, yr_ref, o0_ref, o1_ref, o2_ref, l0_ref, l1_ref, l2_ref,
                  gmix_ref, wg_ref, bg_ref, prnn_ref, pattn_ref, wout_ref,
                  gffn_ref, wr_ref, wrt_ref, br_ref, brt_ref,
                  x2_ref, h2_ref, afft_ref, yattn, stage):
    d = x_ref.shape[1]
    x = x_ref[...]
    hb = _rms(x, gmix_ref[...]).astype(BF16)

    def natural(ref, g, cs):
        dil = ATTN_GROUPS[g][1]
        if dil == 1:
            return ref[:, cs]
        return _interleave([ref[0, r, :, cs] for r in range(dil)], stage, dil)

    o_refs = (o0_ref, o1_ref, o2_ref)
    all_lanes = slice(0, LANES)
    lses = [natural(r, g, all_lanes) for g, r in enumerate((l0_ref, l1_ref, l2_ref))]
    for h in range(HEADS):
        hs = slice(h * HEAD_DIM, (h + 1) * HEAD_DIM)
        cols = [l[:, h:h + 1] for l in lses]
        m = jnp.maximum(jnp.maximum(cols[0], cols[1]), cols[2])
        ws = [jnp.exp(c - m) for c in cols]
        den = ws[0] + ws[1] + ws[2]
        acc = (ws[0] / den) * natural(o_refs[0], 0, hs)
        acc = acc + (ws[1] / den) * natural(o_refs[1], 1, hs)
        acc = acc + (ws[2] / den) * natural(o_refs[2], 2, hs)
        yattn[:, hs] = acc.astype(BF16)

    branch_a = jnp.dot(yr_ref[...], prnn_ref[...], preferred_element_type=F32)
    branch_b = jnp.dot(yattn[...], pattn_ref[...], preferred_element_type=F32)
    ga = jnp.dot(hb, wg_ref[:, 0:d], preferred_element_type=F32) + bg_ref[:, 0:d]
    gb = jnp.dot(hb, wg_ref[:, d:2 * d], preferred_element_type=F32) + bg_ref[:, d:2 * d]
    merged = _sigmoid(ga) * branch_a + _sigmoid(gb) * branch_b
    x2 = x + jnp.dot(merged.astype(BF16), wout_ref[...], preferred_element_type=F32)
    x2_ref[...] = x2

    h2 = _rms(x2, gffn_ref[...]).astype(BF16)
    h2_ref[:, 0:d] = h2
    n_exp = afft_ref.shape[1]
    logits = jnp.dot(h2, wr_ref[...], preferred_element_type=F32) + br_ref[...]
    lane = lax.broadcasted_iota(jnp.int32, logits.shape, 1)
    mx = jnp.max(logits, axis=-1, keepdims=True)
    e = jnp.exp(logits - mx)
    den = jnp.sum(jnp.where(lane < n_exp, e, 0.0), axis=-1, keepdims=True)
    aff = e / den
    head = aff.astype(BF16).astype(F32)
    tile = jnp.where(lane < n_exp, head, jnp.where(lane < 2 * n_exp, aff - head, 0.0))
    h2_ref[:, d:d + LANES] = tile.astype(BF16)
    logits_t = lax.dot_general(wrt_ref[...], h2, (((1,), (1,)), ((), ())),
                               preferred_element_type=F32) + brt_ref[...]
    mxt = jnp.max(logits_t, axis=0, keepdims=True)
    et = jnp.exp(logits_t - mxt)
    afft_ref[0] = et / jnp.sum(et, axis=0, keepdims=True)


def _merge(x2d, y_rnn, outs, lses, gmix, wg, bg, p_rnn, p_attn, w_out, gffn, wr, br, batch):
    T, D = x2d.shape
    S = T // batch
    nseq = S // ROW_TILE
    E = wr.shape[1]
    row = lambda w: pl.BlockSpec((ROW_TILE, w), lambda i: (i, 0))
    full = lambda a: pl.BlockSpec(a.shape, lambda i: (0,) * a.ndim)

    def attn_spec(g, w):
        dil = ATTN_GROUPS[g][1]
        if dil == 1:
            return row(w)
        return pl.BlockSpec((1, dil, ROW_TILE // dil, w), lambda i: (i // nseq, 0, i % nseq, 0))

    wrt = wr.T
    brt = br.reshape(E, 1)
    wr_pad = jnp.concatenate([wr, wr, jnp.zeros((D, LANES - 2 * E), wr.dtype)], axis=1)
    br_pad = jnp.concatenate([br, br, jnp.full((1, LANES - 2 * E), NEG_INF, br.dtype)], axis=1)
    consts = [gmix, wg, bg, p_rnn, p_attn, w_out, gffn, wr_pad, wrt, br_pad, brt]
    args = [x2d, y_rnn, *outs, *lses, *consts]
    in_specs = ([row(D), row(D_RNN)] + [attn_spec(g, ATTN_WIDTH) for g in range(N_GROUPS)]
                + [attn_spec(g, LANES) for g in range(N_GROUPS)] + [full(a) for a in consts])
    dmax = max(dil for _, dil in ATTN_GROUPS)
    return pl.pallas_call(
        _merge_kernel,
        out_shape=(jax.ShapeDtypeStruct((T, D), F32),
                   jax.ShapeDtypeStruct((T, D + LANES), BF16),
                   jax.ShapeDtypeStruct((batch, E, S), F32)),
        grid=(T // ROW_TILE,),
        in_specs=in_specs,
        out_specs=(row(D), row(D + LANES),
                   pl.BlockSpec((1, E, ROW_TILE), lambda i: (i // nseq, 0, i % nseq))),
        scratch_shapes=[pltpu.VMEM((ROW_TILE, ATTN_WIDTH), BF16),
                        pltpu.VMEM((ROW_TILE // dmax * _pitch(dmax), LANES), F32)],
        compiler_params=_cparams(("parallel",)),
        name="merge",
    )(*args)


def _route_kernel(aff_ref, pos_ref, off_ref, *, cap):
    a = aff_ref[0]
    E, R, _ = a.shape
    bits = pltpu.bitcast(a, jnp.int32)

    def count(mask):
        c = jnp.sum(mask.astype(F32), axis=2, keepdims=True)
        return jnp.sum(c, axis=1, keepdims=True)

    def bs(_, carry):
        lo, hi = carry
        mid = lo + ((hi - lo + 1) >> 1)
        ok = count(bits >= mid) >= cap
        return jnp.where(ok, mid, lo), jnp.where(ok, hi, mid - 1)

    lo0 = jnp.zeros((E, 1, 1), jnp.int32)
    hi0 = jnp.full((E, 1, 1), 0x7F7FFFFF, jnp.int32)
    thr, _ = lax.fori_loop(0, 31, bs, (lo0, hi0))

    gt = bits > thr
    eq = bits == thr
    need = cap - count(gt)

    ki = lax.broadcasted_iota(jnp.int32, (LANES, LANES), 0)
    li = lax.broadcasted_iota(jnp.int32, (LANES, LANES), 1)
    upper = (ki <= li).astype(BF16)
    allones = jnp.ones((LANES, LANES), BF16)
    ri = lax.broadcasted_iota(jnp.int32, (R, R), 0)
    ci = lax.broadcasted_iota(jnp.int32, (R, R), 1)
    strict = (ci < ri).astype(BF16)

    def prefix(mask):
        mb = mask.astype(BF16).reshape(E * R, LANES)
        within = jnp.dot(mb, upper, preferred_element_type=F32)
        totals = jnp.dot(mb, allones, preferred_element_type=F32).astype(BF16)
        offs = []
        for e in range(E):
            offs.append(jnp.dot(strict, totals[e * R:(e + 1) * R, :],
                                preferred_element_type=F32))
        rowoff = jnp.concatenate(offs, axis=0)
        excl = within - mb.astype(F32) + rowoff
        return excl.reshape(E, R, LANES), rowoff.reshape(E, R, LANES)

    eq_rank, _ = prefix(eq)
    sel = gt | (eq & (eq_rank < need.astype(F32)))
    slot, rowoff = prefix(sel)
    pos_ref[0] = jnp.where(sel, slot.astype(jnp.int32), -1)
    off_ref[0] = rowoff.astype(jnp.int32)


def _route(aff_t, cap):
    B, E, S = aff_t.shape
    R = S // LANES
    a4 = aff_t.reshape(B, E, R, LANES)
    blk = pl.BlockSpec((1, E, R, LANES), lambda b: (b, 0, 0, 0))
    pos, off = pl.pallas_call(
        functools.partial(_route_kernel, cap=cap),
        out_shape=(jax.ShapeDtypeStruct((B, E, R, LANES), jnp.int32),
                   jax.ShapeDtypeStruct((B, E, R, LANES), jnp.int32)),
        grid=(B,),
        in_specs=[blk],
        out_specs=(blk, blk),
        compiler_params=_cparams(("parallel",)),
        name="route",
    )(a4)
    return pos, off


def _slot_windows(off_ref, base, n_exp, ntb, tb):
    starts = []
    rounds = jnp.int32(0)
    for e in range(n_exp):
        idx = (base + e) * (ntb + 1) + tb
        start = (off_ref[idx] >> 4) << 4
        starts.append(start)
        rounds = jnp.maximum(rounds, (off_ref[idx + 1] - start + SLOT_WIN - 1) >> SLOT_SHIFT)
    return starts, rounds


def _gather_kernel(off_ref, pos_ref, h_ref, o_ref, *, cap, ntb, n_exp):
    b = pl.program_id(0)
    grp = pl.program_id(1)
    tb = pl.program_id(2)

    @pl.when(tb == 0)
    def _():
        o_ref[...] = jnp.zeros_like(o_ref)

    base = (b * pl.num_programs(1) + grp) * n_exp
    starts, rounds = _slot_windows(off_ref, base, n_exp, ntb, tb)
    ji = lax.broadcasted_iota(jnp.int32, (SLOT_WIN, TOK_BLK), 0)
    hblk = h_ref[0]

    def one_round(k, carry):
        pieces, rows = [], []
        for e in range(n_exp):
            first = starts[e] + k * SLOT_WIN
            row0 = pl.multiple_of(jnp.minimum(first, cap - SLOT_WIN), BF16_ROWS)
            slot = ji + row0
            hit = (slot == pos_ref[0, 0, e:e + 1, :]) & (slot >= first)
            pieces.append(jnp.where(hit, 1.0, 0.0).astype(BF16))
            rows.append(row0)
        onehot = jnp.concatenate(pieces, axis=0)
        g = jnp.dot(onehot, hblk, preferred_element_type=F32)
        for e in range(n_exp):
            o_ref[0, e, pl.ds(rows[e], SLOT_WIN), :] += (
                g[e * SLOT_WIN:(e + 1) * SLOT_WIN, :].astype(BF16))
        return carry

    lax.fori_loop(0, rounds, one_round, 0)


def _gather(offs, pos, h2, cap):
    B, E, S = pos.shape
    D = h2.shape[-1]
    ntb = S // TOK_BLK
    n_exp = E // EXPERT_SPLIT
    pos4 = pos.reshape(B, EXPERT_SPLIT, n_exp, S)
    return pl.pallas_call(
        functools.partial(_gather_kernel, cap=cap, ntb=ntb, n_exp=n_exp),
        out_shape=jax.ShapeDtypeStruct((B, E, cap, D), BF16),
        grid_spec=pltpu.PrefetchScalarGridSpec(
            num_scalar_prefetch=1,
            grid=(B, EXPERT_SPLIT, ntb),
            in_specs=[
                pl.BlockSpec((1, 1, n_exp, TOK_BLK), lambda b, g, t, off: (b, g, 0, t)),
                pl.BlockSpec((1, TOK_BLK, D), lambda b, g, t, off: (b, t, 0)),
            ],
            out_specs=pl.BlockSpec((1, n_exp, cap, D), lambda b, g, t, off: (b, g, 0, 0)),
        ),
        compiler_params=_cparams(("parallel", "parallel", "arbitrary")),
        name="gather",
    )(offs, pos4, h2)


def _ffn_kernel(x_ref, wg_ref, wu_ref, wd_ref, o_ref, acc, *, f_chunk):
    e = pl.program_id(0)
    d = wg_ref.shape[1]
    x = x_ref[0, 0, :, 0:d]
    gl = x_ref[0, 0, :, d:d + LANES].astype(F32)
    lane = lax.broadcasted_iota(jnp.int32, gl.shape, 1)
    gate = jnp.sum(jnp.where((lane == e) | (lane == e + pl.num_programs(0)), gl, 0.0),
                   axis=1, keepdims=True)
    nf = wg_ref.shape[2] // f_chunk
    for f in range(nf):
        fs = slice(f * f_chunk, (f + 1) * f_chunk)
        g = jnp.dot(x, wg_ref[0, :, fs], preferred_element_type=F32)
        u = jnp.dot(x, wu_ref[0, :, fs], preferred_element_type=F32)
        hid = (g * _sigmoid(g) * u).astype(BF16)
        part = jnp.dot(hid, wd_ref[0, fs, :], preferred_element_type=F32)
        if f == 0:
            acc[...] = part
        else:
            acc[...] += part
    o_ref[0, 0] = (acc[...] * gate).astype(BF16)


def _ffn(xg, w_gate, w_up, w_down):
    B, E, C, DX = xg.shape
    D = w_gate.shape[1]
    F = w_gate.shape[2]
    return pl.pallas_call(
        functools.partial(_ffn_kernel, f_chunk=512),
        out_shape=jax.ShapeDtypeStruct((B, E, C, D), BF16),
        grid=(E, B),
        in_specs=[
            pl.BlockSpec((1, 1, C, DX), lambda e, b: (b, e, 0, 0)),
            pl.BlockSpec((1, D, F), lambda e, b: (e, 0, 0)),
            pl.BlockSpec((1, D, F), lambda e, b: (e, 0, 0)),
            pl.BlockSpec((1, F, D), lambda e, b: (e, 0, 0)),
        ],
        out_specs=pl.BlockSpec((1, 1, C, D), lambda e, b: (b, e, 0, 0)),
        scratch_shapes=[pltpu.VMEM((C, D), F32)],
        compiler_params=_cparams(("arbitrary", "arbitrary")),
        name="ffn",
    )(xg, w_gate, w_up, w_down)


def _combine_kernel(off_ref, x_ref, pos_ref, eo_ref, g_ref, o_ref, wstack, acc, *, cap, ntb):
    b = pl.program_id(0)
    tb = pl.program_id(1)
    E = pos_ref.shape[2]
    pos = pos_ref[0]
    starts, rounds = _slot_windows(off_ref, b * E, E, ntb, tb)
    lane = lax.broadcasted_iota(jnp.int32, (TOK_BLK, LANES), 1)
    low = lane < SLOT_WIN
    acc[...] = x_ref[0]

    def one_round(k, carry):
        onehot = []
        for p in range(E // 2):
            firsts, row0s = [], []
            for e in (2 * p, 2 * p + 1):
                first = starts[e] + k * SLOT_WIN
                row0 = pl.multiple_of(jnp.minimum(first, cap - SLOT_WIN), BF16_ROWS)
                wstack[e * SLOT_WIN:(e + 1) * SLOT_WIN, :] = eo_ref[0, e, pl.ds(row0, SLOT_WIN), :]
                firsts.append(first)
                row0s.append(row0)
            slot = lane + jnp.where(low, row0s[0], row0s[1] - SLOT_WIN)
            first = jnp.where(low, firsts[0], firsts[1])
            pcol = jnp.where(low, pos[:, 2 * p:2 * p + 1], pos[:, 2 * p + 1:2 * p + 2])
            hit = (slot == pcol) & (slot >= first)
            onehot.append(jnp.where(hit, 1.0, 0.0).astype(BF16))
        acc[...] += jnp.dot(jnp.concatenate(onehot, axis=1), wstack[...],
                            preferred_element_type=F32)
        return carry

    lax.fori_loop(0, rounds, one_round, 0)
    o_ref[0] = _rms(acc[...], g_ref[...])


def _combine(offs, x2, pos_t, eo, g_final, cap):
    B, S, D = x2.shape
    E = pos_t.shape[2]
    ntb = S // TOK_BLK
    return pl.pallas_call(
        functools.partial(_combine_kernel, cap=cap, ntb=ntb),
        out_shape=jax.ShapeDtypeStruct((B, S, D), F32),
        grid_spec=pltpu.PrefetchScalarGridSpec(
            num_scalar_prefetch=1,
            grid=(B, ntb),
            in_specs=[
                pl.BlockSpec((1, TOK_BLK, D), lambda b, t, off: (b, t, 0)),
                pl.BlockSpec((1, TOK_BLK, E), lambda b, t, off: (b, t, 0)),
                pl.BlockSpec((1, E, cap, D), lambda b, t, off: (b, 0, 0, 0),
                             pipeline_mode=pl.Buffered(1)),
                pl.BlockSpec((1, D), lambda b, t, off: (0, 0)),
            ],
            out_specs=pl.BlockSpec((1, TOK_BLK, D), lambda b, t, off: (b, t, 0)),
            scratch_shapes=[pltpu.VMEM((E * SLOT_WIN, D), BF16), pltpu.VMEM((TOK_BLK, D), F32)],
        ),
        compiler_params=_cparams(("parallel", "arbitrary")),
        name="combine",
    )(offs, x2, pos_t, eo, g_final)


def _layer(x, norm_mix, w_in, b_in, conv_w, conv_b, rg_w, rg_b, rg_lambda, p_rnn, p_attn,
           w_out, norm_ffn, w_router, b_router, w_gate, w_up, w_down, norm_out, cos_t, sin_t):
    B, S, D = x.shape
    T = B * S
    x2d = x.reshape(T, D)
    n_rnn = 2 * D_RNN
    n_qkv = 3 * N_GROUPS * ATTN_WIDTH
    gmix = norm_mix.reshape(1, D)
    w_in_b = w_in.astype(BF16)
    b_in2 = b_in.reshape(1, -1)

    xrgr = _proj_f32(x2d, gmix, w_in_b[:, :n_rnn], b_in2[:, :n_rnn])

    def group_cols(a, g):
        part = lambda p: a[:, n_rnn + (p * N_GROUPS + g) * ATTN_WIDTH:
                           n_rnn + (p * N_GROUPS + g + 1) * ATTN_WIDTH]
        return jnp.concatenate([part(0), part(1), part(2)], axis=1)

    ws = [group_cols(w_in_b, g) for g in range(N_GROUPS)]
    b_qkv = jnp.concatenate([group_cols(b_in2, g) for g in range(N_GROUPS)], axis=1)
    qkv = _proj_qkv(x2d, gmix, ws, b_qkv, cos_t, sin_t, B)

    nblk = D_RNN // RNN_BW
    gate_w = (0.5 * jnp.transpose(rg_w, (2, 3, 0, 1, 4))).reshape(nblk, RNN_BW, 4 * RNN_BW).astype(BF16)
    gate_b = 0.5 * jnp.transpose(rg_b.reshape(2, 2, nblk, RNN_BW), (2, 0, 1, 3)).reshape(nblk, 1, 4 * RNN_BW)
    lam = jnp.transpose(rg_lambda.reshape(2, nblk, RNN_BW), (1, 0, 2))
    y_rnn = _rglru(xrgr.reshape(B, S, n_rnn), conv_w, conv_b.reshape(1, D_RNN), gate_w, gate_b, lam)

    outs, lses = [], []
    for g, (_, dilation) in enumerate(ATTN_GROUPS):
        qkv_g = qkv[g].reshape(B, 1, S, 3 * ATTN_WIDTH) if dilation == 1 else qkv[g]
        o, lse = _attention_group(qkv_g)
        if dilation == 1:
            o, lse = o.reshape(T, ATTN_WIDTH), lse.reshape(T, LANES)
        outs.append(o)
        lses.append(lse)

    x2, h2, aff_t = _merge(
        x2d, y_rnn.reshape(T, D_RNN), outs, lses, gmix,
        w_in_b[:, n_rnn + n_qkv:], b_in2[:, n_rnn + n_qkv:],
        p_rnn.astype(BF16), p_attn.astype(BF16), w_out.astype(BF16),
        norm_ffn.reshape(1, D), w_router.astype(BF16), b_router.reshape(1, -1), B)

    E = w_router.shape[1]
    cap = CAPACITY_FACTOR * S // E
    pos4, off4 = _route(aff_t, cap)
    pos = pos4.reshape(B, E, S)
    blk_off = off4[:, :, ::TOK_BLK // LANES, 0]
    offs = jnp.concatenate([blk_off, jnp.full((B, E, 1), cap, jnp.int32)], axis=2).reshape(-1)
    xg = _gather(offs, pos, h2.reshape(B, S, D + LANES), cap)
    eo = _ffn(xg, w_gate.astype(BF16), w_up.astype(BF16), w_down.astype(BF16))
    pos_t = jnp.transpose(pos, (0, 2, 1))
    return _combine(offs, x2.reshape(B, S, D), pos_t, eo, norm_out.reshape(1, D), cap)


def kernel(x, norm_mix, w_in, b_in, conv_w, conv_b, rg_w, rg_b, rg_lambda, p_rnn, p_attn, w_out,
           norm_ffn, w_router, b_router, w_gate, w_up, w_down, norm_final):
    depth = w_in.shape[0]
    assert depth == 1, "final rmsnorm is fused into the single layer's combine step"
    cos_t, sin_t = _rope_tables(x.shape[1])
    return _layer(x, norm_mix[0], w_in[0], b_in[0], conv_w[0], conv_b[0], rg_w[0], rg_b[0],
                  rg_lambda[0], p_rnn[0], p_attn[0], w_out[0], norm_ffn[0], w_router[0],
                  b_router[0], w_gate[0], w_up[0], w_down[0], norm_final, cos_t, sin_t)
```

```python
import functools

import jax
import jax.numpy as jnp
from jax import lax
from jax.experimental import pallas as pl
from jax.experimental.pallas import tpu as pltpu

F32 = jnp.float32
BF16 = jnp.bfloat16

D_RNN = 1024
RNN_BW = 128
CONV_WIDTH = 4
LRU_C = 8.0
HEAD_DIM = 128
HEADS = 4
ATTN_WIDTH = HEADS * HEAD_DIM
ATTN_GROUPS = ((128, 1), (512, 4), (2048, 16))
N_GROUPS = 3
ROPE_THETA = 500000.0
ROPE_DIM = HEAD_DIM // 4
RADIUS = 64
NEG_INF = -1e30
N_EXPERTS = 16
CAPACITY_FACTOR = 2
RMS_EPS = 1e-6

LANES = 128
SUBLANES = 8
BF16_ROWS = 16
MXU_DIM = 256
VMEM_LIMIT = 56 * 1024 * 1024
F32_MIN_NORMAL = 1.1754944e-38
LOG2_E = 1.4426950408889634
LN_2 = 0.6931471805599453

PROJ_TILE = 1024
ROW_TILE = 512
SCAN_SEG = 128
SCAN_CHUNK = 512
ATT_QBLK = 1024
ATT_SUB = 128
TOK_BLK = 256
GATHER_TOK = 1024
COMBINE_TOK = 512
SLOT_SHIFT = 6
SLOT_WIN = 1 << SLOT_SHIFT
EXPERT_SPLIT = 2


def _cparams(sem):
    return pltpu.CompilerParams(dimension_semantics=sem, vmem_limit_bytes=VMEM_LIMIT)


def _rms(x, g):
    ms = jnp.mean(x * x, axis=-1, keepdims=True)
    return x * lax.rsqrt(ms + RMS_EPS) * g


def _pitch(d):
    return d if d <= 4 else d + 4


def _deinterleave(t, stage, d):
    n = t.shape[0]
    p = _pitch(d)
    if p == d:
        stage[0:n, :] = t
    else:
        for j in range(n // d):
            stage[j * p:j * p + d, :] = t[j * d:(j + 1) * d, :]
    return [stage[pl.ds(r, n // d, stride=p), :] for r in range(d)]


def _interleave(parts, stage, d):
    m = parts[0].shape[0]
    p = _pitch(d)
    for r in range(d):
        stage[pl.ds(r, m, stride=p), :] = parts[r]
    if p == d:
        return stage[0:m * d, :]
    return jnp.concatenate([stage[j * p:j * p + d, :] for j in range(m)], axis=0)


def _proj_f32_kernel(x_ref, g_ref, w_ref, b_ref, o_ref, *, n_chunk):
    hb = _rms(x_ref[...], g_ref[...]).astype(BF16)
    for c in range(o_ref.shape[1] // n_chunk):
        sl = slice(c * n_chunk, (c + 1) * n_chunk)
        o_ref[:, sl] = jnp.dot(hb, w_ref[:, sl], preferred_element_type=F32) + b_ref[:, sl]


def _proj_f32(x2d, g, w, b):
    T, D = x2d.shape
    N = w.shape[1]
    return pl.pallas_call(
        functools.partial(_proj_f32_kernel, n_chunk=512),
        out_shape=jax.ShapeDtypeStruct((T, N), F32),
        grid=(T // PROJ_TILE,),
        in_specs=[
            pl.BlockSpec((PROJ_TILE, D), lambda i: (i, 0)),
            pl.BlockSpec((1, D), lambda i: (0, 0)),
            pl.BlockSpec((D, N), lambda i: (0, 0)),
            pl.BlockSpec((1, N), lambda i: (0, 0)),
        ],
        out_specs=pl.BlockSpec((PROJ_TILE, N), lambda i: (i, 0)),
        compiler_params=_cparams(("parallel",)),
        name="proj_rnn",
    )(x2d, g, w, b)


def _proj_qkv_kernel(x_ref, g_ref, w0_ref, w1_ref, w2_ref, b_ref, cos_ref, sin_ref,
                     o0_ref, o1_ref, o2_ref, stage):
    hb = _rms(x_ref[...], g_ref[...]).astype(BF16)
    cosv = cos_ref[...]
    sinv = sin_ref[...]
    lane = lax.broadcasted_iota(jnp.int32, (1, HEAD_DIM), 1)
    first_half = lane < (ROPE_DIM // 2)
    gw = 3 * ATTN_WIDTH
    for g, (w_ref, o_ref) in enumerate(((w0_ref, o0_ref), (w1_ref, o1_ref), (w2_ref, o2_ref))):
        d = ATTN_GROUPS[g][1]
        for c in range(gw // MXU_DIM):
            sl = slice(c * MXU_DIM, (c + 1) * MXU_DIM)
            t2 = jnp.dot(hb, w_ref[:, sl], preferred_element_type=F32)
            t2 = t2 + b_ref[:, g * gw + c * MXU_DIM:g * gw + (c + 1) * MXU_DIM]
            for hh in range(MXU_DIM // HEAD_DIM):
                t = t2[:, hh * HEAD_DIM:(hh + 1) * HEAD_DIM]
                col = c * MXU_DIM + hh * HEAD_DIM
                if col < 2 * ATTN_WIDTH:
                    up = pltpu.roll(t, HEAD_DIM - ROPE_DIM // 2, 1)
                    down = pltpu.roll(t, ROPE_DIM // 2, 1)
                    t = t * cosv + jnp.where(first_half, up, down) * sinv
                cs = slice(col, col + HEAD_DIM)
                if d == 1:
                    o_ref[:, cs] = t.astype(BF16)
                else:
                    for r, part in enumerate(_deinterleave(t, stage, d)):
                        o_ref[0, r, :, cs] = part.astype(BF16)


def _proj_qkv(x2d, g, ws, b, cos_t, sin_t, batch):
    T, D = x2d.shape
    S = T // batch
    nseq = S // PROJ_TILE
    gw = 3 * ATTN_WIDTH
    d1, d2 = ATTN_GROUPS[1][1], ATTN_GROUPS[2][1]
    full = lambda a: pl.BlockSpec(a.shape, lambda i: (0,) * a.ndim)
    dil_spec = lambda d: pl.BlockSpec((1, d, PROJ_TILE // d, gw),
                                      lambda i: (i // nseq, 0, i % nseq, 0))
    return pl.pallas_call(
        _proj_qkv_kernel,
        out_shape=(jax.ShapeDtypeStruct((T, gw), BF16),
                   jax.ShapeDtypeStruct((batch, d1, S // d1, gw), BF16),
                   jax.ShapeDtypeStruct((batch, d2, S // d2, gw), BF16)),
        grid=(T // PROJ_TILE,),
        in_specs=[
            pl.BlockSpec((PROJ_TILE, D), lambda i: (i, 0)),
            full(g), full(ws[0]), full(ws[1]), full(ws[2]), full(b),
            pl.BlockSpec((PROJ_TILE, HEAD_DIM), lambda i: (i % nseq, 0)),
            pl.BlockSpec((PROJ_TILE, HEAD_DIM), lambda i: (i % nseq, 0)),
        ],
        out_specs=(pl.BlockSpec((PROJ_TILE, gw), lambda i: (i, 0)), dil_spec(d1), dil_spec(d2)),
        scratch_shapes=[pltpu.VMEM((PROJ_TILE // d2 * _pitch(d2), LANES), F32)],
        compiler_params=_cparams(("parallel",)),
        name="proj_qkv",
    )(x2d, g, ws[0], ws[1], ws[2], b, cos_t, sin_t)


def _rope_tables(seq):
    pos = jnp.arange(seq, dtype=F32)
    inv = ROPE_THETA ** (-jnp.arange(0, ROPE_DIM, 2, dtype=F32) / ROPE_DIM)
    ang = pos[:, None] * inv[None, :]
    cos, sin = jnp.cos(ang), jnp.sin(ang)
    pad = HEAD_DIM - ROPE_DIM
    cos_t = jnp.concatenate([cos, cos, jnp.ones((seq, pad), F32)], axis=1)
    sin_t = jnp.concatenate([-sin, sin, jnp.zeros((seq, pad), F32)], axis=1)
    return cos_t, sin_t


def _sigmoid(x):
    return 0.5 * jnp.tanh(0.5 * x) + 0.5


def _rglru_kernel(xr_ref, gr_ref, cw_ref, cb_ref, gw_ref, gb_ref, lam_ref, o_ref,
                  xpad, af, uf, ab, ub, cf, cbk, *, seq):
    nseg = seq // SCAN_SEG
    nchunk = seq // SCAN_CHUNK
    seg_per_chunk = SCAN_CHUNK // SCAN_SEG
    pitch = nseg + 4

    cw = cw_ref[...]
    cbias = cb_ref[...]
    gbias = gb_ref[...]
    z = -lam_ref[...]
    sp = jnp.maximum(z, 0.0) + jnp.log1p(jnp.exp(-jnp.abs(z)))
    half_neg_c_sp = (-0.5 * LRU_C) * sp

    zero_rows = jnp.zeros((SUBLANES, LANES), F32)
    xpad[0:SUBLANES, :] = zero_rows
    xpad[seq + SUBLANES:seq + 2 * SUBLANES, :] = zero_rows

    def copy_in(c, carry):
        t0 = pl.multiple_of(c * SCAN_CHUNK, SCAN_CHUNK)
        xpad[pl.ds(t0 + SUBLANES, SCAN_CHUNK), :] = xr_ref[0, pl.ds(t0, SCAN_CHUNK), :]
        return carry

    lax.fori_loop(0, nchunk, copy_in, 0)

    def stage1(c, carry):
        t0 = c * SCAN_CHUNK
        xc = cbias
        for tap in range(CONV_WIDTH):
            lo = t0 + SUBLANES - CONV_WIDTH // 2 + tap
            xc = xc + cw[tap:tap + 1, :] * xpad[pl.ds(lo, SCAN_CHUNK), :]
        th = jnp.tanh(jnp.dot(xc.astype(BF16), gw_ref[...], preferred_element_type=F32) + gbias)
        xc_half = 0.5 * xc
        for d, (a_s, u_s) in enumerate(((af, uf), (ab, ub))):
            t_r = th[:, (2 * d) * LANES:(2 * d + 1) * LANES]
            t_i = th[:, (2 * d + 1) * LANES:(2 * d + 2) * LANES]
            half_k = half_neg_c_sp[d:d + 1, :]
            log_a = t_r * half_k + half_k
            a = jnp.exp(log_a)
            v = jnp.tanh(log_a) * (-1.0 - a * a)
            u = (xc_half * (v * lax.rsqrt(jnp.maximum(v, F32_MIN_NORMAL)))) * (t_i + 1.0)
            for k in range(seg_per_chunk):
                seg = c * seg_per_chunk + k
                a_s[pl.ds(seg, SCAN_SEG, stride=pitch), :] = a[k * SCAN_SEG:(k + 1) * SCAN_SEG, :]
                u_s[pl.ds(seg, SCAN_SEG, stride=pitch), :] = u[k * SCAN_SEG:(k + 1) * SCAN_SEG, :]
        return carry

    lax.fori_loop(0, nchunk, stage1, 0)

    def scan_dir(a_s, u_s, reverse):
        def body(it, carry):
            i = (SCAN_SEG - 1 - it) if reverse else it
            h, cum = carry
            rows = pl.ds(i * pitch, nseg)
            a = a_s[rows, :]
            u = u_s[rows, :]
            h = a * h + u
            cum = a * cum
            u_s[rows, :] = h
            a_s[rows, :] = cum
            return h, cum

        return lax.fori_loop(0, SCAN_SEG, body,
                             (jnp.zeros((nseg, LANES), F32), jnp.ones((nseg, LANES), F32)))

    end_h, end_a = scan_dir(af, uf, False)
    beg_h, beg_a = scan_dir(ab, ub, True)

    c = jnp.zeros((1, LANES), F32)
    for s in range(nseg):
        cf[s:s + 1, :] = c
        c = end_h[s:s + 1, :] + end_a[s:s + 1, :] * c
    c = jnp.zeros((1, LANES), F32)
    for s in range(nseg - 1, -1, -1):
        cbk[s:s + 1, :] = c
        c = beg_h[s:s + 1, :] + beg_a[s:s + 1, :] * c

    def finish(s, carry):
        t0 = pl.multiple_of(s * SCAN_SEG, SCAN_SEG)
        rows = pl.ds(s, SCAN_SEG, stride=pitch)
        hf = uf[rows, :] + af[rows, :] * cf[pl.ds(s, 1), :]
        hb = ub[rows, :] + ab[rows, :] * cbk[pl.ds(s, 1), :]
        g = jax.nn.gelu(gr_ref[0, pl.ds(t0, SCAN_SEG), :])
        o_ref[0, pl.ds(t0, SCAN_SEG), :] = (g * (hf + hb)).astype(BF16)
        return carry

    lax.fori_loop(0, nseg, finish, 0)


def _rglru(xrgr, conv_w, conv_b, gate_w, gate_b, lam):
    B, S, _ = xrgr.shape
    nblk = D_RNN // RNN_BW
    nseg = S // SCAN_SEG
    scr = pltpu.VMEM((SCAN_SEG * (nseg + 4), LANES), F32)
    return pl.pallas_call(
        functools.partial(_rglru_kernel, seq=S),
        out_shape=jax.ShapeDtypeStruct((B, S, D_RNN), BF16),
        grid=(B, nblk),
        in_specs=[
            pl.BlockSpec((1, S, RNN_BW), lambda b, c: (b, 0, c)),
            pl.BlockSpec((1, S, RNN_BW), lambda b, c: (b, 0, nblk + c)),
            pl.BlockSpec((CONV_WIDTH, RNN_BW), lambda b, c: (0, c)),
            pl.BlockSpec((1, RNN_BW), lambda b, c: (0, c)),
            pl.BlockSpec((None, RNN_BW, 4 * RNN_BW), lambda b, c: (c, 0, 0)),
            pl.BlockSpec((None, 1, 4 * RNN_BW), lambda b, c: (c, 0, 0)),
            pl.BlockSpec((None, 2, RNN_BW), lambda b, c: (c, 0, 0)),
        ],
        out_specs=pl.BlockSpec((1, S, RNN_BW), lambda b, c: (b, 0, c)),
        scratch_shapes=[pltpu.VMEM((S + 2 * SUBLANES, LANES), F32), scr, scr, scr, scr,
                        pltpu.VMEM((nseg, LANES), F32), pltpu.VMEM((nseg, LANES), F32)],
        compiler_params=_cparams(("parallel", "parallel")),
        name="rglru",
    )(xrgr, xrgr, conv_w, conv_b, gate_w, gate_b, lam)


def _attn_kernel(q_ref, k_ref, kp_ref, kn_ref, v_ref, vp_ref, vn_ref, o_ref, lse_ref,
                 kext, vext, *, qblk, length):
    j = pl.program_id(2)
    kext[0:RADIUS, :] = kp_ref[0, 0]
    kext[RADIUS:RADIUS + qblk, :] = k_ref[0, 0]
    kext[RADIUS + qblk:RADIUS + qblk + RADIUS, :] = kn_ref[0, 0]
    for h in range(HEADS):
        src = slice(h * HEAD_DIM, (h + 1) * HEAD_DIM)
        dst = slice(2 * h * HEAD_DIM, (2 * h + 1) * HEAD_DIM)
        vext[0:RADIUS, dst] = vp_ref[0, 0, :, src]
        vext[RADIUS:RADIUS + qblk, dst] = v_ref[0, 0, :, src]
        vext[RADIUS + qblk:RADIUS + qblk + RADIUS, dst] = vn_ref[0, 0, :, src]
        vext[:, (2 * h + 1) * HEAD_DIM:(2 * h + 2) * HEAD_DIM] = jnp.ones(
            (qblk + 2 * RADIUS, HEAD_DIM), BF16)

    nkey = ATT_SUB + 2 * RADIUS
    qi = lax.broadcasted_iota(jnp.int32, (ATT_SUB, nkey), 0)
    ki = lax.broadcasted_iota(jnp.int32, (ATT_SUB, nkey), 1)
    band_bias = jnp.where(jnp.abs(ki - RADIUS - qi) <= RADIUS, 0.0, NEG_INF)
    krow = lax.broadcasted_iota(jnp.int32, (1, nkey), 1)
    lane = lax.broadcasted_iota(jnp.int32, (ATT_SUB, LANES), 1)
    scale2 = HEAD_DIM ** -0.5 * LOG2_E

    def sub(i, carry):
        q0 = pl.multiple_of(i * ATT_SUB, ATT_SUB)
        kglob = j * qblk + q0 - RADIUS + krow
        edge_bias = jnp.where((kglob >= 0) & (kglob < length), 0.0, NEG_INF)
        bias = band_bias + edge_bias
        lse_tile = jnp.zeros((ATT_SUB, LANES), F32)
        for h in range(HEADS):
            hs = slice(h * HEAD_DIM, (h + 1) * HEAD_DIM)
            q = q_ref[0, 0, pl.ds(q0, ATT_SUB), hs]
            ks = kext[pl.ds(q0, nkey), hs]
            vs = vext[pl.ds(q0, nkey), 2 * h * HEAD_DIM:(2 * h + 2) * HEAD_DIM]
            s = lax.dot_general(q, ks, (((1,), (1,)), ((), ())),
                                preferred_element_type=F32) * scale2 + bias
            m = jnp.max(s, axis=-1, keepdims=True)
            p = jnp.exp2(s - m)
            pv = jnp.dot(p.astype(BF16), vs, preferred_element_type=F32)
            l = pv[:, HEAD_DIM:2 * HEAD_DIM]
            o_ref[0, 0, pl.ds(q0, ATT_SUB), hs] = pv[:, 0:HEAD_DIM] / l
            lse_tile = jnp.where(lane == h, m * LN_2 + jnp.log(l), lse_tile)
        lse_ref[0, 0, pl.ds(q0, ATT_SUB), :] = lse_tile
        return carry

    lax.fori_loop(0, qblk // ATT_SUB, sub, 0)


def _attention_group(qkv_g):
    B, d, L, _ = qkv_g.shape
    qblk = min(L, ATT_QBLK)
    nblk = L // qblk
    halo_per_blk = qblk // RADIUS
    last_halo = L // RADIUS - 1

    main = lambda part: pl.BlockSpec((1, 1, qblk, ATTN_WIDTH), lambda b, r, j: (b, r, j, part))
    prev = lambda part: pl.BlockSpec(
        (1, 1, RADIUS, ATTN_WIDTH),
        lambda b, r, j: (b, r, jnp.maximum(j * halo_per_blk - 1, 0), part))
    nxt = lambda part: pl.BlockSpec(
        (1, 1, RADIUS, ATTN_WIDTH),
        lambda b, r, j: (b, r, jnp.minimum((j + 1) * halo_per_blk, last_halo), part))
    return pl.pallas_call(
        functools.partial(_attn_kernel, qblk=qblk, length=L),
        out_shape=(jax.ShapeDtypeStruct((B, d, L, ATTN_WIDTH), F32),
                   jax.ShapeDtypeStruct((B, d, L, LANES), F32)),
        grid=(B, d, nblk),
        in_specs=[main(0), main(1), prev(1), nxt(1), main(2), prev(2), nxt(2)],
        out_specs=(pl.BlockSpec((1, 1, qblk, ATTN_WIDTH), lambda b, r, j: (b, r, j, 0)),
                   pl.BlockSpec((1, 1, qblk, LANES), lambda b, r, j: (b, r, j, 0))),
        scratch_shapes=[pltpu.VMEM((qblk + 2 * RADIUS, ATTN_WIDTH), BF16),
                        pltpu.VMEM((qblk + 2 * RADIUS, 2 * ATTN_WIDTH), BF16)],
        compiler_params=_cparams(("parallel", "parallel", "parallel")),
        name=f"attn_d{d}",
    )(qkv_g, qkv_g, qkv_g, qkv_g, qkv_g, qkv_g, qkv_g)


def _merge_kernel(x_ref, yr_ref, o0_ref, o1_ref, o2_ref, l0_ref, l1_ref, l2_ref,
                  gmix_ref, wg_ref, bg_ref, prnn_ref, pattn_ref, wout_ref,
                  gffn_ref, wr_ref, wrt_ref, br_ref, brt_ref,
                  x2_ref, h2_ref, afft_ref, yattn, stage):
    d = x_ref.shape[1]
    x = x_ref[...]
    hb = _rms(x, gmix_ref[...]).astype(BF16)

    def natural(ref, g, cs):
        dil = ATTN_GROUPS[g][1]
        if dil == 1:
            return ref[:, cs]
        return _interleave([ref[0, r, :, cs] for r in range(dil)], stage, dil)

    o_refs = (o0_ref, o1_ref, o2_ref)
    all_lanes = slice(0, LANES)
    lses = [natural(r, g, all_lanes) for g, r in enumerate((l0_ref, l1_ref, l2_ref))]
    for h in range(HEADS):
        hs = slice(h * HEAD_DIM, (h + 1) * HEAD_DIM)
        cols = [l[:, h:h + 1] for l in lses]
        m = jnp.maximum(jnp.maximum(cols[0], cols[1]), cols[2])
        ws = [jnp.exp(c - m) for c in cols]
        den = ws[0] + ws[1] + ws[2]
        acc = (ws[0] / den) * natural(o_refs[0], 0, hs)
        acc = acc + (ws[1] / den) * natural(o_refs[1], 1, hs)
        acc = acc + (ws[2] / den) * natural(o_refs[2], 2, hs)
        yattn[:, hs] = acc.astype(BF16)

    branch_a = jnp.dot(yr_ref[...], prnn_ref[...], preferred_element_type=F32)
    branch_b = jnp.dot(yattn[...], pattn_ref[...], preferred_element_type=F32)
    ga = jnp.dot(hb, wg_ref[:, 0:d], preferred_element_type=F32) + bg_ref[:, 0:d]
    gb = jnp.dot(hb, wg_ref[:, d:2 * d], preferred_element_type=F32) + bg_ref[:, d:2 * d]
    merged = _sigmoid(ga) * branch_a + _sigmoid(gb) * branch_b
    x2 = x + jnp.dot(merged.astype(BF16), wout_ref[...], preferred_element_type=F32)
    x2_ref[...] = x2

    h2 = _rms(x2, gffn_ref[...]).astype(BF16)
    h2_ref[:, 0:d] = h2
    n_exp = afft_ref.shape[1]
    logits = jnp.dot(h2, wr_ref[...], preferred_element_type=F32) + br_ref[...]
    lane = lax.broadcasted_iota(jnp.int32, logits.shape, 1)
    mx = jnp.max(logits, axis=-1, keepdims=True)
    e = jnp.exp(logits - mx)
    den = jnp.sum(jnp.where(lane < n_exp, e, 0.0), axis=-1, keepdims=True)
    aff = e / den
    head = aff.astype(BF16).astype(F32)
    tile = jnp.where(lane < n_exp, head, jnp.where(lane < 2 * n_exp, aff - head, 0.0))
    h2_ref[:, d:d + LANES] = tile.astype(BF16)
    logits_t = lax.dot_general(wrt_ref[...], h2, (((1,), (1,)), ((), ())),
                               preferred_element_type=F32) + brt_ref[...]
    mxt = jnp.max(logits_t, axis=0, keepdims=True)
    et = jnp.exp(logits_t - mxt)
    afft_ref[0] = et / jnp.sum(et, axis=0, keepdims=True)


def _merge(x2d, y_rnn, outs, lses, gmix, wg, bg, p_rnn, p_attn, w_out, gffn, wr, br, batch):
    T, D = x2d.shape
    S = T // batch
    nseq = S // ROW_TILE
    E = wr.shape[1]
    row = lambda w: pl.BlockSpec((ROW_TILE, w), lambda i: (i, 0))
    full = lambda a: pl.BlockSpec(a.shape, lambda i: (0,) * a.ndim)

    def attn_spec(g, w):
        dil = ATTN_GROUPS[g][1]
        if dil == 1:
            return row(w)
        return pl.BlockSpec((1, dil, ROW_TILE // dil, w), lambda i: (i // nseq, 0, i % nseq, 0))

    wrt = wr.T
    brt = br.reshape(E, 1)
    wr_pad = jnp.concatenate([wr, wr, jnp.zeros((D, LANES - 2 * E), wr.dtype)], axis=1)
    br_pad = jnp.concatenate([br, br, jnp.full((1, LANES - 2 * E), NEG_INF, br.dtype)], axis=1)
    consts = [gmix, wg, bg, p_rnn, p_attn, w_out, gffn, wr_pad, wrt, br_pad, brt]
    args = [x2d, y_rnn, *outs, *lses, *consts]
    in_specs = ([row(D), row(D_RNN)] + [attn_spec(g, ATTN_WIDTH) for g in range(N_GROUPS)]
                + [attn_spec(g, LANES) for g in range(N_GROUPS)] + [full(a) for a in consts])
    dmax = max(dil for _, dil in ATTN_GROUPS)
    return pl.pallas_call(
        _merge_kernel,
        out_shape=(jax.ShapeDtypeStruct((T, D), F32),
                   jax.ShapeDtypeStruct((T, D + LANES), BF16),
                   jax.ShapeDtypeStruct((batch, E, S), F32)),
        grid=(T // ROW_TILE,),
        in_specs=in_specs,
        out_specs=(row(D), row(D + LANES),
                   pl.BlockSpec((1, E, ROW_TILE), lambda i: (i // nseq, 0, i % nseq))),
        scratch_shapes=[pltpu.VMEM((ROW_TILE, ATTN_WIDTH), BF16),
                        pltpu.VMEM((ROW_TILE // dmax * _pitch(dmax), LANES), F32)],
        compiler_params=_cparams(("parallel",)),
        name="merge",
    )(*args)


def _route_kernel(aff_ref, pos_ref, off_ref, *, cap):
    a = aff_ref[0]
    E, R, _ = a.shape
    bits = pltpu.bitcast(a, jnp.int32)

    def count(mask):
        c = jnp.sum(mask.astype(F32), axis=2, keepdims=True)
        return jnp.sum(c, axis=1, keepdims=True)

    def bs(_, carry):
        lo, hi = carry
        mid = lo + ((hi - lo + 1) >> 1)
        ok = count(bits >= mid) >= cap
        return jnp.where(ok, mid, lo), jnp.where(ok, hi, mid - 1)

    lo0 = jnp.zeros((E, 1, 1), jnp.int32)
    hi0 = jnp.full((E, 1, 1), 0x7F7FFFFF, jnp.int32)
    thr, _ = lax.fori_loop(0, 31, bs, (lo0, hi0))

    gt = bits > thr
    eq = bits == thr
    need = cap - count(gt)

    ki = lax.broadcasted_iota(jnp.int32, (LANES, LANES), 0)
    li = lax.broadcasted_iota(jnp.int32, (LANES, LANES), 1)
    upper = (ki <= li).astype(BF16)
    allones = jnp.ones((LANES, LANES), BF16)
    ri = lax.broadcasted_iota(jnp.int32, (R, R), 0)
    ci = lax.broadcasted_iota(jnp.int32, (R, R), 1)
    strict = (ci < ri).astype(BF16)

    def prefix(mask):
        mb = mask.astype(BF16).reshape(E * R, LANES)
        within = jnp.dot(mb, upper, preferred_element_type=F32)
        totals = jnp.dot(mb, allones, preferred_element_type=F32).astype(BF16)
        offs = []
        for e in range(E):
            offs.append(jnp.dot(strict, totals[e * R:(e + 1) * R, :],
                                preferred_element_type=F32))
        rowoff = jnp.concatenate(offs, axis=0)
        excl = within - mb.astype(F32) + rowoff
        return excl.reshape(E, R, LANES), rowoff.reshape(E, R, LANES)

    eq_rank, _ = prefix(eq)
    sel = gt | (eq & (eq_rank < need.astype(F32)))
    slot, rowoff = prefix(sel)
    pos_ref[0] = jnp.where(sel, slot.astype(jnp.int32), -1)
    off_ref[0] = rowoff.astype(jnp.int32)


def _route(aff_t, cap):
    B, E, S = aff_t.shape
    R = S // LANES
    a4 = aff_t.reshape(B, E, R, LANES)
    blk = pl.BlockSpec((1, E, R, LANES), lambda b: (b, 0, 0, 0))
    pos, off = pl.pallas_call(
        functools.partial(_route_kernel, cap=cap),
        out_shape=(jax.ShapeDtypeStruct((B, E, R, LANES), jnp.int32),
                   jax.ShapeDtypeStruct((B, E, R, LANES), jnp.int32)),
        grid=(B,),
        in_specs=[blk],
        out_specs=(blk, blk),
        compiler_params=_cparams(("parallel",)),
        name="route",
    )(a4)
    return pos, off


def _slot_windows(off_ref, base, n_exp, ntb, tb):
    starts = []
    rounds = jnp.int32(0)
    for e in range(n_exp):
        idx = (base + e) * (ntb + 1) + tb
        start = (off_ref[idx] >> 4) << 4
        starts.append(start)
        rounds = jnp.maximum(rounds, (off_ref[idx + 1] - start + SLOT_WIN - 1) >> SLOT_SHIFT)
    return starts, rounds


def _gather_kernel(off_ref, pos_ref, h_ref, o_ref, *, cap, ntb, n_exp, nsub):
    b = pl.program_id(0)
    grp = pl.program_id(1)
    t = pl.program_id(2)

    @pl.when(t == 0)
    def _():
        o_ref[...] = jnp.zeros_like(o_ref)

    base = (b * pl.num_programs(1) + grp) * n_exp
    ji = lax.broadcasted_iota(jnp.int32, (SLOT_WIN, TOK_BLK), 0)
    for sb in range(nsub):
        toks = slice(sb * TOK_BLK, (sb + 1) * TOK_BLK)
        starts, rounds = _slot_windows(off_ref, base, n_exp, ntb, t * nsub + sb)

        def one_round(k, carry, toks=toks, starts=starts):
            pieces, rows = [], []
            for e in range(n_exp):
                first = starts[e] + k * SLOT_WIN
                row0 = pl.multiple_of(jnp.minimum(first, cap - SLOT_WIN), BF16_ROWS)
                slot = ji + row0
                hit = (slot == pos_ref[0, 0, e:e + 1, toks]) & (slot >= first)
                pieces.append(jnp.where(hit, 1.0, 0.0).astype(BF16))
                rows.append(row0)
            onehot = jnp.concatenate(pieces, axis=0)
            g = jnp.dot(onehot, h_ref[0, toks, :], preferred_element_type=F32)
            for e in range(n_exp):
                o_ref[0, e, pl.ds(rows[e], SLOT_WIN), :] += (
                    g[e * SLOT_WIN:(e + 1) * SLOT_WIN, :].astype(BF16))
            return carry

        lax.fori_loop(0, rounds, one_round, 0)


def _gather(offs, pos, h2, cap):
    B, E, S = pos.shape
    D = h2.shape[-1]
    ntb = S // TOK_BLK
    n_exp = E // EXPERT_SPLIT
    nsub = GATHER_TOK // TOK_BLK
    pos4 = pos.reshape(B, EXPERT_SPLIT, n_exp, S)
    return pl.pallas_call(
        functools.partial(_gather_kernel, cap=cap, ntb=ntb, n_exp=n_exp, nsub=nsub),
        out_shape=jax.ShapeDtypeStruct((B, E, cap, D), BF16),
        grid_spec=pltpu.PrefetchScalarGridSpec(
            num_scalar_prefetch=1,
            grid=(B, EXPERT_SPLIT, S // GATHER_TOK),
            in_specs=[
                pl.BlockSpec((1, 1, n_exp, GATHER_TOK), lambda b, g, t, off: (b, g, 0, t)),
                pl.BlockSpec((1, GATHER_TOK, D), lambda b, g, t, off: (b, t, 0)),
            ],
            out_specs=pl.BlockSpec((1, n_exp, cap, D), lambda b, g, t, off: (b, g, 0, 0)),
        ),
        compiler_params=_cparams(("parallel", "parallel", "arbitrary")),
        name="gather",
    )(offs, pos4, h2)


def _ffn_kernel(x_ref, wg_ref, wu_ref, wd_ref, o_ref, acc, *, f_chunk):
    e = pl.program_id(0)
    d = wg_ref.shape[1]
    x = x_ref[0, 0, :, 0:d]
    gl = x_ref[0, 0, :, d:d + LANES].astype(F32)
    lane = lax.broadcasted_iota(jnp.int32, gl.shape, 1)
    gate = jnp.sum(jnp.where((lane == e) | (lane == e + pl.num_programs(0)), gl, 0.0),
                   axis=1, keepdims=True)
    nf = wg_ref.shape[2] // f_chunk
    for f in range(nf):
        fs = slice(f * f_chunk, (f + 1) * f_chunk)
        g = jnp.dot(x, wg_ref[0, :, fs], preferred_element_type=F32)
        u = jnp.dot(x, wu_ref[0, :, fs], preferred_element_type=F32)
        hid = (g * _sigmoid(g) * u).astype(BF16)
        part = jnp.dot(hid, wd_ref[0, fs, :], preferred_element_type=F32)
        if f == 0:
            acc[...] = part
        else:
            acc[...] += part
    o_ref[0, 0] = (acc[...] * gate).astype(BF16)


def _ffn(xg, w_gate, w_up, w_down):
    B, E, C, DX = xg.shape
    D = w_gate.shape[1]
    F = w_gate.shape[2]
    return pl.pallas_call(
        functools.partial(_ffn_kernel, f_chunk=512),
        out_shape=jax.ShapeDtypeStruct((B, E, C, D), BF16),
        grid=(E, B),
        in_specs=[
            pl.BlockSpec((1, 1, C, DX), lambda e, b: (b, e, 0, 0)),
            pl.BlockSpec((1, D, F), lambda e, b: (e, 0, 0)),
            pl.BlockSpec((1, D, F), lambda e, b: (e, 0, 0)),
            pl.BlockSpec((1, F, D), lambda e, b: (e, 0, 0)),
        ],
        out_specs=pl.BlockSpec((1, 1, C, D), lambda e, b: (b, e, 0, 0)),
        scratch_shapes=[pltpu.VMEM((C, D), F32)],
        compiler_params=_cparams(("arbitrary", "arbitrary")),
        name="ffn",
    )(xg, w_gate, w_up, w_down)


def _combine_kernel(off_ref, x_ref, pos_ref, eo_ref, g_ref, o_ref, wstack, acc, *, cap, ntb):
    b = pl.program_id(0)
    t = pl.program_id(1)
    E = pos_ref.shape[2]
    nsub = pos_ref.shape[1] // TOK_BLK
    lane = lax.broadcasted_iota(jnp.int32, (TOK_BLK, LANES), 1)
    low = lane < SLOT_WIN
    acc[...] = x_ref[0]

    for sb in range(nsub):
        toks = slice(sb * TOK_BLK, (sb + 1) * TOK_BLK)
        pos = pos_ref[0, toks, :]
        starts, rounds = _slot_windows(off_ref, b * E, E, ntb, t * nsub + sb)

        def one_round(k, carry, toks=toks, pos=pos, starts=starts):
            onehot = []
            for p in range(E // 2):
                firsts, row0s = [], []
                for e in (2 * p, 2 * p + 1):
                    first = starts[e] + k * SLOT_WIN
                    row0 = pl.multiple_of(jnp.minimum(first, cap - SLOT_WIN), BF16_ROWS)
                    wstack[e * SLOT_WIN:(e + 1) * SLOT_WIN, :] = (
                        eo_ref[0, e, pl.ds(row0, SLOT_WIN), :])
                    firsts.append(first)
                    row0s.append(row0)
                slot = lane + jnp.where(low, row0s[0], row0s[1] - SLOT_WIN)
                first = jnp.where(low, firsts[0], firsts[1])
                pcol = jnp.where(low, pos[:, 2 * p:2 * p + 1], pos[:, 2 * p + 1:2 * p + 2])
                hit = (slot == pcol) & (slot >= first)
                onehot.append(jnp.where(hit, 1.0, 0.0).astype(BF16))
            acc[toks, :] += jnp.dot(jnp.concatenate(onehot, axis=1), wstack[...],
                                    preferred_element_type=F32)
            return carry

        lax.fori_loop(0, rounds, one_round, 0)
    o_ref[0] = _rms(acc[...], g_ref[...])


def _combine(offs, x2, pos_t, eo, g_final, cap):
    B, S, D = x2.shape
    E = pos_t.shape[2]
    ntb = S // TOK_BLK
    return pl.pallas_call(
        functools.partial(_combine_kernel, cap=cap, ntb=ntb),
        out_shape=jax.ShapeDtypeStruct((B, S, D), F32),
        grid_spec=pltpu.PrefetchScalarGridSpec(
            num_scalar_prefetch=1,
            grid=(B, S // COMBINE_TOK),
            in_specs=[
                pl.BlockSpec((1, COMBINE_TOK, D), lambda b, t, off: (b, t, 0)),
                pl.BlockSpec((1, COMBINE_TOK, E), lambda b, t, off: (b, t, 0)),
                pl.BlockSpec((1, E, cap, D), lambda b, t, off: (b, 0, 0, 0),
                             pipeline_mode=pl.Buffered(1)),
                pl.BlockSpec((1, D), lambda b, t, off: (0, 0)),
            ],
            out_specs=pl.BlockSpec((1, COMBINE_TOK, D), lambda b, t, off: (b, t, 0)),
            scratch_shapes=[pltpu.VMEM((E * SLOT_WIN, D), BF16),
                            pltpu.VMEM((COMBINE_TOK, D), F32)],
        ),
        compiler_params=_cparams(("parallel", "arbitrary")),
        name="combine",
    )(offs, x2, pos_t, eo, g_final)


def _layer(x, norm_mix, w_in, b_in, conv_w, conv_b, rg_w, rg_b, rg_lambda, p_rnn, p_attn,
           w_out, norm_ffn, w_router, b_router, w_gate, w_up, w_down, norm_out, cos_t, sin_t):
    B, S, D = x.shape
    T = B * S
    x2d = x.reshape(T, D)
    n_rnn = 2 * D_RNN
    n_qkv = 3 * N_GROUPS * ATTN_WIDTH
    gmix = norm_mix.reshape(1, D)
    w_in_b = w_in.astype(BF16)
    b_in2 = b_in.reshape(1, -1)

    xrgr = _proj_f32(x2d, gmix, w_in_b[:, :n_rnn], b_in2[:, :n_rnn])

    def group_cols(a, g):
        part = lambda p: a[:, n_rnn + (p * N_GROUPS + g) * ATTN_WIDTH:
                           n_rnn + (p * N_GROUPS + g + 1) * ATTN_WIDTH]
        return jnp.concatenate([part(0), part(1), part(2)], axis=1)

    ws = [group_cols(w_in_b, g) for g in range(N_GROUPS)]
    b_qkv = jnp.concatenate([group_cols(b_in2, g) for g in range(N_GROUPS)], axis=1)
    qkv = _proj_qkv(x2d, gmix, ws, b_qkv, cos_t, sin_t, B)

    nblk = D_RNN // RNN_BW
    gate_w = (0.5 * jnp.transpose(rg_w, (2, 3, 0, 1, 4))).reshape(nblk, RNN_BW, 4 * RNN_BW).astype(BF16)
    gate_b = 0.5 * jnp.transpose(rg_b.reshape(2, 2, nblk, RNN_BW), (2, 0, 1, 3)).reshape(nblk, 1, 4 * RNN_BW)
    lam = jnp.transpose(rg_lambda.reshape(2, nblk, RNN_BW), (1, 0, 2))
    y_rnn = _rglru(xrgr.reshape(B, S, n_rnn), conv_w, conv_b.reshape(1, D_RNN), gate_w, gate_b, lam)

    outs, lses = [], []
    for g, (_, dilation) in enumerate(ATTN_GROUPS):
        qkv_g = qkv[g].reshape(B, 1, S, 3 * ATTN_WIDTH) if dilation == 1 else qkv[g]
        o, lse = _attention_group(qkv_g)
        if dilation == 1:
            o, lse = o.reshape(T, ATTN_WIDTH), lse.reshape(T, LANES)
        outs.append(o)
        lses.append(lse)

    x2, h2, aff_t = _merge(
        x2d, y_rnn.reshape(T, D_RNN), outs, lses, gmix,
        w_in_b[:, n_rnn + n_qkv:], b_in2[:, n_rnn + n_qkv:],
        p_rnn.astype(BF16), p_attn.astype(BF16), w_out.astype(BF16),
        norm_ffn.reshape(1, D), w_router.astype(BF16), b_router.reshape(1, -1), B)

    E = w_router.shape[1]
    cap = CAPACITY_FACTOR * S // E
    pos4, off4 = _route(aff_t, cap)
    pos = pos4.reshape(B, E, S)
    blk_off = off4[:, :, ::TOK_BLK // LANES, 0]
    offs = jnp.concatenate([blk_off, jnp.full((B, E, 1), cap, jnp.int32)], axis=2).reshape(-1)
    xg = _gather(offs, pos, h2.reshape(B, S, D + LANES), cap)
    eo = _ffn(xg, w_gate.astype(BF16), w_up.astype(BF16), w_down.astype(BF16))
    pos_t = jnp.transpose(pos, (0, 2, 1))
    return _combine(offs, x2.reshape(B, S, D), pos_t, eo, norm_out.reshape(1, D), cap)


def kernel(x, norm_mix, w_in, b_in, conv_w, conv_b, rg_w, rg_b, rg_lambda, p_rnn, p_attn, w_out,
           norm_ffn, w_router, b_router, w_gate, w_up, w_down, norm_final):
    depth = w_in.shape[0]
    assert depth == 1, "final rmsnorm is fused into the single layer's combine step"
    cos_t, sin_t = _rope_tables(x.shape[1])
    return _layer(x, norm_mix[0], w_in[0], b_in[0], conv_w[0], conv_b[0], rg_w[0], rg_b[0],
                  rg_lambda[0], p_rnn[0], p_attn[0], w_out[0], norm_ffn[0], w_router[0],
                  b_router[0], w_gate[0], w_up[0], w_down[0], norm_final, cos_t, sin_t)
```

```python
import functools

import jax
import jax.numpy as jnp
from jax import lax
from jax.experimental import pallas as pl
from jax.experimental.pallas import tpu as pltpu

F32 = jnp.float32
BF16 = jnp.bfloat16

D_RNN = 1024
RNN_BW = 128
CONV_WIDTH = 4
LRU_C = 8.0
HEAD_DIM = 128
HEADS = 4
ATTN_WIDTH = HEADS * HEAD_DIM
ATTN_GROUPS = ((128, 1), (512, 4), (2048, 16))
N_GROUPS = 3
ROPE_THETA = 500000.0
ROPE_DIM = HEAD_DIM // 4
RADIUS = 64
NEG_INF = -1e30
N_EXPERTS = 16
CAPACITY_FACTOR = 2
RMS_EPS = 1e-6

LANES = 128
SUBLANES = 8
BF16_ROWS = 16
MXU_DIM = 256
VMEM_LIMIT = 56 * 1024 * 1024
F32_MIN_NORMAL = 1.1754944e-38
LOG2_E = 1.4426950408889634
LN_2 = 0.6931471805599453

PROJ_TILE = 1024
ROW_TILE = 512
SCAN_SEG = 128
SCAN_CHUNK = 512
ATT_QBLK = 1024
ATT_SUB = 128
TOK_BLK = 256
GATHER_TOK = 1024
COMBINE_TOK = 512
SLOT_SHIFT = 6
SLOT_WIN = 1 << SLOT_SHIFT
EXPERT_SPLIT = 2


def _cparams(sem):
    return pltpu.CompilerParams(dimension_semantics=sem, vmem_limit_bytes=VMEM_LIMIT)


def _rms(x, g):
    ms = jnp.mean(x * x, axis=-1, keepdims=True)
    return x * lax.rsqrt(ms + RMS_EPS) * g


def _pitch(d):
    return d if d <= 4 else d + 4


def _deinterleave(t, stage, d):
    n = t.shape[0]
    p = _pitch(d)
    if p == d:
        stage[0:n, :] = t
    else:
        for j in range(n // d):
            stage[j * p:j * p + d, :] = t[j * d:(j + 1) * d, :]
    return [stage[pl.ds(r, n // d, stride=p), :] for r in range(d)]


def _interleave(parts, stage, d):
    m = parts[0].shape[0]
    p = _pitch(d)
    for r in range(d):
        stage[pl.ds(r, m, stride=p), :] = parts[r]
    if p == d:
        return stage[0:m * d, :]
    return jnp.concatenate([stage[j * p:j * p + d, :] for j in range(m)], axis=0)


def _proj_f32_kernel(x_ref, g_ref, w_ref, b_ref, o_ref, *, n_chunk):
    hb = _rms(x_ref[...], g_ref[...]).astype(BF16)
    for c in range(o_ref.shape[1] // n_chunk):
        sl = slice(c * n_chunk, (c + 1) * n_chunk)
        o_ref[:, sl] = jnp.dot(hb, w_ref[:, sl], preferred_element_type=F32) + b_ref[:, sl]


def _proj_f32(x2d, g, w, b):
    T, D = x2d.shape
    N = w.shape[1]
    return pl.pallas_call(
        functools.partial(_proj_f32_kernel, n_chunk=512),
        out_shape=jax.ShapeDtypeStruct((T, N), F32),
        grid=(T // PROJ_TILE,),
        in_specs=[
            pl.BlockSpec((PROJ_TILE, D), lambda i: (i, 0)),
            pl.BlockSpec((1, D), lambda i: (0, 0)),
            pl.BlockSpec((D, N), lambda i: (0, 0)),
            pl.BlockSpec((1, N), lambda i: (0, 0)),
        ],
        out_specs=pl.BlockSpec((PROJ_TILE, N), lambda i: (i, 0)),
        compiler_params=_cparams(("parallel",)),
        name="proj_rnn",
    )(x2d, g, w, b)


def _proj_qkv_kernel(x_ref, g_ref, w0_ref, w1_ref, w2_ref, b_ref, cos_ref, sin_ref,
                     o0_ref, o1_ref, o2_ref, stage):
    hb = _rms(x_ref[...], g_ref[...]).astype(BF16)
    cosv = cos_ref[...]
    sinv = sin_ref[...]
    lane = lax.broadcasted_iota(jnp.int32, (1, HEAD_DIM), 1)
    first_half = lane < (ROPE_DIM // 2)
    gw = 3 * ATTN_WIDTH
    for g, (w_ref, o_ref) in enumerate(((w0_ref, o0_ref), (w1_ref, o1_ref), (w2_ref, o2_ref))):
        d = ATTN_GROUPS[g][1]
        for c in range(gw // MXU_DIM):
            sl = slice(c * MXU_DIM, (c + 1) * MXU_DIM)
            t2 = jnp.dot(hb, w_ref[:, sl], preferred_element_type=F32)
            t2 = t2 + b_ref[:, g * gw + c * MXU_DIM:g * gw + (c + 1) * MXU_DIM]
            for hh in range(MXU_DIM // HEAD_DIM):
                t = t2[:, hh * HEAD_DIM:(hh + 1) * HEAD_DIM]
                col = c * MXU_DIM + hh * HEAD_DIM
                if col < 2 * ATTN_WIDTH:
                    up = pltpu.roll(t, HEAD_DIM - ROPE_DIM // 2, 1)
                    down = pltpu.roll(t, ROPE_DIM // 2, 1)
                    t = t * cosv + jnp.where(first_half, up, down) * sinv
                cs = slice(col, col + HEAD_DIM)
                if d == 1:
                    o_ref[:, cs] = t.astype(BF16)
                else:
                    for r, part in enumerate(_deinterleave(t, stage, d)):
                        o_ref[0, r, :, cs] = part.astype(BF16)


def _proj_qkv(x2d, g, ws, b, cos_t, sin_t, batch):
    T, D = x2d.shape
    S = T // batch
    nseq = S // PROJ_TILE
    gw = 3 * ATTN_WIDTH
    d1, d2 = ATTN_GROUPS[1][1], ATTN_GROUPS[2][1]
    full = lambda a: pl.BlockSpec(a.shape, lambda i: (0,) * a.ndim)
    dil_spec = lambda d: pl.BlockSpec((1, d, PROJ_TILE // d, gw),
                                      lambda i: (i // nseq, 0, i % nseq, 0))
    return pl.pallas_call(
        _proj_qkv_kernel,
        out_shape=(jax.ShapeDtypeStruct((T, gw), BF16),
                   jax.ShapeDtypeStruct((batch, d1, S // d1, gw), BF16),
                   jax.ShapeDtypeStruct((batch, d2, S // d2, gw), BF16)),
        grid=(T // PROJ_TILE,),
        in_specs=[
            pl.BlockSpec((PROJ_TILE, D), lambda i: (i, 0)),
            full(g), full(ws[0]), full(ws[1]), full(ws[2]), full(b),
            pl.BlockSpec((PROJ_TILE, HEAD_DIM), lambda i: (i % nseq, 0)),
            pl.BlockSpec((PROJ_TILE, HEAD_DIM), lambda i: (i % nseq, 0)),
        ],
        out_specs=(pl.BlockSpec((PROJ_TILE, gw), lambda i: (i, 0)), dil_spec(d1), dil_spec(d2)),
        scratch_shapes=[pltpu.VMEM((PROJ_TILE // d2 * _pitch(d2), LANES), F32)],
        compiler_params=_cparams(("parallel",)),
        name="proj_qkv",
    )(x2d, g, ws[0], ws[1], ws[2], b, cos_t, sin_t)


def _rope_tables(seq):
    pos = jnp.arange(seq, dtype=F32)
    inv = ROPE_THETA ** (-jnp.arange(0, ROPE_DIM, 2, dtype=F32) / ROPE_DIM)
    ang = pos[:, None] * inv[None, :]
    cos, sin = jnp.cos(ang), jnp.sin(ang)
    pad = HEAD_DIM - ROPE_DIM
    cos_t = jnp.concatenate([cos, cos, jnp.ones((seq, pad), F32)], axis=1)
    sin_t = jnp.concatenate([-sin, sin, jnp.zeros((seq, pad), F32)], axis=1)
    return cos_t, sin_t


def _sigmoid(x):
    return 0.5 * jnp.tanh(0.5 * x) + 0.5


def _rglru_kernel(xr_ref, gr_ref, cw_ref, cb_ref, gw_ref, gb_ref, lam_ref, o_ref,
                  xpad, af, uf, ab, ub, cf, cbk, *, seq):
    nseg = seq // SCAN_SEG
    nchunk = seq // SCAN_CHUNK
    seg_per_chunk = SCAN_CHUNK // SCAN_SEG
    pitch = nseg + 4

    cw = cw_ref[...]
    cbias = cb_ref[...]
    gbias = gb_ref[...]
    z = -lam_ref[...]
    sp = jnp.maximum(z, 0.0) + jnp.log1p(jnp.exp(-jnp.abs(z)))
    half_neg_c_sp = (-0.5 * LRU_C) * sp

    zero_rows = jnp.zeros((SUBLANES, LANES), F32)
    xpad[0:SUBLANES, :] = zero_rows
    xpad[seq + SUBLANES:seq + 2 * SUBLANES, :] = zero_rows

    def copy_in(c, carry):
        t0 = pl.multiple_of(c * SCAN_CHUNK, SCAN_CHUNK)
        xpad[pl.ds(t0 + SUBLANES, SCAN_CHUNK), :] = xr_ref[0, pl.ds(t0, SCAN_CHUNK), :]
        return carry

    lax.fori_loop(0, nchunk, copy_in, 0)

    def stage1(c, carry):
        t0 = c * SCAN_CHUNK
        xc = cbias
        for tap in range(CONV_WIDTH):
            lo = t0 + SUBLANES - CONV_WIDTH // 2 + tap
            xc = xc + cw[tap:tap + 1, :] * xpad[pl.ds(lo, SCAN_CHUNK), :]
        th = jnp.tanh(jnp.dot(xc.astype(BF16), gw_ref[...], preferred_element_type=F32) + gbias)
        xc_half = 0.5 * xc
        for d, (a_s, u_s) in enumerate(((af, uf), (ab, ub))):
            t_r = th[:, (2 * d) * LANES:(2 * d + 1) * LANES]
            t_i = th[:, (2 * d + 1) * LANES:(2 * d + 2) * LANES]
            half_k = half_neg_c_sp[d:d + 1, :]
            log_a = t_r * half_k + half_k
            a = jnp.exp(log_a)
            v = jnp.tanh(log_a) * (-1.0 - a * a)
            u = (xc_half * (v * lax.rsqrt(jnp.maximum(v, F32_MIN_NORMAL)))) * (t_i + 1.0)
            for k in range(seg_per_chunk):
                seg = c * seg_per_chunk + k
                a_s[pl.ds(seg, SCAN_SEG, stride=pitch), :] = a[k * SCAN_SEG:(k + 1) * SCAN_SEG, :]
                u_s[pl.ds(seg, SCAN_SEG, stride=pitch), :] = u[k * SCAN_SEG:(k + 1) * SCAN_SEG, :]
        return carry

    lax.fori_loop(0, nchunk, stage1, 0, unroll=4)

    def scan_dir(a_s, u_s, reverse):
        def body(it, carry):
            i = (SCAN_SEG - 1 - it) if reverse else it
            h, cum = carry
            rows = pl.ds(i * pitch, nseg)
            a = a_s[rows, :]
            u = u_s[rows, :]
            h = a * h + u
            cum = a * cum
            u_s[rows, :] = h
            a_s[rows, :] = cum
            return h, cum

        return lax.fori_loop(0, SCAN_SEG, body,
                             (jnp.zeros((nseg, LANES), F32), jnp.ones((nseg, LANES), F32)))

    end_h, end_a = scan_dir(af, uf, False)
    beg_h, beg_a = scan_dir(ab, ub, True)

    c = jnp.zeros((1, LANES), F32)
    for s in range(nseg):
        cf[s:s + 1, :] = c
        c = end_h[s:s + 1, :] + end_a[s:s + 1, :] * c
    c = jnp.zeros((1, LANES), F32)
    for s in range(nseg - 1, -1, -1):
        cbk[s:s + 1, :] = c
        c = beg_h[s:s + 1, :] + beg_a[s:s + 1, :] * c

    def finish(s, carry):
        t0 = pl.multiple_of(s * SCAN_SEG, SCAN_SEG)
        rows = pl.ds(s, SCAN_SEG, stride=pitch)
        hf = uf[rows, :] + af[rows, :] * cf[pl.ds(s, 1), :]
        hb = ub[rows, :] + ab[rows, :] * cbk[pl.ds(s, 1), :]
        g = jax.nn.gelu(gr_ref[0, pl.ds(t0, SCAN_SEG), :])
        o_ref[0, pl.ds(t0, SCAN_SEG), :] = (g * (hf + hb)).astype(BF16)
        return carry

    lax.fori_loop(0, nseg, finish, 0)


def _rglru(xrgr, conv_w, conv_b, gate_w, gate_b, lam):
    B, S, _ = xrgr.shape
    nblk = D_RNN // RNN_BW
    nseg = S // SCAN_SEG
    scr = pltpu.VMEM((SCAN_SEG * (nseg + 4), LANES), F32)
    return pl.pallas_call(
        functools.partial(_rglru_kernel, seq=S),
        out_shape=jax.ShapeDtypeStruct((B, S, D_RNN), BF16),
        grid=(B, nblk),
        in_specs=[
            pl.BlockSpec((1, S, RNN_BW), lambda b, c: (b, 0, c)),
            pl.BlockSpec((1, S, RNN_BW), lambda b, c: (b, 0, nblk + c)),
            pl.BlockSpec((CONV_WIDTH, RNN_BW), lambda b, c: (0, c)),
            pl.BlockSpec((1, RNN_BW), lambda b, c: (0, c)),
            pl.BlockSpec((None, RNN_BW, 4 * RNN_BW), lambda b, c: (c, 0, 0)),
            pl.BlockSpec((None, 1, 4 * RNN_BW), lambda b, c: (c, 0, 0)),
            pl.BlockSpec((None, 2, RNN_BW), lambda b, c: (c, 0, 0)),
        ],
        out_specs=pl.BlockSpec((1, S, RNN_BW), lambda b, c: (b, 0, c)),
        scratch_shapes=[pltpu.VMEM((S + 2 * SUBLANES, LANES), F32), scr, scr, scr, scr,
                        pltpu.VMEM((nseg, LANES), F32), pltpu.VMEM((nseg, LANES), F32)],
        compiler_params=_cparams(("parallel", "parallel")),
        name="rglru",
    )(xrgr, xrgr, conv_w, conv_b, gate_w, gate_b, lam)


def _attn_kernel(q_ref, k_ref, kp_ref, kn_ref, v_ref, vp_ref, vn_ref, o_ref, lse_ref,
                 kext, vext, *, qblk, length):
    j = pl.program_id(2)
    kext[0:RADIUS, :] = kp_ref[0, 0]
    kext[RADIUS:RADIUS + qblk, :] = k_ref[0, 0]
    kext[RADIUS + qblk:RADIUS + qblk + RADIUS, :] = kn_ref[0, 0]
    for h in range(HEADS):
        src = slice(h * HEAD_DIM, (h + 1) * HEAD_DIM)
        dst = slice(2 * h * HEAD_DIM, (2 * h + 1) * HEAD_DIM)
        vext[0:RADIUS, dst] = vp_ref[0, 0, :, src]
        vext[RADIUS:RADIUS + qblk, dst] = v_ref[0, 0, :, src]
        vext[RADIUS + qblk:RADIUS + qblk + RADIUS, dst] = vn_ref[0, 0, :, src]
        vext[:, (2 * h + 1) * HEAD_DIM:(2 * h + 2) * HEAD_DIM] = jnp.ones(
            (qblk + 2 * RADIUS, HEAD_DIM), BF16)

    nkey = ATT_SUB + 2 * RADIUS
    qi = lax.broadcasted_iota(jnp.int32, (ATT_SUB, nkey), 0)
    ki = lax.broadcasted_iota(jnp.int32, (ATT_SUB, nkey), 1)
    band_bias = jnp.where(jnp.abs(ki - RADIUS - qi) <= RADIUS, 0.0, NEG_INF)
    krow = lax.broadcasted_iota(jnp.int32, (1, nkey), 1)
    lane = lax.broadcasted_iota(jnp.int32, (ATT_SUB, LANES), 1)
    scale2 = HEAD_DIM ** -0.5 * LOG2_E

    def sub(i, carry):
        q0 = pl.multiple_of(i * ATT_SUB, ATT_SUB)
        kglob = j * qblk + q0 - RADIUS + krow
        edge_bias = jnp.where((kglob >= 0) & (kglob < length), 0.0, NEG_INF)
        bias = band_bias + edge_bias
        lse_tile = jnp.zeros((ATT_SUB, LANES), F32)
        for h in range(HEADS):
            hs = slice(h * HEAD_DIM, (h + 1) * HEAD_DIM)
            q = q_ref[0, 0, pl.ds(q0, ATT_SUB), hs]
            ks = kext[pl.ds(q0, nkey), hs]
            vs = vext[pl.ds(q0, nkey), 2 * h * HEAD_DIM:(2 * h + 2) * HEAD_DIM]
            s = lax.dot_general(q, ks, (((1,), (1,)), ((), ())),
                                preferred_element_type=F32) * scale2 + bias
            m = jnp.max(s, axis=-1, keepdims=True)
            p = jnp.exp2(s - m)
            pv = jnp.dot(p.astype(BF16), vs, preferred_element_type=F32)
            l = pv[:, HEAD_DIM:2 * HEAD_DIM]
            o_ref[0, 0, pl.ds(q0, ATT_SUB), hs] = pv[:, 0:HEAD_DIM] / l
            lse_tile = jnp.where(lane == h, m * LN_2 + jnp.log(l), lse_tile)
        lse_ref[0, 0, pl.ds(q0, ATT_SUB), :] = lse_tile
        return carry

    lax.fori_loop(0, qblk // ATT_SUB, sub, 0, unroll=True)


def _attention_group(qkv_g):
    B, d, L, _ = qkv_g.shape
    qblk = min(L, ATT_QBLK)
    nblk = L // qblk
    halo_per_blk = qblk // RADIUS
    last_halo = L // RADIUS - 1

    main = lambda part: pl.BlockSpec((1, 1, qblk, ATTN_WIDTH), lambda b, r, j: (b, r, j, part))
    prev = lambda part: pl.BlockSpec(
        (1, 1, RADIUS, ATTN_WIDTH),
        lambda b, r, j: (b, r, jnp.maximum(j * halo_per_blk - 1, 0), part))
    nxt = lambda part: pl.BlockSpec(
        (1, 1, RADIUS, ATTN_WIDTH),
        lambda b, r, j: (b, r, jnp.minimum((j + 1) * halo_per_blk, last_halo), part))
    return pl.pallas_call(
        functools.partial(_attn_kernel, qblk=qblk, length=L),
        out_shape=(jax.ShapeDtypeStruct((B, d, L, ATTN_WIDTH), F32),
                   jax.ShapeDtypeStruct((B, d, L, LANES), F32)),
        grid=(B, d, nblk),
        in_specs=[main(0), main(1), prev(1), nxt(1), main(2), prev(2), nxt(2)],
        out_specs=(pl.BlockSpec((1, 1, qblk, ATTN_WIDTH), lambda b, r, j: (b, r, j, 0)),
                   pl.BlockSpec((1, 1, qblk, LANES), lambda b, r, j: (b, r, j, 0))),
        scratch_shapes=[pltpu.VMEM((qblk + 2 * RADIUS, ATTN_WIDTH), BF16),
                        pltpu.VMEM((qblk + 2 * RADIUS, 2 * ATTN_WIDTH), BF16)],
        compiler_params=_cparams(("parallel", "parallel", "parallel")),
        name=f"attn_d{d}",
    )(qkv_g, qkv_g, qkv_g, qkv_g, qkv_g, qkv_g, qkv_g)


def _merge_kernel(x_ref, yr_ref, o0_ref, o1_ref, o2_ref, l0_ref, l1_ref, l2_ref,
                  gmix_ref, wg_ref, bg_ref, prnn_ref, pattn_ref, wout_ref,
                  gffn_ref, wr_ref, wrt_ref, br_ref, brt_ref,
                  x2_ref, h2_ref, afft_ref, yattn, stage):
    d = x_ref.shape[1]
    x = x_ref[...]
    hb = _rms(x, gmix_ref[...]).astype(BF16)

    def natural(ref, g, cs):
        dil = ATTN_GROUPS[g][1]
        if dil == 1:
            return ref[:, cs]
        return _interleave([ref[0, r, :, cs] for r in range(dil)], stage, dil)

    o_refs = (o0_ref, o1_ref, o2_ref)
    all_lanes = slice(0, LANES)
    lses = [natural(r, g, all_lanes) for g, r in enumerate((l0_ref, l1_ref, l2_ref))]
    for h in range(HEADS):
        hs = slice(h * HEAD_DIM, (h + 1) * HEAD_DIM)
        cols = [l[:, h:h + 1] for l in lses]
        m = jnp.maximum(jnp.maximum(cols[0], cols[1]), cols[2])
        ws = [jnp.exp(c - m) for c in cols]
        den = ws[0] + ws[1] + ws[2]
        acc = (ws[0] / den) * natural(o_refs[0], 0, hs)
        acc = acc + (ws[1] / den) * natural(o_refs[1], 1, hs)
        acc = acc + (ws[2] / den) * natural(o_refs[2], 2, hs)
        yattn[:, hs] = acc.astype(BF16)

    branch_a = jnp.dot(yr_ref[...], prnn_ref[...], preferred_element_type=F32)
    branch_b = jnp.dot(yattn[...], pattn_ref[...], preferred_element_type=F32)
    ga = jnp.dot(hb, wg_ref[:, 0:d], preferred_element_type=F32) + bg_ref[:, 0:d]
    gb = jnp.dot(hb, wg_ref[:, d:2 * d], preferred_element_type=F32) + bg_ref[:, d:2 * d]
    merged = _sigmoid(ga) * branch_a + _sigmoid(gb) * branch_b
    x2 = x + jnp.dot(merged.astype(BF16), wout_ref[...], preferred_element_type=F32)
    x2_ref[...] = x2

    h2 = _rms(x2, gffn_ref[...]).astype(BF16)
    h2_ref[:, 0:d] = h2
    n_exp = afft_ref.shape[1]
    logits = jnp.dot(h2, wr_ref[...], preferred_element_type=F32) + br_ref[...]
    lane = lax.broadcasted_iota(jnp.int32, logits.shape, 1)
    mx = jnp.max(logits, axis=-1, keepdims=True)
    e = jnp.exp(logits - mx)
    den = jnp.sum(jnp.where(lane < n_exp, e, 0.0), axis=-1, keepdims=True)
    aff = e / den
    head = aff.astype(BF16).astype(F32)
    tile = jnp.where(lane < n_exp, head, jnp.where(lane < 2 * n_exp, aff - head, 0.0))
    h2_ref[:, d:d + LANES] = tile.astype(BF16)
    logits_t = lax.dot_general(wrt_ref[...], h2, (((1,), (1,)), ((), ())),
                               preferred_element_type=F32) + brt_ref[...]
    mxt = jnp.max(logits_t, axis=0, keepdims=True)
    et = jnp.exp(logits_t - mxt)
    afft_ref[0] = et / jnp.sum(et, axis=0, keepdims=True)


def _merge(x2d, y_rnn, outs, lses, gmix, wg, bg, p_rnn, p_attn, w_out, gffn, wr, br, batch):
    T, D = x2d.shape
    S = T // batch
    nseq = S // ROW_TILE
    E = wr.shape[1]
    row = lambda w: pl.BlockSpec((ROW_TILE, w), lambda i: (i, 0))
    full = lambda a: pl.BlockSpec(a.shape, lambda i: (0,) * a.ndim)

    def attn_spec(g, w):
        dil = ATTN_GROUPS[g][1]
        if dil == 1:
            return row(w)
        return pl.BlockSpec((1, dil, ROW_TILE // dil, w), lambda i: (i // nseq, 0, i % nseq, 0))

    wrt = wr.T
    brt = br.reshape(E, 1)
    wr_pad = jnp.concatenate([wr, wr, jnp.zeros((D, LANES - 2 * E), wr.dtype)], axis=1)
    br_pad = jnp.concatenate([br, br, jnp.full((1, LANES - 2 * E), NEG_INF, br.dtype)], axis=1)
    consts = [gmix, wg, bg, p_rnn, p_attn, w_out, gffn, wr_pad, wrt, br_pad, brt]
    args = [x2d, y_rnn, *outs, *lses, *consts]
    in_specs = ([row(D), row(D_RNN)] + [attn_spec(g, ATTN_WIDTH) for g in range(N_GROUPS)]
                + [attn_spec(g, LANES) for g in range(N_GROUPS)] + [full(a) for a in consts])
    dmax = max(dil for _, dil in ATTN_GROUPS)
    return pl.pallas_call(
        _merge_kernel,
        out_shape=(jax.ShapeDtypeStruct((T, D), F32),
                   jax.ShapeDtypeStruct((T, D + LANES), BF16),
                   jax.ShapeDtypeStruct((batch, E, S), F32)),
        grid=(T // ROW_TILE,),
        in_specs=in_specs,
        out_specs=(row(D), row(D + LANES),
                   pl.BlockSpec((1, E, ROW_TILE), lambda i: (i // nseq, 0, i % nseq))),
        scratch_shapes=[pltpu.VMEM((ROW_TILE, ATTN_WIDTH), BF16),
                        pltpu.VMEM((ROW_TILE // dmax * _pitch(dmax), LANES), F32)],
        compiler_params=_cparams(("parallel",)),
        name="merge",
    )(*args)


def _route_kernel(aff_ref, pos_ref, off_ref, *, cap):
    a = aff_ref[0]
    E, R, _ = a.shape
    bits = pltpu.bitcast(a, jnp.int32)

    def count(mask):
        c = jnp.sum(mask.astype(F32), axis=2, keepdims=True)
        return jnp.sum(c, axis=1, keepdims=True)

    def bs(_, carry):
        lo, hi = carry
        mid = lo + ((hi - lo + 1) >> 1)
        ok = count(bits >= mid) >= cap
        return jnp.where(ok, mid, lo), jnp.where(ok, hi, mid - 1)

    lo0 = jnp.zeros((E, 1, 1), jnp.int32)
    hi0 = jnp.full((E, 1, 1), 0x7F7FFFFF, jnp.int32)
    thr, _ = lax.fori_loop(0, 31, bs, (lo0, hi0))

    gt = bits > thr
    eq = bits == thr
    need = cap - count(gt)

    ki = lax.broadcasted_iota(jnp.int32, (LANES, LANES), 0)
    li = lax.broadcasted_iota(jnp.int32, (LANES, LANES), 1)
    upper = (ki <= li).astype(BF16)
    allones = jnp.ones((LANES, LANES), BF16)
    ri = lax.broadcasted_iota(jnp.int32, (R, R), 0)
    ci = lax.broadcasted_iota(jnp.int32, (R, R), 1)
    strict = (ci < ri).astype(BF16)

    def prefix(mask):
        mb = mask.astype(BF16).reshape(E * R, LANES)
        within = jnp.dot(mb, upper, preferred_element_type=F32)
        totals = jnp.dot(mb, allones, preferred_element_type=F32).astype(BF16)
        offs = []
        for e in range(E):
            offs.append(jnp.dot(strict, totals[e * R:(e + 1) * R, :],
                                preferred_element_type=F32))
        rowoff = jnp.concatenate(offs, axis=0)
        excl = within - mb.astype(F32) + rowoff
        return excl.reshape(E, R, LANES), rowoff.reshape(E, R, LANES)

    eq_rank, _ = prefix(eq)
    sel = gt | (eq & (eq_rank < need.astype(F32)))
    slot, rowoff = prefix(sel)
    pos_ref[0] = jnp.where(sel, slot.astype(jnp.int32), -1)
    off_ref[0] = rowoff.astype(jnp.int32)


def _route(aff_t, cap):
    B, E, S = aff_t.shape
    R = S // LANES
    a4 = aff_t.reshape(B, E, R, LANES)
    blk = pl.BlockSpec((1, E, R, LANES), lambda b: (b, 0, 0, 0))
    pos, off = pl.pallas_call(
        functools.partial(_route_kernel, cap=cap),
        out_shape=(jax.ShapeDtypeStruct((B, E, R, LANES), jnp.int32),
                   jax.ShapeDtypeStruct((B, E, R, LANES), jnp.int32)),
        grid=(B,),
        in_specs=[blk],
        out_specs=(blk, blk),
        compiler_params=_cparams(("parallel",)),
        name="route",
    )(a4)
    return pos, off


def _slot_windows(off_ref, base, n_exp, ntb, tb):
    starts = []
    rounds = jnp.int32(0)
    for e in range(n_exp):
        idx = (base + e) * (ntb + 1) + tb
        start = (off_ref[idx] >> 4) << 4
        starts.append(start)
        rounds = jnp.maximum(rounds, (off_ref[idx + 1] - start + SLOT_WIN - 1) >> SLOT_SHIFT)
    return starts, rounds


def _gather_kernel(off_ref, pos_ref, h_ref, o_ref, *, cap, ntb, n_exp, nsub):
    b = pl.program_id(0)
    grp = pl.program_id(1)
    t = pl.program_id(2)

    @pl.when(t == 0)
    def _():
        o_ref[...] = jnp.zeros_like(o_ref)

    base = (b * pl.num_programs(1) + grp) * n_exp
    ji = lax.broadcasted_iota(jnp.int32, (SLOT_WIN, TOK_BLK), 0)
    for sb in range(nsub):
        toks = slice(sb * TOK_BLK, (sb + 1) * TOK_BLK)
        starts, rounds = _slot_windows(off_ref, base, n_exp, ntb, t * nsub + sb)

        def one_round(k, carry, toks=toks, starts=starts):
            pieces, rows = [], []
            for e in range(n_exp):
                first = starts[e] + k * SLOT_WIN
                row0 = pl.multiple_of(jnp.minimum(first, cap - SLOT_WIN), BF16_ROWS)
                slot = ji + row0
                hit = (slot == pos_ref[0, 0, e:e + 1, toks]) & (slot >= first)
                pieces.append(jnp.where(hit, 1.0, 0.0).astype(BF16))
                rows.append(row0)
            onehot = jnp.concatenate(pieces, axis=0)
            g = jnp.dot(onehot, h_ref[0, toks, :], preferred_element_type=F32)
            for e in range(n_exp):
                o_ref[0, e, pl.ds(rows[e], SLOT_WIN), :] += (
                    g[e * SLOT_WIN:(e + 1) * SLOT_WIN, :].astype(BF16))
            return carry

        one_round(0, 0)
        lax.fori_loop(1, rounds, one_round, 0)


def _gather(offs, pos, h2, cap):
    B, E, S = pos.shape
    D = h2.shape[-1]
    ntb = S // TOK_BLK
    n_exp = E // EXPERT_SPLIT
    nsub = GATHER_TOK // TOK_BLK
    pos4 = pos.reshape(B, EXPERT_SPLIT, n_exp, S)
    return pl.pallas_call(
        functools.partial(_gather_kernel, cap=cap, ntb=ntb, n_exp=n_exp, nsub=nsub),
        out_shape=jax.ShapeDtypeStruct((B, E, cap, D), BF16),
        grid_spec=pltpu.PrefetchScalarGridSpec(
            num_scalar_prefetch=1,
            grid=(B, EXPERT_SPLIT, S // GATHER_TOK),
            in_specs=[
                pl.BlockSpec((1, 1, n_exp, GATHER_TOK), lambda b, g, t, off: (b, g, 0, t)),
                pl.BlockSpec((1, GATHER_TOK, D), lambda b, g, t, off: (b, t, 0)),
            ],
            out_specs=pl.BlockSpec((1, n_exp, cap, D), lambda b, g, t, off: (b, g, 0, 0)),
        ),
        compiler_params=_cparams(("parallel", "parallel", "arbitrary")),
        name="gather",
    )(offs, pos4, h2)


def _ffn_kernel(x_ref, wg_ref, wu_ref, wd_ref, o_ref, acc, *, f_chunk):
    e = pl.program_id(0)
    d = wg_ref.shape[1]
    x = x_ref[0, 0, :, 0:d]
    gl = x_ref[0, 0, :, d:d + LANES].astype(F32)
    lane = lax.broadcasted_iota(jnp.int32, gl.shape, 1)
    gate = jnp.sum(jnp.where((lane == e) | (lane == e + pl.num_programs(0)), gl, 0.0),
                   axis=1, keepdims=True)
    nf = wg_ref.shape[2] // f_chunk
    for f in range(nf):
        fs = slice(f * f_chunk, (f + 1) * f_chunk)
        g = jnp.dot(x, wg_ref[0, :, fs], preferred_element_type=F32)
        u = jnp.dot(x, wu_ref[0, :, fs], preferred_element_type=F32)
        hid = (g * _sigmoid(g) * u).astype(BF16)
        part = jnp.dot(hid, wd_ref[0, fs, :], preferred_element_type=F32)
        if f == 0:
            acc[...] = part
        else:
            acc[...] += part
    o_ref[0, 0] = (acc[...] * gate).astype(BF16)


def _ffn(xg, w_gate, w_up, w_down):
    B, E, C, DX = xg.shape
    D = w_gate.shape[1]
    F = w_gate.shape[2]
    return pl.pallas_call(
        functools.partial(_ffn_kernel, f_chunk=512),
        out_shape=jax.ShapeDtypeStruct((B, E, C, D), BF16),
        grid=(E, B),
        in_specs=[
            pl.BlockSpec((1, 1, C, DX), lambda e, b: (b, e, 0, 0)),
            pl.BlockSpec((1, D, F), lambda e, b: (e, 0, 0)),
            pl.BlockSpec((1, D, F), lambda e, b: (e, 0, 0)),
            pl.BlockSpec((1, F, D), lambda e, b: (e, 0, 0)),
        ],
        out_specs=pl.BlockSpec((1, 1, C, D), lambda e, b: (b, e, 0, 0)),
        scratch_shapes=[pltpu.VMEM((C, D), F32)],
        compiler_params=_cparams(("arbitrary", "arbitrary")),
        name="ffn",
    )(xg, w_gate, w_up, w_down)


def _combine_kernel(off_ref, x_ref, pos_ref, eo_ref, g_ref, o_ref, wstack, acc, *, cap, ntb):
    b = pl.program_id(0)
    t = pl.program_id(1)
    E = pos_ref.shape[2]
    nsub = pos_ref.shape[1] // TOK_BLK
    lane = lax.broadcasted_iota(jnp.int32, (TOK_BLK, LANES), 1)
    low = lane < SLOT_WIN
    acc[...] = x_ref[0]

    for sb in range(nsub):
        toks = slice(sb * TOK_BLK, (sb + 1) * TOK_BLK)
        pos = pos_ref[0, toks, :]
        starts, rounds = _slot_windows(off_ref, b * E, E, ntb, t * nsub + sb)

        def one_round(k, carry, toks=toks, pos=pos, starts=starts, wst=wstack.at[sb]):
            onehot = []
            for p in range(E // 2):
                firsts, row0s = [], []
                for e in (2 * p, 2 * p + 1):
                    first = starts[e] + k * SLOT_WIN
                    row0 = pl.multiple_of(jnp.minimum(first, cap - SLOT_WIN), BF16_ROWS)
                    wst[e * SLOT_WIN:(e + 1) * SLOT_WIN, :] = (
                        eo_ref[0, e, pl.ds(row0, SLOT_WIN), :])
                    firsts.append(first)
                    row0s.append(row0)
                slot = lane + jnp.where(low, row0s[0], row0s[1] - SLOT_WIN)
                first = jnp.where(low, firsts[0], firsts[1])
                pcol = jnp.where(low, pos[:, 2 * p:2 * p + 1], pos[:, 2 * p + 1:2 * p + 2])
                hit = (slot == pcol) & (slot >= first)
                onehot.append(jnp.where(hit, 1.0, 0.0).astype(BF16))
            acc[toks, :] += jnp.dot(jnp.concatenate(onehot, axis=1), wst[...],
                                    preferred_element_type=F32)
            return carry

        one_round(0, 0)
        lax.fori_loop(1, rounds, one_round, 0)
    o_ref[0] = _rms(acc[...], g_ref[...])


def _combine(offs, x2, pos_t, eo, g_final, cap):
    B, S, D = x2.shape
    E = pos_t.shape[2]
    ntb = S // TOK_BLK
    return pl.pallas_call(
        functools.partial(_combine_kernel, cap=cap, ntb=ntb),
        out_shape=jax.ShapeDtypeStruct((B, S, D), F32),
        grid_spec=pltpu.PrefetchScalarGridSpec(
            num_scalar_prefetch=1,
            grid=(B, S // COMBINE_TOK),
            in_specs=[
                pl.BlockSpec((1, COMBINE_TOK, D), lambda b, t, off: (b, t, 0)),
                pl.BlockSpec((1, COMBINE_TOK, E), lambda b, t, off: (b, t, 0)),
                pl.BlockSpec((1, E, cap, D), lambda b, t, off: (b, 0, 0, 0),
                             pipeline_mode=pl.Buffered(1)),
                pl.BlockSpec((1, D), lambda b, t, off: (0, 0)),
            ],
            out_specs=pl.BlockSpec((1, COMBINE_TOK, D), lambda b, t, off: (b, t, 0)),
            scratch_shapes=[pltpu.VMEM((COMBINE_TOK // TOK_BLK, E * SLOT_WIN, D), BF16),
                            pltpu.VMEM((COMBINE_TOK, D), F32)],
        ),
        compiler_params=_cparams(("parallel", "arbitrary")),
        name="combine",
    )(offs, x2, pos_t, eo, g_final)


def _layer(x, norm_mix, w_in, b_in, conv_w, conv_b, rg_w, rg_b, rg_lambda, p_rnn, p_attn,
           w_out, norm_ffn, w_router, b_router, w_gate, w_up, w_down, norm_out, cos_t, sin_t):
    B, S, D = x.shape
    T = B * S
    x2d = x.reshape(T, D)
    n_rnn = 2 * D_RNN
    n_qkv = 3 * N_GROUPS * ATTN_WIDTH
    gmix = norm_mix.reshape(1, D)
    w_in_b = w_in.astype(BF16)
    b_in2 = b_in.reshape(1, -1)

    xrgr = _proj_f32(x2d, gmix, w_in_b[:, :n_rnn], b_in2[:, :n_rnn])

    def group_cols(a, g):
        part = lambda p: a[:, n_rnn + (p * N_GROUPS + g) * ATTN_WIDTH:
                           n_rnn + (p * N_GROUPS + g + 1) * ATTN_WIDTH]
        return jnp.concatenate([part(0), part(1), part(2)], axis=1)

    ws = [group_cols(w_in_b, g) for g in range(N_GROUPS)]
    b_qkv = jnp.concatenate([group_cols(b_in2, g) for g in range(N_GROUPS)], axis=1)
    qkv = _proj_qkv(x2d, gmix, ws, b_qkv, cos_t, sin_t, B)

    nblk = D_RNN // RNN_BW
    gate_w = (0.5 * jnp.transpose(rg_w, (2, 3, 0, 1, 4))).reshape(nblk, RNN_BW, 4 * RNN_BW).astype(BF16)
    gate_b = 0.5 * jnp.transpose(rg_b.reshape(2, 2, nblk, RNN_BW), (2, 0, 1, 3)).reshape(nblk, 1, 4 * RNN_BW)
    lam = jnp.transpose(rg_lambda.reshape(2, nblk, RNN_BW), (1, 0, 2))
    y_rnn = _rglru(xrgr.reshape(B, S, n_rnn), conv_w, conv_b.reshape(1, D_RNN), gate_w, gate_b, lam)

    outs, lses = [], []
    for g, (_, dilation) in enumerate(ATTN_GROUPS):
        qkv_g = qkv[g].reshape(B, 1, S, 3 * ATTN_WIDTH) if dilation == 1 else qkv[g]
        o, lse = _attention_group(qkv_g)
        if dilation == 1:
            o, lse = o.reshape(T, ATTN_WIDTH), lse.reshape(T, LANES)
        outs.append(o)
        lses.append(lse)

    x2, h2, aff_t = _merge(
        x2d, y_rnn.reshape(T, D_RNN), outs, lses, gmix,
        w_in_b[:, n_rnn + n_qkv:], b_in2[:, n_rnn + n_qkv:],
        p_rnn.astype(BF16), p_attn.astype(BF16), w_out.astype(BF16),
        norm_ffn.reshape(1, D), w_router.astype(BF16), b_router.reshape(1, -1), B)

    E = w_router.shape[1]
    cap = CAPACITY_FACTOR * S // E
    pos4, off4 = _route(aff_t, cap)
    pos = pos4.reshape(B, E, S)
    blk_off = off4[:, :, ::TOK_BLK // LANES, 0]
    offs = jnp.concatenate([blk_off, jnp.full((B, E, 1), cap, jnp.int32)], axis=2).reshape(-1)
    xg = _gather(offs, pos, h2.reshape(B, S, D + LANES), cap)
    eo = _ffn(xg, w_gate.astype(BF16), w_up.astype(BF16), w_down.astype(BF16))
    pos_t = jnp.transpose(pos, (0, 2, 1))
    return _combine(offs, x2.reshape(B, S, D), pos_t, eo, norm_out.reshape(1, D), cap)


def kernel(x, norm_mix, w_in, b_in, conv_w, conv_b, rg_w, rg_b, rg_lambda, p_rnn, p_attn, w_out,
           norm_ffn, w_router, b_router, w_gate, w_up, w_down, norm_final):
    depth = w_in.shape[0]
    assert depth == 1, "final rmsnorm is fused into the single layer's combine step"
    cos_t, sin_t = _rope_tables(x.shape[1])
    return _layer(x, norm_mix[0], w_in[0], b_in[0], conv_w[0], conv_b[0], rg_w[0], rg_b[0],
                  rg_lambda[0], p_rnn[0], p_attn[0], w_out[0], norm_ffn[0], w_router[0],
                  b_router[0], w_gate[0], w_up[0], w_down[0], norm_final, cos_t, sin_t)
```

```python
import functools

import jax
import jax.numpy as jnp
from jax import lax
from jax.experimental import pallas as pl
from jax.experimental.pallas import tpu as pltpu

F32 = jnp.float32
BF16 = jnp.bfloat16

D_RNN = 1024
RNN_BW = 128
CONV_WIDTH = 4
LRU_C = 8.0
HEAD_DIM = 128
HEADS = 4
ATTN_WIDTH = HEADS * HEAD_DIM
ATTN_GROUPS = ((128, 1), (512, 4), (2048, 16))
N_GROUPS = 3
ROPE_THETA = 500000.0
ROPE_DIM = HEAD_DIM // 4
RADIUS = 64
NEG_INF = -1e30
N_EXPERTS = 16
CAPACITY_FACTOR = 2
RMS_EPS = 1e-6

LANES = 128
SUBLANES = 8
BF16_ROWS = 16
VMEM_LIMIT = 56 * 1024 * 1024
F32_MIN_NORMAL = 1.1754944e-38
LOG2_E = 1.4426950408889634
LN_2 = 0.6931471805599453

PROJ_TILE = 1024
QKV_CHUNK = 512
ROW_TILE = 512
SCAN_SEG = 128
SCAN_CHUNK = 512
ATT_QBLK = 1024
ATT_SUB = 128
TOK_BLK = 256
GATHER_TOK = 1024
COMBINE_TOK = 512
SLOT_SHIFT = 6
SLOT_WIN = 1 << SLOT_SHIFT
EXPERT_SPLIT = 2
FFN_CHUNK = 512


def _cparams(sem):
    return pltpu.CompilerParams(dimension_semantics=sem, vmem_limit_bytes=VMEM_LIMIT)


def _rms(x, g):
    ms = jnp.mean(x * x, axis=-1, keepdims=True)
    return x * lax.rsqrt(ms + RMS_EPS) * g


def _pitch(d):
    return d if d <= 4 else d + 4


def _deinterleave(t, stage, d):
    n = t.shape[0]
    p = _pitch(d)
    if p == d:
        stage[0:n, :] = t
    else:
        for j in range(n // d):
            stage[j * p:j * p + d, :] = t[j * d:(j + 1) * d, :]
    return [stage[pl.ds(r, n // d, stride=p), :] for r in range(d)]


def _interleave(parts, stage, d):
    m = parts[0].shape[0]
    p = _pitch(d)
    for r in range(d):
        stage[pl.ds(r, m, stride=p), :] = parts[r]
    if p == d:
        return stage[0:m * d, :]
    return jnp.concatenate([stage[j * p:j * p + d, :] for j in range(m)], axis=0)


def _proj_f32_kernel(x_ref, g_ref, w_ref, b_ref, o_ref, *, n_chunk):
    hb = _rms(x_ref[...], g_ref[...]).astype(BF16)
    for c in range(o_ref.shape[1] // n_chunk):
        sl = slice(c * n_chunk, (c + 1) * n_chunk)
        o_ref[:, sl] = jnp.dot(hb, w_ref[:, sl], preferred_element_type=F32) + b_ref[:, sl]


def _proj_f32(x2d, g, w, b):
    T, D = x2d.shape
    N = w.shape[1]
    return pl.pallas_call(
        functools.partial(_proj_f32_kernel, n_chunk=512),
        out_shape=jax.ShapeDtypeStruct((T, N), F32),
        grid=(T // PROJ_TILE,),
        in_specs=[
            pl.BlockSpec((PROJ_TILE, D), lambda i: (i, 0)),
            pl.BlockSpec((1, D), lambda i: (0, 0)),
            pl.BlockSpec((D, N), lambda i: (0, 0)),
            pl.BlockSpec((1, N), lambda i: (0, 0)),
        ],
        out_specs=pl.BlockSpec((PROJ_TILE, N), lambda i: (i, 0)),
        compiler_params=_cparams(("parallel",)),
        name="proj_rnn",
    )(x2d, g, w, b)


def _proj_qkv_kernel(x_ref, g_ref, w0_ref, w1_ref, w2_ref, b_ref, cos_ref, sin_ref,
                     o0_ref, o1_ref, o2_ref, stage):
    hb = _rms(x_ref[...], g_ref[...]).astype(BF16)
    cosv = cos_ref[...]
    sinv = sin_ref[...]
    lane = lax.broadcasted_iota(jnp.int32, (1, HEAD_DIM), 1)
    first_half = lane < (ROPE_DIM // 2)
    gw = 3 * ATTN_WIDTH
    for g, (w_ref, o_ref) in enumerate(((w0_ref, o0_ref), (w1_ref, o1_ref), (w2_ref, o2_ref))):
        d = ATTN_GROUPS[g][1]
        for c in range(gw // QKV_CHUNK):
            sl = slice(c * QKV_CHUNK, (c + 1) * QKV_CHUNK)
            t2 = jnp.dot(hb, w_ref[:, sl], preferred_element_type=F32)
            t2 = t2 + b_ref[:, g * gw + c * QKV_CHUNK:g * gw + (c + 1) * QKV_CHUNK]
            for hh in range(QKV_CHUNK // HEAD_DIM):
                t = t2[:, hh * HEAD_DIM:(hh + 1) * HEAD_DIM]
                col = c * QKV_CHUNK + hh * HEAD_DIM
                if col < 2 * ATTN_WIDTH:
                    up = pltpu.roll(t, HEAD_DIM - ROPE_DIM // 2, 1)
                    down = pltpu.roll(t, ROPE_DIM // 2, 1)
                    t = t * cosv + jnp.where(first_half, up, down) * sinv
                cs = slice(col, col + HEAD_DIM)
                if d == 1:
                    o_ref[:, cs] = t.astype(BF16)
                else:
                    for r, part in enumerate(_deinterleave(t, stage, d)):
                        o_ref[0, r, :, cs] = part.astype(BF16)


def _proj_qkv(x2d, g, ws, b, cos_t, sin_t, batch):
    T, D = x2d.shape
    S = T // batch
    nseq = S // PROJ_TILE
    gw = 3 * ATTN_WIDTH
    d1, d2 = ATTN_GROUPS[1][1], ATTN_GROUPS[2][1]
    full = lambda a: pl.BlockSpec(a.shape, lambda i: (0,) * a.ndim)
    dil_spec = lambda d: pl.BlockSpec((1, d, PROJ_TILE // d, gw),
                                      lambda i: (i // nseq, 0, i % nseq, 0))
    return pl.pallas_call(
        _proj_qkv_kernel,
        out_shape=(jax.ShapeDtypeStruct((T, gw), BF16),
                   jax.ShapeDtypeStruct((batch, d1, S // d1, gw), BF16),
                   jax.ShapeDtypeStruct((batch, d2, S // d2, gw), BF16)),
        grid=(T // PROJ_TILE,),
        in_specs=[
            pl.BlockSpec((PROJ_TILE, D), lambda i: (i, 0)),
            full(g), full(ws[0]), full(ws[1]), full(ws[2]), full(b),
            pl.BlockSpec((PROJ_TILE, HEAD_DIM), lambda i: (i % nseq, 0)),
            pl.BlockSpec((PROJ_TILE, HEAD_DIM), lambda i: (i % nseq, 0)),
        ],
        out_specs=(pl.BlockSpec((PROJ_TILE, gw), lambda i: (i, 0)), dil_spec(d1), dil_spec(d2)),
        scratch_shapes=[pltpu.VMEM((PROJ_TILE // d2 * _pitch(d2), LANES), F32)],
        compiler_params=_cparams(("parallel",)),
        name="proj_qkv",
    )(x2d, g, ws[0], ws[1], ws[2], b, cos_t, sin_t)


def _rope_tables(seq):
    pos = jnp.arange(seq, dtype=F32)
    inv = ROPE_THETA ** (-jnp.arange(0, ROPE_DIM, 2, dtype=F32) / ROPE_DIM)
    ang = pos[:, None] * inv[None, :]
    cos, sin = jnp.cos(ang), jnp.sin(ang)
    pad = HEAD_DIM - ROPE_DIM
    cos_t = jnp.concatenate([cos, cos, jnp.ones((seq, pad), F32)], axis=1)
    sin_t = jnp.concatenate([-sin, sin, jnp.zeros((seq, pad), F32)], axis=1)
    return cos_t, sin_t


def _sigmoid(x):
    return 0.5 * jnp.tanh(0.5 * x) + 0.5


def _rglru_kernel(xr_ref, gr_ref, cw_ref, cb_ref, gw_ref, gb_ref, lam_ref, o_ref,
                  xpad, af, uf, ab, ub, cf, cbk, *, seq):
    nseg = seq // SCAN_SEG
    nchunk = seq // SCAN_CHUNK
    seg_per_chunk = SCAN_CHUNK // SCAN_SEG
    pitch = nseg + 4

    cw = cw_ref[...]
    cbias = cb_ref[...]
    gbias = gb_ref[...]
    z = -lam_ref[...]
    sp = jnp.maximum(z, 0.0) + jnp.log1p(jnp.exp(-jnp.abs(z)))
    half_neg_c_sp = (-0.5 * LRU_C) * sp

    zero_rows = jnp.zeros((SUBLANES, LANES), F32)
    xpad[0:SUBLANES, :] = zero_rows
    xpad[seq + SUBLANES:seq + 2 * SUBLANES, :] = zero_rows

    def copy_in(c, carry):
        t0 = pl.multiple_of(c * SCAN_CHUNK, SCAN_CHUNK)
        xpad[pl.ds(t0 + SUBLANES, SCAN_CHUNK), :] = xr_ref[0, pl.ds(t0, SCAN_CHUNK), :]
        return carry

    lax.fori_loop(0, nchunk, copy_in, 0)

    def stage1(c, carry):
        t0 = c * SCAN_CHUNK
        xc = cbias
        for tap in range(CONV_WIDTH):
            lo = t0 + SUBLANES - CONV_WIDTH // 2 + tap
            xc = xc + cw[tap:tap + 1, :] * xpad[pl.ds(lo, SCAN_CHUNK), :]
        th = jnp.tanh(jnp.dot(xc.astype(BF16), gw_ref[...], preferred_element_type=F32) + gbias)
        xc_half = 0.5 * xc
        for d, (a_s, u_s) in enumerate(((af, uf), (ab, ub))):
            t_r = th[:, (2 * d) * LANES:(2 * d + 1) * LANES]
            t_i = th[:, (2 * d + 1) * LANES:(2 * d + 2) * LANES]
            half_k = half_neg_c_sp[d:d + 1, :]
            log_a = t_r * half_k + half_k
            a = jnp.exp(log_a)
            v = jnp.tanh(log_a) * (-1.0 - a * a)
            u = (xc_half * (v * lax.rsqrt(jnp.maximum(v, F32_MIN_NORMAL)))) * (t_i + 1.0)
            for k in range(seg_per_chunk):
                seg = c * seg_per_chunk + k
                a_s[pl.ds(seg, SCAN_SEG, stride=pitch), :] = a[k * SCAN_SEG:(k + 1) * SCAN_SEG, :]
                u_s[pl.ds(seg, SCAN_SEG, stride=pitch), :] = u[k * SCAN_SEG:(k + 1) * SCAN_SEG, :]
        return carry

    lax.fori_loop(0, nchunk, stage1, 0, unroll=4)

    def scan_dir(a_s, u_s, reverse):
        def body(it, carry):
            i = (SCAN_SEG - 1 - it) if reverse else it
            h, cum = carry
            rows = pl.ds(i * pitch, nseg)
            a = a_s[rows, :]
            u = u_s[rows, :]
            h = a * h + u
            cum = a * cum
            u_s[rows, :] = h
            a_s[rows, :] = cum
            return h, cum

        return lax.fori_loop(0, SCAN_SEG, body,
                             (jnp.zeros((nseg, LANES), F32), jnp.ones((nseg, LANES), F32)))

    end_h, end_a = scan_dir(af, uf, False)
    beg_h, beg_a = scan_dir(ab, ub, True)

    c = jnp.zeros((1, LANES), F32)
    for s in range(nseg):
        cf[s:s + 1, :] = c
        c = end_h[s:s + 1, :] + end_a[s:s + 1, :] * c
    c = jnp.zeros((1, LANES), F32)
    for s in range(nseg - 1, -1, -1):
        cbk[s:s + 1, :] = c
        c = beg_h[s:s + 1, :] + beg_a[s:s + 1, :] * c

    def finish(s, carry):
        t0 = pl.multiple_of(s * SCAN_SEG, SCAN_SEG)
        rows = pl.ds(s, SCAN_SEG, stride=pitch)
        hf = uf[rows, :] + af[rows, :] * cf[pl.ds(s, 1), :]
        hb = ub[rows, :] + ab[rows, :] * cbk[pl.ds(s, 1), :]
        g = jax.nn.gelu(gr_ref[0, pl.ds(t0, SCAN_SEG), :])
        o_ref[0, pl.ds(t0, SCAN_SEG), :] = (g * (hf + hb)).astype(BF16)
        return carry

    lax.fori_loop(0, nseg, finish, 0)


def _rglru(xrgr, conv_w, conv_b, gate_w, gate_b, lam):
    B, S, _ = xrgr.shape
    nblk = D_RNN // RNN_BW
    nseg = S // SCAN_SEG
    scr = pltpu.VMEM((SCAN_SEG * (nseg + 4), LANES), F32)
    return pl.pallas_call(
        functools.partial(_rglru_kernel, seq=S),
        out_shape=jax.ShapeDtypeStruct((B, S, D_RNN), BF16),
        grid=(B, nblk),
        in_specs=[
            pl.BlockSpec((1, S, RNN_BW), lambda b, c: (b, 0, c)),
            pl.BlockSpec((1, S, RNN_BW), lambda b, c: (b, 0, nblk + c)),
            pl.BlockSpec((CONV_WIDTH, RNN_BW), lambda b, c: (0, c)),
            pl.BlockSpec((1, RNN_BW), lambda b, c: (0, c)),
            pl.BlockSpec((None, RNN_BW, 4 * RNN_BW), lambda b, c: (c, 0, 0)),
            pl.BlockSpec((None, 1, 4 * RNN_BW), lambda b, c: (c, 0, 0)),
            pl.BlockSpec((None, 2, RNN_BW), lambda b, c: (c, 0, 0)),
        ],
        out_specs=pl.BlockSpec((1, S, RNN_BW), lambda b, c: (b, 0, c)),
        scratch_shapes=[pltpu.VMEM((S + 2 * SUBLANES, LANES), F32), scr, scr, scr, scr,
                        pltpu.VMEM((nseg, LANES), F32), pltpu.VMEM((nseg, LANES), F32)],
        compiler_params=_cparams(("parallel", "parallel")),
        name="rglru",
    )(xrgr, xrgr, conv_w, conv_b, gate_w, gate_b, lam)


def _attn_kernel(q_ref, k_ref, kp_ref, kn_ref, v_ref, vp_ref, vn_ref, o_ref, st_ref,
                 kext, vext, *, qblk, length):
    j = pl.program_id(2)
    nkey = ATT_SUB + 2 * RADIUS
    qi = lax.broadcasted_iota(jnp.int32, (ATT_SUB, nkey), 0)
    ki = lax.broadcasted_iota(jnp.int32, (ATT_SUB, nkey), 1)
    band_bias = jnp.where(jnp.abs(ki - RADIUS - qi) <= RADIUS, 0.0, NEG_INF)
    krow = lax.broadcasted_iota(jnp.int32, (1, nkey), 1)
    lane = lax.broadcasted_iota(jnp.int32, (ATT_SUB, LANES), 1)
    scale2 = HEAD_DIM ** -0.5 * LOG2_E

    for rr in range(q_ref.shape[1]):
        kx, vx = kext.at[rr], vext.at[rr]
        kx[0:RADIUS, :] = kp_ref[0, rr]
        kx[RADIUS:RADIUS + qblk, :] = k_ref[0, rr]
        kx[RADIUS + qblk:RADIUS + qblk + RADIUS, :] = kn_ref[0, rr]
        for h in range(HEADS):
            src = slice(h * HEAD_DIM, (h + 1) * HEAD_DIM)
            dst = slice(2 * h * HEAD_DIM, (2 * h + 1) * HEAD_DIM)
            vx[0:RADIUS, dst] = vp_ref[0, rr, :, src]
            vx[RADIUS:RADIUS + qblk, dst] = v_ref[0, rr, :, src]
            vx[RADIUS + qblk:RADIUS + qblk + RADIUS, dst] = vn_ref[0, rr, :, src]
            vx[:, (2 * h + 1) * HEAD_DIM:(2 * h + 2) * HEAD_DIM] = jnp.ones(
                (qblk + 2 * RADIUS, HEAD_DIM), BF16)

        for i in range(qblk // ATT_SUB):
            q0 = i * ATT_SUB
            kglob = j * qblk + q0 - RADIUS + krow
            edge_bias = jnp.where((kglob >= 0) & (kglob < length), 0.0, NEG_INF)
            bias = band_bias + edge_bias
            st_tile = jnp.ones((ATT_SUB, LANES), F32)
            for h in range(HEADS):
                hs = slice(h * HEAD_DIM, (h + 1) * HEAD_DIM)
                q = q_ref[0, rr, q0:q0 + ATT_SUB, hs]
                ks = kx[q0:q0 + nkey, hs]
                vs = vx[q0:q0 + nkey, 2 * h * HEAD_DIM:(2 * h + 2) * HEAD_DIM]
                s = lax.dot_general(q, ks, (((1,), (1,)), ((), ())),
                                    preferred_element_type=F32) * scale2 + bias
                m = jnp.max(s, axis=-1, keepdims=True)
                p = jnp.exp2(s - m)
                pv = jnp.dot(p.astype(BF16), vs, preferred_element_type=F32)
                l = pv[:, HEAD_DIM:2 * HEAD_DIM]
                o_ref[0, rr, q0:q0 + ATT_SUB, hs] = pv[:, 0:HEAD_DIM]
                st_tile = jnp.where(lane == h, m * LN_2,
                                    jnp.where(lane == HEADS + h, l, st_tile))
            st_ref[0, rr, q0:q0 + ATT_SUB, :] = st_tile


def _attention_group(qkv_g):
    B, d, L, _ = qkv_g.shape
    qblk = min(L, ATT_QBLK)
    nblk = L // qblk
    rpb = min(d, max(1, ATT_QBLK // L))
    halo_per_blk = qblk // RADIUS
    last_halo = L // RADIUS - 1

    main = lambda part: pl.BlockSpec((1, rpb, qblk, ATTN_WIDTH), lambda b, r, j: (b, r, j, part))
    prev = lambda part: pl.BlockSpec(
        (1, rpb, RADIUS, ATTN_WIDTH),
        lambda b, r, j: (b, r, jnp.maximum(j * halo_per_blk - 1, 0), part))
    nxt = lambda part: pl.BlockSpec(
        (1, rpb, RADIUS, ATTN_WIDTH),
        lambda b, r, j: (b, r, jnp.minimum((j + 1) * halo_per_blk, last_halo), part))
    return pl.pallas_call(
        functools.partial(_attn_kernel, qblk=qblk, length=L),
        out_shape=(jax.ShapeDtypeStruct((B, d, L, ATTN_WIDTH), F32),
                   jax.ShapeDtypeStruct((B, d, L, LANES), F32)),
        grid=(B, d // rpb, nblk),
        in_specs=[main(0), main(1), prev(1), nxt(1), main(2), prev(2), nxt(2)],
        out_specs=(pl.BlockSpec((1, rpb, qblk, ATTN_WIDTH), lambda b, r, j: (b, r, j, 0)),
                   pl.BlockSpec((1, rpb, qblk, LANES), lambda b, r, j: (b, r, j, 0))),
        scratch_shapes=[pltpu.VMEM((rpb, qblk + 2 * RADIUS, ATTN_WIDTH), BF16),
                        pltpu.VMEM((rpb, qblk + 2 * RADIUS, 2 * ATTN_WIDTH), BF16)],
        compiler_params=_cparams(("parallel", "parallel", "parallel")),
        name=f"attn_d{d}",
    )(qkv_g, qkv_g, qkv_g, qkv_g, qkv_g, qkv_g, qkv_g)


def _merge_kernel(x_ref, yr_ref, o0_ref, o1_ref, o2_ref, l0_ref, l1_ref, l2_ref,
                  gmix_ref, wg_ref, bg_ref, prnn_ref, pattn_ref, wout_ref,
                  gffn_ref, wr_ref, wrt_ref, br_ref, brt_ref,
                  x2_ref, h2_ref, afft_ref, yattn, stage):
    d = x_ref.shape[1]
    x = x_ref[...]
    hb = _rms(x, gmix_ref[...]).astype(BF16)

    def natural(ref, g, cs):
        dil = ATTN_GROUPS[g][1]
        if dil == 1:
            return ref[:, cs]
        return _interleave([ref[0, r, :, cs] for r in range(dil)], stage, dil)

    o_refs = (o0_ref, o1_ref, o2_ref)
    all_lanes = slice(0, LANES)
    stats = [natural(r, g, all_lanes) for g, r in enumerate((l0_ref, l1_ref, l2_ref))]
    head_lane = lax.broadcasted_iota(jnp.int32, stats[0].shape, 1) < HEADS
    sums = [jnp.where(head_lane, pltpu.roll(st, LANES - HEADS, 1), 1.0) for st in stats]
    lses = [st + jnp.log(l) for st, l in zip(stats, sums)]
    m = jnp.maximum(jnp.maximum(lses[0], lses[1]), lses[2])
    ws = [jnp.exp(l - m) for l in lses]
    den = ws[0] + ws[1] + ws[2]
    coef = [w / (den * l) for w, l in zip(ws, sums)]
    for h in range(HEADS):
        hs = slice(h * HEAD_DIM, (h + 1) * HEAD_DIM)
        acc = coef[0][:, h:h + 1] * natural(o_refs[0], 0, hs)
        acc = acc + coef[1][:, h:h + 1] * natural(o_refs[1], 1, hs)
        acc = acc + coef[2][:, h:h + 1] * natural(o_refs[2], 2, hs)
        yattn[:, hs] = acc.astype(BF16)

    branch_a = jnp.dot(yr_ref[...], prnn_ref[...], preferred_element_type=F32)
    branch_b = jnp.dot(yattn[...], pattn_ref[...], preferred_element_type=F32)
    ga = jnp.dot(hb, wg_ref[:, 0:d], preferred_element_type=F32) + bg_ref[:, 0:d]
    gb = jnp.dot(hb, wg_ref[:, d:2 * d], preferred_element_type=F32) + bg_ref[:, d:2 * d]
    merged = _sigmoid(ga) * branch_a + _sigmoid(gb) * branch_b
    x2 = x + jnp.dot(merged.astype(BF16), wout_ref[...], preferred_element_type=F32)
    x2_ref[...] = x2

    h2 = _rms(x2, gffn_ref[...]).astype(BF16)
    h2_ref[:, 0:d] = h2
    n_exp = afft_ref.shape[1]
    logits = jnp.dot(h2, wr_ref[...], preferred_element_type=F32) + br_ref[...]
    lane = lax.broadcasted_iota(jnp.int32, logits.shape, 1)
    mx = jnp.max(logits, axis=-1, keepdims=True)
    e = jnp.exp(logits - mx)
    den = jnp.sum(jnp.where(lane < n_exp, e, 0.0), axis=-1, keepdims=True)
    aff = e / den
    head = aff.astype(BF16).astype(F32)
    tile = jnp.where(lane < n_exp, head, jnp.where(lane < 2 * n_exp, aff - head, 0.0))
    h2_ref[:, d:d + LANES] = tile.astype(BF16)
    logits_t = lax.dot_general(wrt_ref[...], h2, (((1,), (1,)), ((), ())),
                               preferred_element_type=F32) + brt_ref[...]
    mxt = jnp.max(logits_t, axis=0, keepdims=True)
    et = jnp.exp(logits_t - mxt)
    afft_ref[0] = et / jnp.sum(et, axis=0, keepdims=True)


def _merge(x2d, y_rnn, outs, lses, gmix, wg, bg, p_rnn, p_attn, w_out, gffn, wr, br, batch):
    T, D = x2d.shape
    S = T // batch
    nseq = S // ROW_TILE
    E = wr.shape[1]
    row = lambda w: pl.BlockSpec((ROW_TILE, w), lambda i: (i, 0))
    full = lambda a: pl.BlockSpec(a.shape, lambda i: (0,) * a.ndim)

    def attn_spec(g, w):
        dil = ATTN_GROUPS[g][1]
        if dil == 1:
            return row(w)
        return pl.BlockSpec((1, dil, ROW_TILE // dil, w), lambda i: (i // nseq, 0, i % nseq, 0))

    wrt = wr.T
    brt = br.reshape(E, 1)
    wr_pad = jnp.concatenate([wr, wr, jnp.zeros((D, LANES - 2 * E), wr.dtype)], axis=1)
    br_pad = jnp.concatenate([br, br, jnp.full((1, LANES - 2 * E), NEG_INF, br.dtype)], axis=1)
    consts = [gmix, wg, bg, p_rnn, p_attn, w_out, gffn, wr_pad, wrt, br_pad, brt]
    args = [x2d, y_rnn, *outs, *lses, *consts]
    in_specs = ([row(D), row(D_RNN)] + [attn_spec(g, ATTN_WIDTH) for g in range(N_GROUPS)]
                + [attn_spec(g, LANES) for g in range(N_GROUPS)] + [full(a) for a in consts])
    dmax = max(dil for _, dil in ATTN_GROUPS)
    return pl.pallas_call(
        _merge_kernel,
        out_shape=(jax.ShapeDtypeStruct((T, D), F32),
                   jax.ShapeDtypeStruct((T, D + LANES), BF16),
                   jax.ShapeDtypeStruct((batch, E, S), F32)),
        grid=(T // ROW_TILE,),
        in_specs=in_specs,
        out_specs=(row(D), row(D + LANES),
                   pl.BlockSpec((1, E, ROW_TILE), lambda i: (i // nseq, 0, i % nseq))),
        scratch_shapes=[pltpu.VMEM((ROW_TILE, ATTN_WIDTH), BF16),
                        pltpu.VMEM((ROW_TILE // dmax * _pitch(dmax), LANES), F32)],
        compiler_params=_cparams(("parallel",)),
        name="merge",
    )(*args)


def _route_kernel(aff_ref, pos_ref, off_ref, *, cap):
    a = aff_ref[0]
    E, R, _ = a.shape
    bits = pltpu.bitcast(a, jnp.int32)

    def count(mask):
        c = jnp.sum(mask.astype(F32), axis=2, keepdims=True)
        return jnp.sum(c, axis=1, keepdims=True)

    def bs(_, carry):
        lo, hi = carry
        mid = lo + ((hi - lo + 1) >> 1)
        ok = count(bits >= mid) >= cap
        return jnp.where(ok, mid, lo), jnp.where(ok, hi, mid - 1)

    lo0 = jnp.zeros((E, 1, 1), jnp.int32)
    hi0 = jnp.full((E, 1, 1), 0x7F7FFFFF, jnp.int32)
    thr, _ = lax.fori_loop(0, 31, bs, (lo0, hi0))

    gt = bits > thr
    eq = bits == thr
    need = cap - count(gt)

    ki = lax.broadcasted_iota(jnp.int32, (LANES, LANES), 0)
    li = lax.broadcasted_iota(jnp.int32, (LANES, LANES), 1)
    upper = (ki <= li).astype(BF16)
    allones = jnp.ones((LANES, LANES), BF16)
    ri = lax.broadcasted_iota(jnp.int32, (R, R), 0)
    ci = lax.broadcasted_iota(jnp.int32, (R, R), 1)
    strict = (ci < ri).astype(BF16)

    def prefix(mask):
        mb = mask.astype(BF16).reshape(E * R, LANES)
        within = jnp.dot(mb, upper, preferred_element_type=F32)
        totals = jnp.dot(mb, allones, preferred_element_type=F32).astype(BF16)
        offs = []
        for e in range(E):
            offs.append(jnp.dot(strict, totals[e * R:(e + 1) * R, :],
                                preferred_element_type=F32))
        rowoff = jnp.concatenate(offs, axis=0)
        excl = within - mb.astype(F32) + rowoff
        return excl.reshape(E, R, LANES), rowoff.reshape(E, R, LANES)

    eq_rank, _ = prefix(eq)
    sel = gt | (eq & (eq_rank < need.astype(F32)))
    slot, rowoff = prefix(sel)
    pos_ref[0] = jnp.where(sel, slot.astype(jnp.int32), -1)
    off_ref[0] = rowoff.astype(jnp.int32)


def _route(aff_t, cap):
    B, E, S = aff_t.shape
    R = S // LANES
    a4 = aff_t.reshape(B, E, R, LANES)
    blk = pl.BlockSpec((1, E, R, LANES), lambda b: (b, 0, 0, 0))
    pos, off = pl.pallas_call(
        functools.partial(_route_kernel, cap=cap),
        out_shape=(jax.ShapeDtypeStruct((B, E, R, LANES), jnp.int32),
                   jax.ShapeDtypeStruct((B, E, R, LANES), jnp.int32)),
        grid=(B,),
        in_specs=[blk],
        out_specs=(blk, blk),
        compiler_params=_cparams(("parallel",)),
        name="route",
    )(a4)
    return pos, off


def _slot_windows(off_ref, base, n_exp, ntb, tb):
    starts = []
    rounds = jnp.int32(0)
    for e in range(n_exp):
        idx = (base + e) * (ntb + 1) + tb
        start = (off_ref[idx] >> 4) << 4
        starts.append(start)
        rounds = jnp.maximum(rounds, (off_ref[idx + 1] - start + SLOT_WIN - 1) >> SLOT_SHIFT)
    return starts, rounds


def _gather_kernel(off_ref, pos_ref, h_ref, o_ref, *, cap, ntb, n_exp, nsub):
    b = pl.program_id(0)
    grp = pl.program_id(1)
    t = pl.program_id(2)

    @pl.when(t == 0)
    def _():
        o_ref[...] = jnp.zeros_like(o_ref)

    base = (b * pl.num_programs(1) + grp) * n_exp
    ji = lax.broadcasted_iota(jnp.int32, (SLOT_WIN, TOK_BLK), 0)
    for sb in range(nsub):
        toks = slice(sb * TOK_BLK, (sb + 1) * TOK_BLK)
        starts, rounds = _slot_windows(off_ref, base, n_exp, ntb, t * nsub + sb)

        def one_round(k, carry, toks=toks, starts=starts):
            pieces, rows = [], []
            for e in range(n_exp):
                first = starts[e] + k * SLOT_WIN
                row0 = pl.multiple_of(jnp.minimum(first, cap - SLOT_WIN), BF16_ROWS)
                slot = ji + row0
                hit = (slot == pos_ref[0, 0, e:e + 1, toks]) & (slot >= first)
                pieces.append(jnp.where(hit, 1.0, 0.0).astype(BF16))
                rows.append(row0)
            onehot = jnp.concatenate(pieces, axis=0)
            g = jnp.dot(onehot, h_ref[0, toks, :], preferred_element_type=F32)
            for e in range(n_exp):
                o_ref[0, e, pl.ds(rows[e], SLOT_WIN), :] += (
                    g[e * SLOT_WIN:(e + 1) * SLOT_WIN, :].astype(BF16))
            return carry

        one_round(0, 0)
        lax.fori_loop(1, rounds, one_round, 0)


def _gather(offs, pos, h2, cap):
    B, E, S = pos.shape
    D = h2.shape[-1]
    ntb = S // TOK_BLK
    n_exp = E // EXPERT_SPLIT
    nsub = GATHER_TOK // TOK_BLK
    pos4 = pos.reshape(B, EXPERT_SPLIT, n_exp, S)
    return pl.pallas_call(
        functools.partial(_gather_kernel, cap=cap, ntb=ntb, n_exp=n_exp, nsub=nsub),
        out_shape=jax.ShapeDtypeStruct((B, E, cap, D), BF16),
        grid_spec=pltpu.PrefetchScalarGridSpec(
            num_scalar_prefetch=1,
            grid=(B, EXPERT_SPLIT, S // GATHER_TOK),
            in_specs=[
                pl.BlockSpec((1, 1, n_exp, GATHER_TOK), lambda b, g, t, off: (b, g, 0, t)),
                pl.BlockSpec((1, GATHER_TOK, D), lambda b, g, t, off: (b, t, 0)),
            ],
            out_specs=pl.BlockSpec((1, n_exp, cap, D), lambda b, g, t, off: (b, g, 0, 0)),
        ),
        compiler_params=_cparams(("parallel", "parallel", "arbitrary")),
        name="gather",
    )(offs, pos4, h2)


def _ffn_kernel(x_ref, wg_ref, wu_ref, wd_ref, o_ref, acc, *, f_chunk):
    e = pl.program_id(0)
    d = wg_ref.shape[1]
    x = x_ref[0, 0, :, 0:d]
    gl = x_ref[0, 0, :, d:d + LANES].astype(F32)
    lane = lax.broadcasted_iota(jnp.int32, gl.shape, 1)
    gate = jnp.sum(jnp.where((lane == e) | (lane == e + pl.num_programs(0)), gl, 0.0),
                   axis=1, keepdims=True)
    nf = wg_ref.shape[2] // f_chunk
    for f in range(nf):
        fs = slice(f * f_chunk, (f + 1) * f_chunk)
        g = jnp.dot(x, wg_ref[0, :, fs], preferred_element_type=F32)
        u = jnp.dot(x, wu_ref[0, :, fs], preferred_element_type=F32)
        hid = (g * _sigmoid(g) * u).astype(BF16)
        part = jnp.dot(hid, wd_ref[0, fs, :], preferred_element_type=F32)
        if f == 0:
            acc[...] = part
        else:
            acc[...] += part
    o_ref[0, 0] = (acc[...] * gate).astype(BF16)


def _ffn(xg, w_gate, w_up, w_down):
    B, E, C, DX = xg.shape
    D = w_gate.shape[1]
    F = w_gate.shape[2]
    assert F % FFN_CHUNK == 0, (F, FFN_CHUNK)
    return pl.pallas_call(
        functools.partial(_ffn_kernel, f_chunk=FFN_CHUNK),
        out_shape=jax.ShapeDtypeStruct((B, E, C, D), BF16),
        grid=(E, B),
        in_specs=[
            pl.BlockSpec((1, 1, C, DX), lambda e, b: (b, e, 0, 0)),
            pl.BlockSpec((1, D, F), lambda e, b: (e, 0, 0)),
            pl.BlockSpec((1, D, F), lambda e, b: (e, 0, 0)),
            pl.BlockSpec((1, F, D), lambda e, b: (e, 0, 0)),
        ],
        out_specs=pl.BlockSpec((1, 1, C, D), lambda e, b: (b, e, 0, 0)),
        scratch_shapes=[pltpu.VMEM((C, D), F32)],
        compiler_params=_cparams(("arbitrary", "arbitrary")),
        name="ffn",
    )(xg, w_gate, w_up, w_down)


def _combine_kernel(off_ref, x_ref, pos_ref, eo_ref, g_ref, o_ref, wstack, acc, *, cap, ntb):
    b = pl.program_id(0)
    t = pl.program_id(1)
    E = pos_ref.shape[2]
    nsub = pos_ref.shape[1] // TOK_BLK
    lane = lax.broadcasted_iota(jnp.int32, (TOK_BLK, LANES), 1)
    low = lane < SLOT_WIN
    acc[...] = x_ref[0]

    for sb in range(nsub):
        toks = slice(sb * TOK_BLK, (sb + 1) * TOK_BLK)
        pos = pos_ref[0, toks, :]
        starts, rounds = _slot_windows(off_ref, b * E, E, ntb, t * nsub + sb)

        def one_round(k, carry, toks=toks, pos=pos, starts=starts, wst=wstack.at[sb]):
            onehot = []
            for p in range(E // 2):
                firsts, row0s = [], []
                for e in (2 * p, 2 * p + 1):
                    first = starts[e] + k * SLOT_WIN
                    row0 = pl.multiple_of(jnp.minimum(first, cap - SLOT_WIN), BF16_ROWS)
                    wst[e * SLOT_WIN:(e + 1) * SLOT_WIN, :] = (
                        eo_ref[0, e, pl.ds(row0, SLOT_WIN), :])
                    firsts.append(first)
                    row0s.append(row0)
                slot = lane + jnp.where(low, row0s[0], row0s[1] - SLOT_WIN)
                first = jnp.where(low, firsts[0], firsts[1])
                pcol = jnp.where(low, pos[:, 2 * p:2 * p + 1], pos[:, 2 * p + 1:2 * p + 2])
                hit = (slot == pcol) & (slot >= first)
                onehot.append(jnp.where(hit, 1.0, 0.0).astype(BF16))
            acc[toks, :] += jnp.dot(jnp.concatenate(onehot, axis=1), wst[...],
                                    preferred_element_type=F32)
            return carry

        one_round(0, 0)
        lax.fori_loop(1, rounds, one_round, 0)
    o_ref[0] = _rms(acc[...], g_ref[...])


def _combine(offs, x2, pos_t, eo, g_final, cap):
    B, S, D = x2.shape
    E = pos_t.shape[2]
    ntb = S // TOK_BLK
    return pl.pallas_call(
        functools.partial(_combine_kernel, cap=cap, ntb=ntb),
        out_shape=jax.ShapeDtypeStruct((B, S, D), F32),
        grid_spec=pltpu.PrefetchScalarGridSpec(
            num_scalar_prefetch=1,
            grid=(B, S // COMBINE_TOK),
            in_specs=[
                pl.BlockSpec((1, COMBINE_TOK, D), lambda b, t, off: (b, t, 0)),
                pl.BlockSpec((1, COMBINE_TOK, E), lambda b, t, off: (b, t, 0)),
                pl.BlockSpec((1, E, cap, D), lambda b, t, off: (b, 0, 0, 0),
                             pipeline_mode=pl.Buffered(1)),
                pl.BlockSpec((1, D), lambda b, t, off: (0, 0)),
            ],
            out_specs=pl.BlockSpec((1, COMBINE_TOK, D), lambda b, t, off: (b, t, 0)),
            scratch_shapes=[pltpu.VMEM((COMBINE_TOK // TOK_BLK, E * SLOT_WIN, D), BF16),
                            pltpu.VMEM((COMBINE_TOK, D), F32)],
        ),
        compiler_params=_cparams(("parallel", "arbitrary")),
        name="combine",
    )(offs, x2, pos_t, eo, g_final)


def _layer(x, norm_mix, w_in, b_in, conv_w, conv_b, rg_w, rg_b, rg_lambda, p_rnn, p_attn,
           w_out, norm_ffn, w_router, b_router, w_gate, w_up, w_down, norm_out, cos_t, sin_t):
    B, S, D = x.shape
    T = B * S
    x2d = x.reshape(T, D)
    n_rnn = 2 * D_RNN
    n_qkv = 3 * N_GROUPS * ATTN_WIDTH
    gmix = norm_mix.reshape(1, D)
    w_in_b = w_in.astype(BF16)
    b_in2 = b_in.reshape(1, -1)

    xrgr = _proj_f32(x2d, gmix, w_in_b[:, :n_rnn], b_in2[:, :n_rnn])

    def group_cols(a, g):
        part = lambda p: a[:, n_rnn + (p * N_GROUPS + g) * ATTN_WIDTH:
                           n_rnn + (p * N_GROUPS + g + 1) * ATTN_WIDTH]
        return jnp.concatenate([part(0), part(1), part(2)], axis=1)

    ws = [group_cols(w_in_b, g) for g in range(N_GROUPS)]
    b_qkv = jnp.concatenate([group_cols(b_in2, g) for g in range(N_GROUPS)], axis=1)
    qkv = _proj_qkv(x2d, gmix, ws, b_qkv, cos_t, sin_t, B)

    nblk = D_RNN // RNN_BW
    gate_w = (0.5 * jnp.transpose(rg_w, (2, 3, 0, 1, 4))).reshape(nblk, RNN_BW, 4 * RNN_BW).astype(BF16)
    gate_b = 0.5 * jnp.transpose(rg_b.reshape(2, 2, nblk, RNN_BW), (2, 0, 1, 3)).reshape(nblk, 1, 4 * RNN_BW)
    lam = jnp.transpose(rg_lambda.reshape(2, nblk, RNN_BW), (1, 0, 2))
    y_rnn = _rglru(xrgr.reshape(B, S, n_rnn), conv_w, conv_b.reshape(1, D_RNN), gate_w, gate_b, lam)

    outs, lses = [], []
    for g, (_, dilation) in enumerate(ATTN_GROUPS):
        qkv_g = qkv[g].reshape(B, 1, S, 3 * ATTN_WIDTH) if dilation == 1 else qkv[g]
        o, lse = _attention_group(qkv_g)
        if dilation == 1:
            o, lse = o.reshape(T, ATTN_WIDTH), lse.reshape(T, LANES)
        outs.append(o)
        lses.append(lse)

    x2, h2, aff_t = _merge(
        x2d, y_rnn.reshape(T, D_RNN), outs, lses, gmix,
        w_in_b[:, n_rnn + n_qkv:], b_in2[:, n_rnn + n_qkv:],
        p_rnn.astype(BF16), p_attn.astype(BF16), w_out.astype(BF16),
        norm_ffn.reshape(1, D), w_router.astype(BF16), b_router.reshape(1, -1), B)

    E = w_router.shape[1]
    cap = CAPACITY_FACTOR * S // E
    pos4, off4 = _route(aff_t, cap)
    pos = pos4.reshape(B, E, S)
    blk_off = off4[:, :, ::TOK_BLK // LANES, 0]
    offs = jnp.concatenate([blk_off, jnp.full((B, E, 1), cap, jnp.int32)], axis=2).reshape(-1)
    xg = _gather(offs, pos, h2.reshape(B, S, D + LANES), cap)
    eo = _ffn(xg, w_gate.astype(BF16), w_up.astype(BF16), w_down.astype(BF16))
    pos_t = jnp.transpose(pos, (0, 2, 1))
    return _combine(offs, x2.reshape(B, S, D), pos_t, eo, norm_out.reshape(1, D), cap)


def kernel(x, norm_mix, w_in, b_in, conv_w, conv_b, rg_w, rg_b, rg_lambda, p_rnn, p_attn, w_out,
           norm_ffn, w_router, b_router, w_gate, w_up, w_down, norm_final):
    depth = w_in.shape[0]
    assert depth == 1, "final rmsnorm is fused into the single layer's combine step"
    cos_t, sin_t = _rope_tables(x.shape[1])
    return _layer(x, norm_mix[0], w_in[0], b_in[0], conv_w[0], conv_b[0], rg_w[0], rg_b[0],
                  rg_lambda[0], p_rnn[0], p_attn[0], w_out[0], norm_ffn[0], w_router[0],
                  b_router[0], w_gate[0], w_up[0], w_down[0], norm_final, cos_t, sin_t)
```

```python
import functools

import jax
import jax.numpy as jnp
from jax import lax
from jax.experimental import pallas as pl
from jax.experimental.pallas import tpu as pltpu

F32 = jnp.float32
BF16 = jnp.bfloat16

D_RNN = 1024
RNN_BW = 128
CONV_WIDTH = 4
LRU_C = 8.0
HEAD_DIM = 128
HEADS = 4
ATTN_WIDTH = HEADS * HEAD_DIM
ATTN_GROUPS = ((128, 1), (512, 4), (2048, 16))
N_GROUPS = 3
ROPE_THETA = 500000.0
ROPE_DIM = HEAD_DIM // 4
RADIUS = 64
NEG_INF = -1e30
N_EXPERTS = 16
CAPACITY_FACTOR = 2
RMS_EPS = 1e-6

LANES = 128
SUBLANES = 8
BF16_ROWS = 16
VMEM_LIMIT = 56 * 1024 * 1024
F32_MIN_NORMAL = 1.1754944e-38
LOG2_E = 1.4426950408889634
LN_2 = 0.6931471805599453

PROJ_TILE = 1024
QKV_CHUNK = 512
ROW_TILE = 512
SCAN_SEG = 128
SCAN_CHUNK = 512
PROJ_AHEAD = 2
ATT_QBLK = 1024
ATT_SUB = 128
TOK_BLK = 256
GATHER_TOK = 1024
COMBINE_TOK = 512
SLOT_SHIFT = 6
SLOT_WIN = 1 << SLOT_SHIFT
EXPERT_SPLIT = 2
FFN_CHUNK = 512


def _cparams(sem):
    return pltpu.CompilerParams(dimension_semantics=sem, vmem_limit_bytes=VMEM_LIMIT)


def _rms(x, g):
    ms = jnp.mean(x * x, axis=-1, keepdims=True)
    return x * lax.rsqrt(ms + RMS_EPS) * g


def _pitch(d):
    return d if d <= 4 else d + 4


def _deinterleave(t, stage, d):
    n = t.shape[0]
    p = _pitch(d)
    if p == d:
        stage[0:n, :] = t
    else:
        for j in range(n // d):
            stage[j * p:j * p + d, :] = t[j * d:(j + 1) * d, :]
    return [stage[pl.ds(r, n // d, stride=p), :] for r in range(d)]


def _interleave(parts, stage, d):
    m = parts[0].shape[0]
    p = _pitch(d)
    for r in range(d):
        stage[pl.ds(r, m, stride=p), :] = parts[r]
    if p == d:
        return stage[0:m * d, :]
    return jnp.concatenate([stage[j * p:j * p + d, :] for j in range(m)], axis=0)


def _proj_qkv_kernel(x_ref, g_ref, w0_ref, w1_ref, w2_ref, b_ref, cos_ref, sin_ref,
                     h_ref, o0_ref, o1_ref, o2_ref, stage):
    hb = _rms(x_ref[...], g_ref[...]).astype(BF16)
    h_ref[...] = hb
    cosv = cos_ref[...]
    sinv = sin_ref[...]
    lane = lax.broadcasted_iota(jnp.int32, (1, HEAD_DIM), 1)
    first_half = lane < (ROPE_DIM // 2)
    gw = 3 * ATTN_WIDTH
    for g, (w_ref, o_ref) in enumerate(((w0_ref, o0_ref), (w1_ref, o1_ref), (w2_ref, o2_ref))):
        d = ATTN_GROUPS[g][1]
        for c in range(gw // QKV_CHUNK):
            sl = slice(c * QKV_CHUNK, (c + 1) * QKV_CHUNK)
            t2 = jnp.dot(hb, w_ref[:, sl], preferred_element_type=F32)
            t2 = t2 + b_ref[:, g * gw + c * QKV_CHUNK:g * gw + (c + 1) * QKV_CHUNK]
            for hh in range(QKV_CHUNK // HEAD_DIM):
                t = t2[:, hh * HEAD_DIM:(hh + 1) * HEAD_DIM]
                col = c * QKV_CHUNK + hh * HEAD_DIM
                if col < 2 * ATTN_WIDTH:
                    up = pltpu.roll(t, HEAD_DIM - ROPE_DIM // 2, 1)
                    down = pltpu.roll(t, ROPE_DIM // 2, 1)
                    t = t * cosv + jnp.where(first_half, up, down) * sinv
                cs = slice(col, col + HEAD_DIM)
                if d == 1:
                    o_ref[:, cs] = t.astype(BF16)
                else:
                    for r, part in enumerate(_deinterleave(t, stage, d)):
                        o_ref[0, r, :, cs] = part.astype(BF16)


def _proj_qkv(x2d, g, ws, b, cos_t, sin_t, batch):
    T, D = x2d.shape
    S = T // batch
    nseq = S // PROJ_TILE
    gw = 3 * ATTN_WIDTH
    d1, d2 = ATTN_GROUPS[1][1], ATTN_GROUPS[2][1]
    full = lambda a: pl.BlockSpec(a.shape, lambda i: (0,) * a.ndim)
    dil_spec = lambda d: pl.BlockSpec((1, d, PROJ_TILE // d, gw),
                                      lambda i: (i // nseq, 0, i % nseq, 0))
    return pl.pallas_call(
        _proj_qkv_kernel,
        out_shape=(jax.ShapeDtypeStruct((T, D), BF16),
                   jax.ShapeDtypeStruct((T, gw), BF16),
                   jax.ShapeDtypeStruct((batch, d1, S // d1, gw), BF16),
                   jax.ShapeDtypeStruct((batch, d2, S // d2, gw), BF16)),
        grid=(T // PROJ_TILE,),
        in_specs=[
            pl.BlockSpec((PROJ_TILE, D), lambda i: (i, 0)),
            full(g), full(ws[0]), full(ws[1]), full(ws[2]), full(b),
            pl.BlockSpec((PROJ_TILE, HEAD_DIM), lambda i: (i % nseq, 0)),
            pl.BlockSpec((PROJ_TILE, HEAD_DIM), lambda i: (i % nseq, 0)),
        ],
        out_specs=(pl.BlockSpec((PROJ_TILE, D), lambda i: (i, 0)),
                   pl.BlockSpec((PROJ_TILE, gw), lambda i: (i, 0)), dil_spec(d1), dil_spec(d2)),
        scratch_shapes=[pltpu.VMEM((PROJ_TILE // d2 * _pitch(d2), LANES), F32)],
        compiler_params=_cparams(("parallel",)),
        name="proj_qkv",
    )(x2d, g, ws[0], ws[1], ws[2], b, cos_t, sin_t)


def _rope_tables(seq):
    pos = jnp.arange(seq, dtype=F32)
    inv = ROPE_THETA ** (-jnp.arange(0, ROPE_DIM, 2, dtype=F32) / ROPE_DIM)
    ang = pos[:, None] * inv[None, :]
    cos, sin = jnp.cos(ang), jnp.sin(ang)
    pad = HEAD_DIM - ROPE_DIM
    cos_t = jnp.concatenate([cos, cos, jnp.ones((seq, pad), F32)], axis=1)
    sin_t = jnp.concatenate([-sin, sin, jnp.zeros((seq, pad), F32)], axis=1)
    return cos_t, sin_t


def _sigmoid(x):
    return 0.5 * jnp.tanh(0.5 * x) + 0.5


def _rglru_kernel(h_ref, w_ref, b_ref, cw_ref, cb_ref, gw_ref, gb_ref, lam_ref, o_ref,
                  xpad, grs, af, uf, ab, ub, cf, cbk, *, seq):
    nseg = seq // SCAN_SEG
    nchunk = seq // SCAN_CHUNK
    seg_per_chunk = SCAN_CHUNK // SCAN_SEG
    pitch = nseg + 4

    cw = cw_ref[...]
    cbias = cb_ref[...]
    gbias = gb_ref[...]
    pbias = b_ref[...]
    z = -lam_ref[...]
    sp = jnp.maximum(z, 0.0) + jnp.log1p(jnp.exp(-jnp.abs(z)))
    half_neg_c_sp = (-0.5 * LRU_C) * sp

    zero_rows = jnp.zeros((SUBLANES, LANES), F32)
    xpad[0:SUBLANES, :] = zero_rows
    xpad[seq + SUBLANES:seq + 2 * SUBLANES, :] = zero_rows

    def project(c):
        t0 = pl.multiple_of(c * SCAN_CHUNK, SCAN_CHUNK)
        p = jnp.dot(h_ref[0, pl.ds(t0, SCAN_CHUNK), :], w_ref[...],
                    preferred_element_type=F32) + pbias
        xpad[pl.ds(t0 + SUBLANES, SCAN_CHUNK), :] = p[:, 0:LANES]
        grs[pl.ds(t0, SCAN_CHUNK), :] = p[:, LANES:2 * LANES]

    for c in range(PROJ_AHEAD):
        project(c)

    def stage1(c, carry):
        t0 = c * SCAN_CHUNK
        xc = cbias
        for tap in range(CONV_WIDTH):
            lo = t0 + SUBLANES - CONV_WIDTH // 2 + tap
            xc = xc + cw[tap:tap + 1, :] * xpad[pl.ds(lo, SCAN_CHUNK), :]
        project(jnp.minimum(c + PROJ_AHEAD, nchunk - 1))
        th = jnp.tanh(jnp.dot(xc.astype(BF16), gw_ref[...], preferred_element_type=F32) + gbias)
        xc_half = 0.5 * xc
        for d, (a_s, u_s) in enumerate(((af, uf), (ab, ub))):
            t_r = th[:, (2 * d) * LANES:(2 * d + 1) * LANES]
            t_i = th[:, (2 * d + 1) * LANES:(2 * d + 2) * LANES]
            half_k = half_neg_c_sp[d:d + 1, :]
            log_a = t_r * half_k + half_k
            a = jnp.exp(log_a)
            v = jnp.tanh(log_a) * (-1.0 - a * a)
            u = (xc_half * (v * lax.rsqrt(jnp.maximum(v, F32_MIN_NORMAL)))) * (t_i + 1.0)
            for k in range(seg_per_chunk):
                seg = c * seg_per_chunk + k
                a_s[pl.ds(seg, SCAN_SEG, stride=pitch), :] = a[k * SCAN_SEG:(k + 1) * SCAN_SEG, :]
                u_s[pl.ds(seg, SCAN_SEG, stride=pitch), :] = u[k * SCAN_SEG:(k + 1) * SCAN_SEG, :]
        return carry

    lax.fori_loop(0, nchunk, stage1, 0, unroll=4)

    def scan_dir(a_s, u_s, reverse):
        def body(it, carry):
            i = (SCAN_SEG - 1 - it) if reverse else it
            h, cum = carry
            rows = pl.ds(i * pitch, nseg)
            a = a_s[rows, :]
            u = u_s[rows, :]
            h = a * h + u
            cum = a * cum
            u_s[rows, :] = h
            a_s[rows, :] = cum
            return h, cum

        return lax.fori_loop(0, SCAN_SEG, body,
                             (jnp.zeros((nseg, LANES), F32), jnp.ones((nseg, LANES), F32)))

    end_h, end_a = scan_dir(af, uf, False)
    beg_h, beg_a = scan_dir(ab, ub, True)

    c = jnp.zeros((1, LANES), F32)
    for s in range(nseg):
        cf[s:s + 1, :] = c
        c = end_h[s:s + 1, :] + end_a[s:s + 1, :] * c
    c = jnp.zeros((1, LANES), F32)
    for s in range(nseg - 1, -1, -1):
        cbk[s:s + 1, :] = c
        c = beg_h[s:s + 1, :] + beg_a[s:s + 1, :] * c

    def finish(s, carry):
        t0 = pl.multiple_of(s * SCAN_SEG, SCAN_SEG)
        rows = pl.ds(s, SCAN_SEG, stride=pitch)
        hf = uf[rows, :] + af[rows, :] * cf[pl.ds(s, 1), :]
        hb = ub[rows, :] + ab[rows, :] * cbk[pl.ds(s, 1), :]
        g = jax.nn.gelu(grs[pl.ds(t0, SCAN_SEG), :])
        o_ref[0, pl.ds(t0, SCAN_SEG), :] = (g * (hf + hb)).astype(BF16)
        return carry

    lax.fori_loop(0, nseg, finish, 0)


def _rglru(h, w_proj, b_proj, conv_w, conv_b, gate_w, gate_b, lam):
    B, S, D = h.shape
    nblk = D_RNN // RNN_BW
    nseg = S // SCAN_SEG
    scr = pltpu.VMEM((SCAN_SEG * (nseg + 4), LANES), F32)
    return pl.pallas_call(
        functools.partial(_rglru_kernel, seq=S),
        out_shape=jax.ShapeDtypeStruct((B, S, D_RNN), BF16),
        grid=(B, nblk),
        in_specs=[
            pl.BlockSpec((1, S, D), lambda b, c: (b, 0, 0), pipeline_mode=pl.Buffered(1)),
            pl.BlockSpec((None, D, 2 * RNN_BW), lambda b, c: (c, 0, 0)),
            pl.BlockSpec((None, 1, 2 * RNN_BW), lambda b, c: (c, 0, 0)),
            pl.BlockSpec((CONV_WIDTH, RNN_BW), lambda b, c: (0, c)),
            pl.BlockSpec((1, RNN_BW), lambda b, c: (0, c)),
            pl.BlockSpec((None, RNN_BW, 4 * RNN_BW), lambda b, c: (c, 0, 0)),
            pl.BlockSpec((None, 1, 4 * RNN_BW), lambda b, c: (c, 0, 0)),
            pl.BlockSpec((None, 2, RNN_BW), lambda b, c: (c, 0, 0)),
        ],
        out_specs=pl.BlockSpec((1, S, RNN_BW), lambda b, c: (b, 0, c)),
        scratch_shapes=[pltpu.VMEM((S + 2 * SUBLANES, LANES), F32),
                        pltpu.VMEM((S, LANES), F32), scr, scr, scr, scr,
                        pltpu.VMEM((nseg, LANES), F32), pltpu.VMEM((nseg, LANES), F32)],
        compiler_params=_cparams(("parallel", "parallel")),
        name="rglru",
    )(h, w_proj, b_proj, conv_w, conv_b, gate_w, gate_b, lam)


def _attn_kernel(q_ref, k_ref, kp_ref, kn_ref, v_ref, vp_ref, vn_ref, o_ref, st_ref,
                 kext, vext, *, qblk, length):
    j = pl.program_id(2)
    nkey = ATT_SUB + 2 * RADIUS
    qi = lax.broadcasted_iota(jnp.int32, (ATT_SUB, nkey), 0)
    ki = lax.broadcasted_iota(jnp.int32, (ATT_SUB, nkey), 1)
    band_bias = jnp.where(jnp.abs(ki - RADIUS - qi) <= RADIUS, 0.0, NEG_INF)
    krow = lax.broadcasted_iota(jnp.int32, (1, nkey), 1)
    lane = lax.broadcasted_iota(jnp.int32, (ATT_SUB, LANES), 1)
    scale2 = HEAD_DIM ** -0.5 * LOG2_E

    for rr in range(q_ref.shape[1]):
        kx, vx = kext.at[rr], vext.at[rr]
        kx[0:RADIUS, :] = kp_ref[0, rr]
        kx[RADIUS:RADIUS + qblk, :] = k_ref[0, rr]
        kx[RADIUS + qblk:RADIUS + qblk + RADIUS, :] = kn_ref[0, rr]
        for h in range(HEADS):
            src = slice(h * HEAD_DIM, (h + 1) * HEAD_DIM)
            dst = slice(2 * h * HEAD_DIM, (2 * h + 1) * HEAD_DIM)
            vx[0:RADIUS, dst] = vp_ref[0, rr, :, src]
            vx[RADIUS:RADIUS + qblk, dst] = v_ref[0, rr, :, src]
            vx[RADIUS + qblk:RADIUS + qblk + RADIUS, dst] = vn_ref[0, rr, :, src]
            vx[:, (2 * h + 1) * HEAD_DIM:(2 * h + 2) * HEAD_DIM] = jnp.ones(
                (qblk + 2 * RADIUS, HEAD_DIM), BF16)

        for i in range(qblk // ATT_SUB):
            q0 = i * ATT_SUB
            kglob = j * qblk + q0 - RADIUS + krow
            edge_bias = jnp.where((kglob >= 0) & (kglob < length), 0.0, NEG_INF)
            bias = band_bias + edge_bias
            st_tile = jnp.ones((ATT_SUB, LANES), F32)
            for h in range(HEADS):
                hs = slice(h * HEAD_DIM, (h + 1) * HEAD_DIM)
                q = q_ref[0, rr, q0:q0 + ATT_SUB, hs]
                ks = kx[q0:q0 + nkey, hs]
                vs = vx[q0:q0 + nkey, 2 * h * HEAD_DIM:(2 * h + 2) * HEAD_DIM]
                s = lax.dot_general(q, ks, (((1,), (1,)), ((), ())),
                                    preferred_element_type=F32) * scale2 + bias
                m = jnp.max(s, axis=-1, keepdims=True)
                p = jnp.exp2(s - m)
                pv = jnp.dot(p.astype(BF16), vs, preferred_element_type=F32)
                l = pv[:, HEAD_DIM:2 * HEAD_DIM]
                o_ref[0, rr, q0:q0 + ATT_SUB, hs] = pv[:, 0:HEAD_DIM]
                st_tile = jnp.where(lane == h, m * LN_2,
                                    jnp.where(lane == HEADS + h, l, st_tile))
            st_ref[0, rr, q0:q0 + ATT_SUB, :] = st_tile


def _attention_group(qkv_g):
    B, d, L, _ = qkv_g.shape
    qblk = min(L, ATT_QBLK)
    nblk = L // qblk
    rpb = min(d, max(1, ATT_QBLK // L))
    halo_per_blk = qblk // RADIUS
    last_halo = L // RADIUS - 1

    main = lambda part: pl.BlockSpec((1, rpb, qblk, ATTN_WIDTH), lambda b, r, j: (b, r, j, part))
    prev = lambda part: pl.BlockSpec(
        (1, rpb, RADIUS, ATTN_WIDTH),
        lambda b, r, j: (b, r, jnp.maximum(j * halo_per_blk - 1, 0), part))
    nxt = lambda part: pl.BlockSpec(
        (1, rpb, RADIUS, ATTN_WIDTH),
        lambda b, r, j: (b, r, jnp.minimum((j + 1) * halo_per_blk, last_halo), part))
    return pl.pallas_call(
        functools.partial(_attn_kernel, qblk=qblk, length=L),
        out_shape=(jax.ShapeDtypeStruct((B, d, L, ATTN_WIDTH), F32),
                   jax.ShapeDtypeStruct((B, d, L, LANES), F32)),
        grid=(B, d // rpb, nblk),
        in_specs=[main(0), main(1), prev(1), nxt(1), main(2), prev(2), nxt(2)],
        out_specs=(pl.BlockSpec((1, rpb, qblk, ATTN_WIDTH), lambda b, r, j: (b, r, j, 0)),
                   pl.BlockSpec((1, rpb, qblk, LANES), lambda b, r, j: (b, r, j, 0))),
        scratch_shapes=[pltpu.VMEM((rpb, qblk + 2 * RADIUS, ATTN_WIDTH), BF16),
                        pltpu.VMEM((rpb, qblk + 2 * RADIUS, 2 * ATTN_WIDTH), BF16)],
        compiler_params=_cparams(("parallel", "parallel", "parallel")),
        name=f"attn_d{d}",
    )(qkv_g, qkv_g, qkv_g, qkv_g, qkv_g, qkv_g, qkv_g)


def _merge_kernel(x_ref, yr_ref, o0_ref, o1_ref, o2_ref, l0_ref, l1_ref, l2_ref,
                  gmix_ref, wg_ref, bg_ref, prnn_ref, pattn_ref, wout_ref,
                  gffn_ref, wr_ref, wrt_ref, br_ref, brt_ref,
                  x2_ref, h2_ref, afft_ref, yattn, stage):
    d = x_ref.shape[1]
    x = x_ref[...]
    hb = _rms(x, gmix_ref[...]).astype(BF16)

    def natural(ref, g, cs):
        dil = ATTN_GROUPS[g][1]
        if dil == 1:
            return ref[:, cs]
        return _interleave([ref[0, r, :, cs] for r in range(dil)], stage, dil)

    o_refs = (o0_ref, o1_ref, o2_ref)
    all_lanes = slice(0, LANES)
    stats = [natural(r, g, all_lanes) for g, r in enumerate((l0_ref, l1_ref, l2_ref))]
    head_lane = lax.broadcasted_iota(jnp.int32, stats[0].shape, 1) < HEADS
    sums = [jnp.where(head_lane, pltpu.roll(st, LANES - HEADS, 1), 1.0) for st in stats]
    lses = [st + jnp.log(l) for st, l in zip(stats, sums)]
    m = jnp.maximum(jnp.maximum(lses[0], lses[1]), lses[2])
    ws = [jnp.exp(l - m) for l in lses]
    den = ws[0] + ws[1] + ws[2]
    coef = [w / (den * l) for w, l in zip(ws, sums)]
    for h in range(HEADS):
        hs = slice(h * HEAD_DIM, (h + 1) * HEAD_DIM)
        acc = coef[0][:, h:h + 1] * natural(o_refs[0], 0, hs)
        acc = acc + coef[1][:, h:h + 1] * natural(o_refs[1], 1, hs)
        acc = acc + coef[2][:, h:h + 1] * natural(o_refs[2], 2, hs)
        yattn[:, hs] = acc.astype(BF16)

    branch_a = jnp.dot(yr_ref[...], prnn_ref[...], preferred_element_type=F32)
    branch_b = jnp.dot(yattn[...], pattn_ref[...], preferred_element_type=F32)
    ga = jnp.dot(hb, wg_ref[:, 0:d], preferred_element_type=F32) + bg_ref[:, 0:d]
    gb = jnp.dot(hb, wg_ref[:, d:2 * d], preferred_element_type=F32) + bg_ref[:, d:2 * d]
    merged = _sigmoid(ga) * branch_a + _sigmoid(gb) * branch_b
    x2 = x + jnp.dot(merged.astype(BF16), wout_ref[...], preferred_element_type=F32)
    x2_ref[...] = x2

    h2 = _rms(x2, gffn_ref[...]).astype(BF16)
    h2_ref[:, 0:d] = h2
    n_exp = afft_ref.shape[1]
    logits = jnp.dot(h2, wr_ref[...], preferred_element_type=F32) + br_ref[...]
    lane = lax.broadcasted_iota(jnp.int32, logits.shape, 1)
    mx = jnp.max(logits, axis=-1, keepdims=True)
    e = jnp.exp(logits - mx)
    den = jnp.sum(jnp.where(lane < n_exp, e, 0.0), axis=-1, keepdims=True)
    aff = e / den
    head = aff.astype(BF16).astype(F32)
    tile = jnp.where(lane < n_exp, head, jnp.where(lane < 2 * n_exp, aff - head, 0.0))
    h2_ref[:, d:d + LANES] = tile.astype(BF16)
    logits_t = lax.dot_general(wrt_ref[...], h2, (((1,), (1,)), ((), ())),
                               preferred_element_type=F32) + brt_ref[...]
    mxt = jnp.max(logits_t, axis=0, keepdims=True)
    et = jnp.exp(logits_t - mxt)
    afft_ref[0] = et / jnp.sum(et, axis=0, keepdims=True)


def _merge(x2d, y_rnn, outs, lses, gmix, wg, bg, p_rnn, p_attn, w_out, gffn, wr, br, batch):
    T, D = x2d.shape
    S = T // batch
    nseq = S // ROW_TILE
    E = wr.shape[1]
    row = lambda w: pl.BlockSpec((ROW_TILE, w), lambda i: (i, 0))
    full = lambda a: pl.BlockSpec(a.shape, lambda i: (0,) * a.ndim)

    def attn_spec(g, w):
        dil = ATTN_GROUPS[g][1]
        if dil == 1:
            return row(w)
        return pl.BlockSpec((1, dil, ROW_TILE // dil, w), lambda i: (i // nseq, 0, i % nseq, 0))

    wrt = wr.T
    brt = br.reshape(E, 1)
    wr_pad = jnp.concatenate([wr, wr, jnp.zeros((D, LANES - 2 * E), wr.dtype)], axis=1)
    br_pad = jnp.concatenate([br, br, jnp.full((1, LANES - 2 * E), NEG_INF, br.dtype)], axis=1)
    consts = [gmix, wg, bg, p_rnn, p_attn, w_out, gffn, wr_pad, wrt, br_pad, brt]
    args = [x2d, y_rnn, *outs, *lses, *consts]
    in_specs = ([row(D), row(D_RNN)] + [attn_spec(g, ATTN_WIDTH) for g in range(N_GROUPS)]
                + [attn_spec(g, LANES) for g in range(N_GROUPS)] + [full(a) for a in consts])
    dmax = max(dil for _, dil in ATTN_GROUPS)
    return pl.pallas_call(
        _merge_kernel,
        out_shape=(jax.ShapeDtypeStruct((T, D), F32),
                   jax.ShapeDtypeStruct((T, D + LANES), BF16),
                   jax.ShapeDtypeStruct((batch, E, S), F32)),
        grid=(T // ROW_TILE,),
        in_specs=in_specs,
        out_specs=(row(D), row(D + LANES),
                   pl.BlockSpec((1, E, ROW_TILE), lambda i: (i // nseq, 0, i % nseq))),
        scratch_shapes=[pltpu.VMEM((ROW_TILE, ATTN_WIDTH), BF16),
                        pltpu.VMEM((ROW_TILE // dmax * _pitch(dmax), LANES), F32)],
        compiler_params=_cparams(("parallel",)),
        name="merge",
    )(*args)


def _route_kernel(aff_ref, pos_ref, off_ref, *, cap):
    a = aff_ref[0]
    E, R, _ = a.shape
    bits = pltpu.bitcast(a, jnp.int32)

    def count(mask):
        c = jnp.sum(mask.astype(F32), axis=2, keepdims=True)
        return jnp.sum(c, axis=1, keepdims=True)

    def bs(_, carry):
        lo, hi = carry
        mid = lo + ((hi - lo + 1) >> 1)
        ok = count(bits >= mid) >= cap
        return jnp.where(ok, mid, lo), jnp.where(ok, hi, mid - 1)

    lo0 = jnp.zeros((E, 1, 1), jnp.int32)
    hi0 = jnp.full((E, 1, 1), 0x7F7FFFFF, jnp.int32)
    thr, _ = lax.fori_loop(0, 31, bs, (lo0, hi0))

    gt = bits > thr
    eq = bits == thr
    need = cap - count(gt)

    ki = lax.broadcasted_iota(jnp.int32, (LANES, LANES), 0)
    li = lax.broadcasted_iota(jnp.int32, (LANES, LANES), 1)
    upper = (ki <= li).astype(BF16)
    allones = jnp.ones((LANES, LANES), BF16)
    ri = lax.broadcasted_iota(jnp.int32, (R, R), 0)
    ci = lax.broadcasted_iota(jnp.int32, (R, R), 1)
    strict = (ci < ri).astype(BF16)

    def prefix(mask):
        mb = mask.astype(BF16).reshape(E * R, LANES)
        within = jnp.dot(mb, upper, preferred_element_type=F32)
        totals = jnp.dot(mb, allones, preferred_element_type=F32).astype(BF16)
        offs = []
        for e in range(E):
            offs.append(jnp.dot(strict, totals[e * R:(e + 1) * R, :],
                                preferred_element_type=F32))
        rowoff = jnp.concatenate(offs, axis=0)
        excl = within - mb.astype(F32) + rowoff
        return excl.reshape(E, R, LANES), rowoff.reshape(E, R, LANES)

    eq_rank, _ = prefix(eq)
    sel = gt | (eq & (eq_rank < need.astype(F32)))
    slot, rowoff = prefix(sel)
    pos_ref[0] = jnp.where(sel, slot.astype(jnp.int32), -1)
    off_ref[0] = rowoff.astype(jnp.int32)


def _route(aff_t, cap):
    B, E, S = aff_t.shape
    R = S // LANES
    a4 = aff_t.reshape(B, E, R, LANES)
    blk = pl.BlockSpec((1, E, R, LANES), lambda b: (b, 0, 0, 0))
    pos, off = pl.pallas_call(
        functools.partial(_route_kernel, cap=cap),
        out_shape=(jax.ShapeDtypeStruct((B, E, R, LANES), jnp.int32),
                   jax.ShapeDtypeStruct((B, E, R, LANES), jnp.int32)),
        grid=(B,),
        in_specs=[blk],
        out_specs=(blk, blk),
        compiler_params=_cparams(("parallel",)),
        name="route",
    )(a4)
    return pos, off


def _slot_windows(off_ref, base, n_exp, ntb, tb):
    starts = []
    rounds = jnp.int32(0)
    for e in range(n_exp):
        idx = (base + e) * (ntb + 1) + tb
        start = (off_ref[idx] >> 4) << 4
        starts.append(start)
        rounds = jnp.maximum(rounds, (off_ref[idx + 1] - start + SLOT_WIN - 1) >> SLOT_SHIFT)
    return starts, rounds


def _gather_kernel(off_ref, pos_ref, h_ref, o_ref, *, cap, ntb, n_exp, nsub):
    b = pl.program_id(0)
    grp = pl.program_id(1)
    t = pl.program_id(2)

    @pl.when(t == 0)
    def _():
        o_ref[...] = jnp.zeros_like(o_ref)

    base = (b * pl.num_programs(1) + grp) * n_exp
    ji = lax.broadcasted_iota(jnp.int32, (SLOT_WIN, TOK_BLK), 0)
    for sb in range(nsub):
        toks = slice(sb * TOK_BLK, (sb + 1) * TOK_BLK)
        starts, rounds = _slot_windows(off_ref, base, n_exp, ntb, t * nsub + sb)

        def one_round(k, carry, toks=toks, starts=starts):
            pieces, rows = [], []
            for e in range(n_exp):
                first = starts[e] + k * SLOT_WIN
                row0 = pl.multiple_of(jnp.minimum(first, cap - SLOT_WIN), BF16_ROWS)
                slot = ji + row0
                hit = (slot == pos_ref[0, 0, e:e + 1, toks]) & (slot >= first)
                pieces.append(jnp.where(hit, 1.0, 0.0).astype(BF16))
                rows.append(row0)
            onehot = jnp.concatenate(pieces, axis=0)
            g = jnp.dot(onehot, h_ref[0, toks, :], preferred_element_type=F32)
            for e in range(n_exp):
                o_ref[0, e, pl.ds(rows[e], SLOT_WIN), :] += (
                    g[e * SLOT_WIN:(e + 1) * SLOT_WIN, :].astype(BF16))
            return carry

        one_round(0, 0)
        lax.fori_loop(1, rounds, one_round, 0)


def _gather(offs, pos, h2, cap):
    B, E, S = pos.shape
    D = h2.shape[-1]
    ntb = S // TOK_BLK
    n_exp = E // EXPERT_SPLIT
    nsub = GATHER_TOK // TOK_BLK
    pos4 = pos.reshape(B, EXPERT_SPLIT, n_exp, S)
    return pl.pallas_call(
        functools.partial(_gather_kernel, cap=cap, ntb=ntb, n_exp=n_exp, nsub=nsub),
        out_shape=jax.ShapeDtypeStruct((B, E, cap, D), BF16),
        grid_spec=pltpu.PrefetchScalarGridSpec(
            num_scalar_prefetch=1,
            grid=(B, EXPERT_SPLIT, S // GATHER_TOK),
            in_specs=[
                pl.BlockSpec((1, 1, n_exp, GATHER_TOK), lambda b, g, t, off: (b, g, 0, t)),
                pl.BlockSpec((1, GATHER_TOK, D), lambda b, g, t, off: (b, t, 0)),
            ],
            out_specs=pl.BlockSpec((1, n_exp, cap, D), lambda b, g, t, off: (b, g, 0, 0)),
        ),
        compiler_params=_cparams(("parallel", "parallel", "arbitrary")),
        name="gather",
    )(offs, pos4, h2)


def _ffn_kernel(x_ref, wg_ref, wu_ref, wd_ref, o_ref, acc, *, f_chunk):
    e = pl.program_id(0)
    d = wg_ref.shape[1]
    x = x_ref[0, 0, :, 0:d]
    gl = x_ref[0, 0, :, d:d + LANES].astype(F32)
    lane = lax.broadcasted_iota(jnp.int32, gl.shape, 1)
    gate = jnp.sum(jnp.where((lane == e) | (lane == e + pl.num_programs(0)), gl, 0.0),
                   axis=1, keepdims=True)
    nf = wg_ref.shape[2] // f_chunk
    for f in range(nf):
        fs = slice(f * f_chunk, (f + 1) * f_chunk)
        g = jnp.dot(x, wg_ref[0, :, fs], preferred_element_type=F32)
        u = jnp.dot(x, wu_ref[0, :, fs], preferred_element_type=F32)
        hid = (g * _sigmoid(g) * u).astype(BF16)
        part = jnp.dot(hid, wd_ref[0, fs, :], preferred_element_type=F32)
        if f == 0:
            acc[...] = part
        else:
            acc[...] += part
    o_ref[0, 0] = (acc[...] * gate).astype(BF16)


def _ffn(xg, w_gate, w_up, w_down):
    B, E, C, DX = xg.shape
    D = w_gate.shape[1]
    F = w_gate.shape[2]
    assert F % FFN_CHUNK == 0, (F, FFN_CHUNK)
    return pl.pallas_call(
        functools.partial(_ffn_kernel, f_chunk=FFN_CHUNK),
        out_shape=jax.ShapeDtypeStruct((B, E, C, D), BF16),
        grid=(E, B),
        in_specs=[
            pl.BlockSpec((1, 1, C, DX), lambda e, b: (b, e, 0, 0)),
            pl.BlockSpec((1, D, F), lambda e, b: (e, 0, 0)),
            pl.BlockSpec((1, D, F), lambda e, b: (e, 0, 0)),
            pl.BlockSpec((1, F, D), lambda e, b: (e, 0, 0)),
        ],
        out_specs=pl.BlockSpec((1, 1, C, D), lambda e, b: (b, e, 0, 0)),
        scratch_shapes=[pltpu.VMEM((C, D), F32)],
        compiler_params=_cparams(("arbitrary", "arbitrary")),
        name="ffn",
    )(xg, w_gate, w_up, w_down)


def _combine_kernel(off_ref, x_ref, pos_ref, eo_ref, g_ref, o_ref, wstack, acc, *, cap, ntb):
    b = pl.program_id(0)
    t = pl.program_id(1)
    E = pos_ref.shape[2]
    nsub = pos_ref.shape[1] // TOK_BLK
    lane = lax.broadcasted_iota(jnp.int32, (TOK_BLK, LANES), 1)
    low = lane < SLOT_WIN
    acc[...] = x_ref[0]

    for sb in range(nsub):
        toks = slice(sb * TOK_BLK, (sb + 1) * TOK_BLK)
        pos = pos_ref[0, toks, :]
        starts, rounds = _slot_windows(off_ref, b * E, E, ntb, t * nsub + sb)

        def one_round(k, carry, toks=toks, pos=pos, starts=starts, wst=wstack.at[sb]):
            onehot = []
            for p in range(E // 2):
                firsts, row0s = [], []
                for e in (2 * p, 2 * p + 1):
                    first = starts[e] + k * SLOT_WIN
                    row0 = pl.multiple_of(jnp.minimum(first, cap - SLOT_WIN), BF16_ROWS)
                    wst[e * SLOT_WIN:(e + 1) * SLOT_WIN, :] = (
                        eo_ref[0, e, pl.ds(row0, SLOT_WIN), :])
                    firsts.append(first)
                    row0s.append(row0)
                slot = lane + jnp.where(low, row0s[0], row0s[1] - SLOT_WIN)
                first = jnp.where(low, firsts[0], firsts[1])
                pcol = jnp.where(low, pos[:, 2 * p:2 * p + 1], pos[:, 2 * p + 1:2 * p + 2])
                hit = (slot == pcol) & (slot >= first)
                onehot.append(jnp.where(hit, 1.0, 0.0).astype(BF16))
            acc[toks, :] += jnp.dot(jnp.concatenate(onehot, axis=1), wst[...],
                                    preferred_element_type=F32)
            return carry

        one_round(0, 0)
        lax.fori_loop(1, rounds, one_round, 0)
    o_ref[0] = _rms(acc[...], g_ref[...])


def _combine(offs, x2, pos_t, eo, g_final, cap):
    B, S, D = x2.shape
    E = pos_t.shape[2]
    ntb = S // TOK_BLK
    return pl.pallas_call(
        functools.partial(_combine_kernel, cap=cap, ntb=ntb),
        out_shape=jax.ShapeDtypeStruct((B, S, D), F32),
        grid_spec=pltpu.PrefetchScalarGridSpec(
            num_scalar_prefetch=1,
            grid=(B, S // COMBINE_TOK),
            in_specs=[
                pl.BlockSpec((1, COMBINE_TOK, D), lambda b, t, off: (b, t, 0)),
                pl.BlockSpec((1, COMBINE_TOK, E), lambda b, t, off: (b, t, 0)),
                pl.BlockSpec((1, E, cap, D), lambda b, t, off: (b, 0, 0, 0),
                             pipeline_mode=pl.Buffered(1)),
                pl.BlockSpec((1, D), lambda b, t, off: (0, 0)),
            ],
            out_specs=pl.BlockSpec((1, COMBINE_TOK, D), lambda b, t, off: (b, t, 0)),
            scratch_shapes=[pltpu.VMEM((COMBINE_TOK // TOK_BLK, E * SLOT_WIN, D), BF16),
                            pltpu.VMEM((COMBINE_TOK, D), F32)],
        ),
        compiler_params=_cparams(("parallel", "arbitrary")),
        name="combine",
    )(offs, x2, pos_t, eo, g_final)


def _layer(x, norm_mix, w_in, b_in, conv_w, conv_b, rg_w, rg_b, rg_lambda, p_rnn, p_attn,
           w_out, norm_ffn, w_router, b_router, w_gate, w_up, w_down, norm_out, cos_t, sin_t):
    B, S, D = x.shape
    T = B * S
    x2d = x.reshape(T, D)
    n_rnn = 2 * D_RNN
    n_qkv = 3 * N_GROUPS * ATTN_WIDTH
    gmix = norm_mix.reshape(1, D)
    w_in_b = w_in.astype(BF16)
    b_in2 = b_in.reshape(1, -1)

    def group_cols(a, g):
        part = lambda p: a[:, n_rnn + (p * N_GROUPS + g) * ATTN_WIDTH:
                           n_rnn + (p * N_GROUPS + g + 1) * ATTN_WIDTH]
        return jnp.concatenate([part(0), part(1), part(2)], axis=1)

    ws = [group_cols(w_in_b, g) for g in range(N_GROUPS)]
    b_qkv = jnp.concatenate([group_cols(b_in2, g) for g in range(N_GROUPS)], axis=1)
    h, *qkv = _proj_qkv(x2d, gmix, ws, b_qkv, cos_t, sin_t, B)

    nblk = D_RNN // RNN_BW
    w_rnn = jnp.transpose(w_in_b[:, :n_rnn].reshape(D, 2, nblk, RNN_BW), (2, 0, 1, 3))
    w_rnn = w_rnn.reshape(nblk, D, 2 * RNN_BW)
    b_rnn = jnp.transpose(b_in2[:, :n_rnn].reshape(2, nblk, RNN_BW), (1, 0, 2))
    b_rnn = b_rnn.reshape(nblk, 1, 2 * RNN_BW)
    gate_w = (0.5 * jnp.transpose(rg_w, (2, 3, 0, 1, 4))).reshape(nblk, RNN_BW, 4 * RNN_BW).astype(BF16)
    gate_b = 0.5 * jnp.transpose(rg_b.reshape(2, 2, nblk, RNN_BW), (2, 0, 1, 3)).reshape(nblk, 1, 4 * RNN_BW)
    lam = jnp.transpose(rg_lambda.reshape(2, nblk, RNN_BW), (1, 0, 2))
    y_rnn = _rglru(h.reshape(B, S, D), w_rnn, b_rnn, conv_w, conv_b.reshape(1, D_RNN),
                   gate_w, gate_b, lam)

    outs, lses = [], []
    for g, (_, dilation) in enumerate(ATTN_GROUPS):
        qkv_g = qkv[g].reshape(B, 1, S, 3 * ATTN_WIDTH) if dilation == 1 else qkv[g]
        o, lse = _attention_group(qkv_g)
        if dilation == 1:
            o, lse = o.reshape(T, ATTN_WIDTH), lse.reshape(T, LANES)
        outs.append(o)
        lses.append(lse)

    x2, h2, aff_t = _merge(
        x2d, y_rnn.reshape(T, D_RNN), outs, lses, gmix,
        w_in_b[:, n_rnn + n_qkv:], b_in2[:, n_rnn + n_qkv:],
        p_rnn.astype(BF16), p_attn.astype(BF16), w_out.astype(BF16),
        norm_ffn.reshape(1, D), w_router.astype(BF16), b_router.reshape(1, -1), B)

    E = w_router.shape[1]
    cap = CAPACITY_FACTOR * S // E
    pos4, off4 = _route(aff_t, cap)
    pos = pos4.reshape(B, E, S)
    blk_off = off4[:, :, ::TOK_BLK // LANES, 0]
    offs = jnp.concatenate([blk_off, jnp.full((B, E, 1), cap, jnp.int32)], axis=2).reshape(-1)
    xg = _gather(offs, pos, h2.reshape(B, S, D + LANES), cap)
    eo = _ffn(xg, w_gate.astype(BF16), w_up.astype(BF16), w_down.astype(BF16))
    pos_t = jnp.transpose(pos, (0, 2, 1))
    return _combine(offs, x2.reshape(B, S, D), pos_t, eo, norm_out.reshape(1, D), cap)


def kernel(x, norm_mix, w_in, b_in, conv_w, conv_b, rg_w, rg_b, rg_lambda, p_rnn, p_attn, w_out,
           norm_ffn, w_router, b_router, w_gate, w_up, w_down, norm_final):
    depth = w_in.shape[0]
    assert depth == 1, "final rmsnorm is fused into the single layer's combine step"
    cos_t, sin_t = _rope_tables(x.shape[1])
    return _layer(x, norm_mix[0], w_in[0], b_in[0], conv_w[0], conv_b[0], rg_w[0], rg_b[0],
                  rg_lambda[0], p_rnn[0], p_attn[0], w_out[0], norm_ffn[0], w_router[0],
                  b_router[0], w_gate[0], w_up[0], w_down[0], norm_final, cos_t, sin_t)
```

```python
import functools

import jax
import jax.numpy as jnp
from jax import lax
from jax.experimental import pallas as pl
from jax.experimental.pallas import tpu as pltpu

F32 = jnp.float32
BF16 = jnp.bfloat16

D_RNN = 1024
RNN_BW = 128
CONV_WIDTH = 4
LRU_C = 8.0
HEAD_DIM = 128
HEADS = 4
ATTN_WIDTH = HEADS * HEAD_DIM
ATTN_GROUPS = ((128, 1), (512, 4), (2048, 16))
N_GROUPS = 3
ROPE_THETA = 500000.0
ROPE_DIM = HEAD_DIM // 4
RADIUS = 64
NEG_INF = -1e30
N_EXPERTS = 16
CAPACITY_FACTOR = 2
RMS_EPS = 1e-6

LANES = 128
SUBLANES = 8
BF16_ROWS = 16
VMEM_LIMIT = 56 * 1024 * 1024
F32_MIN_NORMAL = 1.1754944e-38
LOG2_E = 1.4426950408889634
LN_2 = 0.6931471805599453

PROJ_TILE = 1024
QKV_CHUNK = 512
ROW_TILE = 512
SCAN_SEG = 128
SCAN_CHUNK = 512
ATT_QBLK = 1024
ATT_SUB = 128
TOK_BLK = 256
GATHER_TOK = 1024
COMBINE_TOK = 512
SLOT_SHIFT = 6
SLOT_WIN = 1 << SLOT_SHIFT
EXPERT_SPLIT = 2
FFN_CHUNK = 512


def _cparams(sem):
    return pltpu.CompilerParams(dimension_semantics=sem, vmem_limit_bytes=VMEM_LIMIT)


def _rms(x, g):
    ms = jnp.mean(x * x, axis=-1, keepdims=True)
    return x * lax.rsqrt(ms + RMS_EPS) * g


def _pitch(d):
    return d if d <= 4 else d + 4


def _deinterleave(t, stage, d):
    n = t.shape[0]
    p = _pitch(d)
    if p == d:
        stage[0:n, :] = t
    else:
        for j in range(n // d):
            stage[j * p:j * p + d, :] = t[j * d:(j + 1) * d, :]
    return [stage[pl.ds(r, n // d, stride=p), :] for r in range(d)]


def _interleave(parts, stage, d):
    m = parts[0].shape[0]
    p = _pitch(d)
    for r in range(d):
        stage[pl.ds(r, m, stride=p), :] = parts[r]
    if p == d:
        return stage[0:m * d, :]
    return jnp.concatenate([stage[j * p:j * p + d, :] for j in range(m)], axis=0)


def _proj_f32_kernel(x_ref, g_ref, w_ref, b_ref, o_ref, *, n_chunk):
    hb = _rms(x_ref[...], g_ref[...]).astype(BF16)
    for c in range(o_ref.shape[1] // n_chunk):
        sl = slice(c * n_chunk, (c + 1) * n_chunk)
        o_ref[:, sl] = jnp.dot(hb, w_ref[:, sl], preferred_element_type=F32) + b_ref[:, sl]


def _proj_f32(x2d, g, w, b):
    T, D = x2d.shape
    N = w.shape[1]
    return pl.pallas_call(
        functools.partial(_proj_f32_kernel, n_chunk=512),
        out_shape=jax.ShapeDtypeStruct((T, N), F32),
        grid=(T // PROJ_TILE,),
        in_specs=[
            pl.BlockSpec((PROJ_TILE, D), lambda i: (i, 0)),
            pl.BlockSpec((1, D), lambda i: (0, 0)),
            pl.BlockSpec((D, N), lambda i: (0, 0)),
            pl.BlockSpec((1, N), lambda i: (0, 0)),
        ],
        out_specs=pl.BlockSpec((PROJ_TILE, N), lambda i: (i, 0)),
        compiler_params=_cparams(("parallel",)),
        name="proj_rnn",
    )(x2d, g, w, b)


def _proj_qkv_kernel(x_ref, g_ref, w0_ref, w1_ref, w2_ref, b_ref, cos_ref, sin_ref,
                     ca_ref, cb_ref, o0_ref, o1_ref, o2_ref, cao_ref, cbo_ref, stage):
    cao_ref[...] = ca_ref[...].astype(BF16)
    cbo_ref[...] = cb_ref[...].astype(BF16)
    hb = _rms(x_ref[...], g_ref[...]).astype(BF16)
    cosv = cos_ref[...]
    sinv = sin_ref[...]
    lane = lax.broadcasted_iota(jnp.int32, (1, HEAD_DIM), 1)
    first_half = lane < (ROPE_DIM // 2)
    gw = 3 * ATTN_WIDTH
    for g, (w_ref, o_ref) in enumerate(((w0_ref, o0_ref), (w1_ref, o1_ref), (w2_ref, o2_ref))):
        d = ATTN_GROUPS[g][1]
        for c in range(gw // QKV_CHUNK):
            sl = slice(c * QKV_CHUNK, (c + 1) * QKV_CHUNK)
            t2 = jnp.dot(hb, w_ref[:, sl], preferred_element_type=F32)
            t2 = t2 + b_ref[:, g * gw + c * QKV_CHUNK:g * gw + (c + 1) * QKV_CHUNK]
            for hh in range(QKV_CHUNK // HEAD_DIM):
                t = t2[:, hh * HEAD_DIM:(hh + 1) * HEAD_DIM]
                col = c * QKV_CHUNK + hh * HEAD_DIM
                if col < 2 * ATTN_WIDTH:
                    up = pltpu.roll(t, HEAD_DIM - ROPE_DIM // 2, 1)
                    down = pltpu.roll(t, ROPE_DIM // 2, 1)
                    t = t * cosv + jnp.where(first_half, up, down) * sinv
                cs = slice(col, col + HEAD_DIM)
                if d == 1:
                    o_ref[:, cs] = t.astype(BF16)
                else:
                    for r, part in enumerate(_deinterleave(t, stage, d)):
                        o_ref[0, r, :, cs] = part.astype(BF16)


def _cast_slab_spec(a, nsteps):
    rows, cols = a.shape
    assert rows % (nsteps * BF16_ROWS) == 0, (a.shape, nsteps)
    return pl.BlockSpec((rows // nsteps, cols), lambda i: (i, 0))


def _proj_qkv(x2d, g, ws, b, cos_t, sin_t, batch, cast_a, cast_b):
    T, D = x2d.shape
    S = T // batch
    nseq = S // PROJ_TILE
    nsteps = T // PROJ_TILE
    gw = 3 * ATTN_WIDTH
    d1, d2 = ATTN_GROUPS[1][1], ATTN_GROUPS[2][1]
    full = lambda a: pl.BlockSpec(a.shape, lambda i: (0,) * a.ndim)
    dil_spec = lambda d: pl.BlockSpec((1, d, PROJ_TILE // d, gw),
                                      lambda i: (i // nseq, 0, i % nseq, 0))
    return pl.pallas_call(
        _proj_qkv_kernel,
        out_shape=(jax.ShapeDtypeStruct((T, gw), BF16),
                   jax.ShapeDtypeStruct((batch, d1, S // d1, gw), BF16),
                   jax.ShapeDtypeStruct((batch, d2, S // d2, gw), BF16),
                   jax.ShapeDtypeStruct(cast_a.shape, BF16),
                   jax.ShapeDtypeStruct(cast_b.shape, BF16)),
        grid=(nsteps,),
        in_specs=[
            pl.BlockSpec((PROJ_TILE, D), lambda i: (i, 0)),
            full(g), full(ws[0]), full(ws[1]), full(ws[2]), full(b),
            pl.BlockSpec((PROJ_TILE, HEAD_DIM), lambda i: (i % nseq, 0)),
            pl.BlockSpec((PROJ_TILE, HEAD_DIM), lambda i: (i % nseq, 0)),
            _cast_slab_spec(cast_a, nsteps), _cast_slab_spec(cast_b, nsteps),
        ],
        out_specs=(pl.BlockSpec((PROJ_TILE, gw), lambda i: (i, 0)), dil_spec(d1), dil_spec(d2),
                   _cast_slab_spec(cast_a, nsteps), _cast_slab_spec(cast_b, nsteps)),
        scratch_shapes=[pltpu.VMEM((PROJ_TILE // d2 * _pitch(d2), LANES), F32)],
        compiler_params=_cparams(("parallel",)),
        name="proj_qkv",
    )(x2d, g, ws[0], ws[1], ws[2], b, cos_t, sin_t, cast_a, cast_b)


def _rope_tables(seq):
    pos = jnp.arange(seq, dtype=F32)
    inv = ROPE_THETA ** (-jnp.arange(0, ROPE_DIM, 2, dtype=F32) / ROPE_DIM)
    ang = pos[:, None] * inv[None, :]
    cos, sin = jnp.cos(ang), jnp.sin(ang)
    pad = HEAD_DIM - ROPE_DIM
    cos_t = jnp.concatenate([cos, cos, jnp.ones((seq, pad), F32)], axis=1)
    sin_t = jnp.concatenate([-sin, sin, jnp.zeros((seq, pad), F32)], axis=1)
    return cos_t, sin_t


def _sigmoid(x):
    return 0.5 * jnp.tanh(0.5 * x) + 0.5


def _rglru_kernel(xr_ref, gr_ref, cw_ref, cb_ref, gw_ref, gb_ref, lam_ref, o_ref,
                  xpad, af, uf, ab, ub, cf, cbk, *, seq):
    nseg = seq // SCAN_SEG
    nchunk = seq // SCAN_CHUNK
    seg_per_chunk = SCAN_CHUNK // SCAN_SEG
    pitch = nseg + 4

    cw = cw_ref[...]
    cbias = cb_ref[...]
    gbias = gb_ref[...]
    z = -lam_ref[...]
    sp = jnp.maximum(z, 0.0) + jnp.log1p(jnp.exp(-jnp.abs(z)))
    half_neg_c_sp = (-0.5 * LRU_C) * sp

    zero_rows = jnp.zeros((SUBLANES, LANES), F32)
    xpad[0:SUBLANES, :] = zero_rows
    xpad[seq + SUBLANES:seq + 2 * SUBLANES, :] = zero_rows

    def copy_in(c, carry):
        t0 = pl.multiple_of(c * SCAN_CHUNK, SCAN_CHUNK)
        xpad[pl.ds(t0 + SUBLANES, SCAN_CHUNK), :] = xr_ref[0, pl.ds(t0, SCAN_CHUNK), :]
        return carry

    lax.fori_loop(0, nchunk, copy_in, 0)

    def stage1(c, carry):
        t0 = c * SCAN_CHUNK
        xc = cbias
        for tap in range(CONV_WIDTH):
            lo = t0 + SUBLANES - CONV_WIDTH // 2 + tap
            xc = xc + cw[tap:tap + 1, :] * xpad[pl.ds(lo, SCAN_CHUNK), :]
        th = jnp.tanh(jnp.dot(xc.astype(BF16), gw_ref[...], preferred_element_type=F32) + gbias)
        xc_half = 0.5 * xc
        for d, (a_s, u_s) in enumerate(((af, uf), (ab, ub))):
            t_r = th[:, (2 * d) * LANES:(2 * d + 1) * LANES]
            t_i = th[:, (2 * d + 1) * LANES:(2 * d + 2) * LANES]
            half_k = half_neg_c_sp[d:d + 1, :]
            log_a = t_r * half_k + half_k
            a = jnp.exp(log_a)
            v = jnp.tanh(log_a) * (-1.0 - a * a)
            u = (xc_half * (v * lax.rsqrt(jnp.maximum(v, F32_MIN_NORMAL)))) * (t_i + 1.0)
            for k in range(seg_per_chunk):
                seg = c * seg_per_chunk + k
                a_s[pl.ds(seg, SCAN_SEG, stride=pitch), :] = a[k * SCAN_SEG:(k + 1) * SCAN_SEG, :]
                u_s[pl.ds(seg, SCAN_SEG, stride=pitch), :] = u[k * SCAN_SEG:(k + 1) * SCAN_SEG, :]
        return carry

    lax.fori_loop(0, nchunk, stage1, 0, unroll=4)

    def scan_dir(a_s, u_s, reverse):
        def body(it, carry):
            i = (SCAN_SEG - 1 - it) if reverse else it
            h, cum = carry
            rows = pl.ds(i * pitch, nseg)
            a = a_s[rows, :]
            u = u_s[rows, :]
            h = a * h + u
            cum = a * cum
            u_s[rows, :] = h
            a_s[rows, :] = cum
            return h, cum

        return lax.fori_loop(0, SCAN_SEG, body,
                             (jnp.zeros((nseg, LANES), F32), jnp.ones((nseg, LANES), F32)))

    end_h, end_a = scan_dir(af, uf, False)
    beg_h, beg_a = scan_dir(ab, ub, True)

    c = jnp.zeros((1, LANES), F32)
    for s in range(nseg):
        cf[s:s + 1, :] = c
        c = end_h[s:s + 1, :] + end_a[s:s + 1, :] * c
    c = jnp.zeros((1, LANES), F32)
    for s in range(nseg - 1, -1, -1):
        cbk[s:s + 1, :] = c
        c = beg_h[s:s + 1, :] + beg_a[s:s + 1, :] * c

    def finish(s, carry):
        t0 = pl.multiple_of(s * SCAN_SEG, SCAN_SEG)
        rows = pl.ds(s, SCAN_SEG, stride=pitch)
        hf = uf[rows, :] + af[rows, :] * cf[pl.ds(s, 1), :]
        hb = ub[rows, :] + ab[rows, :] * cbk[pl.ds(s, 1), :]
        g = jax.nn.gelu(gr_ref[0, pl.ds(t0, SCAN_SEG), :])
        o_ref[0, pl.ds(t0, SCAN_SEG), :] = (g * (hf + hb)).astype(BF16)
        return carry

    lax.fori_loop(0, nseg, finish, 0)


def _rglru(xrgr, conv_w, conv_b, gate_w, gate_b, lam):
    B, S, _ = xrgr.shape
    nblk = D_RNN // RNN_BW
    nseg = S // SCAN_SEG
    scr = pltpu.VMEM((SCAN_SEG * (nseg + 4), LANES), F32)
    return pl.pallas_call(
        functools.partial(_rglru_kernel, seq=S),
        out_shape=jax.ShapeDtypeStruct((B, S, D_RNN), BF16),
        grid=(B, nblk),
        in_specs=[
            pl.BlockSpec((1, S, RNN_BW), lambda b, c: (b, 0, c)),
            pl.BlockSpec((1, S, RNN_BW), lambda b, c: (b, 0, nblk + c)),
            pl.BlockSpec((CONV_WIDTH, RNN_BW), lambda b, c: (0, c)),
            pl.BlockSpec((1, RNN_BW), lambda b, c: (0, c)),
            pl.BlockSpec((None, RNN_BW, 4 * RNN_BW), lambda b, c: (c, 0, 0)),
            pl.BlockSpec((None, 1, 4 * RNN_BW), lambda b, c: (c, 0, 0)),
            pl.BlockSpec((None, 2, RNN_BW), lambda b, c: (c, 0, 0)),
        ],
        out_specs=pl.BlockSpec((1, S, RNN_BW), lambda b, c: (b, 0, c)),
        scratch_shapes=[pltpu.VMEM((S + 2 * SUBLANES, LANES), F32), scr, scr, scr, scr,
                        pltpu.VMEM((nseg, LANES), F32), pltpu.VMEM((nseg, LANES), F32)],
        compiler_params=_cparams(("parallel", "parallel")),
        name="rglru",
    )(xrgr, xrgr, conv_w, conv_b, gate_w, gate_b, lam)


def _attn_kernel(q_ref, k_ref, kp_ref, kn_ref, v_ref, vp_ref, vn_ref, o_ref, st_ref,
                 kext, vext, *, qblk, length):
    j = pl.program_id(2)
    nkey = ATT_SUB + 2 * RADIUS
    qi = lax.broadcasted_iota(jnp.int32, (ATT_SUB, nkey), 0)
    ki = lax.broadcasted_iota(jnp.int32, (ATT_SUB, nkey), 1)
    band_bias = jnp.where(jnp.abs(ki - RADIUS - qi) <= RADIUS, 0.0, NEG_INF)
    krow = lax.broadcasted_iota(jnp.int32, (1, nkey), 1)
    lane = lax.broadcasted_iota(jnp.int32, (ATT_SUB, LANES), 1)
    scale2 = HEAD_DIM ** -0.5 * LOG2_E

    for rr in range(q_ref.shape[1]):
        kx, vx = kext.at[rr], vext.at[rr]
        kx[0:RADIUS, :] = kp_ref[0, rr]
        kx[RADIUS:RADIUS + qblk, :] = k_ref[0, rr]
        kx[RADIUS + qblk:RADIUS + qblk + RADIUS, :] = kn_ref[0, rr]
        for h in range(HEADS):
            src = slice(h * HEAD_DIM, (h + 1) * HEAD_DIM)
            dst = slice(2 * h * HEAD_DIM, (2 * h + 1) * HEAD_DIM)
            vx[0:RADIUS, dst] = vp_ref[0, rr, :, src]
            vx[RADIUS:RADIUS + qblk, dst] = v_ref[0, rr, :, src]
            vx[RADIUS + qblk:RADIUS + qblk + RADIUS, dst] = vn_ref[0, rr, :, src]
            vx[:, (2 * h + 1) * HEAD_DIM:(2 * h + 2) * HEAD_DIM] = jnp.ones(
                (qblk + 2 * RADIUS, HEAD_DIM), BF16)

        for i in range(qblk // ATT_SUB):
            q0 = i * ATT_SUB
            kglob = j * qblk + q0 - RADIUS + krow
            edge_bias = jnp.where((kglob >= 0) & (kglob < length), 0.0, NEG_INF)
            bias = band_bias + edge_bias
            st_tile = jnp.ones((ATT_SUB, LANES), F32)
            for h in range(HEADS):
                hs = slice(h * HEAD_DIM, (h + 1) * HEAD_DIM)
                q = q_ref[0, rr, q0:q0 + ATT_SUB, hs]
                ks = kx[q0:q0 + nkey, hs]
                vs = vx[q0:q0 + nkey, 2 * h * HEAD_DIM:(2 * h + 2) * HEAD_DIM]
                s = lax.dot_general(q, ks, (((1,), (1,)), ((), ())),
                                    preferred_element_type=F32) * scale2 + bias
                m = jnp.max(s, axis=-1, keepdims=True)
                p = jnp.exp2(s - m)
                pv = jnp.dot(p.astype(BF16), vs, preferred_element_type=F32)
                l = pv[:, HEAD_DIM:2 * HEAD_DIM]
                o_ref[0, rr, q0:q0 + ATT_SUB, hs] = pv[:, 0:HEAD_DIM]
                st_tile = jnp.where(lane == h, m * LN_2,
                                    jnp.where(lane == HEADS + h, l, st_tile))
            st_ref[0, rr, q0:q0 + ATT_SUB, :] = st_tile


def _attention_group(qkv_g):
    B, d, L, _ = qkv_g.shape
    qblk = min(L, ATT_QBLK)
    nblk = L // qblk
    rpb = min(d, max(1, ATT_QBLK // L))
    halo_per_blk = qblk // RADIUS
    last_halo = L // RADIUS - 1

    main = lambda part: pl.BlockSpec((1, rpb, qblk, ATTN_WIDTH), lambda b, r, j: (b, r, j, part))
    prev = lambda part: pl.BlockSpec(
        (1, rpb, RADIUS, ATTN_WIDTH),
        lambda b, r, j: (b, r, jnp.maximum(j * halo_per_blk - 1, 0), part))
    nxt = lambda part: pl.BlockSpec(
        (1, rpb, RADIUS, ATTN_WIDTH),
        lambda b, r, j: (b, r, jnp.minimum((j + 1) * halo_per_blk, last_halo), part))
    return pl.pallas_call(
        functools.partial(_attn_kernel, qblk=qblk, length=L),
        out_shape=(jax.ShapeDtypeStruct((B, d, L, ATTN_WIDTH), F32),
                   jax.ShapeDtypeStruct((B, d, L, LANES), F32)),
        grid=(B, d // rpb, nblk),
        in_specs=[main(0), main(1), prev(1), nxt(1), main(2), prev(2), nxt(2)],
        out_specs=(pl.BlockSpec((1, rpb, qblk, ATTN_WIDTH), lambda b, r, j: (b, r, j, 0)),
                   pl.BlockSpec((1, rpb, qblk, LANES), lambda b, r, j: (b, r, j, 0))),
        scratch_shapes=[pltpu.VMEM((rpb, qblk + 2 * RADIUS, ATTN_WIDTH), BF16),
                        pltpu.VMEM((rpb, qblk + 2 * RADIUS, 2 * ATTN_WIDTH), BF16)],
        compiler_params=_cparams(("parallel", "parallel", "parallel")),
        name=f"attn_d{d}",
    )(qkv_g, qkv_g, qkv_g, qkv_g, qkv_g, qkv_g, qkv_g)


def _merge_kernel(x_ref, yr_ref, o0_ref, o1_ref, o2_ref, l0_ref, l1_ref, l2_ref,
                  gmix_ref, wg_ref, bg_ref, prnn_ref, pattn_ref, wout_ref,
                  gffn_ref, wr_ref, wrt_ref, br_ref, brt_ref, cc_ref,
                  x2_ref, h2_ref, afft_ref, cco_ref, yattn, stage):
    cco_ref[...] = cc_ref[...].astype(BF16)
    d = x_ref.shape[1]
    x = x_ref[...]
    hb = _rms(x, gmix_ref[...]).astype(BF16)

    def natural(ref, g, cs):
        dil = ATTN_GROUPS[g][1]
        if dil == 1:
            return ref[:, cs]
        return _interleave([ref[0, r, :, cs] for r in range(dil)], stage, dil)

    o_refs = (o0_ref, o1_ref, o2_ref)
    all_lanes = slice(0, LANES)
    stats = [natural(r, g, all_lanes) for g, r in enumerate((l0_ref, l1_ref, l2_ref))]
    head_lane = lax.broadcasted_iota(jnp.int32, stats[0].shape, 1) < HEADS
    sums = [jnp.where(head_lane, pltpu.roll(st, LANES - HEADS, 1), 1.0) for st in stats]
    lses = [st + jnp.log(l) for st, l in zip(stats, sums)]
    m = jnp.maximum(jnp.maximum(lses[0], lses[1]), lses[2])
    ws = [jnp.exp(l - m) for l in lses]
    den = ws[0] + ws[1] + ws[2]
    coef = [w / (den * l) for w, l in zip(ws, sums)]
    for h in range(HEADS):
        hs = slice(h * HEAD_DIM, (h + 1) * HEAD_DIM)
        acc = coef[0][:, h:h + 1] * natural(o_refs[0], 0, hs)
        acc = acc + coef[1][:, h:h + 1] * natural(o_refs[1], 1, hs)
        acc = acc + coef[2][:, h:h + 1] * natural(o_refs[2], 2, hs)
        yattn[:, hs] = acc.astype(BF16)

    branch_a = jnp.dot(yr_ref[...], prnn_ref[...], preferred_element_type=F32)
    branch_b = jnp.dot(yattn[...], pattn_ref[...], preferred_element_type=F32)
    ga = jnp.dot(hb, wg_ref[:, 0:d], preferred_element_type=F32) + bg_ref[:, 0:d]
    gb = jnp.dot(hb, wg_ref[:, d:2 * d], preferred_element_type=F32) + bg_ref[:, d:2 * d]
    merged = _sigmoid(ga) * branch_a + _sigmoid(gb) * branch_b
    x2 = x + jnp.dot(merged.astype(BF16), wout_ref[...], preferred_element_type=F32)
    x2_ref[...] = x2

    h2 = _rms(x2, gffn_ref[...]).astype(BF16)
    h2_ref[:, 0:d] = h2
    n_exp = afft_ref.shape[1]
    logits = jnp.dot(h2, wr_ref[...], preferred_element_type=F32) + br_ref[...]
    lane = lax.broadcasted_iota(jnp.int32, logits.shape, 1)
    mx = jnp.max(logits, axis=-1, keepdims=True)
    e = jnp.exp(logits - mx)
    den = jnp.sum(jnp.where(lane < n_exp, e, 0.0), axis=-1, keepdims=True)
    aff = e / den
    head = aff.astype(BF16).astype(F32)
    tile = jnp.where(lane < n_exp, head, jnp.where(lane < 2 * n_exp, aff - head, 0.0))
    h2_ref[:, d:d + LANES] = tile.astype(BF16)
    logits_t = lax.dot_general(wrt_ref[...], h2, (((1,), (1,)), ((), ())),
                               preferred_element_type=F32) + brt_ref[...]
    mxt = jnp.max(logits_t, axis=0, keepdims=True)
    et = jnp.exp(logits_t - mxt)
    afft_ref[0] = et / jnp.sum(et, axis=0, keepdims=True)


def _merge(x2d, y_rnn, outs, lses, gmix, wg, bg, p_rnn, p_attn, w_out, gffn, wr, br, batch,
           cast_c):
    T, D = x2d.shape
    S = T // batch
    nseq = S // ROW_TILE
    nsteps = T // ROW_TILE
    E = wr.shape[1]
    row = lambda w: pl.BlockSpec((ROW_TILE, w), lambda i: (i, 0))
    full = lambda a: pl.BlockSpec(a.shape, lambda i: (0,) * a.ndim)

    def attn_spec(g, w):
        dil = ATTN_GROUPS[g][1]
        if dil == 1:
            return row(w)
        return pl.BlockSpec((1, dil, ROW_TILE // dil, w), lambda i: (i // nseq, 0, i % nseq, 0))

    wrt = wr.T
    brt = br.reshape(E, 1)
    wr_pad = jnp.concatenate([wr, wr, jnp.zeros((D, LANES - 2 * E), wr.dtype)], axis=1)
    br_pad = jnp.concatenate([br, br, jnp.full((1, LANES - 2 * E), NEG_INF, br.dtype)], axis=1)
    consts = [gmix, wg, bg, p_rnn, p_attn, w_out, gffn, wr_pad, wrt, br_pad, brt]
    args = [x2d, y_rnn, *outs, *lses, *consts, cast_c]
    in_specs = ([row(D), row(D_RNN)] + [attn_spec(g, ATTN_WIDTH) for g in range(N_GROUPS)]
                + [attn_spec(g, LANES) for g in range(N_GROUPS)] + [full(a) for a in consts]
                + [_cast_slab_spec(cast_c, nsteps)])
    dmax = max(dil for _, dil in ATTN_GROUPS)
    return pl.pallas_call(
        _merge_kernel,
        out_shape=(jax.ShapeDtypeStruct((T, D), F32),
                   jax.ShapeDtypeStruct((T, D + LANES), BF16),
                   jax.ShapeDtypeStruct((batch, E, S), F32),
                   jax.ShapeDtypeStruct(cast_c.shape, BF16)),
        grid=(nsteps,),
        in_specs=in_specs,
        out_specs=(row(D), row(D + LANES),
                   pl.BlockSpec((1, E, ROW_TILE), lambda i: (i // nseq, 0, i % nseq)),
                   _cast_slab_spec(cast_c, nsteps)),
        scratch_shapes=[pltpu.VMEM((ROW_TILE, ATTN_WIDTH), BF16),
                        pltpu.VMEM((ROW_TILE // dmax * _pitch(dmax), LANES), F32)],
        compiler_params=_cparams(("parallel",)),
        name="merge",
    )(*args)


def _route_kernel(aff_ref, pos_ref, off_ref, *, cap):
    a = aff_ref[0]
    E, R, _ = a.shape
    bits = pltpu.bitcast(a, jnp.int32)

    def count(mask):
        c = jnp.sum(mask.astype(F32), axis=2, keepdims=True)
        return jnp.sum(c, axis=1, keepdims=True)

    def bs(_, carry):
        lo, hi = carry
        mid = lo + ((hi - lo + 1) >> 1)
        ok = count(bits >= mid) >= cap
        return jnp.where(ok, mid, lo), jnp.where(ok, hi, mid - 1)

    lo0 = jnp.zeros((E, 1, 1), jnp.int32)
    hi0 = jnp.full((E, 1, 1), 0x7F7FFFFF, jnp.int32)
    thr, _ = lax.fori_loop(0, 31, bs, (lo0, hi0))

    gt = bits > thr
    eq = bits == thr
    need = cap - count(gt)

    ki = lax.broadcasted_iota(jnp.int32, (LANES, LANES), 0)
    li = lax.broadcasted_iota(jnp.int32, (LANES, LANES), 1)
    upper = (ki <= li).astype(BF16)
    allones = jnp.ones((LANES, LANES), BF16)
    ri = lax.broadcasted_iota(jnp.int32, (R, R), 0)
    ci = lax.broadcasted_iota(jnp.int32, (R, R), 1)
    strict = (ci < ri).astype(BF16)

    def prefix(mask):
        mb = mask.astype(BF16).reshape(E * R, LANES)
        within = jnp.dot(mb, upper, preferred_element_type=F32)
        totals = jnp.dot(mb, allones, preferred_element_type=F32).astype(BF16)
        offs = []
        for e in range(E):
            offs.append(jnp.dot(strict, totals[e * R:(e + 1) * R, :],
                                preferred_element_type=F32))
        rowoff = jnp.concatenate(offs, axis=0)
        excl = within - mb.astype(F32) + rowoff
        return excl.reshape(E, R, LANES), rowoff.reshape(E, R, LANES)

    eq_rank, _ = prefix(eq)
    sel = gt | (eq & (eq_rank < need.astype(F32)))
    slot, rowoff = prefix(sel)
    pos_ref[0] = jnp.where(sel, slot.astype(jnp.int32), -1)
    off_ref[0] = rowoff.astype(jnp.int32)


def _route(aff_t, cap):
    B, E, S = aff_t.shape
    R = S // LANES
    a4 = aff_t.reshape(B, E, R, LANES)
    blk = pl.BlockSpec((1, E, R, LANES), lambda b: (b, 0, 0, 0))
    pos, off = pl.pallas_call(
        functools.partial(_route_kernel, cap=cap),
        out_shape=(jax.ShapeDtypeStruct((B, E, R, LANES), jnp.int32),
                   jax.ShapeDtypeStruct((B, E, R, LANES), jnp.int32)),
        grid=(B,),
        in_specs=[blk],
        out_specs=(blk, blk),
        compiler_params=_cparams(("parallel",)),
        name="route",
    )(a4)
    return pos, off


def _slot_windows(off_ref, base, n_exp, ntb, tb):
    starts = []
    rounds = jnp.int32(0)
    for e in range(n_exp):
        idx = (base + e) * (ntb + 1) + tb
        start = (off_ref[idx] >> 4) << 4
        starts.append(start)
        rounds = jnp.maximum(rounds, (off_ref[idx + 1] - start + SLOT_WIN - 1) >> SLOT_SHIFT)
    return starts, rounds


def _gather_kernel(off_ref, pos_ref, h_ref, o_ref, *, cap, ntb, n_exp, nsub):
    b = pl.program_id(0)
    grp = pl.program_id(1)
    t = pl.program_id(2)

    @pl.when(t == 0)
    def _():
        o_ref[...] = jnp.zeros_like(o_ref)

    base = (b * pl.num_programs(1) + grp) * n_exp
    ji = lax.broadcasted_iota(jnp.int32, (SLOT_WIN, TOK_BLK), 0)
    for sb in range(nsub):
        toks = slice(sb * TOK_BLK, (sb + 1) * TOK_BLK)
        starts, rounds = _slot_windows(off_ref, base, n_exp, ntb, t * nsub + sb)

        def one_round(k, carry, toks=toks, starts=starts):
            pieces, rows = [], []
            for e in range(n_exp):
                first = starts[e] + k * SLOT_WIN
                row0 = pl.multiple_of(jnp.minimum(first, cap - SLOT_WIN), BF16_ROWS)
                slot = ji + row0
                hit = (slot == pos_ref[0, 0, e:e + 1, toks]) & (slot >= first)
                pieces.append(jnp.where(hit, 1.0, 0.0).astype(BF16))
                rows.append(row0)
            onehot = jnp.concatenate(pieces, axis=0)
            g = jnp.dot(onehot, h_ref[0, toks, :], preferred_element_type=F32)
            for e in range(n_exp):
                o_ref[0, e, pl.ds(rows[e], SLOT_WIN), :] += (
                    g[e * SLOT_WIN:(e + 1) * SLOT_WIN, :].astype(BF16))
            return carry

        one_round(0, 0)
        lax.fori_loop(1, rounds, one_round, 0)


def _gather(offs, pos, h2, cap):
    B, E, S = pos.shape
    D = h2.shape[-1]
    ntb = S // TOK_BLK
    n_exp = E // EXPERT_SPLIT
    nsub = GATHER_TOK // TOK_BLK
    pos4 = pos.reshape(B, EXPERT_SPLIT, n_exp, S)
    return pl.pallas_call(
        functools.partial(_gather_kernel, cap=cap, ntb=ntb, n_exp=n_exp, nsub=nsub),
        out_shape=jax.ShapeDtypeStruct((B, E, cap, D), BF16),
        grid_spec=pltpu.PrefetchScalarGridSpec(
            num_scalar_prefetch=1,
            grid=(B, EXPERT_SPLIT, S // GATHER_TOK),
            in_specs=[
                pl.BlockSpec((1, 1, n_exp, GATHER_TOK), lambda b, g, t, off: (b, g, 0, t)),
                pl.BlockSpec((1, GATHER_TOK, D), lambda b, g, t, off: (b, t, 0)),
            ],
            out_specs=pl.BlockSpec((1, n_exp, cap, D), lambda b, g, t, off: (b, g, 0, 0)),
        ),
        compiler_params=_cparams(("parallel", "parallel", "arbitrary")),
        name="gather",
    )(offs, pos4, h2)


def _ffn_kernel(x_ref, wg_ref, wu_ref, wd_ref, o_ref, acc, *, f_chunk):
    e = pl.program_id(0)
    d = wg_ref.shape[1]
    x = x_ref[0, 0, :, 0:d]
    gl = x_ref[0, 0, :, d:d + LANES].astype(F32)
    lane = lax.broadcasted_iota(jnp.int32, gl.shape, 1)
    gate = jnp.sum(jnp.where((lane == e) | (lane == e + pl.num_programs(0)), gl, 0.0),
                   axis=1, keepdims=True)
    nf = wg_ref.shape[2] // f_chunk
    for f in range(nf):
        fs = slice(f * f_chunk, (f + 1) * f_chunk)
        g = jnp.dot(x, wg_ref[0, :, fs], preferred_element_type=F32)
        u = jnp.dot(x, wu_ref[0, :, fs], preferred_element_type=F32)
        hid = (g * _sigmoid(g) * u).astype(BF16)
        part = jnp.dot(hid, wd_ref[0, fs, :], preferred_element_type=F32)
        if f == 0:
            acc[...] = part
        else:
            acc[...] += part
    o_ref[0, 0] = (acc[...] * gate).astype(BF16)


def _ffn(xg, w_gate, w_up, w_down):
    B, E, C, DX = xg.shape
    D = w_gate.shape[1]
    F = w_gate.shape[2]
    assert F % FFN_CHUNK == 0, (F, FFN_CHUNK)
    return pl.pallas_call(
        functools.partial(_ffn_kernel, f_chunk=FFN_CHUNK),
        out_shape=jax.ShapeDtypeStruct((B, E, C, D), BF16),
        grid=(E, B),
        in_specs=[
            pl.BlockSpec((1, 1, C, DX), lambda e, b: (b, e, 0, 0)),
            pl.BlockSpec((1, D, F), lambda e, b: (e, 0, 0)),
            pl.BlockSpec((1, D, F), lambda e, b: (e, 0, 0)),
            pl.BlockSpec((1, F, D), lambda e, b: (e, 0, 0)),
        ],
        out_specs=pl.BlockSpec((1, 1, C, D), lambda e, b: (b, e, 0, 0)),
        scratch_shapes=[pltpu.VMEM((C, D), F32)],
        compiler_params=_cparams(("arbitrary", "arbitrary")),
        name="ffn",
    )(xg, w_gate, w_up, w_down)


def _combine_kernel(off_ref, x_ref, pos_ref, eo_ref, g_ref, o_ref, wstack, acc, *, cap, ntb):
    b = pl.program_id(0)
    t = pl.program_id(1)
    E = pos_ref.shape[2]
    nsub = pos_ref.shape[1] // TOK_BLK
    lane = lax.broadcasted_iota(jnp.int32, (TOK_BLK, LANES), 1)
    low = lane < SLOT_WIN
    acc[...] = x_ref[0]

    for sb in range(nsub):
        toks = slice(sb * TOK_BLK, (sb + 1) * TOK_BLK)
        pos = pos_ref[0, toks, :]
        starts, rounds = _slot_windows(off_ref, b * E, E, ntb, t * nsub + sb)

        def one_round(k, carry, toks=toks, pos=pos, starts=starts, wst=wstack.at[sb]):
            onehot = []
            for p in range(E // 2):
                firsts, row0s = [], []
                for e in (2 * p, 2 * p + 1):
                    first = starts[e] + k * SLOT_WIN
                    row0 = pl.multiple_of(jnp.minimum(first, cap - SLOT_WIN), BF16_ROWS)
                    wst[e * SLOT_WIN:(e + 1) * SLOT_WIN, :] = (
                        eo_ref[0, e, pl.ds(row0, SLOT_WIN), :])
                    firsts.append(first)
                    row0s.append(row0)
                slot = lane + jnp.where(low, row0s[0], row0s[1] - SLOT_WIN)
                first = jnp.where(low, firsts[0], firsts[1])
                pcol = jnp.where(low, pos[:, 2 * p:2 * p + 1], pos[:, 2 * p + 1:2 * p + 2])
                hit = (slot == pcol) & (slot >= first)
                onehot.append(jnp.where(hit, 1.0, 0.0).astype(BF16))
            acc[toks, :] += jnp.dot(jnp.concatenate(onehot, axis=1), wst[...],
                                    preferred_element_type=F32)
            return carry

        one_round(0, 0)
        lax.fori_loop(1, rounds, one_round, 0)
    o_ref[0] = _rms(acc[...], g_ref[...])


def _combine(offs, x2, pos_t, eo, g_final, cap):
    B, S, D = x2.shape
    E = pos_t.shape[2]
    ntb = S // TOK_BLK
    return pl.pallas_call(
        functools.partial(_combine_kernel, cap=cap, ntb=ntb),
        out_shape=jax.ShapeDtypeStruct((B, S, D), F32),
        grid_spec=pltpu.PrefetchScalarGridSpec(
            num_scalar_prefetch=1,
            grid=(B, S // COMBINE_TOK),
            in_specs=[
                pl.BlockSpec((1, COMBINE_TOK, D), lambda b, t, off: (b, t, 0)),
                pl.BlockSpec((1, COMBINE_TOK, E), lambda b, t, off: (b, t, 0)),
                pl.BlockSpec((1, E, cap, D), lambda b, t, off: (b, 0, 0, 0),
                             pipeline_mode=pl.Buffered(1)),
                pl.BlockSpec((1, D), lambda b, t, off: (0, 0)),
            ],
            out_specs=pl.BlockSpec((1, COMBINE_TOK, D), lambda b, t, off: (b, t, 0)),
            scratch_shapes=[pltpu.VMEM((COMBINE_TOK // TOK_BLK, E * SLOT_WIN, D), BF16),
                            pltpu.VMEM((COMBINE_TOK, D), F32)],
        ),
        compiler_params=_cparams(("parallel", "arbitrary")),
        name="combine",
    )(offs, x2, pos_t, eo, g_final)


def _layer(x, norm_mix, w_in, b_in, conv_w, conv_b, rg_w, rg_b, rg_lambda, p_rnn, p_attn,
           w_out, norm_ffn, w_router, b_router, w_gate, w_up, w_down, norm_out, cos_t, sin_t):
    B, S, D = x.shape
    T = B * S
    x2d = x.reshape(T, D)
    n_rnn = 2 * D_RNN
    n_qkv = 3 * N_GROUPS * ATTN_WIDTH
    gmix = norm_mix.reshape(1, D)
    w_in_b = w_in.astype(BF16)
    b_in2 = b_in.reshape(1, -1)

    xrgr = _proj_f32(x2d, gmix, w_in_b[:, :n_rnn], b_in2[:, :n_rnn])

    def group_cols(a, g):
        part = lambda p: a[:, n_rnn + (p * N_GROUPS + g) * ATTN_WIDTH:
                           n_rnn + (p * N_GROUPS + g + 1) * ATTN_WIDTH]
        return jnp.concatenate([part(0), part(1), part(2)], axis=1)

    ws = [group_cols(w_in_b, g) for g in range(N_GROUPS)]
    b_qkv = jnp.concatenate([group_cols(b_in2, g) for g in range(N_GROUPS)], axis=1)
    n_e, d_in, d_ff = w_gate.shape
    *qkv, w_gate_b, w_up_b = _proj_qkv(
        x2d, gmix, ws, b_qkv, cos_t, sin_t, B,
        w_gate.reshape(n_e * d_in, d_ff), w_up.reshape(n_e * d_in, d_ff))

    nblk = D_RNN // RNN_BW
    gate_w = (0.5 * jnp.transpose(rg_w, (2, 3, 0, 1, 4))).reshape(nblk, RNN_BW, 4 * RNN_BW).astype(BF16)
    gate_b = 0.5 * jnp.transpose(rg_b.reshape(2, 2, nblk, RNN_BW), (2, 0, 1, 3)).reshape(nblk, 1, 4 * RNN_BW)
    lam = jnp.transpose(rg_lambda.reshape(2, nblk, RNN_BW), (1, 0, 2))
    y_rnn = _rglru(xrgr.reshape(B, S, n_rnn), conv_w, conv_b.reshape(1, D_RNN), gate_w, gate_b, lam)

    outs, lses = [], []
    for g, (_, dilation) in enumerate(ATTN_GROUPS):
        qkv_g = qkv[g].reshape(B, 1, S, 3 * ATTN_WIDTH) if dilation == 1 else qkv[g]
        o, lse = _attention_group(qkv_g)
        if dilation == 1:
            o, lse = o.reshape(T, ATTN_WIDTH), lse.reshape(T, LANES)
        outs.append(o)
        lses.append(lse)

    x2, h2, aff_t, w_down_b = _merge(
        x2d, y_rnn.reshape(T, D_RNN), outs, lses, gmix,
        w_in_b[:, n_rnn + n_qkv:], b_in2[:, n_rnn + n_qkv:],
        p_rnn.astype(BF16), p_attn.astype(BF16), w_out.astype(BF16),
        norm_ffn.reshape(1, D), w_router.astype(BF16), b_router.reshape(1, -1), B,
        w_down.reshape(n_e * d_ff, d_in))

    E = w_router.shape[1]
    cap = CAPACITY_FACTOR * S // E
    pos4, off4 = _route(aff_t, cap)
    pos = pos4.reshape(B, E, S)
    blk_off = off4[:, :, ::TOK_BLK // LANES, 0]
    offs = jnp.concatenate([blk_off, jnp.full((B, E, 1), cap, jnp.int32)], axis=2).reshape(-1)
    xg = _gather(offs, pos, h2.reshape(B, S, D + LANES), cap)
    eo = _ffn(xg, w_gate_b.reshape(n_e, d_in, d_ff), w_up_b.reshape(n_e, d_in, d_ff),
              w_down_b.reshape(n_e, d_ff, d_in))
    pos_t = jnp.transpose(pos, (0, 2, 1))
    return _combine(offs, x2.reshape(B, S, D), pos_t, eo, norm_out.reshape(1, D), cap)


def kernel(x, norm_mix, w_in, b_in, conv_w, conv_b, rg_w, rg_b, rg_lambda, p_rnn, p_attn, w_out,
           norm_ffn, w_router, b_router, w_gate, w_up, w_down, norm_final):
    depth = w_in.shape[0]
    assert depth == 1, "final rmsnorm is fused into the single layer's combine step"
    cos_t, sin_t = _rope_tables(x.shape[1])
    return _layer(x, norm_mix[0], w_in[0], b_in[0], conv_w[0], conv_b[0], rg_w[0], rg_b[0],
                  rg_lambda[0], p_rnn[0], p_attn[0], w_out[0], norm_ffn[0], w_router[0],
                  b_router[0], w_gate[0], w_up[0], w_down[0], norm_final, cos_t, sin_t)
```

```python
import functools

import jax
import jax.numpy as jnp
from jax import lax
from jax.experimental import pallas as pl
from jax.experimental.pallas import tpu as pltpu

F32 = jnp.float32
BF16 = jnp.bfloat16

D_RNN = 1024
RNN_BW = 128
CONV_WIDTH = 4
LRU_C = 8.0
HEAD_DIM = 128
HEADS = 4
ATTN_WIDTH = HEADS * HEAD_DIM
ATTN_GROUPS = ((128, 1), (512, 4), (2048, 16))
N_GROUPS = 3
ROPE_THETA = 500000.0
ROPE_DIM = HEAD_DIM // 4
RADIUS = 64
NEG_INF = -1e30
N_EXPERTS = 16
CAPACITY_FACTOR = 2
RMS_EPS = 1e-6

LANES = 128
SUBLANES = 8
BF16_ROWS = 16
VMEM_LIMIT = 56 * 1024 * 1024
F32_MIN_NORMAL = 1.1754944e-38
LOG2_E = 1.4426950408889634
LN_2 = 0.6931471805599453

PROJ_TILE = 1024
QKV_CHUNK = 512
ROW_TILE = 512
SCAN_SEG = 128
SCAN_CHUNK = 512
ATT_QBLK = 1024
ATT_SUB = 128
TOK_BLK = 256
GATHER_TOK = 1024
COMBINE_TOK = 512
SLOT_SHIFT = 6
SLOT_WIN = 1 << SLOT_SHIFT
EXPERT_SPLIT = 2
FFN_CHUNK = 512
SEARCH_STEPS = 17


def _cparams(sem):
    return pltpu.CompilerParams(dimension_semantics=sem, vmem_limit_bytes=VMEM_LIMIT)


def _rms(x, g):
    ms = jnp.mean(x * x, axis=-1, keepdims=True)
    return x * lax.rsqrt(ms + RMS_EPS) * g


def _pitch(d):
    return d if d <= 4 else d + 4


def _deinterleave(t, stage, d):
    n = t.shape[0]
    p = _pitch(d)
    if p == d:
        stage[0:n, :] = t
    else:
        for j in range(n // d):
            stage[j * p:j * p + d, :] = t[j * d:(j + 1) * d, :]
    return [stage[pl.ds(r, n // d, stride=p), :] for r in range(d)]


def _interleave(parts, stage, d):
    m = parts[0].shape[0]
    p = _pitch(d)
    for r in range(d):
        stage[pl.ds(r, m, stride=p), :] = parts[r]
    if p == d:
        return stage[0:m * d, :]
    return jnp.concatenate([stage[j * p:j * p + d, :] for j in range(m)], axis=0)


def _proj_f32_kernel(x_ref, g_ref, w_ref, b_ref, o_ref, *, n_chunk):
    hb = _rms(x_ref[...], g_ref[...]).astype(BF16)
    for c in range(o_ref.shape[1] // n_chunk):
        sl = slice(c * n_chunk, (c + 1) * n_chunk)
        o_ref[:, sl] = jnp.dot(hb, w_ref[:, sl], preferred_element_type=F32) + b_ref[:, sl]


def _proj_f32(x2d, g, w, b):
    T, D = x2d.shape
    N = w.shape[1]
    return pl.pallas_call(
        functools.partial(_proj_f32_kernel, n_chunk=512),
        out_shape=jax.ShapeDtypeStruct((T, N), F32),
        grid=(T // PROJ_TILE,),
        in_specs=[
            pl.BlockSpec((PROJ_TILE, D), lambda i: (i, 0)),
            pl.BlockSpec((1, D), lambda i: (0, 0)),
            pl.BlockSpec((D, N), lambda i: (0, 0)),
            pl.BlockSpec((1, N), lambda i: (0, 0)),
        ],
        out_specs=pl.BlockSpec((PROJ_TILE, N), lambda i: (i, 0)),
        compiler_params=_cparams(("parallel",)),
        name="proj_rnn",
    )(x2d, g, w, b)


def _proj_qkv_kernel(x_ref, g_ref, w0_ref, w1_ref, w2_ref, b_ref, cos_ref, sin_ref,
                     ca_ref, cb_ref, o0_ref, o1_ref, o2_ref, cao_ref, cbo_ref, stage):
    cao_ref[...] = ca_ref[...].astype(BF16)
    cbo_ref[...] = cb_ref[...].astype(BF16)
    hb = _rms(x_ref[...], g_ref[...]).astype(BF16)
    cosv = cos_ref[...]
    sinv = sin_ref[...]
    lane = lax.broadcasted_iota(jnp.int32, (1, HEAD_DIM), 1)
    first_half = lane < (ROPE_DIM // 2)
    gw = 3 * ATTN_WIDTH
    for g, (w_ref, o_ref) in enumerate(((w0_ref, o0_ref), (w1_ref, o1_ref), (w2_ref, o2_ref))):
        d = ATTN_GROUPS[g][1]
        for c in range(gw // QKV_CHUNK):
            sl = slice(c * QKV_CHUNK, (c + 1) * QKV_CHUNK)
            t2 = jnp.dot(hb, w_ref[:, sl], preferred_element_type=F32)
            t2 = t2 + b_ref[:, g * gw + c * QKV_CHUNK:g * gw + (c + 1) * QKV_CHUNK]
            for hh in range(QKV_CHUNK // HEAD_DIM):
                t = t2[:, hh * HEAD_DIM:(hh + 1) * HEAD_DIM]
                col = c * QKV_CHUNK + hh * HEAD_DIM
                if col < 2 * ATTN_WIDTH:
                    up = pltpu.roll(t, HEAD_DIM - ROPE_DIM // 2, 1)
                    down = pltpu.roll(t, ROPE_DIM // 2, 1)
                    t = t * cosv + jnp.where(first_half, up, down) * sinv
                cs = slice(col, col + HEAD_DIM)
                if d == 1:
                    o_ref[:, cs] = t.astype(BF16)
                else:
                    for r, part in enumerate(_deinterleave(t, stage, d)):
                        o_ref[0, r, :, cs] = part.astype(BF16)


def _cast_slab_spec(a, nsteps):
    rows, cols = a.shape
    assert rows % (nsteps * BF16_ROWS) == 0, (a.shape, nsteps)
    return pl.BlockSpec((rows // nsteps, cols), lambda i: (i, 0))


def _proj_qkv(x2d, g, ws, b, cos_t, sin_t, batch, cast_a, cast_b):
    T, D = x2d.shape
    S = T // batch
    nseq = S // PROJ_TILE
    nsteps = T // PROJ_TILE
    gw = 3 * ATTN_WIDTH
    d1, d2 = ATTN_GROUPS[1][1], ATTN_GROUPS[2][1]
    full = lambda a: pl.BlockSpec(a.shape, lambda i: (0,) * a.ndim)
    dil_spec = lambda d: pl.BlockSpec((1, d, PROJ_TILE // d, gw),
                                      lambda i: (i // nseq, 0, i % nseq, 0))
    return pl.pallas_call(
        _proj_qkv_kernel,
        out_shape=(jax.ShapeDtypeStruct((T, gw), BF16),
                   jax.ShapeDtypeStruct((batch, d1, S // d1, gw), BF16),
                   jax.ShapeDtypeStruct((batch, d2, S // d2, gw), BF16),
                   jax.ShapeDtypeStruct(cast_a.shape, BF16),
                   jax.ShapeDtypeStruct(cast_b.shape, BF16)),
        grid=(nsteps,),
        in_specs=[
            pl.BlockSpec((PROJ_TILE, D), lambda i: (i, 0)),
            full(g), full(ws[0]), full(ws[1]), full(ws[2]), full(b),
            pl.BlockSpec((PROJ_TILE, HEAD_DIM), lambda i: (i % nseq, 0)),
            pl.BlockSpec((PROJ_TILE, HEAD_DIM), lambda i: (i % nseq, 0)),
            _cast_slab_spec(cast_a, nsteps), _cast_slab_spec(cast_b, nsteps),
        ],
        out_specs=(pl.BlockSpec((PROJ_TILE, gw), lambda i: (i, 0)), dil_spec(d1), dil_spec(d2),
                   _cast_slab_spec(cast_a, nsteps), _cast_slab_spec(cast_b, nsteps)),
        scratch_shapes=[pltpu.VMEM((PROJ_TILE // d2 * _pitch(d2), LANES), F32)],
        compiler_params=_cparams(("parallel",)),
        name="proj_qkv",
    )(x2d, g, ws[0], ws[1], ws[2], b, cos_t, sin_t, cast_a, cast_b)


def _rope_tables(seq):
    pos = jnp.arange(seq, dtype=F32)
    inv = ROPE_THETA ** (-jnp.arange(0, ROPE_DIM, 2, dtype=F32) / ROPE_DIM)
    ang = pos[:, None] * inv[None, :]
    cos, sin = jnp.cos(ang), jnp.sin(ang)
    pad = HEAD_DIM - ROPE_DIM
    cos_t = jnp.concatenate([cos, cos, jnp.ones((seq, pad), F32)], axis=1)
    sin_t = jnp.concatenate([-sin, sin, jnp.zeros((seq, pad), F32)], axis=1)
    return cos_t, sin_t


def _sigmoid(x):
    return 0.5 * jnp.tanh(0.5 * x) + 0.5


def _rglru_kernel(xr_ref, gr_ref, cw_ref, cb_ref, gw_ref, gb_ref, lam_ref, o_ref,
                  xpad, af, uf, ab, ub, cf, cbk, *, seq):
    nseg = seq // SCAN_SEG
    nchunk = seq // SCAN_CHUNK
    seg_per_chunk = SCAN_CHUNK // SCAN_SEG
    pitch = nseg + 4

    cw = cw_ref[...]
    cbias = cb_ref[...]
    gbias = gb_ref[...]
    z = -lam_ref[...]
    sp = jnp.maximum(z, 0.0) + jnp.log1p(jnp.exp(-jnp.abs(z)))
    half_neg_c_sp = (-0.5 * LRU_C) * sp

    zero_rows = jnp.zeros((SUBLANES, LANES), F32)
    xpad[0:SUBLANES, :] = zero_rows
    xpad[seq + SUBLANES:seq + 2 * SUBLANES, :] = zero_rows

    def copy_in(c, carry):
        t0 = pl.multiple_of(c * SCAN_CHUNK, SCAN_CHUNK)
        xpad[pl.ds(t0 + SUBLANES, SCAN_CHUNK), :] = xr_ref[0, pl.ds(t0, SCAN_CHUNK), :]
        return carry

    lax.fori_loop(0, nchunk, copy_in, 0)

    def stage1(c, carry):
        t0 = c * SCAN_CHUNK
        xc = cbias
        for tap in range(CONV_WIDTH):
            lo = t0 + SUBLANES - CONV_WIDTH // 2 + tap
            xc = xc + cw[tap:tap + 1, :] * xpad[pl.ds(lo, SCAN_CHUNK), :]
        th = jnp.tanh(jnp.dot(xc.astype(BF16), gw_ref[...], preferred_element_type=F32) + gbias)
        xc_half = 0.5 * xc
        for d, (a_s, u_s) in enumerate(((af, uf), (ab, ub))):
            t_r = th[:, (2 * d) * LANES:(2 * d + 1) * LANES]
            t_i = th[:, (2 * d + 1) * LANES:(2 * d + 2) * LANES]
            half_k = half_neg_c_sp[d:d + 1, :]
            log_a = t_r * half_k + half_k
            a = jnp.exp(log_a)
            v = jnp.tanh(log_a) * (-1.0 - a * a)
            u = (xc_half * (v * lax.rsqrt(jnp.maximum(v, F32_MIN_NORMAL)))) * (t_i + 1.0)
            for k in range(seg_per_chunk):
                seg = c * seg_per_chunk + k
                a_s[pl.ds(seg, SCAN_SEG, stride=pitch), :] = a[k * SCAN_SEG:(k + 1) * SCAN_SEG, :]
                u_s[pl.ds(seg, SCAN_SEG, stride=pitch), :] = u[k * SCAN_SEG:(k + 1) * SCAN_SEG, :]
        return carry

    lax.fori_loop(0, nchunk, stage1, 0, unroll=4)

    def scan_dir(a_s, u_s, reverse):
        def body(it, carry):
            i = (SCAN_SEG - 1 - it) if reverse else it
            h, cum = carry
            rows = pl.ds(i * pitch, nseg)
            a = a_s[rows, :]
            u = u_s[rows, :]
            h = a * h + u
            cum = a * cum
            u_s[rows, :] = h
            a_s[rows, :] = cum
            return h, cum

        return lax.fori_loop(0, SCAN_SEG, body,
                             (jnp.zeros((nseg, LANES), F32), jnp.ones((nseg, LANES), F32)))

    end_h, end_a = scan_dir(af, uf, False)
    beg_h, beg_a = scan_dir(ab, ub, True)

    c = jnp.zeros((1, LANES), F32)
    for s in range(nseg):
        cf[s:s + 1, :] = c
        c = end_h[s:s + 1, :] + end_a[s:s + 1, :] * c
    c = jnp.zeros((1, LANES), F32)
    for s in range(nseg - 1, -1, -1):
        cbk[s:s + 1, :] = c
        c = beg_h[s:s + 1, :] + beg_a[s:s + 1, :] * c

    def finish(s, carry):
        t0 = pl.multiple_of(s * SCAN_SEG, SCAN_SEG)
        rows = pl.ds(s, SCAN_SEG, stride=pitch)
        hf = uf[rows, :] + af[rows, :] * cf[pl.ds(s, 1), :]
        hb = ub[rows, :] + ab[rows, :] * cbk[pl.ds(s, 1), :]
        g = jax.nn.gelu(gr_ref[0, pl.ds(t0, SCAN_SEG), :])
        o_ref[0, pl.ds(t0, SCAN_SEG), :] = (g * (hf + hb)).astype(BF16)
        return carry

    lax.fori_loop(0, nseg, finish, 0)


def _rglru(xrgr, conv_w, conv_b, gate_w, gate_b, lam):
    B, S, _ = xrgr.shape
    nblk = D_RNN // RNN_BW
    nseg = S // SCAN_SEG
    scr = pltpu.VMEM((SCAN_SEG * (nseg + 4), LANES), F32)
    return pl.pallas_call(
        functools.partial(_rglru_kernel, seq=S),
        out_shape=jax.ShapeDtypeStruct((B, S, D_RNN), BF16),
        grid=(B, nblk),
        in_specs=[
            pl.BlockSpec((1, S, RNN_BW), lambda b, c: (b, 0, c)),
            pl.BlockSpec((1, S, RNN_BW), lambda b, c: (b, 0, nblk + c)),
            pl.BlockSpec((CONV_WIDTH, RNN_BW), lambda b, c: (0, c)),
            pl.BlockSpec((1, RNN_BW), lambda b, c: (0, c)),
            pl.BlockSpec((None, RNN_BW, 4 * RNN_BW), lambda b, c: (c, 0, 0)),
            pl.BlockSpec((None, 1, 4 * RNN_BW), lambda b, c: (c, 0, 0)),
            pl.BlockSpec((None, 2, RNN_BW), lambda b, c: (c, 0, 0)),
        ],
        out_specs=pl.BlockSpec((1, S, RNN_BW), lambda b, c: (b, 0, c)),
        scratch_shapes=[pltpu.VMEM((S + 2 * SUBLANES, LANES), F32), scr, scr, scr, scr,
                        pltpu.VMEM((nseg, LANES), F32), pltpu.VMEM((nseg, LANES), F32)],
        compiler_params=_cparams(("parallel", "parallel")),
        name="rglru",
    )(xrgr, xrgr, conv_w, conv_b, gate_w, gate_b, lam)


def _attn_kernel(q_ref, k_ref, kp_ref, kn_ref, v_ref, vp_ref, vn_ref, o_ref, st_ref,
                 kext, vext, *, qblk, length):
    j = pl.program_id(2)
    nkey = ATT_SUB + 2 * RADIUS
    qi = lax.broadcasted_iota(jnp.int32, (ATT_SUB, nkey), 0)
    ki = lax.broadcasted_iota(jnp.int32, (ATT_SUB, nkey), 1)
    band_bias = jnp.where(jnp.abs(ki - RADIUS - qi) <= RADIUS, 0.0, NEG_INF)
    krow = lax.broadcasted_iota(jnp.int32, (1, nkey), 1)
    lane = lax.broadcasted_iota(jnp.int32, (ATT_SUB, LANES), 1)
    scale2 = HEAD_DIM ** -0.5 * LOG2_E

    for rr in range(q_ref.shape[1]):
        kx, vx = kext.at[rr], vext.at[rr]
        kx[0:RADIUS, :] = kp_ref[0, rr]
        kx[RADIUS:RADIUS + qblk, :] = k_ref[0, rr]
        kx[RADIUS + qblk:RADIUS + qblk + RADIUS, :] = kn_ref[0, rr]
        for h in range(HEADS):
            src = slice(h * HEAD_DIM, (h + 1) * HEAD_DIM)
            dst = slice(2 * h * HEAD_DIM, (2 * h + 1) * HEAD_DIM)
            vx[0:RADIUS, dst] = vp_ref[0, rr, :, src]
            vx[RADIUS:RADIUS + qblk, dst] = v_ref[0, rr, :, src]
            vx[RADIUS + qblk:RADIUS + qblk + RADIUS, dst] = vn_ref[0, rr, :, src]
            vx[:, (2 * h + 1) * HEAD_DIM:(2 * h + 2) * HEAD_DIM] = jnp.ones(
                (qblk + 2 * RADIUS, HEAD_DIM), BF16)

        for i in range(qblk // ATT_SUB):
            q0 = i * ATT_SUB
            kglob = j * qblk + q0 - RADIUS + krow
            edge_bias = jnp.where((kglob >= 0) & (kglob < length), 0.0, NEG_INF)
            bias = band_bias + edge_bias
            st_tile = jnp.ones((ATT_SUB, LANES), F32)
            for h in range(HEADS):
                hs = slice(h * HEAD_DIM, (h + 1) * HEAD_DIM)
                q = q_ref[0, rr, q0:q0 + ATT_SUB, hs]
                ks = kx[q0:q0 + nkey, hs]
                vs = vx[q0:q0 + nkey, 2 * h * HEAD_DIM:(2 * h + 2) * HEAD_DIM]
                s = lax.dot_general(q, ks, (((1,), (1,)), ((), ())),
                                    preferred_element_type=F32) * scale2 + bias
                m = jnp.max(s, axis=-1, keepdims=True)
                p = jnp.exp2(s - m)
                pv = jnp.dot(p.astype(BF16), vs, preferred_element_type=F32)
                l = pv[:, HEAD_DIM:2 * HEAD_DIM]
                o_ref[0, rr, q0:q0 + ATT_SUB, hs] = pv[:, 0:HEAD_DIM]
                st_tile = jnp.where(lane == h, m * LN_2,
                                    jnp.where(lane == HEADS + h, l, st_tile))
            st_ref[0, rr, q0:q0 + ATT_SUB, :] = st_tile


def _attention_group(qkv_g):
    B, d, L, _ = qkv_g.shape
    qblk = min(L, ATT_QBLK)
    nblk = L // qblk
    rpb = min(d, max(1, ATT_QBLK // L))
    halo_per_blk = qblk // RADIUS
    last_halo = L // RADIUS - 1

    main = lambda part: pl.BlockSpec((1, rpb, qblk, ATTN_WIDTH), lambda b, r, j: (b, r, j, part))
    prev = lambda part: pl.BlockSpec(
        (1, rpb, RADIUS, ATTN_WIDTH),
        lambda b, r, j: (b, r, jnp.maximum(j * halo_per_blk - 1, 0), part))
    nxt = lambda part: pl.BlockSpec(
        (1, rpb, RADIUS, ATTN_WIDTH),
        lambda b, r, j: (b, r, jnp.minimum((j + 1) * halo_per_blk, last_halo), part))
    return pl.pallas_call(
        functools.partial(_attn_kernel, qblk=qblk, length=L),
        out_shape=(jax.ShapeDtypeStruct((B, d, L, ATTN_WIDTH), F32),
                   jax.ShapeDtypeStruct((B, d, L, LANES), F32)),
        grid=(B, d // rpb, nblk),
        in_specs=[main(0), main(1), prev(1), nxt(1), main(2), prev(2), nxt(2)],
        out_specs=(pl.BlockSpec((1, rpb, qblk, ATTN_WIDTH), lambda b, r, j: (b, r, j, 0)),
                   pl.BlockSpec((1, rpb, qblk, LANES), lambda b, r, j: (b, r, j, 0))),
        scratch_shapes=[pltpu.VMEM((rpb, qblk + 2 * RADIUS, ATTN_WIDTH), BF16),
                        pltpu.VMEM((rpb, qblk + 2 * RADIUS, 2 * ATTN_WIDTH), BF16)],
        compiler_params=_cparams(("parallel", "parallel", "parallel")),
        name=f"attn_d{d}",
    )(qkv_g, qkv_g, qkv_g, qkv_g, qkv_g, qkv_g, qkv_g)


def _merge_kernel(x_ref, yr_ref, o0_ref, o1_ref, o2_ref, l0_ref, l1_ref, l2_ref,
                  gmix_ref, wg_ref, bg_ref, prnn_ref, pattn_ref, wout_ref,
                  gffn_ref, wr_ref, wrt_ref, br_ref, brt_ref, cc_ref,
                  x2_ref, h2_ref, afft_ref, cco_ref, yattn, stage):
    cco_ref[...] = cc_ref[...].astype(BF16)
    d = x_ref.shape[1]
    x = x_ref[...]
    hb = _rms(x, gmix_ref[...]).astype(BF16)
    branch_a = jnp.dot(yr_ref[...], prnn_ref[...], preferred_element_type=F32)
    ga = jnp.dot(hb, wg_ref[:, 0:d], preferred_element_type=F32) + bg_ref[:, 0:d]
    gb = jnp.dot(hb, wg_ref[:, d:2 * d], preferred_element_type=F32) + bg_ref[:, d:2 * d]

    def natural(ref, g, cs):
        dil = ATTN_GROUPS[g][1]
        if dil == 1:
            return ref[:, cs]
        return _interleave([ref[0, r, :, cs] for r in range(dil)], stage, dil)

    o_refs = (o0_ref, o1_ref, o2_ref)
    all_lanes = slice(0, LANES)
    stats = [natural(r, g, all_lanes) for g, r in enumerate((l0_ref, l1_ref, l2_ref))]
    head_lane = lax.broadcasted_iota(jnp.int32, stats[0].shape, 1) < HEADS
    sums = [jnp.where(head_lane, pltpu.roll(st, LANES - HEADS, 1), 1.0) for st in stats]
    lses = [st + jnp.log(l) for st, l in zip(stats, sums)]
    m = jnp.maximum(jnp.maximum(lses[0], lses[1]), lses[2])
    ws = [jnp.exp(l - m) for l in lses]
    den = ws[0] + ws[1] + ws[2]
    coef = [w / (den * l) for w, l in zip(ws, sums)]
    for h in range(HEADS):
        hs = slice(h * HEAD_DIM, (h + 1) * HEAD_DIM)
        acc = coef[0][:, h:h + 1] * natural(o_refs[0], 0, hs)
        acc = acc + coef[1][:, h:h + 1] * natural(o_refs[1], 1, hs)
        acc = acc + coef[2][:, h:h + 1] * natural(o_refs[2], 2, hs)
        yattn[:, hs] = acc.astype(BF16)

    branch_b = jnp.dot(yattn[...], pattn_ref[...], preferred_element_type=F32)
    merged = _sigmoid(ga) * branch_a + _sigmoid(gb) * branch_b
    x2 = x + jnp.dot(merged.astype(BF16), wout_ref[...], preferred_element_type=F32)
    x2_ref[...] = x2

    h2 = _rms(x2, gffn_ref[...]).astype(BF16)
    h2_ref[:, 0:d] = h2
    n_exp = afft_ref.shape[1]
    logits = jnp.dot(h2, wr_ref[...], preferred_element_type=F32) + br_ref[...]
    lane = lax.broadcasted_iota(jnp.int32, logits.shape, 1)
    mx = jnp.max(logits, axis=-1, keepdims=True)
    e = jnp.exp(logits - mx)
    den = jnp.sum(jnp.where(lane < n_exp, e, 0.0), axis=-1, keepdims=True)
    aff = e / den
    head = aff.astype(BF16).astype(F32)
    tile = jnp.where(lane < n_exp, head, jnp.where(lane < 2 * n_exp, aff - head, 0.0))
    h2_ref[:, d:d + LANES] = tile.astype(BF16)
    logits_t = lax.dot_general(wrt_ref[...], h2, (((1,), (1,)), ((), ())),
                               preferred_element_type=F32) + brt_ref[...]
    mxt = jnp.max(logits_t, axis=0, keepdims=True)
    et = jnp.exp(logits_t - mxt)
    afft_ref[0] = et / jnp.sum(et, axis=0, keepdims=True)


def _merge(x2d, y_rnn, outs, lses, gmix, wg, bg, p_rnn, p_attn, w_out, gffn, wr, br, batch,
           cast_c):
    T, D = x2d.shape
    S = T // batch
    nseq = S // ROW_TILE
    nsteps = T // ROW_TILE
    E = wr.shape[1]
    row = lambda w: pl.BlockSpec((ROW_TILE, w), lambda i: (i, 0))
    full = lambda a: pl.BlockSpec(a.shape, lambda i: (0,) * a.ndim)

    def attn_spec(g, w):
        dil = ATTN_GROUPS[g][1]
        if dil == 1:
            return row(w)
        return pl.BlockSpec((1, dil, ROW_TILE // dil, w), lambda i: (i // nseq, 0, i % nseq, 0))

    wrt = wr.T
    brt = br.reshape(E, 1)
    wr_pad = jnp.concatenate([wr, wr, jnp.zeros((D, LANES - 2 * E), wr.dtype)], axis=1)
    br_pad = jnp.concatenate([br, br, jnp.full((1, LANES - 2 * E), NEG_INF, br.dtype)], axis=1)
    consts = [gmix, wg, bg, p_rnn, p_attn, w_out, gffn, wr_pad, wrt, br_pad, brt]
    args = [x2d, y_rnn, *outs, *lses, *consts, cast_c]
    in_specs = ([row(D), row(D_RNN)] + [attn_spec(g, ATTN_WIDTH) for g in range(N_GROUPS)]
                + [attn_spec(g, LANES) for g in range(N_GROUPS)] + [full(a) for a in consts]
                + [_cast_slab_spec(cast_c, nsteps)])
    dmax = max(dil for _, dil in ATTN_GROUPS)
    return pl.pallas_call(
        _merge_kernel,
        out_shape=(jax.ShapeDtypeStruct((T, D), F32),
                   jax.ShapeDtypeStruct((T, D + LANES), BF16),
                   jax.ShapeDtypeStruct((batch, E, S), F32),
                   jax.ShapeDtypeStruct(cast_c.shape, BF16)),
        grid=(nsteps,),
        in_specs=in_specs,
        out_specs=(row(D), row(D + LANES),
                   pl.BlockSpec((1, E, ROW_TILE), lambda i: (i // nseq, 0, i % nseq)),
                   _cast_slab_spec(cast_c, nsteps)),
        scratch_shapes=[pltpu.VMEM((ROW_TILE, ATTN_WIDTH), BF16),
                        pltpu.VMEM((ROW_TILE // dmax * _pitch(dmax), LANES), F32)],
        compiler_params=_cparams(("parallel",)),
        name="merge",
    )(*args)


def _route_kernel(aff_ref, pos_ref, off_ref, *, cap):
    a = aff_ref[0]
    E, R, _ = a.shape
    bits = pltpu.bitcast(a, jnp.int32)

    def count(mask):
        c = jnp.sum(mask.astype(F32), axis=1, keepdims=True)
        return jnp.sum(c, axis=2, keepdims=True)

    def search(_, carry):
        lo, hi = carry
        q = (hi - lo + 3) >> 2
        m1 = jnp.minimum(lo + q, hi)
        m2 = jnp.minimum(m1 + q, hi)
        m3 = jnp.minimum(m2 + q, hi)
        ok1 = count(bits >= m1) >= cap
        ok2 = count(bits >= m2) >= cap
        ok3 = count(bits >= m3) >= cap
        new_lo = jnp.where(ok3, m3, jnp.where(ok2, m2, jnp.where(ok1, m1, lo)))
        new_hi = jnp.where(ok3, hi, jnp.where(ok2, m3 - 1, jnp.where(ok1, m2 - 1, m1 - 1)))
        return new_lo, new_hi

    lo0 = jnp.zeros((E, 1, 1), jnp.int32)
    hi0 = jnp.full((E, 1, 1), 0x7F7FFFFF, jnp.int32)
    thr, _ = lax.fori_loop(0, SEARCH_STEPS, search, (lo0, hi0))

    gt = bits > thr
    eq = bits == thr
    need = cap - count(gt)

    ki = lax.broadcasted_iota(jnp.int32, (LANES, LANES), 0)
    li = lax.broadcasted_iota(jnp.int32, (LANES, LANES), 1)
    upper = (ki <= li).astype(BF16)
    allones = jnp.ones((LANES, LANES), BF16)
    ri = lax.broadcasted_iota(jnp.int32, (R, R), 0)
    ci = lax.broadcasted_iota(jnp.int32, (R, R), 1)
    strict = (ci < ri).astype(BF16)

    def prefix(mask):
        mb = mask.astype(BF16).reshape(E * R, LANES)
        within = jnp.dot(mb, upper, preferred_element_type=F32)
        totals = jnp.dot(mb, allones, preferred_element_type=F32).astype(BF16)
        offs = []
        for e in range(E):
            offs.append(jnp.dot(strict, totals[e * R:(e + 1) * R, :],
                                preferred_element_type=F32))
        rowoff = jnp.concatenate(offs, axis=0)
        excl = within - mb.astype(F32) + rowoff
        return excl.reshape(E, R, LANES), rowoff.reshape(E, R, LANES)

    eq_rank, _ = prefix(eq)
    sel = gt | (eq & (eq_rank < need.astype(F32)))
    slot, rowoff = prefix(sel)
    pos_ref[0] = jnp.where(sel, slot.astype(jnp.int32), -1)
    off_ref[0] = rowoff.astype(jnp.int32)


def _route(aff_t, cap):
    B, E, S = aff_t.shape
    R = S // LANES
    a4 = aff_t.reshape(B, E, R, LANES)
    blk = pl.BlockSpec((1, E, R, LANES), lambda b: (b, 0, 0, 0))
    pos, off = pl.pallas_call(
        functools.partial(_route_kernel, cap=cap),
        out_shape=(jax.ShapeDtypeStruct((B, E, R, LANES), jnp.int32),
                   jax.ShapeDtypeStruct((B, E, R, LANES), jnp.int32)),
        grid=(B,),
        in_specs=[blk],
        out_specs=(blk, blk),
        compiler_params=_cparams(("parallel",)),
        name="route",
    )(a4)
    return pos, off


def _slot_windows(off_ref, base, n_exp, ntb, tb):
    starts = []
    rounds = jnp.int32(0)
    for e in range(n_exp):
        idx = (base + e) * (ntb + 1) + tb
        start = (off_ref[idx] >> 4) << 4
        starts.append(start)
        rounds = jnp.maximum(rounds, (off_ref[idx + 1] - start + SLOT_WIN - 1) >> SLOT_SHIFT)
    return starts, rounds


def _gather_kernel(off_ref, pos_ref, h_ref, o_ref, *, cap, ntb, n_exp, nsub):
    b = pl.program_id(0)
    grp = pl.program_id(1)
    t = pl.program_id(2)

    @pl.when(t == 0)
    def _():
        o_ref[...] = jnp.zeros_like(o_ref)

    base = (b * pl.num_programs(1) + grp) * n_exp
    ji = lax.broadcasted_iota(jnp.int32, (SLOT_WIN, TOK_BLK), 0)
    for sb in range(nsub):
        toks = slice(sb * TOK_BLK, (sb + 1) * TOK_BLK)
        starts, rounds = _slot_windows(off_ref, base, n_exp, ntb, t * nsub + sb)

        def one_round(k, carry, toks=toks, starts=starts):
            pieces, rows = [], []
            for e in range(n_exp):
                first = starts[e] + k * SLOT_WIN
                row0 = pl.multiple_of(jnp.minimum(first, cap - SLOT_WIN), BF16_ROWS)
                slot = ji + row0
                hit = (slot == pos_ref[0, 0, e:e + 1, toks]) & (slot >= first)
                pieces.append(jnp.where(hit, 1.0, 0.0).astype(BF16))
                rows.append(row0)
            onehot = jnp.concatenate(pieces, axis=0)
            g = jnp.dot(onehot, h_ref[0, toks, :], preferred_element_type=F32)
            for e in range(n_exp):
                o_ref[0, e, pl.ds(rows[e], SLOT_WIN), :] += (
                    g[e * SLOT_WIN:(e + 1) * SLOT_WIN, :].astype(BF16))
            return carry

        one_round(0, 0)
        lax.fori_loop(1, rounds, one_round, 0)


def _gather(offs, pos, h2, cap):
    B, E, S = pos.shape
    D = h2.shape[-1]
    ntb = S // TOK_BLK
    n_exp = E // EXPERT_SPLIT
    nsub = GATHER_TOK // TOK_BLK
    pos4 = pos.reshape(B, EXPERT_SPLIT, n_exp, S)
    return pl.pallas_call(
        functools.partial(_gather_kernel, cap=cap, ntb=ntb, n_exp=n_exp, nsub=nsub),
        out_shape=jax.ShapeDtypeStruct((B, E, cap, D), BF16),
        grid_spec=pltpu.PrefetchScalarGridSpec(
            num_scalar_prefetch=1,
            grid=(B, EXPERT_SPLIT, S // GATHER_TOK),
            in_specs=[
                pl.BlockSpec((1, 1, n_exp, GATHER_TOK), lambda b, g, t, off: (b, g, 0, t)),
                pl.BlockSpec((1, GATHER_TOK, D), lambda b, g, t, off: (b, t, 0)),
            ],
            out_specs=pl.BlockSpec((1, n_exp, cap, D), lambda b, g, t, off: (b, g, 0, 0)),
        ),
        compiler_params=_cparams(("parallel", "parallel", "arbitrary")),
        name="gather",
    )(offs, pos4, h2)


def _ffn_kernel(x_ref, wg_ref, wu_ref, wd_ref, o_ref, acc, *, f_chunk):
    e = pl.program_id(0)
    d = wg_ref.shape[1]
    x = x_ref[0, 0, :, 0:d]
    gl = x_ref[0, 0, :, d:d + LANES].astype(F32)
    lane = lax.broadcasted_iota(jnp.int32, gl.shape, 1)
    gate = jnp.sum(jnp.where((lane == e) | (lane == e + pl.num_programs(0)), gl, 0.0),
                   axis=1, keepdims=True)
    nf = wg_ref.shape[2] // f_chunk
    for f in range(nf):
        fs = slice(f * f_chunk, (f + 1) * f_chunk)
        g = jnp.dot(x, wg_ref[0, :, fs], preferred_element_type=F32)
        u = jnp.dot(x, wu_ref[0, :, fs], preferred_element_type=F32)
        hid = (g * _sigmoid(g) * u).astype(BF16)
        part = jnp.dot(hid, wd_ref[0, fs, :], preferred_element_type=F32)
        if f == 0:
            acc[...] = part
        else:
            acc[...] += part
    o_ref[0, 0] = (acc[...] * gate).astype(BF16)


def _ffn(xg, w_gate, w_up, w_down):
    B, E, C, DX = xg.shape
    D = w_gate.shape[1]
    F = w_gate.shape[2]
    assert F % FFN_CHUNK == 0, (F, FFN_CHUNK)
    return pl.pallas_call(
        functools.partial(_ffn_kernel, f_chunk=FFN_CHUNK),
        out_shape=jax.ShapeDtypeStruct((B, E, C, D), BF16),
        grid=(E, B),
        in_specs=[
            pl.BlockSpec((1, 1, C, DX), lambda e, b: (b, e, 0, 0)),
            pl.BlockSpec((1, D, F), lambda e, b: (e, 0, 0)),
            pl.BlockSpec((1, D, F), lambda e, b: (e, 0, 0)),
            pl.BlockSpec((1, F, D), lambda e, b: (e, 0, 0)),
        ],
        out_specs=pl.BlockSpec((1, 1, C, D), lambda e, b: (b, e, 0, 0)),
        scratch_shapes=[pltpu.VMEM((C, D), F32)],
        compiler_params=_cparams(("arbitrary", "arbitrary")),
        name="ffn",
    )(xg, w_gate, w_up, w_down)


def _combine_kernel(off_ref, x_ref, pos_ref, eo_ref, g_ref, o_ref, wstack, acc, *, cap, ntb):
    b = pl.program_id(0)
    t = pl.program_id(1)
    E = pos_ref.shape[2]
    nsub = pos_ref.shape[1] // TOK_BLK
    lane = lax.broadcasted_iota(jnp.int32, (TOK_BLK, LANES), 1)
    low = lane < SLOT_WIN

    for sb in range(nsub):
        toks = slice(sb * TOK_BLK, (sb + 1) * TOK_BLK)
        pos = pos_ref[0, toks, :]
        starts, rounds = _slot_windows(off_ref, b * E, E, ntb, t * nsub + sb)

        def one_round(k, carry, toks=toks, pos=pos, starts=starts, wst=wstack.at[sb],
                      first_round=False):
            onehot = []
            for p in range(E // 2):
                firsts, row0s = [], []
                for e in (2 * p, 2 * p + 1):
                    first = starts[e] + k * SLOT_WIN
                    row0 = pl.multiple_of(jnp.minimum(first, cap - SLOT_WIN), BF16_ROWS)
                    wst[e * SLOT_WIN:(e + 1) * SLOT_WIN, :] = (
                        eo_ref[0, e, pl.ds(row0, SLOT_WIN), :])
                    firsts.append(first)
                    row0s.append(row0)
                slot = lane + jnp.where(low, row0s[0], row0s[1] - SLOT_WIN)
                first = jnp.where(low, firsts[0], firsts[1])
                pcol = jnp.where(low, pos[:, 2 * p:2 * p + 1], pos[:, 2 * p + 1:2 * p + 2])
                hit = (slot == pcol) & (slot >= first)
                onehot.append(jnp.where(hit, 1.0, 0.0).astype(BF16))
            y = jnp.dot(jnp.concatenate(onehot, axis=1), wst[...], preferred_element_type=F32)
            acc[toks, :] = y + (x_ref[0, toks, :] if first_round else acc[toks, :])
            return carry

        one_round(0, 0, first_round=True)
        lax.fori_loop(1, rounds, one_round, 0)
    o_ref[0] = _rms(acc[...], g_ref[...])


def _combine(offs, x2, pos_t, eo, g_final, cap):
    B, S, D = x2.shape
    E = pos_t.shape[2]
    ntb = S // TOK_BLK
    return pl.pallas_call(
        functools.partial(_combine_kernel, cap=cap, ntb=ntb),
        out_shape=jax.ShapeDtypeStruct((B, S, D), F32),
        grid_spec=pltpu.PrefetchScalarGridSpec(
            num_scalar_prefetch=1,
            grid=(B, S // COMBINE_TOK),
            in_specs=[
                pl.BlockSpec((1, COMBINE_TOK, D), lambda b, t, off: (b, t, 0)),
                pl.BlockSpec((1, COMBINE_TOK, E), lambda b, t, off: (b, t, 0)),
                pl.BlockSpec((1, E, cap, D), lambda b, t, off: (b, 0, 0, 0),
                             pipeline_mode=pl.Buffered(1)),
                pl.BlockSpec((1, D), lambda b, t, off: (0, 0)),
            ],
            out_specs=pl.BlockSpec((1, COMBINE_TOK, D), lambda b, t, off: (b, t, 0)),
            scratch_shapes=[pltpu.VMEM((COMBINE_TOK // TOK_BLK, E * SLOT_WIN, D), BF16),
                            pltpu.VMEM((COMBINE_TOK, D), F32)],
        ),
        compiler_params=_cparams(("parallel", "arbitrary")),
        name="combine",
    )(offs, x2, pos_t, eo, g_final)


def _layer(x, norm_mix, w_in, b_in, conv_w, conv_b, rg_w, rg_b, rg_lambda, p_rnn, p_attn,
           w_out, norm_ffn, w_router, b_router, w_gate, w_up, w_down, norm_out, cos_t, sin_t):
    B, S, D = x.shape
    T = B * S
    x2d = x.reshape(T, D)
    n_rnn = 2 * D_RNN
    n_qkv = 3 * N_GROUPS * ATTN_WIDTH
    gmix = norm_mix.reshape(1, D)
    w_in_b = w_in.astype(BF16)
    b_in2 = b_in.reshape(1, -1)

    xrgr = _proj_f32(x2d, gmix, w_in_b[:, :n_rnn], b_in2[:, :n_rnn])

    def group_cols(a, g):
        part = lambda p: a[:, n_rnn + (p * N_GROUPS + g) * ATTN_WIDTH:
                           n_rnn + (p * N_GROUPS + g + 1) * ATTN_WIDTH]
        return jnp.concatenate([part(0), part(1), part(2)], axis=1)

    ws = [group_cols(w_in_b, g) for g in range(N_GROUPS)]
    b_qkv = jnp.concatenate([group_cols(b_in2, g) for g in range(N_GROUPS)], axis=1)
    n_e, d_in, d_ff = w_gate.shape
    *qkv, w_gate_b, w_up_b = _proj_qkv(
        x2d, gmix, ws, b_qkv, cos_t, sin_t, B,
        w_gate.reshape(n_e * d_in, d_ff), w_up.reshape(n_e * d_in, d_ff))

    nblk = D_RNN // RNN_BW
    gate_w = (0.5 * jnp.transpose(rg_w, (2, 3, 0, 1, 4))).reshape(nblk, RNN_BW, 4 * RNN_BW).astype(BF16)
    gate_b = 0.5 * jnp.transpose(rg_b.reshape(2, 2, nblk, RNN_BW), (2, 0, 1, 3)).reshape(nblk, 1, 4 * RNN_BW)
    lam = jnp.transpose(rg_lambda.reshape(2, nblk, RNN_BW), (1, 0, 2))
    y_rnn = _rglru(xrgr.reshape(B, S, n_rnn), conv_w, conv_b.reshape(1, D_RNN), gate_w, gate_b, lam)

    outs, lses = [], []
    for g, (_, dilation) in enumerate(ATTN_GROUPS):
        qkv_g = qkv[g].reshape(B, 1, S, 3 * ATTN_WIDTH) if dilation == 1 else qkv[g]
        o, lse = _attention_group(qkv_g)
        if dilation == 1:
            o, lse = o.reshape(T, ATTN_WIDTH), lse.reshape(T, LANES)
        outs.append(o)
        lses.append(lse)

    x2, h2, aff_t, w_down_b = _merge(
        x2d, y_rnn.reshape(T, D_RNN), outs, lses, gmix,
        w_in_b[:, n_rnn + n_qkv:], b_in2[:, n_rnn + n_qkv:],
        p_rnn.astype(BF16), p_attn.astype(BF16), w_out.astype(BF16),
        norm_ffn.reshape(1, D), w_router.astype(BF16), b_router.reshape(1, -1), B,
        w_down.reshape(n_e * d_ff, d_in))

    E = w_router.shape[1]
    cap = CAPACITY_FACTOR * S // E
    pos4, off4 = _route(aff_t, cap)
    pos = pos4.reshape(B, E, S)
    blk_off = off4[:, :, ::TOK_BLK // LANES, 0]
    offs = jnp.concatenate([blk_off, jnp.full((B, E, 1), cap, jnp.int32)], axis=2).reshape(-1)
    xg = _gather(offs, pos, h2.reshape(B, S, D + LANES), cap)
    eo = _ffn(xg, w_gate_b.reshape(n_e, d_in, d_ff), w_up_b.reshape(n_e, d_in, d_ff),
              w_down_b.reshape(n_e, d_ff, d_in))
    pos_t = jnp.transpose(pos, (0, 2, 1))
    return _combine(offs, x2.reshape(B, S, D), pos_t, eo, norm_out.reshape(1, D), cap)


def kernel(x, norm_mix, w_in, b_in, conv_w, conv_b, rg_w, rg_b, rg_lambda, p_rnn, p_attn, w_out,
           norm_ffn, w_router, b_router, w_gate, w_up, w_down, norm_final):
    depth = w_in.shape[0]
    assert depth == 1, "final rmsnorm is fused into the single layer's combine step"
    cos_t, sin_t = _rope_tables(x.shape[1])
    return _layer(x, norm_mix[0], w_in[0], b_in[0], conv_w[0], conv_b[0], rg_w[0], rg_b[0],
                  rg_lambda[0], p_rnn[0], p_attn[0], w_out[0], norm_ffn[0], w_router[0],
                  b_router[0], w_gate[0], w_up[0], w_down[0], norm_final, cos_t, sin_t)
```

```python
import functools

import jax
import jax.numpy as jnp
from jax import lax
from jax.experimental import pallas as pl
from jax.experimental.pallas import tpu as pltpu

F32 = jnp.float32
BF16 = jnp.bfloat16

D_RNN = 1024
RNN_BW = 128
CONV_WIDTH = 4
LRU_C = 8.0
HEAD_DIM = 128
HEADS = 4
ATTN_WIDTH = HEADS * HEAD_DIM
ATTN_GROUPS = ((128, 1), (512, 4), (2048, 16))
N_GROUPS = 3
ROPE_THETA = 500000.0
ROPE_DIM = HEAD_DIM // 4
RADIUS = 64
NEG_INF = -1e30
N_EXPERTS = 16
CAPACITY_FACTOR = 2
RMS_EPS = 1e-6

LANES = 128
SUBLANES = 8
BF16_ROWS = 16
VMEM_LIMIT = 56 * 1024 * 1024
F32_MIN_NORMAL = 1.1754944e-38
LOG2_E = 1.4426950408889634
LN_2 = 0.6931471805599453

PROJ_TILE = 1024
QKV_CHUNK = 512
ROW_TILE = 512
SCAN_SEG = 128
SCAN_CHUNK = 512
ATT_QBLK = 2048
ATT_SUB = 128
TOK_BLK = 256
GATHER_TOK = 2048
COMBINE_TOK = 512
SLOT_SHIFT = 6
SLOT_WIN = 1 << SLOT_SHIFT
EXPERT_SPLIT = 2
FFN_CHUNK = 512
SEARCH_STEPS = 17


def _cparams(sem):
    return pltpu.CompilerParams(dimension_semantics=sem, vmem_limit_bytes=VMEM_LIMIT)


def _rms(x, g):
    ms = jnp.mean(x * x, axis=-1, keepdims=True)
    return x * lax.rsqrt(ms + RMS_EPS) * g


def _pitch(d):
    return d if d <= 4 else d + 4


def _deinterleave(t, stage, d):
    n = t.shape[0]
    p = _pitch(d)
    if p == d:
        stage[0:n, :] = t
    else:
        for j in range(n // d):
            stage[j * p:j * p + d, :] = t[j * d:(j + 1) * d, :]
    return [stage[pl.ds(r, n // d, stride=p), :] for r in range(d)]


def _interleave(parts, stage, d):
    m = parts[0].shape[0]
    p = _pitch(d)
    for r in range(d):
        stage[pl.ds(r, m, stride=p), :] = parts[r]
    if p == d:
        return stage[0:m * d, :]
    return jnp.concatenate([stage[j * p:j * p + d, :] for j in range(m)], axis=0)


def _proj_f32_kernel(x_ref, g_ref, w_ref, b_ref, o_ref, *, n_chunk):
    hb = _rms(x_ref[...], g_ref[...]).astype(BF16)
    for c in range(o_ref.shape[1] // n_chunk):
        sl = slice(c * n_chunk, (c + 1) * n_chunk)
        o_ref[:, sl] = jnp.dot(hb, w_ref[:, sl], preferred_element_type=F32) + b_ref[:, sl]


def _proj_f32(x2d, g, w, b):
    T, D = x2d.shape
    N = w.shape[1]
    return pl.pallas_call(
        functools.partial(_proj_f32_kernel, n_chunk=512),
        out_shape=jax.ShapeDtypeStruct((T, N), F32),
        grid=(T // PROJ_TILE,),
        in_specs=[
            pl.BlockSpec((PROJ_TILE, D), lambda i: (i, 0)),
            pl.BlockSpec((1, D), lambda i: (0, 0)),
            pl.BlockSpec((D, N), lambda i: (0, 0)),
            pl.BlockSpec((1, N), lambda i: (0, 0)),
        ],
        out_specs=pl.BlockSpec((PROJ_TILE, N), lambda i: (i, 0)),
        compiler_params=_cparams(("parallel",)),
        name="proj_rnn",
    )(x2d, g, w, b)


def _proj_qkv_kernel(x_ref, g_ref, w0_ref, w1_ref, w2_ref, b_ref, cos_ref, sin_ref,
                     ca_ref, cb_ref, o0_ref, o1_ref, o2_ref, cao_ref, cbo_ref, stage):
    cao_ref[...] = ca_ref[...].astype(BF16)
    cbo_ref[...] = cb_ref[...].astype(BF16)
    hb = _rms(x_ref[...], g_ref[...]).astype(BF16)
    cosv = cos_ref[...]
    sinv = sin_ref[...]
    lane = lax.broadcasted_iota(jnp.int32, (1, HEAD_DIM), 1)
    first_half = lane < (ROPE_DIM // 2)
    gw = 3 * ATTN_WIDTH
    for g, (w_ref, o_ref) in enumerate(((w0_ref, o0_ref), (w1_ref, o1_ref), (w2_ref, o2_ref))):
        d = ATTN_GROUPS[g][1]
        for c in range(gw // QKV_CHUNK):
            sl = slice(c * QKV_CHUNK, (c + 1) * QKV_CHUNK)
            t2 = jnp.dot(hb, w_ref[:, sl], preferred_element_type=F32)
            t2 = t2 + b_ref[:, g * gw + c * QKV_CHUNK:g * gw + (c + 1) * QKV_CHUNK]
            for hh in range(QKV_CHUNK // HEAD_DIM):
                t = t2[:, hh * HEAD_DIM:(hh + 1) * HEAD_DIM]
                col = c * QKV_CHUNK + hh * HEAD_DIM
                if col < 2 * ATTN_WIDTH:
                    up = pltpu.roll(t, HEAD_DIM - ROPE_DIM // 2, 1)
                    down = pltpu.roll(t, ROPE_DIM // 2, 1)
                    t = t * cosv + jnp.where(first_half, up, down) * sinv
                cs = slice(col, col + HEAD_DIM)
                if d == 1:
                    o_ref[:, cs] = t.astype(BF16)
                else:
                    for r, part in enumerate(_deinterleave(t, stage, d)):
                        o_ref[0, r, :, cs] = part.astype(BF16)


def _cast_slab_spec(a, nsteps):
    rows, cols = a.shape
    assert rows % (nsteps * BF16_ROWS) == 0, (a.shape, nsteps)
    return pl.BlockSpec((rows // nsteps, cols), lambda i: (i, 0))


def _proj_qkv(x2d, g, ws, b, cos_t, sin_t, batch, cast_a, cast_b):
    T, D = x2d.shape
    S = T // batch
    nseq = S // PROJ_TILE
    nsteps = T // PROJ_TILE
    gw = 3 * ATTN_WIDTH
    d1, d2 = ATTN_GROUPS[1][1], ATTN_GROUPS[2][1]
    full = lambda a: pl.BlockSpec(a.shape, lambda i: (0,) * a.ndim)
    dil_spec = lambda d: pl.BlockSpec((1, d, PROJ_TILE // d, gw),
                                      lambda i: (i // nseq, 0, i % nseq, 0))
    return pl.pallas_call(
        _proj_qkv_kernel,
        out_shape=(jax.ShapeDtypeStruct((T, gw), BF16),
                   jax.ShapeDtypeStruct((batch, d1, S // d1, gw), BF16),
                   jax.ShapeDtypeStruct((batch, d2, S // d2, gw), BF16),
                   jax.ShapeDtypeStruct(cast_a.shape, BF16),
                   jax.ShapeDtypeStruct(cast_b.shape, BF16)),
        grid=(nsteps,),
        in_specs=[
            pl.BlockSpec((PROJ_TILE, D), lambda i: (i, 0)),
            full(g), full(ws[0]), full(ws[1]), full(ws[2]), full(b),
            pl.BlockSpec((PROJ_TILE, HEAD_DIM), lambda i: (i % nseq, 0)),
            pl.BlockSpec((PROJ_TILE, HEAD_DIM), lambda i: (i % nseq, 0)),
            _cast_slab_spec(cast_a, nsteps), _cast_slab_spec(cast_b, nsteps),
        ],
        out_specs=(pl.BlockSpec((PROJ_TILE, gw), lambda i: (i, 0)), dil_spec(d1), dil_spec(d2),
                   _cast_slab_spec(cast_a, nsteps), _cast_slab_spec(cast_b, nsteps)),
        scratch_shapes=[pltpu.VMEM((PROJ_TILE // d2 * _pitch(d2), LANES), F32)],
        compiler_params=_cparams(("parallel",)),
        name="proj_qkv",
    )(x2d, g, ws[0], ws[1], ws[2], b, cos_t, sin_t, cast_a, cast_b)


def _rope_tables(seq):
    pos = jnp.arange(seq, dtype=F32)
    inv = ROPE_THETA ** (-jnp.arange(0, ROPE_DIM, 2, dtype=F32) / ROPE_DIM)
    ang = pos[:, None] * inv[None, :]
    cos, sin = jnp.cos(ang), jnp.sin(ang)
    pad = HEAD_DIM - ROPE_DIM
    cos_t = jnp.concatenate([cos, cos, jnp.ones((seq, pad), F32)], axis=1)
    sin_t = jnp.concatenate([-sin, sin, jnp.zeros((seq, pad), F32)], axis=1)
    return cos_t, sin_t


def _sigmoid(x):
    return 0.5 * jnp.tanh(0.5 * x) + 0.5


def _rglru_kernel(xr_ref, gr_ref, cw_ref, cb_ref, gw_ref, gb_ref, lam_ref, o_ref,
                  xpad, af, uf, ab, ub, cf, cbk, *, seq):
    nseg = seq // SCAN_SEG
    nchunk = seq // SCAN_CHUNK
    seg_per_chunk = SCAN_CHUNK // SCAN_SEG
    pitch = nseg + 4

    cw = cw_ref[...]
    cbias = cb_ref[...]
    gbias = gb_ref[...]
    z = -lam_ref[...]
    sp = jnp.maximum(z, 0.0) + jnp.log1p(jnp.exp(-jnp.abs(z)))
    half_neg_c_sp = (-0.5 * LRU_C) * sp

    zero_rows = jnp.zeros((SUBLANES, LANES), F32)
    xpad[0:SUBLANES, :] = zero_rows
    xpad[seq + SUBLANES:seq + 2 * SUBLANES, :] = zero_rows

    def copy_in(c, carry):
        t0 = pl.multiple_of(c * SCAN_CHUNK, SCAN_CHUNK)
        xpad[pl.ds(t0 + SUBLANES, SCAN_CHUNK), :] = xr_ref[0, pl.ds(t0, SCAN_CHUNK), :]
        return carry

    lax.fori_loop(0, nchunk, copy_in, 0)

    def stage1(c, carry):
        t0 = c * SCAN_CHUNK
        xc = cbias
        for tap in range(CONV_WIDTH):
            lo = t0 + SUBLANES - CONV_WIDTH // 2 + tap
            xc = xc + cw[tap:tap + 1, :] * xpad[pl.ds(lo, SCAN_CHUNK), :]
        th = jnp.tanh(jnp.dot(xc.astype(BF16), gw_ref[...], preferred_element_type=F32) + gbias)
        xc_half = 0.5 * xc
        for d, (a_s, u_s) in enumerate(((af, uf), (ab, ub))):
            t_r = th[:, (2 * d) * LANES:(2 * d + 1) * LANES]
            t_i = th[:, (2 * d + 1) * LANES:(2 * d + 2) * LANES]
            half_k = half_neg_c_sp[d:d + 1, :]
            log_a = t_r * half_k + half_k
            a = jnp.exp(log_a)
            v = jnp.tanh(log_a) * (-1.0 - a * a)
            u = (xc_half * (v * lax.rsqrt(jnp.maximum(v, F32_MIN_NORMAL)))) * (t_i + 1.0)
            for k in range(seg_per_chunk):
                seg = c * seg_per_chunk + k
                a_s[pl.ds(seg, SCAN_SEG, stride=pitch), :] = a[k * SCAN_SEG:(k + 1) * SCAN_SEG, :]
                u_s[pl.ds(seg, SCAN_SEG, stride=pitch), :] = u[k * SCAN_SEG:(k + 1) * SCAN_SEG, :]
        return carry

    lax.fori_loop(0, nchunk, stage1, 0, unroll=4)

    def scan_dir(a_s, u_s, reverse):
        def body(it, carry):
            i = (SCAN_SEG - 1 - it) if reverse else it
            h, cum = carry
            rows = pl.ds(i * pitch, nseg)
            a = a_s[rows, :]
            u = u_s[rows, :]
            h = a * h + u
            cum = a * cum
            u_s[rows, :] = h
            a_s[rows, :] = cum
            return h, cum

        return lax.fori_loop(0, SCAN_SEG, body,
                             (jnp.zeros((nseg, LANES), F32), jnp.ones((nseg, LANES), F32)))

    end_h, end_a = scan_dir(af, uf, False)
    beg_h, beg_a = scan_dir(ab, ub, True)

    c = jnp.zeros((1, LANES), F32)
    for s in range(nseg):
        cf[s:s + 1, :] = c
        c = end_h[s:s + 1, :] + end_a[s:s + 1, :] * c
    c = jnp.zeros((1, LANES), F32)
    for s in range(nseg - 1, -1, -1):
        cbk[s:s + 1, :] = c
        c = beg_h[s:s + 1, :] + beg_a[s:s + 1, :] * c

    def finish(s, carry):
        t0 = pl.multiple_of(s * SCAN_SEG, SCAN_SEG)
        rows = pl.ds(s, SCAN_SEG, stride=pitch)
        hf = uf[rows, :] + af[rows, :] * cf[pl.ds(s, 1), :]
        hb = ub[rows, :] + ab[rows, :] * cbk[pl.ds(s, 1), :]
        g = jax.nn.gelu(gr_ref[0, pl.ds(t0, SCAN_SEG), :])
        o_ref[0, pl.ds(t0, SCAN_SEG), :] = (g * (hf + hb)).astype(BF16)
        return carry

    lax.fori_loop(0, nseg, finish, 0)


def _rglru(xrgr, conv_w, conv_b, gate_w, gate_b, lam):
    B, S, _ = xrgr.shape
    nblk = D_RNN // RNN_BW
    nseg = S // SCAN_SEG
    scr = pltpu.VMEM((SCAN_SEG * (nseg + 4), LANES), F32)
    return pl.pallas_call(
        functools.partial(_rglru_kernel, seq=S),
        out_shape=jax.ShapeDtypeStruct((B, S, D_RNN), BF16),
        grid=(B, nblk),
        in_specs=[
            pl.BlockSpec((1, S, RNN_BW), lambda b, c: (b, 0, c)),
            pl.BlockSpec((1, S, RNN_BW), lambda b, c: (b, 0, nblk + c)),
            pl.BlockSpec((CONV_WIDTH, RNN_BW), lambda b, c: (0, c)),
            pl.BlockSpec((1, RNN_BW), lambda b, c: (0, c)),
            pl.BlockSpec((None, RNN_BW, 4 * RNN_BW), lambda b, c: (c, 0, 0)),
            pl.BlockSpec((None, 1, 4 * RNN_BW), lambda b, c: (c, 0, 0)),
            pl.BlockSpec((None, 2, RNN_BW), lambda b, c: (c, 0, 0)),
        ],
        out_specs=pl.BlockSpec((1, S, RNN_BW), lambda b, c: (b, 0, c)),
        scratch_shapes=[pltpu.VMEM((S + 2 * SUBLANES, LANES), F32), scr, scr, scr, scr,
                        pltpu.VMEM((nseg, LANES), F32), pltpu.VMEM((nseg, LANES), F32)],
        compiler_params=_cparams(("parallel", "parallel")),
        name="rglru",
    )(xrgr, xrgr, conv_w, conv_b, gate_w, gate_b, lam)


def _attn_kernel(q_ref, k_ref, kp_ref, kn_ref, v_ref, vp_ref, vn_ref, o_ref, st_ref,
                 kext, vext, *, qblk, length):
    j = pl.program_id(2)
    nkey = ATT_SUB + 2 * RADIUS
    qi = lax.broadcasted_iota(jnp.int32, (ATT_SUB, nkey), 0)
    ki = lax.broadcasted_iota(jnp.int32, (ATT_SUB, nkey), 1)
    band_bias = jnp.where(jnp.abs(ki - RADIUS - qi) <= RADIUS, 0.0, NEG_INF)
    krow = lax.broadcasted_iota(jnp.int32, (1, nkey), 1)
    lane = lax.broadcasted_iota(jnp.int32, (ATT_SUB, LANES), 1)
    scale2 = HEAD_DIM ** -0.5 * LOG2_E

    for rr in range(q_ref.shape[1]):
        kx, vx = kext.at[rr], vext.at[rr]
        kx[0:RADIUS, :] = kp_ref[0, rr]
        kx[RADIUS:RADIUS + qblk, :] = k_ref[0, rr]
        kx[RADIUS + qblk:RADIUS + qblk + RADIUS, :] = kn_ref[0, rr]
        for h in range(HEADS):
            src = slice(h * HEAD_DIM, (h + 1) * HEAD_DIM)
            dst = slice(2 * h * HEAD_DIM, (2 * h + 1) * HEAD_DIM)
            vx[0:RADIUS, dst] = vp_ref[0, rr, :, src]
            vx[RADIUS:RADIUS + qblk, dst] = v_ref[0, rr, :, src]
            vx[RADIUS + qblk:RADIUS + qblk + RADIUS, dst] = vn_ref[0, rr, :, src]
            vx[:, (2 * h + 1) * HEAD_DIM:(2 * h + 2) * HEAD_DIM] = jnp.ones(
                (qblk + 2 * RADIUS, HEAD_DIM), BF16)

        for i in range(qblk // ATT_SUB):
            q0 = i * ATT_SUB
            kglob = j * qblk + q0 - RADIUS + krow
            edge_bias = jnp.where((kglob >= 0) & (kglob < length), 0.0, NEG_INF)
            bias = band_bias + edge_bias
            st_tile = jnp.ones((ATT_SUB, LANES), F32)
            for h in range(HEADS):
                hs = slice(h * HEAD_DIM, (h + 1) * HEAD_DIM)
                q = q_ref[0, rr, q0:q0 + ATT_SUB, hs]
                ks = kx[q0:q0 + nkey, hs]
                vs = vx[q0:q0 + nkey, 2 * h * HEAD_DIM:(2 * h + 2) * HEAD_DIM]
                s = lax.dot_general(q, ks, (((1,), (1,)), ((), ())),
                                    preferred_element_type=F32) * scale2 + bias
                m = jnp.max(s, axis=-1, keepdims=True)
                p = jnp.exp2(s - m)
                pv = jnp.dot(p.astype(BF16), vs, preferred_element_type=F32)
                l = pv[:, HEAD_DIM:2 * HEAD_DIM]
                o_ref[0, rr, q0:q0 + ATT_SUB, hs] = pv[:, 0:HEAD_DIM]
                st_tile = jnp.where(lane == h, m * LN_2,
                                    jnp.where(lane == HEADS + h, l, st_tile))
            st_ref[0, rr, q0:q0 + ATT_SUB, :] = st_tile


def _attention_group(qkv_g):
    B, d, L, _ = qkv_g.shape
    qblk = min(L, ATT_QBLK)
    nblk = L // qblk
    rpb = min(d, max(1, ATT_QBLK // L))
    halo_per_blk = qblk // RADIUS
    last_halo = L // RADIUS - 1

    main = lambda part: pl.BlockSpec((1, rpb, qblk, ATTN_WIDTH), lambda b, r, j: (b, r, j, part))
    prev = lambda part: pl.BlockSpec(
        (1, rpb, RADIUS, ATTN_WIDTH),
        lambda b, r, j: (b, r, jnp.maximum(j * halo_per_blk - 1, 0), part))
    nxt = lambda part: pl.BlockSpec(
        (1, rpb, RADIUS, ATTN_WIDTH),
        lambda b, r, j: (b, r, jnp.minimum((j + 1) * halo_per_blk, last_halo), part))
    return pl.pallas_call(
        functools.partial(_attn_kernel, qblk=qblk, length=L),
        out_shape=(jax.ShapeDtypeStruct((B, d, L, ATTN_WIDTH), F32),
                   jax.ShapeDtypeStruct((B, d, L, LANES), F32)),
        grid=(B, d // rpb, nblk),
        in_specs=[main(0), main(1), prev(1), nxt(1), main(2), prev(2), nxt(2)],
        out_specs=(pl.BlockSpec((1, rpb, qblk, ATTN_WIDTH), lambda b, r, j: (b, r, j, 0)),
                   pl.BlockSpec((1, rpb, qblk, LANES), lambda b, r, j: (b, r, j, 0))),
        scratch_shapes=[pltpu.VMEM((rpb, qblk + 2 * RADIUS, ATTN_WIDTH), BF16),
                        pltpu.VMEM((rpb, qblk + 2 * RADIUS, 2 * ATTN_WIDTH), BF16)],
        compiler_params=_cparams(("parallel", "parallel", "parallel")),
        name=f"attn_d{d}",
    )(qkv_g, qkv_g, qkv_g, qkv_g, qkv_g, qkv_g, qkv_g)


def _merge_kernel(x_ref, yr_ref, o0_ref, o1_ref, o2_ref, l0_ref, l1_ref, l2_ref,
                  gmix_ref, wg_ref, bg_ref, prnn_ref, pattn_ref, wout_ref,
                  gffn_ref, wr_ref, wrt_ref, br_ref, brt_ref, cc_ref,
                  x2_ref, h2_ref, afft_ref, cco_ref, yattn, stage):
    cco_ref[...] = cc_ref[...].astype(BF16)
    d = x_ref.shape[1]
    x = x_ref[...]
    hb = _rms(x, gmix_ref[...]).astype(BF16)
    branch_a = jnp.dot(yr_ref[...], prnn_ref[...], preferred_element_type=F32)
    ga = jnp.dot(hb, wg_ref[:, 0:d], preferred_element_type=F32) + bg_ref[:, 0:d]
    gb = jnp.dot(hb, wg_ref[:, d:2 * d], preferred_element_type=F32) + bg_ref[:, d:2 * d]

    def natural(ref, g, cs):
        dil = ATTN_GROUPS[g][1]
        if dil == 1:
            return ref[:, cs]
        return _interleave([ref[0, r, :, cs] for r in range(dil)], stage, dil)

    o_refs = (o0_ref, o1_ref, o2_ref)
    all_lanes = slice(0, LANES)
    stats = [natural(r, g, all_lanes) for g, r in enumerate((l0_ref, l1_ref, l2_ref))]
    head_lane = lax.broadcasted_iota(jnp.int32, stats[0].shape, 1) < HEADS
    sums = [jnp.where(head_lane, pltpu.roll(st, LANES - HEADS, 1), 1.0) for st in stats]
    lses = [st + jnp.log(l) for st, l in zip(stats, sums)]
    m = jnp.maximum(jnp.maximum(lses[0], lses[1]), lses[2])
    ws = [jnp.exp(l - m) for l in lses]
    den = ws[0] + ws[1] + ws[2]
    coef = [w / (den * l) for w, l in zip(ws, sums)]
    for h in range(HEADS):
        hs = slice(h * HEAD_DIM, (h + 1) * HEAD_DIM)
        acc = coef[0][:, h:h + 1] * natural(o_refs[0], 0, hs)
        acc = acc + coef[1][:, h:h + 1] * natural(o_refs[1], 1, hs)
        acc = acc + coef[2][:, h:h + 1] * natural(o_refs[2], 2, hs)
        yattn[:, hs] = acc.astype(BF16)

    branch_b = jnp.dot(yattn[...], pattn_ref[...], preferred_element_type=F32)
    merged = _sigmoid(ga) * branch_a + _sigmoid(gb) * branch_b
    x2 = x + jnp.dot(merged.astype(BF16), wout_ref[...], preferred_element_type=F32)
    x2_ref[...] = x2

    h2 = _rms(x2, gffn_ref[...]).astype(BF16)
    h2_ref[:, 0:d] = h2
    n_exp = afft_ref.shape[1]
    logits = jnp.dot(h2, wr_ref[...], preferred_element_type=F32) + br_ref[...]
    lane = lax.broadcasted_iota(jnp.int32, logits.shape, 1)
    mx = jnp.max(logits, axis=-1, keepdims=True)
    e = jnp.exp(logits - mx)
    den = jnp.sum(jnp.where(lane < n_exp, e, 0.0), axis=-1, keepdims=True)
    aff = e / den
    head = aff.astype(BF16).astype(F32)
    tile = jnp.where(lane < n_exp, head, jnp.where(lane < 2 * n_exp, aff - head, 0.0))
    h2_ref[:, d:d + LANES] = tile.astype(BF16)
    logits_t = lax.dot_general(wrt_ref[...], h2, (((1,), (1,)), ((), ())),
                               preferred_element_type=F32) + brt_ref[...]
    mxt = jnp.max(logits_t, axis=0, keepdims=True)
    et = jnp.exp(logits_t - mxt)
    afft_ref[0] = et / jnp.sum(et, axis=0, keepdims=True)


def _merge(x2d, y_rnn, outs, lses, gmix, wg, bg, p_rnn, p_attn, w_out, gffn, wr, br, batch,
           cast_c):
    T, D = x2d.shape
    S = T // batch
    nseq = S // ROW_TILE
    nsteps = T // ROW_TILE
    E = wr.shape[1]
    row = lambda w: pl.BlockSpec((ROW_TILE, w), lambda i: (i, 0))
    full = lambda a: pl.BlockSpec(a.shape, lambda i: (0,) * a.ndim)

    def attn_spec(g, w):
        dil = ATTN_GROUPS[g][1]
        if dil == 1:
            return row(w)
        return pl.BlockSpec((1, dil, ROW_TILE // dil, w), lambda i: (i // nseq, 0, i % nseq, 0))

    wrt = wr.T
    brt = br.reshape(E, 1)
    wr_pad = jnp.concatenate([wr, wr, jnp.zeros((D, LANES - 2 * E), wr.dtype)], axis=1)
    br_pad = jnp.concatenate([br, br, jnp.full((1, LANES - 2 * E), NEG_INF, br.dtype)], axis=1)
    consts = [gmix, wg, bg, p_rnn, p_attn, w_out, gffn, wr_pad, wrt, br_pad, brt]
    args = [x2d, y_rnn, *outs, *lses, *consts, cast_c]
    in_specs = ([row(D), row(D_RNN)] + [attn_spec(g, ATTN_WIDTH) for g in range(N_GROUPS)]
                + [attn_spec(g, LANES) for g in range(N_GROUPS)] + [full(a) for a in consts]
                + [_cast_slab_spec(cast_c, nsteps)])
    dmax = max(dil for _, dil in ATTN_GROUPS)
    return pl.pallas_call(
        _merge_kernel,
        out_shape=(jax.ShapeDtypeStruct((T, D), F32),
                   jax.ShapeDtypeStruct((T, D + LANES), BF16),
                   jax.ShapeDtypeStruct((batch, E, S), F32),
                   jax.ShapeDtypeStruct(cast_c.shape, BF16)),
        grid=(nsteps,),
        in_specs=in_specs,
        out_specs=(row(D), row(D + LANES),
                   pl.BlockSpec((1, E, ROW_TILE), lambda i: (i // nseq, 0, i % nseq)),
                   _cast_slab_spec(cast_c, nsteps)),
        scratch_shapes=[pltpu.VMEM((ROW_TILE, ATTN_WIDTH), BF16),
                        pltpu.VMEM((ROW_TILE // dmax * _pitch(dmax), LANES), F32)],
        compiler_params=_cparams(("parallel",)),
        name="merge",
    )(*args)


def _route_kernel(aff_ref, pos_ref, off_ref, *, cap):
    a = aff_ref[0]
    E, R, _ = a.shape
    bits = pltpu.bitcast(a, jnp.int32)

    def count(mask):
        c = jnp.sum(mask.astype(F32), axis=1, keepdims=True)
        return jnp.sum(c, axis=2, keepdims=True)

    def search(_, carry):
        lo, hi = carry
        q = (hi - lo + 3) >> 2
        m1 = jnp.minimum(lo + q, hi)
        m2 = jnp.minimum(m1 + q, hi)
        m3 = jnp.minimum(m2 + q, hi)
        ok1 = count(bits >= m1) >= cap
        ok2 = count(bits >= m2) >= cap
        ok3 = count(bits >= m3) >= cap
        new_lo = jnp.where(ok3, m3, jnp.where(ok2, m2, jnp.where(ok1, m1, lo)))
        new_hi = jnp.where(ok3, hi, jnp.where(ok2, m3 - 1, jnp.where(ok1, m2 - 1, m1 - 1)))
        return new_lo, new_hi

    lo0 = jnp.zeros((E, 1, 1), jnp.int32)
    hi0 = jnp.full((E, 1, 1), 0x7F7FFFFF, jnp.int32)
    thr, _ = lax.fori_loop(0, SEARCH_STEPS, search, (lo0, hi0))

    gt = bits > thr
    eq = bits == thr
    need = cap - count(gt)

    ki = lax.broadcasted_iota(jnp.int32, (LANES, LANES), 0)
    li = lax.broadcasted_iota(jnp.int32, (LANES, LANES), 1)
    upper = (ki <= li).astype(BF16)
    allones = jnp.ones((LANES, LANES), BF16)
    ri = lax.broadcasted_iota(jnp.int32, (R, R), 0)
    ci = lax.broadcasted_iota(jnp.int32, (R, R), 1)
    strict = (ci < ri).astype(BF16)

    def prefix(mask):
        mb = mask.astype(BF16).reshape(E * R, LANES)
        within = jnp.dot(mb, upper, preferred_element_type=F32)
        totals = jnp.dot(mb, allones, preferred_element_type=F32).astype(BF16)
        offs = []
        for e in range(E):
            offs.append(jnp.dot(strict, totals[e * R:(e + 1) * R, :],
                                preferred_element_type=F32))
        rowoff = jnp.concatenate(offs, axis=0)
        excl = within - mb.astype(F32) + rowoff
        return excl.reshape(E, R, LANES), rowoff.reshape(E, R, LANES)

    eq_rank, _ = prefix(eq)
    sel = gt | (eq & (eq_rank < need.astype(F32)))
    slot, rowoff = prefix(sel)
    pos_ref[0] = jnp.where(sel, slot.astype(jnp.int32), -1)
    off_ref[0] = rowoff.astype(jnp.int32)


def _route(aff_t, cap):
    B, E, S = aff_t.shape
    R = S // LANES
    a4 = aff_t.reshape(B, E, R, LANES)
    blk = pl.BlockSpec((1, E, R, LANES), lambda b: (b, 0, 0, 0))
    pos, off = pl.pallas_call(
        functools.partial(_route_kernel, cap=cap),
        out_shape=(jax.ShapeDtypeStruct((B, E, R, LANES), jnp.int32),
                   jax.ShapeDtypeStruct((B, E, R, LANES), jnp.int32)),
        grid=(B,),
        in_specs=[blk],
        out_specs=(blk, blk),
        compiler_params=_cparams(("parallel",)),
        name="route",
    )(a4)
    return pos, off


def _slot_windows(off_ref, base, n_exp, ntb, tb):
    starts = []
    rounds = jnp.int32(0)
    for e in range(n_exp):
        idx = (base + e) * (ntb + 1) + tb
        start = (off_ref[idx] >> 4) << 4
        starts.append(start)
        rounds = jnp.maximum(rounds, (off_ref[idx + 1] - start + SLOT_WIN - 1) >> SLOT_SHIFT)
    return starts, rounds


def _gather_kernel(off_ref, pos_ref, h_ref, o_ref, *, cap, ntb, n_exp, nsub):
    b = pl.program_id(0)
    grp = pl.program_id(1)
    t = pl.program_id(2)

    @pl.when(t == 0)
    def _():
        o_ref[...] = jnp.zeros_like(o_ref)

    base = (b * pl.num_programs(1) + grp) * n_exp
    ji = lax.broadcasted_iota(jnp.int32, (SLOT_WIN, TOK_BLK), 0)
    for sb in range(nsub):
        toks = slice(sb * TOK_BLK, (sb + 1) * TOK_BLK)
        starts, rounds = _slot_windows(off_ref, base, n_exp, ntb, t * nsub + sb)

        def one_round(k, carry, toks=toks, starts=starts):
            pieces, rows = [], []
            for e in range(n_exp):
                first = starts[e] + k * SLOT_WIN
                row0 = pl.multiple_of(jnp.minimum(first, cap - SLOT_WIN), BF16_ROWS)
                slot = ji + row0
                hit = (slot == pos_ref[0, 0, e:e + 1, toks]) & (slot >= first)
                pieces.append(jnp.where(hit, 1.0, 0.0).astype(BF16))
                rows.append(row0)
            onehot = jnp.concatenate(pieces, axis=0)
            g = jnp.dot(onehot, h_ref[0, toks, :], preferred_element_type=F32)
            for e in range(n_exp):
                o_ref[0, e, pl.ds(rows[e], SLOT_WIN), :] += (
                    g[e * SLOT_WIN:(e + 1) * SLOT_WIN, :].astype(BF16))
            return carry

        one_round(0, 0)
        lax.fori_loop(1, rounds, one_round, 0)


def _gather(offs, pos, h2, cap):
    B, E, S = pos.shape
    D = h2.shape[-1]
    ntb = S // TOK_BLK
    n_exp = E // EXPERT_SPLIT
    nsub = GATHER_TOK // TOK_BLK
    pos4 = pos.reshape(B, EXPERT_SPLIT, n_exp, S)
    return pl.pallas_call(
        functools.partial(_gather_kernel, cap=cap, ntb=ntb, n_exp=n_exp, nsub=nsub),
        out_shape=jax.ShapeDtypeStruct((B, E, cap, D), BF16),
        grid_spec=pltpu.PrefetchScalarGridSpec(
            num_scalar_prefetch=1,
            grid=(B, EXPERT_SPLIT, S // GATHER_TOK),
            in_specs=[
                pl.BlockSpec((1, 1, n_exp, GATHER_TOK), lambda b, g, t, off: (b, g, 0, t)),
                pl.BlockSpec((1, GATHER_TOK, D), lambda b, g, t, off: (b, t, 0)),
            ],
            out_specs=pl.BlockSpec((1, n_exp, cap, D), lambda b, g, t, off: (b, g, 0, 0)),
        ),
        compiler_params=_cparams(("parallel", "parallel", "arbitrary")),
        name="gather",
    )(offs, pos4, h2)


def _ffn_kernel(x_ref, wg_ref, wu_ref, wd_ref, o_ref, acc, *, f_chunk):
    e = pl.program_id(0)
    d = wg_ref.shape[1]
    x = x_ref[0, 0, :, 0:d]
    gl = x_ref[0, 0, :, d:d + LANES].astype(F32)
    lane = lax.broadcasted_iota(jnp.int32, gl.shape, 1)
    gate = jnp.sum(jnp.where((lane == e) | (lane == e + pl.num_programs(0)), gl, 0.0),
                   axis=1, keepdims=True)
    nf = wg_ref.shape[2] // f_chunk
    for f in range(nf):
        fs = slice(f * f_chunk, (f + 1) * f_chunk)
        g = jnp.dot(x, wg_ref[0, :, fs], preferred_element_type=F32)
        u = jnp.dot(x, wu_ref[0, :, fs], preferred_element_type=F32)
        hid = (g * _sigmoid(g) * u).astype(BF16)
        part = jnp.dot(hid, wd_ref[0, fs, :], preferred_element_type=F32)
        if f == 0:
            acc[...] = part
        else:
            acc[...] += part
    o_ref[0, 0] = (acc[...] * gate).astype(BF16)


def _ffn(xg, w_gate, w_up, w_down):
    B, E, C, DX = xg.shape
    D = w_gate.shape[1]
    F = w_gate.shape[2]
    assert F % FFN_CHUNK == 0, (F, FFN_CHUNK)
    return pl.pallas_call(
        functools.partial(_ffn_kernel, f_chunk=FFN_CHUNK),
        out_shape=jax.ShapeDtypeStruct((B, E, C, D), BF16),
        grid=(E, B),
        in_specs=[
            pl.BlockSpec((1, 1, C, DX), lambda e, b: (b, e, 0, 0)),
            pl.BlockSpec((1, D, F), lambda e, b: (e, 0, 0)),
            pl.BlockSpec((1, D, F), lambda e, b: (e, 0, 0)),
            pl.BlockSpec((1, F, D), lambda e, b: (e, 0, 0)),
        ],
        out_specs=pl.BlockSpec((1, 1, C, D), lambda e, b: (b, e, 0, 0)),
        scratch_shapes=[pltpu.VMEM((C, D), F32)],
        compiler_params=_cparams(("arbitrary", "arbitrary")),
        name="ffn",
    )(xg, w_gate, w_up, w_down)


def _combine_kernel(off_ref, x_ref, pos_ref, eo_ref, g_ref, o_ref, wstack, acc, *, cap, ntb):
    b = pl.program_id(0)
    t = pl.program_id(1)
    E = pos_ref.shape[2]
    nsub = pos_ref.shape[1] // TOK_BLK
    lane = lax.broadcasted_iota(jnp.int32, (TOK_BLK, LANES), 1)
    low = lane < SLOT_WIN

    for sb in range(nsub):
        toks = slice(sb * TOK_BLK, (sb + 1) * TOK_BLK)
        pos = pos_ref[0, toks, :]
        starts, rounds = _slot_windows(off_ref, b * E, E, ntb, t * nsub + sb)

        def one_round(k, carry, toks=toks, pos=pos, starts=starts, wst=wstack.at[sb],
                      first_round=False):
            onehot = []
            for p in range(E // 2):
                firsts, row0s = [], []
                for e in (2 * p, 2 * p + 1):
                    first = starts[e] + k * SLOT_WIN
                    row0 = pl.multiple_of(jnp.minimum(first, cap - SLOT_WIN), BF16_ROWS)
                    wst[e * SLOT_WIN:(e + 1) * SLOT_WIN, :] = (
                        eo_ref[0, e, pl.ds(row0, SLOT_WIN), :])
                    firsts.append(first)
                    row0s.append(row0)
                slot = lane + jnp.where(low, row0s[0], row0s[1] - SLOT_WIN)
                first = jnp.where(low, firsts[0], firsts[1])
                pcol = jnp.where(low, pos[:, 2 * p:2 * p + 1], pos[:, 2 * p + 1:2 * p + 2])
                hit = (slot == pcol) & (slot >= first)
                onehot.append(jnp.where(hit, 1.0, 0.0).astype(BF16))
            y = jnp.dot(jnp.concatenate(onehot, axis=1), wst[...], preferred_element_type=F32)
            acc[toks, :] = y + (x_ref[0, toks, :] if first_round else acc[toks, :])
            return carry

        one_round(0, 0, first_round=True)
        lax.fori_loop(1, rounds, one_round, 0)
    o_ref[0] = _rms(acc[...], g_ref[...])


def _combine(offs, x2, pos_t, eo, g_final, cap):
    B, S, D = x2.shape
    E = pos_t.shape[2]
    ntb = S // TOK_BLK
    return pl.pallas_call(
        functools.partial(_combine_kernel, cap=cap, ntb=ntb),
        out_shape=jax.ShapeDtypeStruct((B, S, D), F32),
        grid_spec=pltpu.PrefetchScalarGridSpec(
            num_scalar_prefetch=1,
            grid=(B, S // COMBINE_TOK),
            in_specs=[
                pl.BlockSpec((1, COMBINE_TOK, D), lambda b, t, off: (b, t, 0)),
                pl.BlockSpec((1, COMBINE_TOK, E), lambda b, t, off: (b, t, 0)),
                pl.BlockSpec((1, E, cap, D), lambda b, t, off: (b, 0, 0, 0),
                             pipeline_mode=pl.Buffered(1)),
                pl.BlockSpec((1, D), lambda b, t, off: (0, 0)),
            ],
            out_specs=pl.BlockSpec((1, COMBINE_TOK, D), lambda b, t, off: (b, t, 0)),
            scratch_shapes=[pltpu.VMEM((COMBINE_TOK // TOK_BLK, E * SLOT_WIN, D), BF16),
                            pltpu.VMEM((COMBINE_TOK, D), F32)],
        ),
        compiler_params=_cparams(("parallel", "arbitrary")),
        name="combine",
    )(offs, x2, pos_t, eo, g_final)


def _layer(x, norm_mix, w_in, b_in, conv_w, conv_b, rg_w, rg_b, rg_lambda, p_rnn, p_attn,
           w_out, norm_ffn, w_router, b_router, w_gate, w_up, w_down, norm_out, cos_t, sin_t):
    B, S, D = x.shape
    T = B * S
    x2d = x.reshape(T, D)
    n_rnn = 2 * D_RNN
    n_qkv = 3 * N_GROUPS * ATTN_WIDTH
    gmix = norm_mix.reshape(1, D)
    w_in_b = w_in.astype(BF16)
    b_in2 = b_in.reshape(1, -1)

    xrgr = _proj_f32(x2d, gmix, w_in_b[:, :n_rnn], b_in2[:, :n_rnn])

    def group_cols(a, g):
        part = lambda p: a[:, n_rnn + (p * N_GROUPS + g) * ATTN_WIDTH:
                           n_rnn + (p * N_GROUPS + g + 1) * ATTN_WIDTH]
        return jnp.concatenate([part(0), part(1), part(2)], axis=1)

    ws = [group_cols(w_in_b, g) for g in range(N_GROUPS)]
    b_qkv = jnp.concatenate([group_cols(b_in2, g) for g in range(N_GROUPS)], axis=1)
    n_e, d_in, d_ff = w_gate.shape
    *qkv, w_gate_b, w_up_b = _proj_qkv(
        x2d, gmix, ws, b_qkv, cos_t, sin_t, B,
        w_gate.reshape(n_e * d_in, d_ff), w_up.reshape(n_e * d_in, d_ff))

    nblk = D_RNN // RNN_BW
    gate_w = (0.5 * jnp.transpose(rg_w, (2, 3, 0, 1, 4))).reshape(nblk, RNN_BW, 4 * RNN_BW).astype(BF16)
    gate_b = 0.5 * jnp.transpose(rg_b.reshape(2, 2, nblk, RNN_BW), (2, 0, 1, 3)).reshape(nblk, 1, 4 * RNN_BW)
    lam = jnp.transpose(rg_lambda.reshape(2, nblk, RNN_BW), (1, 0, 2))
    y_rnn = _rglru(xrgr.reshape(B, S, n_rnn), conv_w, conv_b.reshape(1, D_RNN), gate_w, gate_b, lam)

    outs, lses = [], []
    for g, (_, dilation) in enumerate(ATTN_GROUPS):
        qkv_g = qkv[g].reshape(B, 1, S, 3 * ATTN_WIDTH) if dilation == 1 else qkv[g]
        o, lse = _attention_group(qkv_g)
        if dilation == 1:
            o, lse = o.reshape(T, ATTN_WIDTH), lse.reshape(T, LANES)
        outs.append(o)
        lses.append(lse)

    x2, h2, aff_t, w_down_b = _merge(
        x2d, y_rnn.reshape(T, D_RNN), outs, lses, gmix,
        w_in_b[:, n_rnn + n_qkv:], b_in2[:, n_rnn + n_qkv:],
        p_rnn.astype(BF16), p_attn.astype(BF16), w_out.astype(BF16),
        norm_ffn.reshape(1, D), w_router.astype(BF16), b_router.reshape(1, -1), B,
        w_down.reshape(n_e * d_ff, d_in))

    E = w_router.shape[1]
    cap = CAPACITY_FACTOR * S // E
    pos4, off4 = _route(aff_t, cap)
    pos = pos4.reshape(B, E, S)
    blk_off = off4[:, :, ::TOK_BLK // LANES, 0]
    offs = jnp.concatenate([blk_off, jnp.full((B, E, 1), cap, jnp.int32)], axis=2).reshape(-1)
    xg = _gather(offs, pos, h2.reshape(B, S, D + LANES), cap)
    eo = _ffn(xg, w_gate_b.reshape(n_e, d_in, d_ff), w_up_b.reshape(n_e, d_in, d_ff),
              w_down_b.reshape(n_e, d_ff, d_in))
    pos_t = jnp.transpose(pos, (0, 2, 1))
    return _combine(offs, x2.reshape(B, S, D), pos_t, eo, norm_out.reshape(1, D), cap)


def kernel(x, norm_mix, w_in, b_in, conv_w, conv_b, rg_w, rg_b, rg_lambda, p_rnn, p_attn, w_out,
           norm_ffn, w_router, b_router, w_gate, w_up, w_down, norm_final):
    depth = w_in.shape[0]
    assert depth == 1, "final rmsnorm is fused into the single layer's combine step"
    cos_t, sin_t = _rope_tables(x.shape[1])
    return _layer(x, norm_mix[0], w_in[0], b_in[0], conv_w[0], conv_b[0], rg_w[0], rg_b[0],
                  rg_lambda[0], p_rnn[0], p_attn[0], w_out[0], norm_ffn[0], w_router[0],
                  b_router[0], w_gate[0], w_up[0], w_down[0], norm_final, cos_t, sin_t)
```

```python
import functools

import jax
import jax.numpy as jnp
from jax import lax
from jax.experimental import pallas as pl
from jax.experimental.pallas import tpu as pltpu

F32 = jnp.float32
BF16 = jnp.bfloat16

D_RNN = 1024
RNN_BW = 128
CONV_WIDTH = 4
LRU_C = 8.0
HEAD_DIM = 128
HEADS = 4
ATTN_WIDTH = HEADS * HEAD_DIM
ATTN_GROUPS = ((128, 1), (512, 4), (2048, 16))
N_GROUPS = 3
ROPE_THETA = 500000.0
ROPE_DIM = HEAD_DIM // 4
RADIUS = 64
NEG_INF = -1e30
N_EXPERTS = 16
CAPACITY_FACTOR = 2
RMS_EPS = 1e-6

LANES = 128
SUBLANES = 8
BF16_ROWS = 16
VMEM_LIMIT = 56 * 1024 * 1024
F32_MIN_NORMAL = 1.1754944e-38
LOG2_E = 1.4426950408889634
LN_2 = 0.6931471805599453

PROJ_TILE = 1024
QKV_CHUNK = 512
ROW_TILE = 512
SCAN_SEG = 128
SCAN_CHUNK = 512
ATT_QBLK = 2048
ATT_SUB = 128
TOK_BLK = 256
GATHER_TOK = 2048
COMBINE_TOK = 512
SLOT_SHIFT = 6
SLOT_WIN = 1 << SLOT_SHIFT
EXPERT_SPLIT = 2
FFN_CHUNK = 512
SEARCH_STEPS = 17


def _cparams(sem):
    return pltpu.CompilerParams(dimension_semantics=sem, vmem_limit_bytes=VMEM_LIMIT)


def _rms(x, g):
    ms = jnp.mean(x * x, axis=-1, keepdims=True)
    return x * lax.rsqrt(ms + RMS_EPS) * g


def _pitch(d):
    return d if d <= 4 else d + 4


def _deinterleave(t, stage, d):
    n = t.shape[0]
    p = _pitch(d)
    if p == d:
        stage[0:n, :] = t
    else:
        for j in range(n // d):
            stage[j * p:j * p + d, :] = t[j * d:(j + 1) * d, :]
    return [stage[pl.ds(r, n // d, stride=p), :] for r in range(d)]


def _interleave(parts, stage, d):
    m = parts[0].shape[0]
    p = _pitch(d)
    for r in range(d):
        stage[pl.ds(r, m, stride=p), :] = parts[r]
    if p == d:
        return stage[0:m * d, :]
    return jnp.concatenate([stage[j * p:j * p + d, :] for j in range(m)], axis=0)


def _proj_f32_kernel(x_ref, g_ref, w_ref, b_ref, o_ref, *, n_chunk):
    hb = _rms(x_ref[...], g_ref[...]).astype(BF16)
    for c in range(o_ref.shape[1] // n_chunk):
        sl = slice(c * n_chunk, (c + 1) * n_chunk)
        o_ref[:, sl] = jnp.dot(hb, w_ref[:, sl], preferred_element_type=F32) + b_ref[:, sl]


def _proj_f32(x2d, g, w, b):
    T, D = x2d.shape
    N = w.shape[1]
    return pl.pallas_call(
        functools.partial(_proj_f32_kernel, n_chunk=512),
        out_shape=jax.ShapeDtypeStruct((T, N), F32),
        grid=(T // PROJ_TILE,),
        in_specs=[
            pl.BlockSpec((PROJ_TILE, D), lambda i: (i, 0)),
            pl.BlockSpec((1, D), lambda i: (0, 0)),
            pl.BlockSpec((D, N), lambda i: (0, 0)),
            pl.BlockSpec((1, N), lambda i: (0, 0)),
        ],
        out_specs=pl.BlockSpec((PROJ_TILE, N), lambda i: (i, 0)),
        compiler_params=_cparams(("parallel",)),
        name="proj_rnn",
    )(x2d, g, w, b)


def _proj_qkv_kernel(x_ref, g_ref, w0_ref, w1_ref, w2_ref, b_ref, cos_ref, sin_ref,
                     ca_ref, cb_ref, o0_ref, o1_ref, o2_ref, cao_ref, cbo_ref, stage):
    cao_ref[...] = ca_ref[...].astype(BF16)
    cbo_ref[...] = cb_ref[...].astype(BF16)
    hb = _rms(x_ref[...], g_ref[...]).astype(BF16)
    cosv = cos_ref[...]
    sinv = sin_ref[...]
    lane = lax.broadcasted_iota(jnp.int32, (1, HEAD_DIM), 1)
    first_half = lane < (ROPE_DIM // 2)
    gw = 3 * ATTN_WIDTH
    for g, (w_ref, o_ref) in enumerate(((w0_ref, o0_ref), (w1_ref, o1_ref), (w2_ref, o2_ref))):
        d = ATTN_GROUPS[g][1]
        for c in range(gw // QKV_CHUNK):
            sl = slice(c * QKV_CHUNK, (c + 1) * QKV_CHUNK)
            t2 = jnp.dot(hb, w_ref[:, sl], preferred_element_type=F32)
            t2 = t2 + b_ref[:, g * gw + c * QKV_CHUNK:g * gw + (c + 1) * QKV_CHUNK]
            for hh in range(QKV_CHUNK // HEAD_DIM):
                t = t2[:, hh * HEAD_DIM:(hh + 1) * HEAD_DIM]
                col = c * QKV_CHUNK + hh * HEAD_DIM
                if col < 2 * ATTN_WIDTH:
                    up = pltpu.roll(t, HEAD_DIM - ROPE_DIM // 2, 1)
                    down = pltpu.roll(t, ROPE_DIM // 2, 1)
                    t = t * cosv + jnp.where(first_half, up, down) * sinv
                cs = slice(col, col + HEAD_DIM)
                if d == 1:
                    o_ref[:, cs] = t.astype(BF16)
                else:
                    for r, part in enumerate(_deinterleave(t, stage, d)):
                        o_ref[0, r, :, cs] = part.astype(BF16)


def _cast_slab_spec(a, nsteps):
    rows, cols = a.shape
    assert rows % (nsteps * BF16_ROWS) == 0, (a.shape, nsteps)
    return pl.BlockSpec((rows // nsteps, cols), lambda i: (i, 0))


def _proj_qkv(x2d, g, ws, b, cos_t, sin_t, batch, cast_a, cast_b):
    T, D = x2d.shape
    S = T // batch
    nseq = S // PROJ_TILE
    nsteps = T // PROJ_TILE
    gw = 3 * ATTN_WIDTH
    d1, d2 = ATTN_GROUPS[1][1], ATTN_GROUPS[2][1]
    full = lambda a: pl.BlockSpec(a.shape, lambda i: (0,) * a.ndim)
    dil_spec = lambda d: pl.BlockSpec((1, d, PROJ_TILE // d, gw),
                                      lambda i: (i // nseq, 0, i % nseq, 0))
    return pl.pallas_call(
        _proj_qkv_kernel,
        out_shape=(jax.ShapeDtypeStruct((T, gw), BF16),
                   jax.ShapeDtypeStruct((batch, d1, S // d1, gw), BF16),
                   jax.ShapeDtypeStruct((batch, d2, S // d2, gw), BF16),
                   jax.ShapeDtypeStruct(cast_a.shape, BF16),
                   jax.ShapeDtypeStruct(cast_b.shape, BF16)),
        grid=(nsteps,),
        in_specs=[
            pl.BlockSpec((PROJ_TILE, D), lambda i: (i, 0)),
            full(g), full(ws[0]), full(ws[1]), full(ws[2]), full(b),
            pl.BlockSpec((PROJ_TILE, HEAD_DIM), lambda i: (i % nseq, 0)),
            pl.BlockSpec((PROJ_TILE, HEAD_DIM), lambda i: (i % nseq, 0)),
            _cast_slab_spec(cast_a, nsteps), _cast_slab_spec(cast_b, nsteps),
        ],
        out_specs=(pl.BlockSpec((PROJ_TILE, gw), lambda i: (i, 0)), dil_spec(d1), dil_spec(d2),
                   _cast_slab_spec(cast_a, nsteps), _cast_slab_spec(cast_b, nsteps)),
        scratch_shapes=[pltpu.VMEM((PROJ_TILE // d2 * _pitch(d2), LANES), F32)],
        compiler_params=_cparams(("parallel",)),
        name="proj_qkv",
    )(x2d, g, ws[0], ws[1], ws[2], b, cos_t, sin_t, cast_a, cast_b)


def _rope_tables(seq):
    pos = jnp.arange(seq, dtype=F32)
    inv = ROPE_THETA ** (-jnp.arange(0, ROPE_DIM, 2, dtype=F32) / ROPE_DIM)
    ang = pos[:, None] * inv[None, :]
    cos, sin = jnp.cos(ang), jnp.sin(ang)
    pad = HEAD_DIM - ROPE_DIM
    cos_t = jnp.concatenate([cos, cos, jnp.ones((seq, pad), F32)], axis=1)
    sin_t = jnp.concatenate([-sin, sin, jnp.zeros((seq, pad), F32)], axis=1)
    return cos_t, sin_t


def _sigmoid(x):
    return 0.5 * jnp.tanh(0.5 * x) + 0.5


def _rglru_kernel(xr_ref, gr_ref, cw_ref, cb_ref, gw_ref, gb_ref, lam_ref, o_ref,
                  xpad, af, uf, ab, ub, cf, cbk, *, seq):
    nseg = seq // SCAN_SEG
    nchunk = seq // SCAN_CHUNK
    seg_per_chunk = SCAN_CHUNK // SCAN_SEG
    pitch = nseg + 4

    cw = cw_ref[...]
    cbias = cb_ref[...]
    gbias = gb_ref[...]
    z = -lam_ref[...]
    sp = jnp.maximum(z, 0.0) + jnp.log1p(jnp.exp(-jnp.abs(z)))
    half_neg_c_sp = (-0.5 * LRU_C) * sp

    zero_rows = jnp.zeros((SUBLANES, LANES), F32)
    xpad[0:SUBLANES, :] = zero_rows
    xpad[seq + SUBLANES:seq + 2 * SUBLANES, :] = zero_rows

    def copy_in(c, carry):
        t0 = pl.multiple_of(c * SCAN_CHUNK, SCAN_CHUNK)
        xpad[pl.ds(t0 + SUBLANES, SCAN_CHUNK), :] = xr_ref[0, pl.ds(t0, SCAN_CHUNK), :]
        return carry

    lax.fori_loop(0, nchunk, copy_in, 0)

    def stage1(c, carry):
        t0 = c * SCAN_CHUNK
        xc = cbias
        for tap in range(CONV_WIDTH):
            lo = t0 + SUBLANES - CONV_WIDTH // 2 + tap
            xc = xc + cw[tap:tap + 1, :] * xpad[pl.ds(lo, SCAN_CHUNK), :]
        th = jnp.tanh(jnp.dot(xc.astype(BF16), gw_ref[...], preferred_element_type=F32) + gbias)
        xc_half = 0.5 * xc
        for d, (a_s, u_s) in enumerate(((af, uf), (ab, ub))):
            t_r = th[:, (2 * d) * LANES:(2 * d + 1) * LANES]
            t_i = th[:, (2 * d + 1) * LANES:(2 * d + 2) * LANES]
            half_k = half_neg_c_sp[d:d + 1, :]
            log_a = t_r * half_k + half_k
            a = jnp.exp(log_a)
            v = jnp.tanh(log_a) * (-1.0 - a * a)
            u = (xc_half * (v * lax.rsqrt(jnp.maximum(v, F32_MIN_NORMAL)))) * (t_i + 1.0)
            for k in range(seg_per_chunk):
                seg = c * seg_per_chunk + k
                a_s[pl.ds(seg, SCAN_SEG, stride=pitch), :] = a[k * SCAN_SEG:(k + 1) * SCAN_SEG, :]
                u_s[pl.ds(seg, SCAN_SEG, stride=pitch), :] = u[k * SCAN_SEG:(k + 1) * SCAN_SEG, :]
        return carry

    lax.fori_loop(0, nchunk, stage1, 0, unroll=4)

    def scan_dir(a_s, u_s, reverse):
        def body(it, carry):
            i = (SCAN_SEG - 1 - it) if reverse else it
            h, cum = carry
            rows = pl.ds(i * pitch, nseg)
            a = a_s[rows, :]
            u = u_s[rows, :]
            h = a * h + u
            cum = a * cum
            u_s[rows, :] = h
            a_s[rows, :] = cum
            return h, cum

        return lax.fori_loop(0, SCAN_SEG, body,
                             (jnp.zeros((nseg, LANES), F32), jnp.ones((nseg, LANES), F32)))

    end_h, end_a = scan_dir(af, uf, False)
    beg_h, beg_a = scan_dir(ab, ub, True)

    c = jnp.zeros((1, LANES), F32)
    for s in range(nseg):
        cf[s:s + 1, :] = c
        c = end_h[s:s + 1, :] + end_a[s:s + 1, :] * c
    c = jnp.zeros((1, LANES), F32)
    for s in range(nseg - 1, -1, -1):
        cbk[s:s + 1, :] = c
        c = beg_h[s:s + 1, :] + beg_a[s:s + 1, :] * c

    def finish(s, carry):
        t0 = pl.multiple_of(s * SCAN_SEG, SCAN_SEG)
        rows = pl.ds(s, SCAN_SEG, stride=pitch)
        hf = uf[rows, :] + af[rows, :] * cf[pl.ds(s, 1), :]
        hb = ub[rows, :] + ab[rows, :] * cbk[pl.ds(s, 1), :]
        g = jax.nn.gelu(gr_ref[0, pl.ds(t0, SCAN_SEG), :])
        o_ref[0, pl.ds(t0, SCAN_SEG), :] = (g * (hf + hb)).astype(BF16)
        return carry

    lax.fori_loop(0, nseg, finish, 0)


def _rglru(xrgr, conv_w, conv_b, gate_w, gate_b, lam):
    B, S, _ = xrgr.shape
    nblk = D_RNN // RNN_BW
    nseg = S // SCAN_SEG
    scr = pltpu.VMEM((SCAN_SEG * (nseg + 4), LANES), F32)
    return pl.pallas_call(
        functools.partial(_rglru_kernel, seq=S),
        out_shape=jax.ShapeDtypeStruct((B, S, D_RNN), BF16),
        grid=(B, nblk),
        in_specs=[
            pl.BlockSpec((1, S, RNN_BW), lambda b, c: (b, 0, c)),
            pl.BlockSpec((1, S, RNN_BW), lambda b, c: (b, 0, nblk + c)),
            pl.BlockSpec((CONV_WIDTH, RNN_BW), lambda b, c: (0, c)),
            pl.BlockSpec((1, RNN_BW), lambda b, c: (0, c)),
            pl.BlockSpec((None, RNN_BW, 4 * RNN_BW), lambda b, c: (c, 0, 0)),
            pl.BlockSpec((None, 1, 4 * RNN_BW), lambda b, c: (c, 0, 0)),
            pl.BlockSpec((None, 2, RNN_BW), lambda b, c: (c, 0, 0)),
        ],
        out_specs=pl.BlockSpec((1, S, RNN_BW), lambda b, c: (b, 0, c)),
        scratch_shapes=[pltpu.VMEM((S + 2 * SUBLANES, LANES), F32), scr, scr, scr, scr,
                        pltpu.VMEM((nseg, LANES), F32), pltpu.VMEM((nseg, LANES), F32)],
        compiler_params=_cparams(("parallel", "parallel")),
        name="rglru",
    )(xrgr, xrgr, conv_w, conv_b, gate_w, gate_b, lam)


def _attn_kernel(q_ref, k_ref, kp_ref, kn_ref, v_ref, vp_ref, vn_ref, o_ref, st_ref,
                 kext, vext, *, qblk, length):
    j = pl.program_id(2)
    nkey = ATT_SUB + 2 * RADIUS
    qi = lax.broadcasted_iota(jnp.int32, (ATT_SUB, nkey), 0)
    ki = lax.broadcasted_iota(jnp.int32, (ATT_SUB, nkey), 1)
    band_bias = jnp.where(jnp.abs(ki - RADIUS - qi) <= RADIUS, 0.0, NEG_INF)
    krow = lax.broadcasted_iota(jnp.int32, (1, nkey), 1)
    lane = lax.broadcasted_iota(jnp.int32, (ATT_SUB, LANES), 1)
    scale2 = HEAD_DIM ** -0.5 * LOG2_E

    for rr in range(q_ref.shape[1]):
        kx, vx = kext.at[rr], vext.at[rr]
        kx[0:RADIUS, :] = kp_ref[0, rr]
        kx[RADIUS:RADIUS + qblk, :] = k_ref[0, rr]
        kx[RADIUS + qblk:RADIUS + qblk + RADIUS, :] = kn_ref[0, rr]
        for h in range(HEADS):
            src = slice(h * HEAD_DIM, (h + 1) * HEAD_DIM)
            dst = slice(2 * h * HEAD_DIM, (2 * h + 1) * HEAD_DIM)
            vx[0:RADIUS, dst] = vp_ref[0, rr, :, src]
            vx[RADIUS:RADIUS + qblk, dst] = v_ref[0, rr, :, src]
            vx[RADIUS + qblk:RADIUS + qblk + RADIUS, dst] = vn_ref[0, rr, :, src]
            vx[:, (2 * h + 1) * HEAD_DIM:(2 * h + 2) * HEAD_DIM] = jnp.ones(
                (qblk + 2 * RADIUS, HEAD_DIM), BF16)

        for i in range(qblk // ATT_SUB):
            q0 = i * ATT_SUB
            kglob = j * qblk + q0 - RADIUS + krow
            edge_bias = jnp.where((kglob >= 0) & (kglob < length), 0.0, NEG_INF)
            bias = band_bias + edge_bias
            st_tile = jnp.ones((ATT_SUB, LANES), F32)
            for h in range(HEADS):
                hs = slice(h * HEAD_DIM, (h + 1) * HEAD_DIM)
                q = q_ref[0, rr, q0:q0 + ATT_SUB, hs]
                ks = kx[q0:q0 + nkey, hs]
                vs = vx[q0:q0 + nkey, 2 * h * HEAD_DIM:(2 * h + 2) * HEAD_DIM]
                s = lax.dot_general(q, ks, (((1,), (1,)), ((), ())),
                                    preferred_element_type=F32) * scale2 + bias
                m = jnp.max(s, axis=-1, keepdims=True)
                p = jnp.exp2(s - m)
                pv = jnp.dot(p.astype(BF16), vs, preferred_element_type=F32)
                l = pv[:, HEAD_DIM:2 * HEAD_DIM]
                o_ref[0, rr, q0:q0 + ATT_SUB, hs] = pv[:, 0:HEAD_DIM]
                st_tile = jnp.where(lane == h, m * LN_2,
                                    jnp.where(lane == HEADS + h, l, st_tile))
            st_ref[0, rr, q0:q0 + ATT_SUB, :] = st_tile


def _attention_group(qkv_g):
    B, d, L, _ = qkv_g.shape
    qblk = min(L, ATT_QBLK)
    nblk = L // qblk
    rpb = min(d, max(1, ATT_QBLK // L))
    halo_per_blk = qblk // RADIUS
    last_halo = L // RADIUS - 1

    main = lambda part: pl.BlockSpec((1, rpb, qblk, ATTN_WIDTH), lambda b, r, j: (b, r, j, part))
    prev = lambda part: pl.BlockSpec(
        (1, rpb, RADIUS, ATTN_WIDTH),
        lambda b, r, j: (b, r, jnp.maximum(j * halo_per_blk - 1, 0), part))
    nxt = lambda part: pl.BlockSpec(
        (1, rpb, RADIUS, ATTN_WIDTH),
        lambda b, r, j: (b, r, jnp.minimum((j + 1) * halo_per_blk, last_halo), part))
    return pl.pallas_call(
        functools.partial(_attn_kernel, qblk=qblk, length=L),
        out_shape=(jax.ShapeDtypeStruct((B, d, L, ATTN_WIDTH), F32),
                   jax.ShapeDtypeStruct((B, d, L, LANES), F32)),
        grid=(B, d // rpb, nblk),
        in_specs=[main(0), main(1), prev(1), nxt(1), main(2), prev(2), nxt(2)],
        out_specs=(pl.BlockSpec((1, rpb, qblk, ATTN_WIDTH), lambda b, r, j: (b, r, j, 0)),
                   pl.BlockSpec((1, rpb, qblk, LANES), lambda b, r, j: (b, r, j, 0))),
        scratch_shapes=[pltpu.VMEM((rpb, qblk + 2 * RADIUS, ATTN_WIDTH), BF16),
                        pltpu.VMEM((rpb, qblk + 2 * RADIUS, 2 * ATTN_WIDTH), BF16)],
        compiler_params=_cparams(("parallel", "parallel", "parallel")),
        name=f"attn_d{d}",
    )(qkv_g, qkv_g, qkv_g, qkv_g, qkv_g, qkv_g, qkv_g)


def _merge_kernel(x_ref, yr_ref, o0_ref, o1_ref, o2_ref, l0_ref, l1_ref, l2_ref,
                  gmix_ref, wg_ref, bg_ref, prnn_ref, pattn_ref, wout_ref,
                  gffn_ref, wr_ref, wrt_ref, br_ref, brt_ref, cc_ref,
                  x2_ref, h2_ref, afft_ref, cco_ref, yattn, stage):
    cco_ref[...] = cc_ref[...].astype(BF16)
    d = x_ref.shape[1]
    x = x_ref[...]
    hb = _rms(x, gmix_ref[...]).astype(BF16)
    branch_a = jnp.dot(yr_ref[...], prnn_ref[...], preferred_element_type=F32)
    ga = jnp.dot(hb, wg_ref[:, 0:d], preferred_element_type=F32) + bg_ref[:, 0:d]
    gb = jnp.dot(hb, wg_ref[:, d:2 * d], preferred_element_type=F32) + bg_ref[:, d:2 * d]

    def natural(ref, g, cs):
        dil = ATTN_GROUPS[g][1]
        if dil == 1:
            return ref[:, cs]
        return _interleave([ref[0, r, :, cs] for r in range(dil)], stage, dil)

    o_refs = (o0_ref, o1_ref, o2_ref)
    all_lanes = slice(0, LANES)
    stats = [natural(r, g, all_lanes) for g, r in enumerate((l0_ref, l1_ref, l2_ref))]
    head_lane = lax.broadcasted_iota(jnp.int32, stats[0].shape, 1) < HEADS
    sums = [jnp.where(head_lane, pltpu.roll(st, LANES - HEADS, 1), 1.0) for st in stats]
    lses = [st + jnp.log(l) for st, l in zip(stats, sums)]
    m = jnp.maximum(jnp.maximum(lses[0], lses[1]), lses[2])
    ws = [jnp.exp(l - m) for l in lses]
    den = ws[0] + ws[1] + ws[2]
    coef = [w / (den * l) for w, l in zip(ws, sums)]
    for h in range(HEADS):
        hs = slice(h * HEAD_DIM, (h + 1) * HEAD_DIM)
        acc = coef[0][:, h:h + 1] * natural(o_refs[0], 0, hs)
        acc = acc + coef[1][:, h:h + 1] * natural(o_refs[1], 1, hs)
        acc = acc + coef[2][:, h:h + 1] * natural(o_refs[2], 2, hs)
        yattn[:, hs] = acc.astype(BF16)

    branch_b = jnp.dot(yattn[...], pattn_ref[...], preferred_element_type=F32)
    merged = _sigmoid(ga) * branch_a + _sigmoid(gb) * branch_b
    x2 = x + jnp.dot(merged.astype(BF16), wout_ref[...], preferred_element_type=F32)
    x2_ref[...] = x2

    h2 = _rms(x2, gffn_ref[...]).astype(BF16)
    h2_ref[:, 0:d] = h2
    n_exp = afft_ref.shape[1]
    logits = jnp.dot(h2, wr_ref[...], preferred_element_type=F32) + br_ref[...]
    lane = lax.broadcasted_iota(jnp.int32, logits.shape, 1)
    mx = jnp.max(logits, axis=-1, keepdims=True)
    e = jnp.exp(logits - mx)
    den = jnp.sum(jnp.where(lane < n_exp, e, 0.0), axis=-1, keepdims=True)
    aff = e / den
    head = aff.astype(BF16).astype(F32)
    tile = jnp.where(lane < n_exp, head, jnp.where(lane < 2 * n_exp, aff - head, 0.0))
    h2_ref[:, d:d + LANES] = tile.astype(BF16)
    logits_t = lax.dot_general(wrt_ref[...], h2, (((1,), (1,)), ((), ())),
                               preferred_element_type=F32) + brt_ref[...]
    mxt = jnp.max(logits_t, axis=0, keepdims=True)
    et = jnp.exp(logits_t - mxt)
    afft_ref[0] = et / jnp.sum(et, axis=0, keepdims=True)


def _merge(x2d, y_rnn, outs, lses, gmix, wg, bg, p_rnn, p_attn, w_out, gffn, wr, br, batch,
           cast_c):
    T, D = x2d.shape
    S = T // batch
    nseq = S // ROW_TILE
    nsteps = T // ROW_TILE
    E = wr.shape[1]
    row = lambda w: pl.BlockSpec((ROW_TILE, w), lambda i: (i, 0))
    full = lambda a: pl.BlockSpec(a.shape, lambda i: (0,) * a.ndim)

    def attn_spec(g, w):
        dil = ATTN_GROUPS[g][1]
        if dil == 1:
            return row(w)
        return pl.BlockSpec((1, dil, ROW_TILE // dil, w), lambda i: (i // nseq, 0, i % nseq, 0))

    wrt = wr.T
    brt = br.reshape(E, 1)
    wr_pad = jnp.concatenate([wr, wr, jnp.zeros((D, LANES - 2 * E), wr.dtype)], axis=1)
    br_pad = jnp.concatenate([br, br, jnp.full((1, LANES - 2 * E), NEG_INF, br.dtype)], axis=1)
    consts = [gmix, wg, bg, p_rnn, p_attn, w_out, gffn, wr_pad, wrt, br_pad, brt]
    args = [x2d, y_rnn, *outs, *lses, *consts, cast_c]
    in_specs = ([row(D), row(D_RNN)] + [attn_spec(g, ATTN_WIDTH) for g in range(N_GROUPS)]
                + [attn_spec(g, LANES) for g in range(N_GROUPS)] + [full(a) for a in consts]
                + [_cast_slab_spec(cast_c, nsteps)])
    dmax = max(dil for _, dil in ATTN_GROUPS)
    return pl.pallas_call(
        _merge_kernel,
        out_shape=(jax.ShapeDtypeStruct((T, D), F32),
                   jax.ShapeDtypeStruct((T, D + LANES), BF16),
                   jax.ShapeDtypeStruct((batch, E, S), F32),
                   jax.ShapeDtypeStruct(cast_c.shape, BF16)),
        grid=(nsteps,),
        in_specs=in_specs,
        out_specs=(row(D), row(D + LANES),
                   pl.BlockSpec((1, E, ROW_TILE), lambda i: (i // nseq, 0, i % nseq)),
                   _cast_slab_spec(cast_c, nsteps)),
        scratch_shapes=[pltpu.VMEM((ROW_TILE, ATTN_WIDTH), BF16),
                        pltpu.VMEM((ROW_TILE // dmax * _pitch(dmax), LANES), F32)],
        compiler_params=_cparams(("parallel",)),
        name="merge",
    )(*args)


def _route_kernel(aff_ref, pos_ref, off_ref, *, cap):
    a = aff_ref[0]
    E, R, _ = a.shape
    bits = pltpu.bitcast(a, jnp.int32)

    def count(mask):
        c = jnp.sum(mask.astype(F32), axis=1, keepdims=True)
        return jnp.sum(c, axis=2, keepdims=True)

    def search(_, carry):
        lo, hi = carry
        q = (hi - lo + 3) >> 2
        m1 = jnp.minimum(lo + q, hi)
        m2 = jnp.minimum(m1 + q, hi)
        m3 = jnp.minimum(m2 + q, hi)
        ok1 = count(bits >= m1) >= cap
        ok2 = count(bits >= m2) >= cap
        ok3 = count(bits >= m3) >= cap
        new_lo = jnp.where(ok3, m3, jnp.where(ok2, m2, jnp.where(ok1, m1, lo)))
        new_hi = jnp.where(ok3, hi, jnp.where(ok2, m3 - 1, jnp.where(ok1, m2 - 1, m1 - 1)))
        return new_lo, new_hi

    lo0 = jnp.zeros((E, 1, 1), jnp.int32)
    hi0 = jnp.full((E, 1, 1), 0x7F7FFFFF, jnp.int32)
    thr, _ = lax.fori_loop(0, SEARCH_STEPS, search, (lo0, hi0))

    gt = bits > thr
    eq = bits == thr
    need = cap - count(gt)

    ki = lax.broadcasted_iota(jnp.int32, (LANES, LANES), 0)
    li = lax.broadcasted_iota(jnp.int32, (LANES, LANES), 1)
    upper = (ki <= li).astype(BF16)
    allones = jnp.ones((LANES, LANES), BF16)
    ri = lax.broadcasted_iota(jnp.int32, (R, R), 0)
    ci = lax.broadcasted_iota(jnp.int32, (R, R), 1)
    strict = (ci < ri).astype(BF16)

    def prefix(mask):
        mb = mask.astype(BF16).reshape(E * R, LANES)
        within = jnp.dot(mb, upper, preferred_element_type=F32)
        totals = jnp.dot(mb, allones, preferred_element_type=F32).astype(BF16)
        offs = []
        for e in range(E):
            offs.append(jnp.dot(strict, totals[e * R:(e + 1) * R, :],
                                preferred_element_type=F32))
        rowoff = jnp.concatenate(offs, axis=0)
        excl = within - mb.astype(F32) + rowoff
        return excl.reshape(E, R, LANES), rowoff.reshape(E, R, LANES)

    eq_rank, _ = prefix(eq)
    sel = gt | (eq & (eq_rank < need.astype(F32)))
    slot, rowoff = prefix(sel)
    pos_ref[0] = jnp.where(sel, slot.astype(jnp.int32), -1)
    off_ref[0] = rowoff.astype(jnp.int32)


def _route(aff_t, cap):
    B, E, S = aff_t.shape
    R = S // LANES
    a4 = aff_t.reshape(B, E, R, LANES)
    blk = pl.BlockSpec((1, E, R, LANES), lambda b: (b, 0, 0, 0))
    pos, off = pl.pallas_call(
        functools.partial(_route_kernel, cap=cap),
        out_shape=(jax.ShapeDtypeStruct((B, E, R, LANES), jnp.int32),
                   jax.ShapeDtypeStruct((B, E, R, LANES), jnp.int32)),
        grid=(B,),
        in_specs=[blk],
        out_specs=(blk, blk),
        compiler_params=_cparams(("parallel",)),
        name="route",
    )(a4)
    return pos, off


def _slot_windows(off_ref, base, n_exp, ntb, tb):
    starts = []
    rounds = jnp.int32(0)
    for e in range(n_exp):
        idx = (base + e) * (ntb + 1) + tb
        start = (off_ref[idx] >> 4) << 4
        starts.append(start)
        rounds = jnp.maximum(rounds, (off_ref[idx + 1] - start + SLOT_WIN - 1) >> SLOT_SHIFT)
    return starts, rounds


def _gather_kernel(off_ref, pos_ref, h_ref, o_ref, *, cap, ntb, n_exp, nsub):
    b = pl.program_id(0)
    grp = pl.program_id(1)
    t = pl.program_id(2)

    @pl.when(t == 0)
    def _():
        o_ref[...] = jnp.zeros_like(o_ref)

    base = (b * pl.num_programs(1) + grp) * n_exp
    ji = lax.broadcasted_iota(jnp.int32, (SLOT_WIN, TOK_BLK), 0)
    for sb in range(nsub):
        toks = slice(sb * TOK_BLK, (sb + 1) * TOK_BLK)
        starts, rounds = _slot_windows(off_ref, base, n_exp, ntb, t * nsub + sb)

        def one_round(k, carry, toks=toks, starts=starts):
            pieces, rows = [], []
            for e in range(n_exp):
                first = starts[e] + k * SLOT_WIN
                row0 = pl.multiple_of(jnp.minimum(first, cap - SLOT_WIN), BF16_ROWS)
                slot = ji + row0
                hit = (slot == pos_ref[0, 0, e:e + 1, toks]) & (slot >= first)
                pieces.append(jnp.where(hit, 1.0, 0.0).astype(BF16))
                rows.append(row0)
            onehot = jnp.concatenate(pieces, axis=0)
            g = jnp.dot(onehot, h_ref[0, toks, :], preferred_element_type=F32)
            for e in range(n_exp):
                o_ref[0, e, pl.ds(rows[e], SLOT_WIN), :] += (
                    g[e * SLOT_WIN:(e + 1) * SLOT_WIN, :].astype(BF16))
            return carry

        one_round(0, 0)
        lax.fori_loop(1, rounds, one_round, 0)


def _gather(offs, pos, h2, cap):
    B, E, S = pos.shape
    D = h2.shape[-1]
    ntb = S // TOK_BLK
    n_exp = E // EXPERT_SPLIT
    nsub = GATHER_TOK // TOK_BLK
    pos4 = pos.reshape(B, EXPERT_SPLIT, n_exp, S)
    return pl.pallas_call(
        functools.partial(_gather_kernel, cap=cap, ntb=ntb, n_exp=n_exp, nsub=nsub),
        out_shape=jax.ShapeDtypeStruct((B, E, cap, D), BF16),
        grid_spec=pltpu.PrefetchScalarGridSpec(
            num_scalar_prefetch=1,
            grid=(B, EXPERT_SPLIT, S // GATHER_TOK),
            in_specs=[
                pl.BlockSpec((1, 1, n_exp, GATHER_TOK), lambda b, g, t, off: (b, g, 0, t)),
                pl.BlockSpec((1, GATHER_TOK, D), lambda b, g, t, off: (b, t, 0)),
            ],
            out_specs=pl.BlockSpec((1, n_exp, cap, D), lambda b, g, t, off: (b, g, 0, 0)),
        ),
        compiler_params=_cparams(("parallel", "parallel", "arbitrary")),
        name="gather",
    )(offs, pos4, h2)


def _ffn_kernel(x_ref, wg_ref, wu_ref, wd_ref, o_ref, acc, *, f_chunk):
    e = pl.program_id(0)
    d = wg_ref.shape[1]
    x = x_ref[0, 0, :, 0:d]
    gl = x_ref[0, 0, :, d:d + LANES].astype(F32)
    lane = lax.broadcasted_iota(jnp.int32, gl.shape, 1)
    gate = jnp.sum(jnp.where((lane == e) | (lane == e + pl.num_programs(0)), gl, 0.0),
                   axis=1, keepdims=True)
    nf = wg_ref.shape[2] // f_chunk
    for f in range(nf):
        fs = slice(f * f_chunk, (f + 1) * f_chunk)
        g = jnp.dot(x, wg_ref[0, :, fs], preferred_element_type=F32)
        u = jnp.dot(x, wu_ref[0, :, fs], preferred_element_type=F32)
        hid = (g * _sigmoid(g) * u).astype(BF16)
        part = jnp.dot(hid, wd_ref[0, fs, :], preferred_element_type=F32)
        if f == 0:
            acc[...] = part
        else:
            acc[...] += part
    o_ref[0, 0] = (acc[...] * gate).astype(BF16)


def _ffn(xg, w_gate, w_up, w_down):
    B, E, C, DX = xg.shape
    D = w_gate.shape[1]
    F = w_gate.shape[2]
    assert F % FFN_CHUNK == 0, (F, FFN_CHUNK)
    return pl.pallas_call(
        functools.partial(_ffn_kernel, f_chunk=FFN_CHUNK),
        out_shape=jax.ShapeDtypeStruct((B, E, C, D), BF16),
        grid=(E, B),
        in_specs=[
            pl.BlockSpec((1, 1, C, DX), lambda e, b: (b, e, 0, 0)),
            pl.BlockSpec((1, D, F), lambda e, b: (e, 0, 0)),
            pl.BlockSpec((1, D, F), lambda e, b: (e, 0, 0)),
            pl.BlockSpec((1, F, D), lambda e, b: (e, 0, 0)),
        ],
        out_specs=pl.BlockSpec((1, 1, C, D), lambda e, b: (b, e, 0, 0)),
        scratch_shapes=[pltpu.VMEM((C, D), F32)],
        compiler_params=_cparams(("arbitrary", "arbitrary")),
        name="ffn",
    )(xg, w_gate, w_up, w_down)


def _combine_kernel(off_ref, x_ref, pos_ref, eo_ref, g_ref, o_ref, wstack, acc, *, cap, ntb):
    b = pl.program_id(0)
    t = pl.program_id(1)
    E = pos_ref.shape[1]
    nsub = pos_ref.shape[2] // TOK_BLK
    ji = lax.broadcasted_iota(jnp.int32, (SLOT_WIN, TOK_BLK), 0)

    for sb in range(nsub):
        toks = slice(sb * TOK_BLK, (sb + 1) * TOK_BLK)
        starts, rounds = _slot_windows(off_ref, b * E, E, ntb, t * nsub + sb)

        def one_round(k, carry, toks=toks, starts=starts, wst=wstack.at[sb], first_round=False):
            pieces = []
            for e in range(E):
                first = starts[e] + k * SLOT_WIN
                row0 = pl.multiple_of(jnp.minimum(first, cap - SLOT_WIN), BF16_ROWS)
                wst[e * SLOT_WIN:(e + 1) * SLOT_WIN, :] = eo_ref[0, e, pl.ds(row0, SLOT_WIN), :]
                slot = ji + row0
                hit = (slot == pos_ref[0, e:e + 1, toks]) & (slot >= first)
                pieces.append(jnp.where(hit, 1.0, 0.0).astype(BF16))
            onehot_t = jnp.concatenate(pieces, axis=0)
            y = lax.dot_general(onehot_t, wst[...], (((0,), (0,)), ((), ())),
                                preferred_element_type=F32)
            acc[toks, :] = y + (x_ref[0, toks, :] if first_round else acc[toks, :])
            return carry

        one_round(0, 0, first_round=True)
        lax.fori_loop(1, rounds, one_round, 0)
    o_ref[0] = _rms(acc[...], g_ref[...])


def _combine(offs, x2, pos, eo, g_final, cap):
    B, S, D = x2.shape
    E = pos.shape[1]
    ntb = S // TOK_BLK
    return pl.pallas_call(
        functools.partial(_combine_kernel, cap=cap, ntb=ntb),
        out_shape=jax.ShapeDtypeStruct((B, S, D), F32),
        grid_spec=pltpu.PrefetchScalarGridSpec(
            num_scalar_prefetch=1,
            grid=(B, S // COMBINE_TOK),
            in_specs=[
                pl.BlockSpec((1, COMBINE_TOK, D), lambda b, t, off: (b, t, 0)),
                pl.BlockSpec((1, E, COMBINE_TOK), lambda b, t, off: (b, 0, t)),
                pl.BlockSpec((1, E, cap, D), lambda b, t, off: (b, 0, 0, 0),
                             pipeline_mode=pl.Buffered(1)),
                pl.BlockSpec((1, D), lambda b, t, off: (0, 0)),
            ],
            out_specs=pl.BlockSpec((1, COMBINE_TOK, D), lambda b, t, off: (b, t, 0)),
            scratch_shapes=[pltpu.VMEM((COMBINE_TOK // TOK_BLK, E * SLOT_WIN, D), BF16),
                            pltpu.VMEM((COMBINE_TOK, D), F32)],
        ),
        compiler_params=_cparams(("parallel", "arbitrary")),
        name="combine",
    )(offs, x2, pos, eo, g_final)


def _layer(x, norm_mix, w_in, b_in, conv_w, conv_b, rg_w, rg_b, rg_lambda, p_rnn, p_attn,
           w_out, norm_ffn, w_router, b_router, w_gate, w_up, w_down, norm_out, cos_t, sin_t):
    B, S, D = x.shape
    T = B * S
    x2d = x.reshape(T, D)
    n_rnn = 2 * D_RNN
    n_qkv = 3 * N_GROUPS * ATTN_WIDTH
    gmix = norm_mix.reshape(1, D)
    w_in_b = w_in.astype(BF16)
    b_in2 = b_in.reshape(1, -1)

    xrgr = _proj_f32(x2d, gmix, w_in_b[:, :n_rnn], b_in2[:, :n_rnn])

    def group_cols(a, g):
        part = lambda p: a[:, n_rnn + (p * N_GROUPS + g) * ATTN_WIDTH:
                           n_rnn + (p * N_GROUPS + g + 1) * ATTN_WIDTH]
        return jnp.concatenate([part(0), part(1), part(2)], axis=1)

    ws = [group_cols(w_in_b, g) for g in range(N_GROUPS)]
    b_qkv = jnp.concatenate([group_cols(b_in2, g) for g in range(N_GROUPS)], axis=1)
    n_e, d_in, d_ff = w_gate.shape
    *qkv, w_gate_b, w_up_b = _proj_qkv(
        x2d, gmix, ws, b_qkv, cos_t, sin_t, B,
        w_gate.reshape(n_e * d_in, d_ff), w_up.reshape(n_e * d_in, d_ff))

    nblk = D_RNN // RNN_BW
    gate_w = (0.5 * jnp.transpose(rg_w, (2, 3, 0, 1, 4))).reshape(nblk, RNN_BW, 4 * RNN_BW).astype(BF16)
    gate_b = 0.5 * jnp.transpose(rg_b.reshape(2, 2, nblk, RNN_BW), (2, 0, 1, 3)).reshape(nblk, 1, 4 * RNN_BW)
    lam = jnp.transpose(rg_lambda.reshape(2, nblk, RNN_BW), (1, 0, 2))
    y_rnn = _rglru(xrgr.reshape(B, S, n_rnn), conv_w, conv_b.reshape(1, D_RNN), gate_w, gate_b, lam)

    outs, lses = [], []
    for g, (_, dilation) in enumerate(ATTN_GROUPS):
        qkv_g = qkv[g].reshape(B, 1, S, 3 * ATTN_WIDTH) if dilation == 1 else qkv[g]
        o, lse = _attention_group(qkv_g)
        if dilation == 1:
            o, lse = o.reshape(T, ATTN_WIDTH), lse.reshape(T, LANES)
        outs.append(o)
        lses.append(lse)

    x2, h2, aff_t, w_down_b = _merge(
        x2d, y_rnn.reshape(T, D_RNN), outs, lses, gmix,
        w_in_b[:, n_rnn + n_qkv:], b_in2[:, n_rnn + n_qkv:],
        p_rnn.astype(BF16), p_attn.astype(BF16), w_out.astype(BF16),
        norm_ffn.reshape(1, D), w_router.astype(BF16), b_router.reshape(1, -1), B,
        w_down.reshape(n_e * d_ff, d_in))

    E = w_router.shape[1]
    cap = CAPACITY_FACTOR * S // E
    pos4, off4 = _route(aff_t, cap)
    pos = pos4.reshape(B, E, S)
    blk_off = off4[:, :, ::TOK_BLK // LANES, 0]
    offs = jnp.concatenate([blk_off, jnp.full((B, E, 1), cap, jnp.int32)], axis=2).reshape(-1)
    xg = _gather(offs, pos, h2.reshape(B, S, D + LANES), cap)
    eo = _ffn(xg, w_gate_b.reshape(n_e, d_in, d_ff), w_up_b.reshape(n_e, d_in, d_ff),
              w_down_b.reshape(n_e, d_ff, d_in))
    return _combine(offs, x2.reshape(B, S, D), pos, eo, norm_out.reshape(1, D), cap)


def kernel(x, norm_mix, w_in, b_in, conv_w, conv_b, rg_w, rg_b, rg_lambda, p_rnn, p_attn, w_out,
           norm_ffn, w_router, b_router, w_gate, w_up, w_down, norm_final):
    depth = w_in.shape[0]
    assert depth == 1, "final rmsnorm is fused into the single layer's combine step"
    cos_t, sin_t = _rope_tables(x.shape[1])
    return _layer(x, norm_mix[0], w_in[0], b_in[0], conv_w[0], conv_b[0], rg_w[0], rg_b[0],
                  rg_lambda[0], p_rnn[0], p_attn[0], w_out[0], norm_ffn[0], w_router[0],
                  b_router[0], w_gate[0], w_up[0], w_down[0], norm_final, cos_t, sin_t)
```

```python
import functools

import jax
import jax.numpy as jnp
from jax import lax
from jax.experimental import pallas as pl
from jax.experimental.pallas import tpu as pltpu

F32 = jnp.float32
BF16 = jnp.bfloat16

D_RNN = 1024
RNN_BW = 128
CONV_WIDTH = 4
LRU_C = 8.0
HEAD_DIM = 128
HEADS = 4
ATTN_WIDTH = HEADS * HEAD_DIM
ATTN_GROUPS = ((128, 1), (512, 4), (2048, 16))
N_GROUPS = 3
ROPE_THETA = 500000.0
ROPE_DIM = HEAD_DIM // 4
RADIUS = 64
NEG_INF = -1e30
N_EXPERTS = 16
CAPACITY_FACTOR = 2
RMS_EPS = 1e-6

LANES = 128
SUBLANES = 8
BF16_ROWS = 16
VMEM_LIMIT = 56 * 1024 * 1024
F32_MIN_NORMAL = 1.1754944e-38
LOG2_E = 1.4426950408889634
LN_2 = 0.6931471805599453

PROJ_TILE = 1024
QKV_CHUNK = 512
ROW_TILE = 512
SCAN_SEG = 128
SCAN_CHUNK = 512
ATT_QBLK = 2048
ATT_SUB = 128
TOK_BLK = 256
GATHER_TOK = 2048
COMBINE_TOK = 512
SLOT_SHIFT = 6
SLOT_WIN = 1 << SLOT_SHIFT
EXPERT_SPLIT = 2
FFN_CHUNK = 512
SEARCH_STEPS = 17


def _cparams(sem):
    return pltpu.CompilerParams(dimension_semantics=sem, vmem_limit_bytes=VMEM_LIMIT)


def _rms(x, g):
    ms = jnp.mean(x * x, axis=-1, keepdims=True)
    return x * lax.rsqrt(ms + RMS_EPS) * g


def _pitch(d):
    return d if d <= 4 else d + 4


def _deinterleave(t, stage, d):
    n = t.shape[0]
    p = _pitch(d)
    if p == d:
        stage[0:n, :] = t
    else:
        for j in range(n // d):
            stage[j * p:j * p + d, :] = t[j * d:(j + 1) * d, :]
    return [stage[pl.ds(r, n // d, stride=p), :] for r in range(d)]


def _interleave(parts, stage, d):
    m = parts[0].shape[0]
    p = _pitch(d)
    for r in range(d):
        stage[pl.ds(r, m, stride=p), :] = parts[r]
    if p == d:
        return stage[0:m * d, :]
    return jnp.concatenate([stage[j * p:j * p + d, :] for j in range(m)], axis=0)


def _proj_f32_kernel(h_ref, w_ref, b_ref, o_ref, *, n_chunk):
    hb = h_ref[...]
    for c in range(o_ref.shape[1] // n_chunk):
        sl = slice(c * n_chunk, (c + 1) * n_chunk)
        o_ref[:, sl] = jnp.dot(hb, w_ref[:, sl], preferred_element_type=F32) + b_ref[:, sl]


def _proj_f32(h, w, b):
    T, D = h.shape
    N = w.shape[1]
    return pl.pallas_call(
        functools.partial(_proj_f32_kernel, n_chunk=512),
        out_shape=jax.ShapeDtypeStruct((T, N), F32),
        grid=(T // PROJ_TILE,),
        in_specs=[
            pl.BlockSpec((PROJ_TILE, D), lambda i: (i, 0)),
            pl.BlockSpec((D, N), lambda i: (0, 0)),
            pl.BlockSpec((1, N), lambda i: (0, 0)),
        ],
        out_specs=pl.BlockSpec((PROJ_TILE, N), lambda i: (i, 0)),
        compiler_params=_cparams(("parallel",)),
        name="proj_rnn",
    )(h, w, b)


def _proj_qkv_kernel(x_ref, g_ref, w0_ref, w1_ref, w2_ref, b_ref, cos_ref, sin_ref,
                     ca_ref, h_ref, o0_ref, o1_ref, o2_ref, cao_ref, stage):
    cao_ref[...] = ca_ref[...].astype(BF16)
    hb =_rms(x_ref[...], g_ref[...]).astype(BF16)
    h_ref[...] = hb
    cosv = cos_ref[...]
    sinv = sin_ref[...]
    lane = lax.broadcasted_iota(jnp.int32, (1, HEAD_DIM), 1)
    first_half = lane < (ROPE_DIM // 2)
    gw = 3 * ATTN_WIDTH
    for g, (w_ref, o_ref) in enumerate(((w0_ref, o0_ref), (w1_ref, o1_ref), (w2_ref, o2_ref))):
        d = ATTN_GROUPS[g][1]
        for c in range(gw // QKV_CHUNK):
            sl = slice(c * QKV_CHUNK, (c + 1) * QKV_CHUNK)
            t2 = jnp.dot(hb, w_ref[:, sl], preferred_element_type=F32)
            t2 = t2 + b_ref[:, g * gw + c * QKV_CHUNK:g * gw + (c + 1) * QKV_CHUNK]
            for hh in range(QKV_CHUNK // HEAD_DIM):
                t = t2[:, hh * HEAD_DIM:(hh + 1) * HEAD_DIM]
                col = c * QKV_CHUNK + hh * HEAD_DIM
                if col < 2 * ATTN_WIDTH:
                    up = pltpu.roll(t, HEAD_DIM - ROPE_DIM // 2, 1)
                    down = pltpu.roll(t, ROPE_DIM // 2, 1)
                    t = t * cosv + jnp.where(first_half, up, down) * sinv
                cs = slice(col, col + HEAD_DIM)
                if d == 1:
                    o_ref[:, cs] = t.astype(BF16)
                else:
                    for r, part in enumerate(_deinterleave(t, stage, d)):
                        o_ref[0, r, :, cs] = part.astype(BF16)


def _cast_slab_spec(a, nsteps):
    rows, cols = a.shape
    assert rows % (nsteps * BF16_ROWS) == 0, (a.shape, nsteps)
    return pl.BlockSpec((rows // nsteps, cols), lambda i: (i, 0))


def _proj_qkv(x2d, g, ws, b, cos_t, sin_t, batch, cast_a):
    T, D = x2d.shape
    S = T // batch
    nseq = S // PROJ_TILE
    nsteps = T // PROJ_TILE
    gw = 3 * ATTN_WIDTH
    d1, d2 = ATTN_GROUPS[1][1], ATTN_GROUPS[2][1]
    full = lambda a: pl.BlockSpec(a.shape, lambda i: (0,) * a.ndim)
    dil_spec = lambda d: pl.BlockSpec((1, d, PROJ_TILE // d, gw),
                                      lambda i: (i // nseq, 0, i % nseq, 0))
    return pl.pallas_call(
        _proj_qkv_kernel,
        out_shape=(jax.ShapeDtypeStruct((T, D), BF16),
                   jax.ShapeDtypeStruct((T, gw), BF16),
                   jax.ShapeDtypeStruct((batch, d1, S // d1, gw), BF16),
                   jax.ShapeDtypeStruct((batch, d2, S // d2, gw), BF16),
                   jax.ShapeDtypeStruct(cast_a.shape, BF16)),
        grid=(nsteps,),
        in_specs=[
            pl.BlockSpec((PROJ_TILE, D), lambda i: (i, 0)),
            full(g), full(ws[0]), full(ws[1]), full(ws[2]), full(b),
            pl.BlockSpec((PROJ_TILE, HEAD_DIM), lambda i: (i % nseq, 0)),
            pl.BlockSpec((PROJ_TILE, HEAD_DIM), lambda i: (i % nseq, 0)),
            _cast_slab_spec(cast_a, nsteps),
        ],
        out_specs=(pl.BlockSpec((PROJ_TILE, D), lambda i: (i, 0)),
                   pl.BlockSpec((PROJ_TILE, gw), lambda i: (i, 0)), dil_spec(d1), dil_spec(d2),
                   _cast_slab_spec(cast_a, nsteps)),
        scratch_shapes=[pltpu.VMEM((PROJ_TILE // d2 * _pitch(d2), LANES), F32)],
        compiler_params=_cparams(("parallel",)),
        name="proj_qkv",
    )(x2d, g, ws[0], ws[1], ws[2], b, cos_t, sin_t, cast_a)


def _rope_tables(seq):
    pos = jnp.arange(seq, dtype=F32)
    inv = ROPE_THETA ** (-jnp.arange(0, ROPE_DIM, 2, dtype=F32) / ROPE_DIM)
    ang = pos[:, None] * inv[None, :]
    cos, sin = jnp.cos(ang), jnp.sin(ang)
    pad = HEAD_DIM - ROPE_DIM
    cos_t = jnp.concatenate([cos, cos, jnp.ones((seq, pad), F32)], axis=1)
    sin_t = jnp.concatenate([-sin, sin, jnp.zeros((seq, pad), F32)], axis=1)
    return cos_t, sin_t


def _sigmoid(x):
    return 0.5 * jnp.tanh(0.5 * x) + 0.5


def _rglru_kernel(xr_ref, gr_ref, cw_ref, cb_ref, gw_ref, gb_ref, lam_ref, o_ref,
                  xpad, af, uf, ab, ub, cf, cbk, *, seq):
    nseg = seq // SCAN_SEG
    nchunk = seq // SCAN_CHUNK
    seg_per_chunk = SCAN_CHUNK // SCAN_SEG
    pitch = nseg + 4

    cw = cw_ref[...]
    cbias = cb_ref[...]
    gbias = gb_ref[...]
    z = -lam_ref[...]
    sp = jnp.maximum(z, 0.0) + jnp.log1p(jnp.exp(-jnp.abs(z)))
    half_neg_c_sp = (-0.5 * LRU_C) * sp

    zero_rows = jnp.zeros((SUBLANES, LANES), F32)
    xpad[0:SUBLANES, :] = zero_rows
    xpad[seq + SUBLANES:seq + 2 * SUBLANES, :] = zero_rows

    def copy_in(c, carry):
        t0 = pl.multiple_of(c * SCAN_CHUNK, SCAN_CHUNK)
        xpad[pl.ds(t0 + SUBLANES, SCAN_CHUNK), :] = xr_ref[0, pl.ds(t0, SCAN_CHUNK), :]
        return carry

    lax.fori_loop(0, nchunk, copy_in, 0)

    def stage1(c, carry):
        t0 = c * SCAN_CHUNK
        xc = cbias
        for tap in range(CONV_WIDTH):
            lo = t0 + SUBLANES - CONV_WIDTH // 2 + tap
            xc = xc + cw[tap:tap + 1, :] * xpad[pl.ds(lo, SCAN_CHUNK), :]
        th = jnp.tanh(jnp.dot(xc.astype(BF16), gw_ref[...], preferred_element_type=F32) + gbias)
        xc_half = 0.5 * xc
        for d, (a_s, u_s) in enumerate(((af, uf), (ab, ub))):
            t_r = th[:, (2 * d) * LANES:(2 * d + 1) * LANES]
            t_i = th[:, (2 * d + 1) * LANES:(2 * d + 2) * LANES]
            half_k = half_neg_c_sp[d:d + 1, :]
            log_a = t_r * half_k + half_k
            a = jnp.exp(log_a)
            v = jnp.tanh(log_a) * (-1.0 - a * a)
            u = (xc_half * (v * lax.rsqrt(jnp.maximum(v, F32_MIN_NORMAL)))) * (t_i + 1.0)
            for k in range(seg_per_chunk):
                seg = c * seg_per_chunk + k
                a_s[pl.ds(seg, SCAN_SEG, stride=pitch), :] = a[k * SCAN_SEG:(k + 1) * SCAN_SEG, :]
                u_s[pl.ds(seg, SCAN_SEG, stride=pitch), :] = u[k * SCAN_SEG:(k + 1) * SCAN_SEG, :]
        return carry

    lax.fori_loop(0, nchunk, stage1, 0, unroll=4)

    def scan_dir(a_s, u_s, reverse):
        def body(it, carry):
            i = (SCAN_SEG - 1 - it) if reverse else it
            h, cum = carry
            rows = pl.ds(i * pitch, nseg)
            a = a_s[rows, :]
            u = u_s[rows, :]
            h = a * h + u
            cum = a * cum
            u_s[rows, :] = h
            a_s[rows, :] = cum
            return h, cum

        return lax.fori_loop(0, SCAN_SEG, body,
                             (jnp.zeros((nseg, LANES), F32), jnp.ones((nseg, LANES), F32)))

    end_h, end_a = scan_dir(af, uf, False)
    beg_h, beg_a = scan_dir(ab, ub, True)

    c = jnp.zeros((1, LANES), F32)
    for s in range(nseg):
        cf[s:s + 1, :] = c
        c = end_h[s:s + 1, :] + end_a[s:s + 1, :] * c
    c = jnp.zeros((1, LANES), F32)
    for s in range(nseg - 1, -1, -1):
        cbk[s:s + 1, :] = c
        c = beg_h[s:s + 1, :] + beg_a[s:s + 1, :] * c

    def finish(s, carry):
        t0 = pl.multiple_of(s * SCAN_SEG, SCAN_SEG)
        rows = pl.ds(s, SCAN_SEG, stride=pitch)
        hf = uf[rows, :] + af[rows, :] * cf[pl.ds(s, 1), :]
        hb = ub[rows, :] + ab[rows, :] * cbk[pl.ds(s, 1), :]
        g = jax.nn.gelu(gr_ref[0, pl.ds(t0, SCAN_SEG), :])
        o_ref[0, pl.ds(t0, SCAN_SEG), :] = (g * (hf + hb)).astype(BF16)
        return carry

    lax.fori_loop(0, nseg, finish, 0)


def _rglru(xrgr, conv_w, conv_b, gate_w, gate_b, lam):
    B, S, _ = xrgr.shape
    nblk = D_RNN // RNN_BW
    nseg = S // SCAN_SEG
    scr = pltpu.VMEM((SCAN_SEG * (nseg + 4), LANES), F32)
    return pl.pallas_call(
        functools.partial(_rglru_kernel, seq=S),
        out_shape=jax.ShapeDtypeStruct((B, S, D_RNN), BF16),
        grid=(B, nblk),
        in_specs=[
            pl.BlockSpec((1, S, RNN_BW), lambda b, c: (b, 0, c)),
            pl.BlockSpec((1, S, RNN_BW), lambda b, c: (b, 0, nblk + c)),
            pl.BlockSpec((CONV_WIDTH, RNN_BW), lambda b, c: (0, c)),
            pl.BlockSpec((1, RNN_BW), lambda b, c: (0, c)),
            pl.BlockSpec((None, RNN_BW, 4 * RNN_BW), lambda b, c: (c, 0, 0)),
            pl.BlockSpec((None, 1, 4 * RNN_BW), lambda b, c: (c, 0, 0)),
            pl.BlockSpec((None, 2, RNN_BW), lambda b, c: (c, 0, 0)),
        ],
        out_specs=pl.BlockSpec((1, S, RNN_BW), lambda b, c: (b, 0, c)),
        scratch_shapes=[pltpu.VMEM((S + 2 * SUBLANES, LANES), F32), scr, scr, scr, scr,
                        pltpu.VMEM((nseg, LANES), F32), pltpu.VMEM((nseg, LANES), F32)],
        compiler_params=_cparams(("parallel", "parallel")),
        name="rglru",
    )(xrgr, xrgr, conv_w, conv_b, gate_w, gate_b, lam)


def _attn_kernel(q_ref, k_ref, kp_ref, kn_ref, v_ref, vp_ref, vn_ref, o_ref, st_ref,
                 kext, vext, *, qblk, length):
    j = pl.program_id(2)
    nkey = ATT_SUB + 2 * RADIUS
    qi = lax.broadcasted_iota(jnp.int32, (ATT_SUB, nkey), 0)
    ki = lax.broadcasted_iota(jnp.int32, (ATT_SUB, nkey), 1)
    band_bias = jnp.where(jnp.abs(ki - RADIUS - qi) <= RADIUS, 0.0, NEG_INF)
    krow = lax.broadcasted_iota(jnp.int32, (1, nkey), 1)
    lane = lax.broadcasted_iota(jnp.int32, (ATT_SUB, LANES), 1)
    scale2 = HEAD_DIM ** -0.5 * LOG2_E

    for rr in range(q_ref.shape[1]):
        kx, vx = kext.at[rr], vext.at[rr]
        kx[0:RADIUS, :] = kp_ref[0, rr]
        kx[RADIUS:RADIUS + qblk, :] = k_ref[0, rr]
        kx[RADIUS + qblk:RADIUS + qblk + RADIUS, :] = kn_ref[0, rr]
        for h in range(HEADS):
            src = slice(h * HEAD_DIM, (h + 1) * HEAD_DIM)
            dst = slice(2 * h * HEAD_DIM, (2 * h + 1) * HEAD_DIM)
            vx[0:RADIUS, dst] = vp_ref[0, rr, :, src]
            vx[RADIUS:RADIUS + qblk, dst] = v_ref[0, rr, :, src]
            vx[RADIUS + qblk:RADIUS + qblk + RADIUS, dst] = vn_ref[0, rr, :, src]
            vx[:, (2 * h + 1) * HEAD_DIM:(2 * h + 2) * HEAD_DIM] = jnp.ones(
                (qblk + 2 * RADIUS, HEAD_DIM), BF16)

        for i in range(qblk // ATT_SUB):
            q0 = i * ATT_SUB
            kglob = j * qblk + q0 - RADIUS + krow
            edge_bias = jnp.where((kglob >= 0) & (kglob < length), 0.0, NEG_INF)
            bias = band_bias + edge_bias
            st_tile = jnp.ones((ATT_SUB, LANES), F32)
            for h in range(HEADS):
                hs = slice(h * HEAD_DIM, (h + 1) * HEAD_DIM)
                q = q_ref[0, rr, q0:q0 + ATT_SUB, hs]
                ks = kx[q0:q0 + nkey, hs]
                vs = vx[q0:q0 + nkey, 2 * h * HEAD_DIM:(2 * h + 2) * HEAD_DIM]
                s = lax.dot_general(q, ks, (((1,), (1,)), ((), ())),
                                    preferred_element_type=F32) * scale2 + bias
                m = jnp.max(s, axis=-1, keepdims=True)
                p = jnp.exp2(s - m)
                pv = jnp.dot(p.astype(BF16), vs, preferred_element_type=F32)
                l = pv[:, HEAD_DIM:2 * HEAD_DIM]
                o_ref[0, rr, q0:q0 + ATT_SUB, hs] = pv[:, 0:HEAD_DIM]
                st_tile = jnp.where(lane == h, m * LN_2,
                                    jnp.where(lane == HEADS + h, l, st_tile))
            st_ref[0, rr, q0:q0 + ATT_SUB, :] = st_tile


def _attention_group(qkv_g):
    B, d, L, _ = qkv_g.shape
    qblk = min(L, ATT_QBLK)
    nblk = L // qblk
    rpb = min(d, max(1, ATT_QBLK // L))
    halo_per_blk = qblk // RADIUS
    last_halo = L // RADIUS - 1

    main = lambda part: pl.BlockSpec((1, rpb, qblk, ATTN_WIDTH), lambda b, r, j: (b, r, j, part))
    prev = lambda part: pl.BlockSpec(
        (1, rpb, RADIUS, ATTN_WIDTH),
        lambda b, r, j: (b, r, jnp.maximum(j * halo_per_blk - 1, 0), part))
    nxt = lambda part: pl.BlockSpec(
        (1, rpb, RADIUS, ATTN_WIDTH),
        lambda b, r, j: (b, r, jnp.minimum((j + 1) * halo_per_blk, last_halo), part))
    return pl.pallas_call(
        functools.partial(_attn_kernel, qblk=qblk, length=L),
        out_shape=(jax.ShapeDtypeStruct((B, d, L, ATTN_WIDTH), F32),
                   jax.ShapeDtypeStruct((B, d, L, LANES), F32)),
        grid=(B, d // rpb, nblk),
        in_specs=[main(0), main(1), prev(1), nxt(1), main(2), prev(2), nxt(2)],
        out_specs=(pl.BlockSpec((1, rpb, qblk, ATTN_WIDTH), lambda b, r, j: (b, r, j, 0)),
                   pl.BlockSpec((1, rpb, qblk, LANES), lambda b, r, j: (b, r, j, 0))),
        scratch_shapes=[pltpu.VMEM((rpb, qblk + 2 * RADIUS, ATTN_WIDTH), BF16),
                        pltpu.VMEM((rpb, qblk + 2 * RADIUS, 2 * ATTN_WIDTH), BF16)],
        compiler_params=_cparams(("parallel", "parallel", "parallel")),
        name=f"attn_d{d}",
    )(qkv_g, qkv_g, qkv_g, qkv_g, qkv_g, qkv_g, qkv_g)


def _merge_kernel(x_ref, yr_ref, o0_ref, o1_ref, o2_ref, l0_ref, l1_ref, l2_ref,
                  gmix_ref, wg_ref, bg_ref, prnn_ref, pattn_ref, wout_ref,
                  gffn_ref, wr_ref, wrt_ref, br_ref, brt_ref, cb_ref, cc_ref,
                  x2_ref, h2_ref, afft_ref, cbo_ref, cco_ref, yattn, stage):
    cbo_ref[...] = cb_ref[...].astype(BF16)
    cco_ref[...] = cc_ref[...].astype(BF16)
    d = x_ref.shape[1]
    x = x_ref[...]
    hb = _rms(x, gmix_ref[...]).astype(BF16)
    branch_a = jnp.dot(yr_ref[...], prnn_ref[...], preferred_element_type=F32)
    ga = jnp.dot(hb, wg_ref[:, 0:d], preferred_element_type=F32) + bg_ref[:, 0:d]
    gb = jnp.dot(hb, wg_ref[:, d:2 * d], preferred_element_type=F32) + bg_ref[:, d:2 * d]

    def natural(ref, g, cs):
        dil = ATTN_GROUPS[g][1]
        if dil == 1:
            return ref[:, cs]
        return _interleave([ref[0, r, :, cs] for r in range(dil)], stage, dil)

    o_refs = (o0_ref, o1_ref, o2_ref)
    all_lanes = slice(0, LANES)
    stats = [natural(r, g, all_lanes) for g, r in enumerate((l0_ref, l1_ref, l2_ref))]
    head_lane = lax.broadcasted_iota(jnp.int32, stats[0].shape, 1) < HEADS
    sums = [jnp.where(head_lane, pltpu.roll(st, LANES - HEADS, 1), 1.0) for st in stats]
    lses = [st + jnp.log(l) for st, l in zip(stats, sums)]
    m = jnp.maximum(jnp.maximum(lses[0], lses[1]), lses[2])
    ws = [jnp.exp(l - m) for l in lses]
    den = ws[0] + ws[1] + ws[2]
    coef = [w / (den * l) for w, l in zip(ws, sums)]
    for h in range(HEADS):
        hs = slice(h * HEAD_DIM, (h + 1) * HEAD_DIM)
        acc = coef[0][:, h:h + 1] * natural(o_refs[0], 0, hs)
        acc = acc + coef[1][:, h:h + 1] * natural(o_refs[1], 1, hs)
        acc = acc + coef[2][:, h:h + 1] * natural(o_refs[2], 2, hs)
        yattn[:, hs] = acc.astype(BF16)

    branch_b = jnp.dot(yattn[...], pattn_ref[...], preferred_element_type=F32)
    merged = _sigmoid(ga) * branch_a + _sigmoid(gb) * branch_b
    x2 = x + jnp.dot(merged.astype(BF16), wout_ref[...], preferred_element_type=F32)
    x2_ref[...] = x2

    h2 = _rms(x2, gffn_ref[...]).astype(BF16)
    h2_ref[:, 0:d] = h2
    n_exp = afft_ref.shape[1]
    logits = jnp.dot(h2, wr_ref[...], preferred_element_type=F32) + br_ref[...]
    lane = lax.broadcasted_iota(jnp.int32, logits.shape, 1)
    mx = jnp.max(logits, axis=-1, keepdims=True)
    e = jnp.exp(logits - mx)
    den = jnp.sum(jnp.where(lane < n_exp, e, 0.0), axis=-1, keepdims=True)
    aff = e / den
    head = aff.astype(BF16).astype(F32)
    tile = jnp.where(lane < n_exp, head, jnp.where(lane < 2 * n_exp, aff - head, 0.0))
    h2_ref[:, d:d + LANES] = tile.astype(BF16)
    logits_t = lax.dot_general(wrt_ref[...], h2, (((1,), (1,)), ((), ())),
                               preferred_element_type=F32) + brt_ref[...]
    mxt = jnp.max(logits_t, axis=0, keepdims=True)
    et = jnp.exp(logits_t - mxt)
    afft_ref[0] = et / jnp.sum(et, axis=0, keepdims=True)


def _merge(x2d, y_rnn, outs, lses, gmix, wg, bg, p_rnn, p_attn, w_out, gffn, wr, br, batch,
           cast_b, cast_c):
    T, D = x2d.shape
    S = T // batch
    nseq = S // ROW_TILE
    nsteps = T // ROW_TILE
    E = wr.shape[1]
    row = lambda w: pl.BlockSpec((ROW_TILE, w), lambda i: (i, 0))
    full = lambda a: pl.BlockSpec(a.shape, lambda i: (0,) * a.ndim)

    def attn_spec(g, w):
        dil = ATTN_GROUPS[g][1]
        if dil == 1:
            return row(w)
        return pl.BlockSpec((1, dil, ROW_TILE // dil, w), lambda i: (i // nseq, 0, i % nseq, 0))

    wrt = wr.T
    brt = br.reshape(E, 1)
    wr_pad = jnp.concatenate([wr, wr, jnp.zeros((D, LANES - 2 * E), wr.dtype)], axis=1)
    br_pad = jnp.concatenate([br, br, jnp.full((1, LANES - 2 * E), NEG_INF, br.dtype)], axis=1)
    consts = [gmix, wg, bg, p_rnn, p_attn, w_out, gffn, wr_pad, wrt, br_pad, brt]
    args = [x2d, y_rnn, *outs, *lses, *consts, cast_b, cast_c]
    in_specs = ([row(D), row(D_RNN)] + [attn_spec(g, ATTN_WIDTH) for g in range(N_GROUPS)]
                + [attn_spec(g, LANES) for g in range(N_GROUPS)] + [full(a) for a in consts]
                + [_cast_slab_spec(cast_b, nsteps), _cast_slab_spec(cast_c, nsteps)])
    dmax = max(dil for _, dil in ATTN_GROUPS)
    return pl.pallas_call(
        _merge_kernel,
        out_shape=(jax.ShapeDtypeStruct((T, D), F32),
                   jax.ShapeDtypeStruct((T, D + LANES), BF16),
                   jax.ShapeDtypeStruct((batch, E, S), F32),
                   jax.ShapeDtypeStruct(cast_b.shape, BF16),
                   jax.ShapeDtypeStruct(cast_c.shape, BF16)),
        grid=(nsteps,),
        in_specs=in_specs,
        out_specs=(row(D), row(D + LANES),
                   pl.BlockSpec((1, E, ROW_TILE), lambda i: (i // nseq, 0, i % nseq)),
                   _cast_slab_spec(cast_b, nsteps), _cast_slab_spec(cast_c, nsteps)),
        scratch_shapes=[pltpu.VMEM((ROW_TILE, ATTN_WIDTH), BF16),
                        pltpu.VMEM((ROW_TILE // dmax * _pitch(dmax), LANES), F32)],
        compiler_params=_cparams(("parallel",)),
        name="merge",
    )(*args)


def _route_kernel(aff_ref, pos_ref, off_ref, *, cap):
    a = aff_ref[0]
    E, R, _ = a.shape
    bits = pltpu.bitcast(a, jnp.int32)

    def count(mask):
        c = jnp.sum(mask.astype(F32), axis=1, keepdims=True)
        return jnp.sum(c, axis=2, keepdims=True)

    def search(_, carry):
        lo, hi = carry
        q = (hi - lo + 3) >> 2
        m1 = jnp.minimum(lo + q, hi)
        m2 = jnp.minimum(m1 + q, hi)
        m3 = jnp.minimum(m2 + q, hi)
        ok1 = count(bits >= m1) >= cap
        ok2 = count(bits >= m2) >= cap
        ok3 = count(bits >= m3) >= cap
        new_lo = jnp.where(ok3, m3, jnp.where(ok2, m2, jnp.where(ok1, m1, lo)))
        new_hi = jnp.where(ok3, hi, jnp.where(ok2, m3 - 1, jnp.where(ok1, m2 - 1, m1 - 1)))
        return new_lo, new_hi

    lo0 = jnp.zeros((E, 1, 1), jnp.int32)
    hi0 = jnp.full((E, 1, 1), 0x7F7FFFFF, jnp.int32)
    thr, _ = lax.fori_loop(0, SEARCH_STEPS, search, (lo0, hi0))

    gt = bits > thr
    eq = bits == thr
    need = cap - count(gt)

    ki = lax.broadcasted_iota(jnp.int32, (LANES, LANES), 0)
    li = lax.broadcasted_iota(jnp.int32, (LANES, LANES), 1)
    upper = (ki <= li).astype(BF16)
    allones = jnp.ones((LANES, LANES), BF16)
    ri = lax.broadcasted_iota(jnp.int32, (R, R), 0)
    ci = lax.broadcasted_iota(jnp.int32, (R, R), 1)
    strict = (ci < ri).astype(BF16)

    def prefix(mask):
        mb = mask.astype(BF16).reshape(E * R, LANES)
        within = jnp.dot(mb, upper, preferred_element_type=F32)
        totals = jnp.dot(mb, allones, preferred_element_type=F32).astype(BF16)
        offs = []
        for e in range(E):
            offs.append(jnp.dot(strict, totals[e * R:(e + 1) * R, :],
                                preferred_element_type=F32))
        rowoff = jnp.concatenate(offs, axis=0)
        excl = within - mb.astype(F32) + rowoff
        return excl.reshape(E, R, LANES), rowoff.reshape(E, R, LANES)

    eq_rank, _ = prefix(eq)
    sel = gt | (eq & (eq_rank < need.astype(F32)))
    slot, rowoff = prefix(sel)
    pos_ref[0] = jnp.where(sel, slot.astype(jnp.int32), -1)
    off_ref[0] = rowoff.astype(jnp.int32)


def _route(aff_t, cap):
    B, E, S = aff_t.shape
    R = S // LANES
    a4 = aff_t.reshape(B, E, R, LANES)
    blk = pl.BlockSpec((1, E, R, LANES), lambda b: (b, 0, 0, 0))
    pos, off = pl.pallas_call(
        functools.partial(_route_kernel, cap=cap),
        out_shape=(jax.ShapeDtypeStruct((B, E, R, LANES), jnp.int32),
                   jax.ShapeDtypeStruct((B, E, R, LANES), jnp.int32)),
        grid=(B,),
        in_specs=[blk],
        out_specs=(blk, blk),
        compiler_params=_cparams(("parallel",)),
        name="route",
    )(a4)
    return pos, off


def _slot_windows(off_ref, base, n_exp, ntb, tb):
    starts = []
    rounds = jnp.int32(0)
    for e in range(n_exp):
        idx = (base + e) * (ntb + 1) + tb
        start = (off_ref[idx] >> 4) << 4
        starts.append(start)
        rounds = jnp.maximum(rounds, (off_ref[idx + 1] - start + SLOT_WIN - 1) >> SLOT_SHIFT)
    return starts, rounds


def _gather_kernel(off_ref, pos_ref, h_ref, o_ref, *, cap, ntb, n_exp, nsub):
    b = pl.program_id(0)
    grp = pl.program_id(1)
    t = pl.program_id(2)

    @pl.when(t == 0)
    def _():
        o_ref[...] = jnp.zeros_like(o_ref)

    base = (b * pl.num_programs(1) + grp) * n_exp
    ji = lax.broadcasted_iota(jnp.int32, (SLOT_WIN, TOK_BLK), 0)
    for sb in range(nsub):
        toks = slice(sb * TOK_BLK, (sb + 1) * TOK_BLK)
        starts, rounds = _slot_windows(off_ref, base, n_exp, ntb, t * nsub + sb)

        def one_round(k, carry, toks=toks, starts=starts):
            pieces, rows = [], []
            for e in range(n_exp):
                first = starts[e] + k * SLOT_WIN
                row0 = pl.multiple_of(jnp.minimum(first, cap - SLOT_WIN), BF16_ROWS)
                slot = ji + row0
                hit = (slot == pos_ref[0, 0, e:e + 1, toks]) & (slot >= first)
                pieces.append(jnp.where(hit, 1.0, 0.0).astype(BF16))
                rows.append(row0)
            onehot = jnp.concatenate(pieces, axis=0)
            g = jnp.dot(onehot, h_ref[0, toks, :], preferred_element_type=F32)
            for e in range(n_exp):
                o_ref[0, e, pl.ds(rows[e], SLOT_WIN), :] += (
                    g[e * SLOT_WIN:(e + 1) * SLOT_WIN, :].astype(BF16))
            return carry

        one_round(0, 0)
        lax.fori_loop(1, rounds, one_round, 0)


def _gather(offs, pos, h2, cap):
    B, E, S = pos.shape
    D = h2.shape[-1]
    ntb = S // TOK_BLK
    n_exp = E // EXPERT_SPLIT
    nsub = GATHER_TOK // TOK_BLK
    pos4 = pos.reshape(B, EXPERT_SPLIT, n_exp, S)
    return pl.pallas_call(
        functools.partial(_gather_kernel, cap=cap, ntb=ntb, n_exp=n_exp, nsub=nsub),
        out_shape=jax.ShapeDtypeStruct((B, E, cap, D), BF16),
        grid_spec=pltpu.PrefetchScalarGridSpec(
            num_scalar_prefetch=1,
            grid=(B, EXPERT_SPLIT, S // GATHER_TOK),
            in_specs=[
                pl.BlockSpec((1, 1, n_exp, GATHER_TOK), lambda b, g, t, off: (b, g, 0, t)),
                pl.BlockSpec((1, GATHER_TOK, D), lambda b, g, t, off: (b, t, 0)),
            ],
            out_specs=pl.BlockSpec((1, n_exp, cap, D), lambda b, g, t, off: (b, g, 0, 0)),
        ),
        compiler_params=_cparams(("parallel", "parallel", "arbitrary")),
        name="gather",
    )(offs, pos4, h2)


def _ffn_kernel(x_ref, wg_ref, wu_ref, wd_ref, o_ref, acc, *, f_chunk):
    e = pl.program_id(0)
    d = wg_ref.shape[1]
    x = x_ref[0, 0, :, 0:d]
    gl = x_ref[0, 0, :, d:d + LANES].astype(F32)
    lane = lax.broadcasted_iota(jnp.int32, gl.shape, 1)
    gate = jnp.sum(jnp.where((lane == e) | (lane == e + pl.num_programs(0)), gl, 0.0),
                   axis=1, keepdims=True)
    nf = wg_ref.shape[2] // f_chunk
    for f in range(nf):
        fs = slice(f * f_chunk, (f + 1) * f_chunk)
        g = jnp.dot(x, wg_ref[0, :, fs], preferred_element_type=F32)
        u = jnp.dot(x, wu_ref[0, :, fs], preferred_element_type=F32)
        hid = (g * _sigmoid(g) * u).astype(BF16)
        part = jnp.dot(hid, wd_ref[0, fs, :], preferred_element_type=F32)
        if f == 0:
            acc[...] = part
        else:
            acc[...] += part
    o_ref[0, 0] = (acc[...] * gate).astype(BF16)


def _ffn(xg, w_gate, w_up, w_down):
    B, E, C, DX = xg.shape
    D = w_gate.shape[1]
    F = w_gate.shape[2]
    assert F % FFN_CHUNK == 0, (F, FFN_CHUNK)
    return pl.pallas_call(
        functools.partial(_ffn_kernel, f_chunk=FFN_CHUNK),
        out_shape=jax.ShapeDtypeStruct((B, E, C, D), BF16),
        grid=(E, B),
        in_specs=[
            pl.BlockSpec((1, 1, C, DX), lambda e, b: (b, e, 0, 0)),
            pl.BlockSpec((1, D, F), lambda e, b: (e, 0, 0)),
            pl.BlockSpec((1, D, F), lambda e, b: (e, 0, 0)),
            pl.BlockSpec((1, F, D), lambda e, b: (e, 0, 0)),
        ],
        out_specs=pl.BlockSpec((1, 1, C, D), lambda e, b: (b, e, 0, 0)),
        scratch_shapes=[pltpu.VMEM((C, D), F32)],
        compiler_params=_cparams(("arbitrary", "arbitrary")),
        name="ffn",
    )(xg, w_gate, w_up, w_down)


def _combine_kernel(off_ref, x_ref, pos_ref, eo_ref, g_ref, o_ref, wstack, acc, *, cap, ntb):
    b = pl.program_id(0)
    t = pl.program_id(1)
    E = pos_ref.shape[1]
    nsub = pos_ref.shape[2] // TOK_BLK
    ji = lax.broadcasted_iota(jnp.int32, (SLOT_WIN, TOK_BLK), 0)

    for sb in range(nsub):
        toks = slice(sb * TOK_BLK, (sb + 1) * TOK_BLK)
        starts, rounds = _slot_windows(off_ref, b * E, E, ntb, t * nsub + sb)

        def one_round(k, carry, toks=toks, starts=starts, wst=wstack.at[sb], first_round=False):
            pieces = []
            for e in range(E):
                first = starts[e] + k * SLOT_WIN
                row0 = pl.multiple_of(jnp.minimum(first, cap - SLOT_WIN), BF16_ROWS)
                wst[e * SLOT_WIN:(e + 1) * SLOT_WIN, :] = eo_ref[0, e, pl.ds(row0, SLOT_WIN), :]
                slot = ji + row0
                hit = (slot == pos_ref[0, e:e + 1, toks]) & (slot >= first)
                pieces.append(jnp.where(hit, 1.0, 0.0).astype(BF16))
            onehot_t = jnp.concatenate(pieces, axis=0)
            y = lax.dot_general(onehot_t, wst[...], (((0,), (0,)), ((), ())),
                                preferred_element_type=F32)
            acc[toks, :] = y + (x_ref[0, toks, :] if first_round else acc[toks, :])
            return carry

        one_round(0, 0, first_round=True)
        lax.fori_loop(1, rounds, one_round, 0)
    o_ref[0] = _rms(acc[...], g_ref[...])


def _combine(offs, x2, pos, eo, g_final, cap):
    B, S, D = x2.shape
    E = pos.shape[1]
    ntb = S // TOK_BLK
    return pl.pallas_call(
        functools.partial(_combine_kernel, cap=cap, ntb=ntb),
        out_shape=jax.ShapeDtypeStruct((B, S, D), F32),
        grid_spec=pltpu.PrefetchScalarGridSpec(
            num_scalar_prefetch=1,
            grid=(B, S // COMBINE_TOK),
            in_specs=[
                pl.BlockSpec((1, COMBINE_TOK, D), lambda b, t, off: (b, t, 0)),
                pl.BlockSpec((1, E, COMBINE_TOK), lambda b, t, off: (b, 0, t)),
                pl.BlockSpec((1, E, cap, D), lambda b, t, off: (b, 0, 0, 0),
                             pipeline_mode=pl.Buffered(1)),
                pl.BlockSpec((1, D), lambda b, t, off: (0, 0)),
            ],
            out_specs=pl.BlockSpec((1, COMBINE_TOK, D), lambda b, t, off: (b, t, 0)),
            scratch_shapes=[pltpu.VMEM((COMBINE_TOK // TOK_BLK, E * SLOT_WIN, D), BF16),
                            pltpu.VMEM((COMBINE_TOK, D), F32)],
        ),
        compiler_params=_cparams(("parallel", "arbitrary")),
        name="combine",
    )(offs, x2, pos, eo, g_final)


def _layer(x, norm_mix, w_in, b_in, conv_w, conv_b, rg_w, rg_b, rg_lambda, p_rnn, p_attn,
           w_out, norm_ffn, w_router, b_router, w_gate, w_up, w_down, norm_out, cos_t, sin_t):
    B, S, D = x.shape
    T = B * S
    x2d = x.reshape(T, D)
    n_rnn = 2 * D_RNN
    n_qkv = 3 * N_GROUPS * ATTN_WIDTH
    gmix = norm_mix.reshape(1, D)
    w_in_b = w_in.astype(BF16)
    b_in2 = b_in.reshape(1, -1)

    def group_cols(a, g):
        part = lambda p: a[:, n_rnn + (p * N_GROUPS + g) * ATTN_WIDTH:
                           n_rnn + (p * N_GROUPS + g + 1) * ATTN_WIDTH]
        return jnp.concatenate([part(0), part(1), part(2)], axis=1)

    ws = [group_cols(w_in_b, g) for g in range(N_GROUPS)]
    b_qkv = jnp.concatenate([group_cols(b_in2, g) for g in range(N_GROUPS)], axis=1)
    n_e, d_in, d_ff = w_gate.shape
    h, *qkv, w_gate_b = _proj_qkv(x2d, gmix, ws, b_qkv, cos_t, sin_t, B,
                                  w_gate.reshape(n_e * d_in, d_ff))
    xrgr = _proj_f32(h, w_in_b[:, :n_rnn], b_in2[:, :n_rnn])

    nblk = D_RNN // RNN_BW
    gate_w = (0.5 * jnp.transpose(rg_w, (2, 3, 0, 1, 4))).reshape(nblk, RNN_BW, 4 * RNN_BW).astype(BF16)
    gate_b = 0.5 * jnp.transpose(rg_b.reshape(2, 2, nblk, RNN_BW), (2, 0, 1, 3)).reshape(nblk, 1, 4 * RNN_BW)
    lam = jnp.transpose(rg_lambda.reshape(2, nblk, RNN_BW), (1, 0, 2))
    y_rnn = _rglru(xrgr.reshape(B, S, n_rnn), conv_w, conv_b.reshape(1, D_RNN), gate_w, gate_b, lam)

    outs, lses = [], []
    for g, (_, dilation) in enumerate(ATTN_GROUPS):
        qkv_g = qkv[g].reshape(B, 1, S, 3 * ATTN_WIDTH) if dilation == 1 else qkv[g]
        o, lse = _attention_group(qkv_g)
        if dilation == 1:
            o, lse = o.reshape(T, ATTN_WIDTH), lse.reshape(T, LANES)
        outs.append(o)
        lses.append(lse)

    x2, h2, aff_t, w_up_b, w_down_b = _merge(
        x2d, y_rnn.reshape(T, D_RNN), outs, lses, gmix,
        w_in_b[:, n_rnn + n_qkv:], b_in2[:, n_rnn + n_qkv:],
        p_rnn.astype(BF16), p_attn.astype(BF16), w_out.astype(BF16),
        norm_ffn.reshape(1, D), w_router.astype(BF16), b_router.reshape(1, -1), B,
        w_up.reshape(n_e * d_in, d_ff), w_down.reshape(n_e * d_ff, d_in))

    E = w_router.shape[1]
    cap = CAPACITY_FACTOR * S // E
    pos4, off4 = _route(aff_t, cap)
    pos = pos4.reshape(B, E, S)
    blk_off = off4[:, :, ::TOK_BLK // LANES, 0]
    offs = jnp.concatenate([blk_off, jnp.full((B, E, 1), cap, jnp.int32)], axis=2).reshape(-1)
    xg = _gather(offs, pos, h2.reshape(B, S, D + LANES), cap)
    eo = _ffn(xg, w_gate_b.reshape(n_e, d_in, d_ff), w_up_b.reshape(n_e, d_in, d_ff),
              w_down_b.reshape(n_e, d_ff, d_in))
    return _combine(offs, x2.reshape(B, S, D), pos, eo, norm_out.reshape(1, D), cap)


def kernel(x, norm_mix, w_in, b_in, conv_w, conv_b, rg_w, rg_b, rg_lambda, p_rnn, p_attn, w_out,
           norm_ffn, w_router, b_router, w_gate, w_up, w_down, norm_final):
    depth = w_in.shape[0]
    assert depth == 1, "final rmsnorm is fused into the single layer's combine step"
    cos_t, sin_t = _rope_tables(x.shape[1])
    return _layer(x, norm_mix[0], w_in[0], b_in[0], conv_w[0], conv_b[0], rg_w[0], rg_b[0],
                  rg_lambda[0], p_rnn[0], p_attn[0], w_out[0], norm_ffn[0], w_router[0],
                  b_router[0], w_gate[0], w_up[0], w_down[0], norm_final, cos_t, sin_t)
```

```python
import functools

import jax
import jax.numpy as jnp
from jax import lax
from jax.experimental import pallas as pl
from jax.experimental.pallas import tpu as pltpu

F32 = jnp.float32
BF16 = jnp.bfloat16

D_RNN = 1024
RNN_BW = 128
CONV_WIDTH = 4
LRU_C = 8.0
HEAD_DIM = 128
HEADS = 4
ATTN_WIDTH = HEADS * HEAD_DIM
ATTN_GROUPS = ((128, 1), (512, 4), (2048, 16))
N_GROUPS = 3
ROPE_THETA = 500000.0
ROPE_DIM = HEAD_DIM // 4
RADIUS = 64
NEG_INF = -1e30
CAPACITY_FACTOR = 2
RMS_EPS = 1e-6

LANES = 128
SUBLANES = 8
BF16_ROWS = 16
VMEM_LIMIT = 56 * 1024 * 1024
F32_MIN_NORMAL = 1.1754944e-38
LOG2_E = 1.4426950408889634
LN_2 = 0.6931471805599453

PROJ_TILE = 1024
QKV_CHUNK = 512
ROW_TILE = 512
SCAN_SEG = 128
SCAN_CHUNK = 512
ATT_QBLK = 2048
ATT_SUB = 128
TOK_BLK = 256
GATHER_TOK = 2048
COMBINE_TOK = 512
SLOT_SHIFT = 6
SLOT_WIN = 1 << SLOT_SHIFT
EXPERT_SPLIT = 2
FFN_CHUNK = 512
SEARCH_STEPS = 17


def _cparams(sem):
    return pltpu.CompilerParams(dimension_semantics=sem, vmem_limit_bytes=VMEM_LIMIT)


def _rms(x, g):
    ms = jnp.mean(x * x, axis=-1, keepdims=True)
    return x * lax.rsqrt(ms + RMS_EPS) * g


def _pitch(d):
    return d if d <= 4 else d + 4


def _deinterleave(t, stage, d):
    n = t.shape[0]
    p = _pitch(d)
    if p == d:
        stage[0:n, :] = t
    else:
        for j in range(n // d):
            stage[j * p:j * p + d, :] = t[j * d:(j + 1) * d, :]
    return [stage[pl.ds(r, n // d, stride=p), :] for r in range(d)]


def _interleave(parts, stage, d):
    m = parts[0].shape[0]
    p = _pitch(d)
    for r in range(d):
        stage[pl.ds(r, m, stride=p), :] = parts[r]
    if p == d:
        return stage[0:m * d, :]
    return jnp.concatenate([stage[j * p:j * p + d, :] for j in range(m)], axis=0)


def _proj_f32_kernel(h_ref, w_ref, b_ref, o_ref, *, n_chunk):
    hb = h_ref[...]
    for c in range(o_ref.shape[1] // n_chunk):
        sl = slice(c * n_chunk, (c + 1) * n_chunk)
        o_ref[:, sl] = jnp.dot(hb, w_ref[:, sl], preferred_element_type=F32) + b_ref[:, sl]


def _proj_f32(h, w, b):
    T, D = h.shape
    N = w.shape[1]
    return pl.pallas_call(
        functools.partial(_proj_f32_kernel, n_chunk=512),
        out_shape=jax.ShapeDtypeStruct((T, N), F32),
        grid=(T // PROJ_TILE,),
        in_specs=[
            pl.BlockSpec((PROJ_TILE, D), lambda i: (i, 0)),
            pl.BlockSpec((D, N), lambda i: (0, 0)),
            pl.BlockSpec((1, N), lambda i: (0, 0)),
        ],
        out_specs=pl.BlockSpec((PROJ_TILE, N), lambda i: (i, 0)),
        compiler_params=_cparams(("parallel",)),
        name="proj_rnn",
    )(h, w, b)


def _proj_qkv_kernel(x_ref, g_ref, w0_ref, w1_ref, w2_ref, b_ref, cos_ref, sin_ref,
                     ca_ref, h_ref, o0_ref, o1_ref, o2_ref, cao_ref, stage):
    cao_ref[...] = ca_ref[...].astype(BF16)
    hb =_rms(x_ref[...], g_ref[...]).astype(BF16)
    h_ref[...] = hb
    cosv = cos_ref[...]
    sinv = sin_ref[...]
    lane = lax.broadcasted_iota(jnp.int32, (1, HEAD_DIM), 1)
    first_half = lane < (ROPE_DIM // 2)
    gw = 3 * ATTN_WIDTH
    for g, (w_ref, o_ref) in enumerate(((w0_ref, o0_ref), (w1_ref, o1_ref), (w2_ref, o2_ref))):
        d = ATTN_GROUPS[g][1]
        for c in range(gw // QKV_CHUNK):
            sl = slice(c * QKV_CHUNK, (c + 1) * QKV_CHUNK)
            t2 = jnp.dot(hb, w_ref[:, sl], preferred_element_type=F32)
            t2 = t2 + b_ref[:, g * gw + c * QKV_CHUNK:g * gw + (c + 1) * QKV_CHUNK]
            for hh in range(QKV_CHUNK // HEAD_DIM):
                t = t2[:, hh * HEAD_DIM:(hh + 1) * HEAD_DIM]
                col = c * QKV_CHUNK + hh * HEAD_DIM
                if col < 2 * ATTN_WIDTH:
                    up = pltpu.roll(t, HEAD_DIM - ROPE_DIM // 2, 1)
                    down = pltpu.roll(t, ROPE_DIM // 2, 1)
                    t = t * cosv + jnp.where(first_half, up, down) * sinv
                cs = slice(col, col + HEAD_DIM)
                if d == 1:
                    o_ref[:, cs] = t.astype(BF16)
                else:
                    for r, part in enumerate(_deinterleave(t, stage, d)):
                        o_ref[0, r, :, cs] = part.astype(BF16)


def _cast_slab_spec(a, nsteps):
    rows, cols = a.shape
    assert rows % (nsteps * BF16_ROWS) == 0, (a.shape, nsteps)
    return pl.BlockSpec((rows // nsteps, cols), lambda i: (i, 0))


def _proj_qkv(x2d, g, ws, b, cos_t, sin_t, batch, cast_a):
    T, D = x2d.shape
    S = T // batch
    nseq = S // PROJ_TILE
    nsteps = T // PROJ_TILE
    gw = 3 * ATTN_WIDTH
    d1, d2 = ATTN_GROUPS[1][1], ATTN_GROUPS[2][1]
    full = lambda a: pl.BlockSpec(a.shape, lambda i: (0,) * a.ndim)
    dil_spec = lambda d: pl.BlockSpec((1, d, PROJ_TILE // d, gw),
                                      lambda i: (i // nseq, 0, i % nseq, 0))
    return pl.pallas_call(
        _proj_qkv_kernel,
        out_shape=(jax.ShapeDtypeStruct((T, D), BF16),
                   jax.ShapeDtypeStruct((T, gw), BF16),
                   jax.ShapeDtypeStruct((batch, d1, S // d1, gw), BF16),
                   jax.ShapeDtypeStruct((batch, d2, S // d2, gw), BF16),
                   jax.ShapeDtypeStruct(cast_a.shape, BF16)),
        grid=(nsteps,),
        in_specs=[
            pl.BlockSpec((PROJ_TILE, D), lambda i: (i, 0)),
            full(g), full(ws[0]), full(ws[1]), full(ws[2]), full(b),
            pl.BlockSpec((PROJ_TILE, HEAD_DIM), lambda i: (i % nseq, 0)),
            pl.BlockSpec((PROJ_TILE, HEAD_DIM), lambda i: (i % nseq, 0)),
            _cast_slab_spec(cast_a, nsteps),
        ],
        out_specs=(pl.BlockSpec((PROJ_TILE, D), lambda i: (i, 0)),
                   pl.BlockSpec((PROJ_TILE, gw), lambda i: (i, 0)), dil_spec(d1), dil_spec(d2),
                   _cast_slab_spec(cast_a, nsteps)),
        scratch_shapes=[pltpu.VMEM((PROJ_TILE // d2 * _pitch(d2), LANES), F32)],
        compiler_params=_cparams(("parallel",)),
        name="proj_qkv",
    )(x2d, g, ws[0], ws[1], ws[2], b, cos_t, sin_t, cast_a)


def _rope_tables(seq):
    pos = jnp.arange(seq, dtype=F32)
    inv = ROPE_THETA ** (-jnp.arange(0, ROPE_DIM, 2, dtype=F32) / ROPE_DIM)
    ang = pos[:, None] * inv[None, :]
    cos, sin = jnp.cos(ang), jnp.sin(ang)
    pad = HEAD_DIM - ROPE_DIM
    cos_t = jnp.concatenate([cos, cos, jnp.ones((seq, pad), F32)], axis=1)
    sin_t = jnp.concatenate([-sin, sin, jnp.zeros((seq, pad), F32)], axis=1)
    return cos_t, sin_t


def _sigmoid(x):
    return 0.5 * jnp.tanh(0.5 * x) + 0.5


def _rglru_kernel(xr_ref, gr_ref, cw_ref, cb_ref, gw_ref, gb_ref, lam_ref, o_ref,
                  xpad, af, uf, ab, ub, cf, cbk, *, seq):
    nseg = seq // SCAN_SEG
    nchunk = seq // SCAN_CHUNK
    seg_per_chunk = SCAN_CHUNK // SCAN_SEG
    pitch = nseg + 4

    cw = cw_ref[...]
    cbias = cb_ref[...]
    gbias = gb_ref[...]
    z = -lam_ref[...]
    sp = jnp.maximum(z, 0.0) + jnp.log1p(jnp.exp(-jnp.abs(z)))
    half_neg_c_sp = (-0.5 * LRU_C) * sp

    zero_rows = jnp.zeros((SUBLANES, LANES), F32)
    xpad[0:SUBLANES, :] = zero_rows
    xpad[seq + SUBLANES:seq + 2 * SUBLANES, :] = zero_rows

    def copy_in(c, carry):
        t0 = pl.multiple_of(c * SCAN_CHUNK, SCAN_CHUNK)
        xpad[pl.ds(t0 + SUBLANES, SCAN_CHUNK), :] = xr_ref[0, pl.ds(t0, SCAN_CHUNK), :]
        return carry

    lax.fori_loop(0, nchunk, copy_in, 0)

    def stage1(c, carry):
        t0 = c * SCAN_CHUNK
        xc = cbias
        for tap in range(CONV_WIDTH):
            lo = t0 + SUBLANES - CONV_WIDTH // 2 + tap
            xc = xc + cw[tap:tap + 1, :] * xpad[pl.ds(lo, SCAN_CHUNK), :]
        th = jnp.tanh(jnp.dot(xc.astype(BF16), gw_ref[...], preferred_element_type=F32) + gbias)
        xc_half = 0.5 * xc
        for d, (a_s, u_s) in enumerate(((af, uf), (ab, ub))):
            t_r = th[:, (2 * d) * LANES:(2 * d + 1) * LANES]
            t_i = th[:, (2 * d + 1) * LANES:(2 * d + 2) * LANES]
            half_k = half_neg_c_sp[d:d + 1, :]
            log_a = t_r * half_k + half_k
            a = jnp.exp(log_a)
            v = jnp.tanh(log_a) * (-1.0 - a * a)
            u = (xc_half * (v * lax.rsqrt(jnp.maximum(v, F32_MIN_NORMAL)))) * (t_i + 1.0)
            for k in range(seg_per_chunk):
                seg = c * seg_per_chunk + k
                a_s[pl.ds(seg, SCAN_SEG, stride=pitch), :] = a[k * SCAN_SEG:(k + 1) * SCAN_SEG, :]
                u_s[pl.ds(seg, SCAN_SEG, stride=pitch), :] = u[k * SCAN_SEG:(k + 1) * SCAN_SEG, :]
        return carry

    lax.fori_loop(0, nchunk, stage1, 0, unroll=4)

    def scan_dir(a_s, u_s, reverse):
        def body(it, carry):
            i = (SCAN_SEG - 1 - it) if reverse else it
            h, cum = carry
            rows = pl.ds(i * pitch, nseg)
            a = a_s[rows, :]
            u = u_s[rows, :]
            h = a * h + u
            cum = a * cum
            u_s[rows, :] = h
            a_s[rows, :] = cum
            return h, cum

        return lax.fori_loop(0, SCAN_SEG, body,
                             (jnp.zeros((nseg, LANES), F32), jnp.ones((nseg, LANES), F32)))

    end_h, end_a = scan_dir(af, uf, False)
    beg_h, beg_a = scan_dir(ab, ub, True)

    c = jnp.zeros((1, LANES), F32)
    for s in range(nseg):
        cf[s:s + 1, :] = c
        c = end_h[s:s + 1, :] + end_a[s:s + 1, :] * c
    c = jnp.zeros((1, LANES), F32)
    for s in range(nseg - 1, -1, -1):
        cbk[s:s + 1, :] = c
        c = beg_h[s:s + 1, :] + beg_a[s:s + 1, :] * c

    def finish(s, carry):
        t0 = pl.multiple_of(s * SCAN_SEG, SCAN_SEG)
        rows = pl.ds(s, SCAN_SEG, stride=pitch)
        hf = uf[rows, :] + af[rows, :] * cf[pl.ds(s, 1), :]
        hb = ub[rows, :] + ab[rows, :] * cbk[pl.ds(s, 1), :]
        g = jax.nn.gelu(gr_ref[0, pl.ds(t0, SCAN_SEG), :])
        o_ref[0, pl.ds(t0, SCAN_SEG), :] = (g * (hf + hb)).astype(BF16)
        return carry

    lax.fori_loop(0, nseg, finish, 0)


def _rglru(xrgr, conv_w, conv_b, gate_w, gate_b, lam):
    B, S, _ = xrgr.shape
    nblk = D_RNN // RNN_BW
    nseg = S // SCAN_SEG
    scr = pltpu.VMEM((SCAN_SEG * (nseg + 4), LANES), F32)
    return pl.pallas_call(
        functools.partial(_rglru_kernel, seq=S),
        out_shape=jax.ShapeDtypeStruct((B, S, D_RNN), BF16),
        grid=(B, nblk),
        in_specs=[
            pl.BlockSpec((1, S, RNN_BW), lambda b, c: (b, 0, c)),
            pl.BlockSpec((1, S, RNN_BW), lambda b, c: (b, 0, nblk + c)),
            pl.BlockSpec((CONV_WIDTH, RNN_BW), lambda b, c: (0, c)),
            pl.BlockSpec((1, RNN_BW), lambda b, c: (0, c)),
            pl.BlockSpec((None, RNN_BW, 4 * RNN_BW), lambda b, c: (c, 0, 0)),
            pl.BlockSpec((None, 1, 4 * RNN_BW), lambda b, c: (c, 0, 0)),
            pl.BlockSpec((None, 2, RNN_BW), lambda b, c: (c, 0, 0)),
        ],
        out_specs=pl.BlockSpec((1, S, RNN_BW), lambda b, c: (b, 0, c)),
        scratch_shapes=[pltpu.VMEM((S + 2 * SUBLANES, LANES), F32), scr, scr, scr, scr,
                        pltpu.VMEM((nseg, LANES), F32), pltpu.VMEM((nseg, LANES), F32)],
        compiler_params=_cparams(("parallel", "parallel")),
        name="rglru",
    )(xrgr, xrgr, conv_w, conv_b, gate_w, gate_b, lam)


def _attn_kernel(q_ref, k_ref, kp_ref, kn_ref, v_ref, vp_ref, vn_ref, o_ref, st_ref,
                 kext, vext, *, qblk, length):
    j = pl.program_id(2)
    nkey = ATT_SUB + 2 * RADIUS
    qi = lax.broadcasted_iota(jnp.int32, (ATT_SUB, nkey), 0)
    ki = lax.broadcasted_iota(jnp.int32, (ATT_SUB, nkey), 1)
    band_bias = jnp.where(jnp.abs(ki - RADIUS - qi) <= RADIUS, 0.0, NEG_INF)
    krow = lax.broadcasted_iota(jnp.int32, (1, nkey), 1)
    lane = lax.broadcasted_iota(jnp.int32, (ATT_SUB, LANES), 1)
    scale2 = HEAD_DIM ** -0.5 * LOG2_E

    for rr in range(q_ref.shape[1]):
        kx, vx = kext.at[rr], vext.at[rr]
        kx[0:RADIUS, :] = kp_ref[0, rr]
        kx[RADIUS:RADIUS + qblk, :] = k_ref[0, rr]
        kx[RADIUS + qblk:RADIUS + qblk + RADIUS, :] = kn_ref[0, rr]
        for h in range(HEADS):
            src = slice(h * HEAD_DIM, (h + 1) * HEAD_DIM)
            dst = slice(2 * h * HEAD_DIM, (2 * h + 1) * HEAD_DIM)
            vx[0:RADIUS, dst] = vp_ref[0, rr, :, src]
            vx[RADIUS:RADIUS + qblk, dst] = v_ref[0, rr, :, src]
            vx[RADIUS + qblk:RADIUS + qblk + RADIUS, dst] = vn_ref[0, rr, :, src]
            vx[:, (2 * h + 1) * HEAD_DIM:(2 * h + 2) * HEAD_DIM] = jnp.ones(
                (qblk + 2 * RADIUS, HEAD_DIM), BF16)

        for i in range(qblk // ATT_SUB):
            q0 = i * ATT_SUB
            kglob = j * qblk + q0 - RADIUS + krow
            edge_bias = jnp.where((kglob >= 0) & (kglob < length), 0.0, NEG_INF)
            bias = band_bias + edge_bias
            st_tile = jnp.ones((ATT_SUB, LANES), F32)
            for h in range(HEADS):
                hs = slice(h * HEAD_DIM, (h + 1) * HEAD_DIM)
                q = q_ref[0, rr, q0:q0 + ATT_SUB, hs]
                ks = kx[q0:q0 + nkey, hs]
                vs = vx[q0:q0 + nkey, 2 * h * HEAD_DIM:(2 * h + 2) * HEAD_DIM]
                s = lax.dot_general(q, ks, (((1,), (1,)), ((), ())),
                                    preferred_element_type=F32) * scale2 + bias
                m = jnp.max(s, axis=-1, keepdims=True)
                p = jnp.exp2(s - m)
                pv = jnp.dot(p.astype(BF16), vs, preferred_element_type=F32)
                l = pv[:, HEAD_DIM:2 * HEAD_DIM]
                o_ref[0, rr, q0:q0 + ATT_SUB, hs] = pv[:, 0:HEAD_DIM]
                st_tile = jnp.where(lane == h, m * LN_2,
                                    jnp.where(lane == HEADS + h, l, st_tile))
            st_ref[0, rr, q0:q0 + ATT_SUB, :] = st_tile


def _attention_group(qkv_g):
    B, d, L, _ = qkv_g.shape
    qblk = min(L, ATT_QBLK)
    nblk = L // qblk
    rpb = min(d, max(1, ATT_QBLK // L))
    halo_per_blk = qblk // RADIUS
    last_halo = L // RADIUS - 1

    main = lambda part: pl.BlockSpec((1, rpb, qblk, ATTN_WIDTH), lambda b, r, j: (b, r, j, part))
    prev = lambda part: pl.BlockSpec(
        (1, rpb, RADIUS, ATTN_WIDTH),
        lambda b, r, j: (b, r, jnp.maximum(j * halo_per_blk - 1, 0), part))
    nxt = lambda part: pl.BlockSpec(
        (1, rpb, RADIUS, ATTN_WIDTH),
        lambda b, r, j: (b, r, jnp.minimum((j + 1) * halo_per_blk, last_halo), part))
    return pl.pallas_call(
        functools.partial(_attn_kernel, qblk=qblk, length=L),
        out_shape=(jax.ShapeDtypeStruct((B, d, L, ATTN_WIDTH), F32),
                   jax.ShapeDtypeStruct((B, d, L, LANES), F32)),
        grid=(B, d // rpb, nblk),
        in_specs=[main(0), main(1), prev(1), nxt(1), main(2), prev(2), nxt(2)],
        out_specs=(pl.BlockSpec((1, rpb, qblk, ATTN_WIDTH), lambda b, r, j: (b, r, j, 0)),
                   pl.BlockSpec((1, rpb, qblk, LANES), lambda b, r, j: (b, r, j, 0))),
        scratch_shapes=[pltpu.VMEM((rpb, qblk + 2 * RADIUS, ATTN_WIDTH), BF16),
                        pltpu.VMEM((rpb, qblk + 2 * RADIUS, 2 * ATTN_WIDTH), BF16)],
        compiler_params=_cparams(("parallel", "parallel", "parallel")),
        name=f"attn_d{d}",
    )(qkv_g, qkv_g, qkv_g, qkv_g, qkv_g, qkv_g, qkv_g)


def _merge_kernel(x_ref, yr_ref, o0_ref, o1_ref, o2_ref, l0_ref, l1_ref, l2_ref,
                  gmix_ref, wg_ref, bg_ref, prnn_ref, pattn_ref, wout_ref,
                  gffn_ref, wr_ref, wrt_ref, br_ref, brt_ref, cb_ref, cc_ref,
                  x2_ref, h2_ref, afft_ref, cbo_ref, cco_ref, yattn, stage):
    cbo_ref[...] = cb_ref[...].astype(BF16)
    cco_ref[...] = cc_ref[...].astype(BF16)
    d = x_ref.shape[1]
    x = x_ref[...]
    hb = _rms(x, gmix_ref[...]).astype(BF16)
    branch_a = jnp.dot(yr_ref[...], prnn_ref[...], preferred_element_type=F32)
    ga = jnp.dot(hb, wg_ref[:, 0:d], preferred_element_type=F32) + bg_ref[:, 0:d]
    gb = jnp.dot(hb, wg_ref[:, d:2 * d], preferred_element_type=F32) + bg_ref[:, d:2 * d]

    def natural(ref, g, cs):
        dil = ATTN_GROUPS[g][1]
        if dil == 1:
            return ref[:, cs]
        return _interleave([ref[0, r, :, cs] for r in range(dil)], stage, dil)

    o_refs = (o0_ref, o1_ref, o2_ref)
    all_lanes = slice(0, LANES)
    stats = [natural(r, g, all_lanes) for g, r in enumerate((l0_ref, l1_ref, l2_ref))]
    head_lane = lax.broadcasted_iota(jnp.int32, stats[0].shape, 1) < HEADS
    sums = [jnp.where(head_lane, pltpu.roll(st, LANES - HEADS, 1), 1.0) for st in stats]
    lses = [st + jnp.log(l) for st, l in zip(stats, sums)]
    m = jnp.maximum(jnp.maximum(lses[0], lses[1]), lses[2])
    ws = [jnp.exp(l - m) for l in lses]
    den = ws[0] + ws[1] + ws[2]
    coef = [w / (den * l) for w, l in zip(ws, sums)]
    for h in range(HEADS):
        hs = slice(h * HEAD_DIM, (h + 1) * HEAD_DIM)
        acc = coef[0][:, h:h + 1] * natural(o_refs[0], 0, hs)
        acc = acc + coef[1][:, h:h + 1] * natural(o_refs[1], 1, hs)
        acc = acc + coef[2][:, h:h + 1] * natural(o_refs[2], 2, hs)
        yattn[:, hs] = acc.astype(BF16)

    branch_b = jnp.dot(yattn[...], pattn_ref[...], preferred_element_type=F32)
    merged = _sigmoid(ga) * branch_a + _sigmoid(gb) * branch_b
    x2 = x + jnp.dot(merged.astype(BF16), wout_ref[...], preferred_element_type=F32)
    x2_ref[...] = x2

    h2 = _rms(x2, gffn_ref[...]).astype(BF16)
    h2_ref[:, 0:d] = h2
    n_exp = afft_ref.shape[1]
    logits = jnp.dot(h2, wr_ref[...], preferred_element_type=F32) + br_ref[...]
    lane = lax.broadcasted_iota(jnp.int32, logits.shape, 1)
    mx = jnp.max(logits, axis=-1, keepdims=True)
    e = jnp.exp(logits - mx)
    den = jnp.sum(jnp.where(lane < n_exp, e, 0.0), axis=-1, keepdims=True)
    aff = e / den
    head = aff.astype(BF16).astype(F32)
    tile = jnp.where(lane < n_exp, head, jnp.where(lane < 2 * n_exp, aff - head, 0.0))
    h2_ref[:, d:d + LANES] = tile.astype(BF16)
    logits_t = lax.dot_general(wrt_ref[...], h2, (((1,), (1,)), ((), ())),
                               preferred_element_type=F32) + brt_ref[...]
    mxt = jnp.max(logits_t, axis=0, keepdims=True)
    et = jnp.exp(logits_t - mxt)
    afft_ref[0] = et / jnp.sum(et, axis=0, keepdims=True)


def _merge(x2d, y_rnn, outs, lses, gmix, wg, bg, p_rnn, p_attn, w_out, gffn, wr, br, batch,
           cast_b, cast_c):
    T, D = x2d.shape
    S = T // batch
    nseq = S // ROW_TILE
    nsteps = T // ROW_TILE
    E = wr.shape[1]
    row = lambda w: pl.BlockSpec((ROW_TILE, w), lambda i: (i, 0))
    full = lambda a: pl.BlockSpec(a.shape, lambda i: (0,) * a.ndim)

    def attn_spec(g, w):
        dil = ATTN_GROUPS[g][1]
        if dil == 1:
            return row(w)
        return pl.BlockSpec((1, dil, ROW_TILE // dil, w), lambda i: (i // nseq, 0, i % nseq, 0))

    wrt = wr.T
    brt = br.reshape(E, 1)
    wr_pad = jnp.concatenate([wr, wr, jnp.zeros((D, LANES - 2 * E), wr.dtype)], axis=1)
    br_pad = jnp.concatenate([br, br, jnp.full((1, LANES - 2 * E), NEG_INF, br.dtype)], axis=1)
    consts = [gmix, wg, bg, p_rnn, p_attn, w_out, gffn, wr_pad, wrt, br_pad, brt]
    args = [x2d, y_rnn, *outs, *lses, *consts, cast_b, cast_c]
    in_specs = ([row(D), row(D_RNN)] + [attn_spec(g, ATTN_WIDTH) for g in range(N_GROUPS)]
                + [attn_spec(g, LANES) for g in range(N_GROUPS)] + [full(a) for a in consts]
                + [_cast_slab_spec(cast_b, nsteps), _cast_slab_spec(cast_c, nsteps)])
    dmax = max(dil for _, dil in ATTN_GROUPS)
    return pl.pallas_call(
        _merge_kernel,
        out_shape=(jax.ShapeDtypeStruct((T, D), F32),
                   jax.ShapeDtypeStruct((T, D + LANES), BF16),
                   jax.ShapeDtypeStruct((batch, E, S), F32),
                   jax.ShapeDtypeStruct(cast_b.shape, BF16),
                   jax.ShapeDtypeStruct(cast_c.shape, BF16)),
        grid=(nsteps,),
        in_specs=in_specs,
        out_specs=(row(D), row(D + LANES),
                   pl.BlockSpec((1, E, ROW_TILE), lambda i: (i // nseq, 0, i % nseq)),
                   _cast_slab_spec(cast_b, nsteps), _cast_slab_spec(cast_c, nsteps)),
        scratch_shapes=[pltpu.VMEM((ROW_TILE, ATTN_WIDTH), BF16),
                        pltpu.VMEM((ROW_TILE // dmax * _pitch(dmax), LANES), F32)],
        compiler_params=_cparams(("parallel",)),
        name="merge",
    )(*args)


def _route_kernel(aff_ref, pos_ref, off_ref, *, cap):
    a = aff_ref[0]
    E, R, _ = a.shape
    bits = pltpu.bitcast(a, jnp.int32)

    def count(mask):
        c = jnp.sum(mask.astype(F32), axis=1, keepdims=True)
        return jnp.sum(c, axis=2, keepdims=True)

    def search(_, carry):
        lo, hi = carry
        q = (hi - lo + 3) >> 2
        m1 = jnp.minimum(lo + q, hi)
        m2 = jnp.minimum(m1 + q, hi)
        m3 = jnp.minimum(m2 + q, hi)
        ok1 = count(bits >= m1) >= cap
        ok2 = count(bits >= m2) >= cap
        ok3 = count(bits >= m3) >= cap
        new_lo = jnp.where(ok3, m3, jnp.where(ok2, m2, jnp.where(ok1, m1, lo)))
        new_hi = jnp.where(ok3, hi, jnp.where(ok2, m3 - 1, jnp.where(ok1, m2 - 1, m1 - 1)))
        return new_lo, new_hi

    lo0 = jnp.zeros((E, 1, 1), jnp.int32)
    hi0 = jnp.full((E, 1, 1), 0x7F7FFFFF, jnp.int32)
    thr, _ = lax.fori_loop(0, SEARCH_STEPS, search, (lo0, hi0))

    gt = bits > thr
    eq = bits == thr
    need = cap - count(gt)

    ki = lax.broadcasted_iota(jnp.int32, (LANES, LANES), 0)
    li = lax.broadcasted_iota(jnp.int32, (LANES, LANES), 1)
    upper = (ki <= li).astype(BF16)
    allones = jnp.ones((LANES, LANES), BF16)
    ri = lax.broadcasted_iota(jnp.int32, (R, R), 0)
    ci = lax.broadcasted_iota(jnp.int32, (R, R), 1)
    strict = (ci < ri).astype(BF16)

    def prefix(mask):
        mb = mask.astype(BF16).reshape(E * R, LANES)
        within = jnp.dot(mb, upper, preferred_element_type=F32)
        totals = jnp.dot(mb, allones, preferred_element_type=F32).astype(BF16)
        offs = []
        for e in range(E):
            offs.append(jnp.dot(strict, totals[e * R:(e + 1) * R, :],
                                preferred_element_type=F32))
        rowoff = jnp.concatenate(offs, axis=0)
        excl = within - mb.astype(F32) + rowoff
        return excl.reshape(E, R, LANES), totals

    eq_rank, _ = prefix(eq)
    sel = gt | (eq & (eq_rank < need.astype(F32)))
    slot, totals = prefix(sel)
    pos_ref[0] = jnp.where(sel, slot.astype(jnp.int32), -1)

    nblk = off_ref.shape[2]
    bi = lax.broadcasted_iota(jnp.int32, (nblk, R), 0)
    rj = lax.broadcasted_iota(jnp.int32, (nblk, R), 1)
    before_blk = (rj < bi * (R // nblk)).astype(BF16)
    blk = [jnp.dot(before_blk, totals[e * R:(e + 1) * R, :], preferred_element_type=F32)
           for e in range(E)]
    off_ref[0] = jnp.concatenate(blk, axis=0).reshape(E, nblk, LANES).astype(jnp.int32)


def _route(aff_t, cap):
    B, E, S = aff_t.shape
    R = S // LANES
    ntb = S // TOK_BLK
    a4 = aff_t.reshape(B, E, R, LANES)
    blk = pl.BlockSpec((1, E, R, LANES), lambda b: (b, 0, 0, 0))
    pos, off = pl.pallas_call(
        functools.partial(_route_kernel, cap=cap),
        out_shape=(jax.ShapeDtypeStruct((B, E, R, LANES), jnp.int32),
                   jax.ShapeDtypeStruct((B, E, ntb, LANES), jnp.int32)),
        grid=(B,),
        in_specs=[blk],
        out_specs=(blk, pl.BlockSpec((1, E, ntb, LANES), lambda b: (b, 0, 0, 0))),
        compiler_params=_cparams(("parallel",)),
        name="route",
    )(a4)
    return pos, off


def _slot_windows(off_ref, base, n_exp, ntb, tb):
    starts = []
    rounds = jnp.int32(0)
    for e in range(n_exp):
        idx = (base + e) * (ntb + 1) + tb
        start = (off_ref[idx] >> 4) << 4
        starts.append(start)
        rounds = jnp.maximum(rounds, (off_ref[idx + 1] - start + SLOT_WIN - 1) >> SLOT_SHIFT)
    return starts, rounds


def _gather_kernel(off_ref, pos_ref, h_ref, o_ref, *, cap, ntb, n_exp, nsub):
    b = pl.program_id(0)
    grp = pl.program_id(1)
    t = pl.program_id(2)

    @pl.when(t == 0)
    def _():
        o_ref[...] = jnp.zeros_like(o_ref)

    base = (b * pl.num_programs(1) + grp) * n_exp
    ji = lax.broadcasted_iota(jnp.int32, (SLOT_WIN, TOK_BLK), 0)
    for sb in range(nsub):
        toks = slice(sb * TOK_BLK, (sb + 1) * TOK_BLK)
        starts, rounds = _slot_windows(off_ref, base, n_exp, ntb, t * nsub + sb)

        def one_round(k, carry, toks=toks, starts=starts):
            pieces, rows = [], []
            for e in range(n_exp):
                first = starts[e] + k * SLOT_WIN
                row0 = pl.multiple_of(jnp.minimum(first, cap - SLOT_WIN), BF16_ROWS)
                slot = ji + row0
                hit = (slot == pos_ref[0, 0, e:e + 1, toks]) & (slot >= first)
                pieces.append(jnp.where(hit, 1.0, 0.0).astype(BF16))
                rows.append(row0)
            onehot = jnp.concatenate(pieces, axis=0)
            g = jnp.dot(onehot, h_ref[0, toks, :], preferred_element_type=F32)
            for e in range(n_exp):
                o_ref[0, e, pl.ds(rows[e], SLOT_WIN), :] += (
                    g[e * SLOT_WIN:(e + 1) * SLOT_WIN, :].astype(BF16))
            return carry

        one_round(0, 0)
        lax.fori_loop(1, rounds, one_round, 0)


def _gather(offs, pos, h2, cap):
    B, E, S = pos.shape
    D = h2.shape[-1]
    ntb = S // TOK_BLK
    n_exp = E // EXPERT_SPLIT
    nsub = GATHER_TOK // TOK_BLK
    pos4 = pos.reshape(B, EXPERT_SPLIT, n_exp, S)
    return pl.pallas_call(
        functools.partial(_gather_kernel, cap=cap, ntb=ntb, n_exp=n_exp, nsub=nsub),
        out_shape=jax.ShapeDtypeStruct((B, E, cap, D), BF16),
        grid_spec=pltpu.PrefetchScalarGridSpec(
            num_scalar_prefetch=1,
            grid=(B, EXPERT_SPLIT, S // GATHER_TOK),
            in_specs=[
                pl.BlockSpec((1, 1, n_exp, GATHER_TOK), lambda b, g, t, off: (b, g, 0, t)),
                pl.BlockSpec((1, GATHER_TOK, D), lambda b, g, t, off: (b, t, 0)),
            ],
            out_specs=pl.BlockSpec((1, n_exp, cap, D), lambda b, g, t, off: (b, g, 0, 0)),
        ),
        compiler_params=_cparams(("parallel", "parallel", "arbitrary")),
        name="gather",
    )(offs, pos4, h2)


def _ffn_kernel(x_ref, wg_ref, wu_ref, wd_ref, o_ref, acc, *, f_chunk):
    e = pl.program_id(0)
    d = wg_ref.shape[1]
    x = x_ref[0, 0, :, 0:d]
    gl = x_ref[0, 0, :, d:d + LANES].astype(F32)
    lane = lax.broadcasted_iota(jnp.int32, gl.shape, 1)
    gate = jnp.sum(jnp.where((lane == e) | (lane == e + pl.num_programs(0)), gl, 0.0),
                   axis=1, keepdims=True)
    nf = wg_ref.shape[2] // f_chunk
    for f in range(nf):
        fs = slice(f * f_chunk, (f + 1) * f_chunk)
        g = jnp.dot(x, wg_ref[0, :, fs], preferred_element_type=F32)
        u = jnp.dot(x, wu_ref[0, :, fs], preferred_element_type=F32)
        hid = (g * _sigmoid(g) * u).astype(BF16)
        part = jnp.dot(hid, wd_ref[0, fs, :], preferred_element_type=F32)
        if f == 0:
            acc[...] = part
        else:
            acc[...] += part
    o_ref[0, 0] = (acc[...] * gate).astype(BF16)


def _ffn(xg, w_gate, w_up, w_down):
    B, E, C, DX = xg.shape
    D = w_gate.shape[1]
    F = w_gate.shape[2]
    assert F % FFN_CHUNK == 0, (F, FFN_CHUNK)
    return pl.pallas_call(
        functools.partial(_ffn_kernel, f_chunk=FFN_CHUNK),
        out_shape=jax.ShapeDtypeStruct((B, E, C, D), BF16),
        grid=(E, B),
        in_specs=[
            pl.BlockSpec((1, 1, C, DX), lambda e, b: (b, e, 0, 0)),
            pl.BlockSpec((1, D, F), lambda e, b: (e, 0, 0)),
            pl.BlockSpec((1, D, F), lambda e, b: (e, 0, 0)),
            pl.BlockSpec((1, F, D), lambda e, b: (e, 0, 0)),
        ],
        out_specs=pl.BlockSpec((1, 1, C, D), lambda e, b: (b, e, 0, 0)),
        scratch_shapes=[pltpu.VMEM((C, D), F32)],
        compiler_params=_cparams(("arbitrary", "arbitrary")),
        name="ffn",
    )(xg, w_gate, w_up, w_down)


def _combine_kernel(off_ref, x_ref, pos_ref, eo_ref, g_ref, o_ref, wstack, acc, *, cap, ntb):
    b = pl.program_id(0)
    t = pl.program_id(1)
    E = pos_ref.shape[1]
    nsub = pos_ref.shape[2] // TOK_BLK
    ji = lax.broadcasted_iota(jnp.int32, (SLOT_WIN, TOK_BLK), 0)

    for sb in range(nsub):
        toks = slice(sb * TOK_BLK, (sb + 1) * TOK_BLK)
        starts, rounds = _slot_windows(off_ref, b * E, E, ntb, t * nsub + sb)

        def one_round(k, carry, toks=toks, starts=starts, wst=wstack.at[sb], first_round=False):
            pieces = []
            for e in range(E):
                first = starts[e] + k * SLOT_WIN
                row0 = pl.multiple_of(jnp.minimum(first, cap - SLOT_WIN), BF16_ROWS)
                wst[e * SLOT_WIN:(e + 1) * SLOT_WIN, :] = eo_ref[0, e, pl.ds(row0, SLOT_WIN), :]
                slot = ji + row0
                hit = (slot == pos_ref[0, e:e + 1, toks]) & (slot >= first)
                pieces.append(jnp.where(hit, 1.0, 0.0).astype(BF16))
            onehot_t = jnp.concatenate(pieces, axis=0)
            y = lax.dot_general(onehot_t, wst[...], (((0,), (0,)), ((), ())),
                                preferred_element_type=F32)
            acc[toks, :] = y + (x_ref[0, toks, :] if first_round else acc[toks, :])
            return carry

        one_round(0, 0, first_round=True)
        lax.fori_loop(1, rounds, one_round, 0)
    o_ref[0] = _rms(acc[...], g_ref[...])


def _combine(offs, x2, pos, eo, g_final, cap):
    B, S, D = x2.shape
    E = pos.shape[1]
    ntb = S // TOK_BLK
    return pl.pallas_call(
        functools.partial(_combine_kernel, cap=cap, ntb=ntb),
        out_shape=jax.ShapeDtypeStruct((B, S, D), F32),
        grid_spec=pltpu.PrefetchScalarGridSpec(
            num_scalar_prefetch=1,
            grid=(B, S // COMBINE_TOK),
            in_specs=[
                pl.BlockSpec((1, COMBINE_TOK, D), lambda b, t, off: (b, t, 0)),
                pl.BlockSpec((1, E, COMBINE_TOK), lambda b, t, off: (b, 0, t)),
                pl.BlockSpec((1, E, cap, D), lambda b, t, off: (b, 0, 0, 0),
                             pipeline_mode=pl.Buffered(1)),
                pl.BlockSpec((1, D), lambda b, t, off: (0, 0)),
            ],
            out_specs=pl.BlockSpec((1, COMBINE_TOK, D), lambda b, t, off: (b, t, 0)),
            scratch_shapes=[pltpu.VMEM((COMBINE_TOK // TOK_BLK, E * SLOT_WIN, D), BF16),
                            pltpu.VMEM((COMBINE_TOK, D), F32)],
        ),
        compiler_params=_cparams(("parallel", "arbitrary")),
        name="combine",
    )(offs, x2, pos, eo, g_final)


def _layer(x, norm_mix, w_in, b_in, conv_w, conv_b, rg_w, rg_b, rg_lambda, p_rnn, p_attn,
           w_out, norm_ffn, w_router, b_router, w_gate, w_up, w_down, norm_out, cos_t, sin_t):
    B, S, D = x.shape
    T = B * S
    x2d = x.reshape(T, D)
    n_rnn = 2 * D_RNN
    n_qkv = 3 * N_GROUPS * ATTN_WIDTH
    gmix = norm_mix.reshape(1, D)
    w_in_b = w_in.astype(BF16)
    b_in2 = b_in.reshape(1, -1)

    def group_cols(a, g):
        part = lambda p: a[:, n_rnn + (p * N_GROUPS + g) * ATTN_WIDTH:
                           n_rnn + (p * N_GROUPS + g + 1) * ATTN_WIDTH]
        return jnp.concatenate([part(0), part(1), part(2)], axis=1)

    ws = [group_cols(w_in_b, g) for g in range(N_GROUPS)]
    b_qkv = jnp.concatenate([group_cols(b_in2, g) for g in range(N_GROUPS)], axis=1)
    n_e, d_in, d_ff = w_gate.shape
    h, *qkv, w_gate_b = _proj_qkv(x2d, gmix, ws, b_qkv, cos_t, sin_t, B,
                                  w_gate.reshape(n_e * d_in, d_ff))
    xrgr = _proj_f32(h, w_in_b[:, :n_rnn], b_in2[:, :n_rnn])

    nblk = D_RNN // RNN_BW
    gate_w = (0.5 * jnp.transpose(rg_w, (2, 3, 0, 1, 4))).reshape(nblk, RNN_BW, 4 * RNN_BW).astype(BF16)
    gate_b = 0.5 * jnp.transpose(rg_b.reshape(2, 2, nblk, RNN_BW), (2, 0, 1, 3)).reshape(nblk, 1, 4 * RNN_BW)
    lam = jnp.transpose(rg_lambda.reshape(2, nblk, RNN_BW), (1, 0, 2))
    y_rnn = _rglru(xrgr.reshape(B, S, n_rnn), conv_w, conv_b.reshape(1, D_RNN), gate_w, gate_b, lam)

    outs, lses = [], []
    for g, (_, dilation) in enumerate(ATTN_GROUPS):
        qkv_g = qkv[g].reshape(B, 1, S, 3 * ATTN_WIDTH) if dilation == 1 else qkv[g]
        o, lse = _attention_group(qkv_g)
        if dilation == 1:
            o, lse = o.reshape(T, ATTN_WIDTH), lse.reshape(T, LANES)
        outs.append(o)
        lses.append(lse)

    x2, h2, aff_t, w_up_b, w_down_b = _merge(
        x2d, y_rnn.reshape(T, D_RNN), outs, lses, gmix,
        w_in_b[:, n_rnn + n_qkv:], b_in2[:, n_rnn + n_qkv:],
        p_rnn.astype(BF16), p_attn.astype(BF16), w_out.astype(BF16),
        norm_ffn.reshape(1, D), w_router.astype(BF16), b_router.reshape(1, -1), B,
        w_up.reshape(n_e * d_in, d_ff), w_down.reshape(n_e * d_ff, d_in))

    E = w_router.shape[1]
    cap = CAPACITY_FACTOR * S // E
    pos4, off4 = _route(aff_t, cap)
    pos = pos4.reshape(B, E, S)
    blk_off = off4[:, :, :, 0]
    offs = jnp.concatenate([blk_off, jnp.full((B, E, 1), cap, jnp.int32)], axis=2).reshape(-1)
    xg = _gather(offs, pos, h2.reshape(B, S, D + LANES), cap)
    eo = _ffn(xg, w_gate_b.reshape(n_e, d_in, d_ff), w_up_b.reshape(n_e, d_in, d_ff),
              w_down_b.reshape(n_e, d_ff, d_in))
    return _combine(offs, x2.reshape(B, S, D), pos, eo, norm_out.reshape(1, D), cap)


def kernel(x, norm_mix, w_in, b_in, conv_w, conv_b, rg_w, rg_b, rg_lambda, p_rnn, p_attn, w_out,
           norm_ffn, w_router, b_router, w_gate, w_up, w_down, norm_final):
    depth = w_in.shape[0]
    assert depth == 1, "final rmsnorm is fused into the single layer's combine step"
    cos_t, sin_t = _rope_tables(x.shape[1])
    return _layer(x, norm_mix[0], w_in[0], b_in[0], conv_w[0], conv_b[0], rg_w[0], rg_b[0],
                  rg_lambda[0], p_rnn[0], p_attn[0], w_out[0], norm_ffn[0], w_router[0],
                  b_router[0], w_gate[0], w_up[0], w_down[0], norm_final, cos_t, sin_t)
```

```python
import functools

import jax
import jax.numpy as jnp
from jax import lax
from jax.experimental import pallas as pl
from jax.experimental.pallas import tpu as pltpu

F32 = jnp.float32
BF16 = jnp.bfloat16

D_RNN = 1024
RNN_BW = 128
CONV_WIDTH = 4
LRU_C = 8.0
HEAD_DIM = 128
HEADS = 4
ATTN_WIDTH = HEADS * HEAD_DIM
ATTN_GROUPS = ((128, 1), (512, 4), (2048, 16))
N_GROUPS = 3
ROPE_THETA = 500000.0
ROPE_DIM = HEAD_DIM // 4
RADIUS = 64
NEG_INF = -1e30
CAPACITY_FACTOR = 2
RMS_EPS = 1e-6

LANES = 128
SUBLANES = 8
BF16_ROWS = 16
VMEM_LIMIT = 56 * 1024 * 1024
F32_MIN_NORMAL = 1.1754944e-38
LOG2_E = 1.4426950408889634
LN_2 = 0.6931471805599453

PROJ_TILE = 1024
QKV_CHUNK = 512
ROW_TILE = 512
SCAN_SEG = 128
SCAN_CHUNK = 512
ATT_QBLK = 2048
ATT_SUB = 128
TOK_BLK = 256
GATHER_TOK = 2048
COMBINE_TOK = 512
SLOT_SHIFT = 6
SLOT_WIN = 1 << SLOT_SHIFT
EXPERT_SPLIT = 2
FFN_CHUNK = 512
SEARCH_STEPS = 17


def _cparams(sem):
    return pltpu.CompilerParams(dimension_semantics=sem, vmem_limit_bytes=VMEM_LIMIT)


def _rms(x, g):
    ms = jnp.mean(x * x, axis=-1, keepdims=True)
    return x * lax.rsqrt(ms + RMS_EPS) * g


def _pitch(d):
    return d if d <= 4 else d + 4


def _deinterleave(t, stage, d):
    n = t.shape[0]
    p = _pitch(d)
    if p == d:
        stage[0:n, :] = t
    else:
        for j in range(n // d):
            stage[j * p:j * p + d, :] = t[j * d:(j + 1) * d, :]
    return [stage[pl.ds(r, n // d, stride=p), :] for r in range(d)]


def _interleave(parts, stage, d):
    m = parts[0].shape[0]
    p = _pitch(d)
    for r in range(d):
        stage[pl.ds(r, m, stride=p), :] = parts[r]
    if p == d:
        return stage[0:m * d, :]
    return jnp.concatenate([stage[j * p:j * p + d, :] for j in range(m)], axis=0)


def _proj_f32_kernel(h_ref, w_ref, b_ref, o_ref, *, n_chunk):
    hb = h_ref[...]
    for c in range(o_ref.shape[1] // n_chunk):
        sl = slice(c * n_chunk, (c + 1) * n_chunk)
        o_ref[:, sl] = jnp.dot(hb, w_ref[:, sl], preferred_element_type=F32) + b_ref[:, sl]


def _proj_f32(h, w, b):
    T, D = h.shape
    N = w.shape[1]
    return pl.pallas_call(
        functools.partial(_proj_f32_kernel, n_chunk=512),
        out_shape=jax.ShapeDtypeStruct((T, N), F32),
        grid=(T // PROJ_TILE,),
        in_specs=[
            pl.BlockSpec((PROJ_TILE, D), lambda i: (i, 0)),
            pl.BlockSpec((D, N), lambda i: (0, 0)),
            pl.BlockSpec((1, N), lambda i: (0, 0)),
        ],
        out_specs=pl.BlockSpec((PROJ_TILE, N), lambda i: (i, 0)),
        compiler_params=_cparams(("parallel",)),
        name="proj_rnn",
    )(h, w, b)


def _proj_qkv_kernel(x_ref, g_ref, w0_ref, w1_ref, w2_ref, b_ref, cos_ref, sin_ref,
                     ca_ref, h_ref, o0_ref, o1_ref, o2_ref, cao_ref, stage):
    cao_ref[...] = ca_ref[...].astype(BF16)
    hb =_rms(x_ref[...], g_ref[...]).astype(BF16)
    h_ref[...] = hb
    cosv = cos_ref[...]
    sinv = sin_ref[...]
    lane = lax.broadcasted_iota(jnp.int32, (1, HEAD_DIM), 1)
    first_half = lane < (ROPE_DIM // 2)
    gw = 3 * ATTN_WIDTH
    for g, (w_ref, o_ref) in enumerate(((w0_ref, o0_ref), (w1_ref, o1_ref), (w2_ref, o2_ref))):
        d = ATTN_GROUPS[g][1]
        for c in range(gw // QKV_CHUNK):
            sl = slice(c * QKV_CHUNK, (c + 1) * QKV_CHUNK)
            t2 = jnp.dot(hb, w_ref[:, sl], preferred_element_type=F32)
            t2 = t2 + b_ref[:, g * gw + c * QKV_CHUNK:g * gw + (c + 1) * QKV_CHUNK]
            for hh in range(QKV_CHUNK // HEAD_DIM):
                t = t2[:, hh * HEAD_DIM:(hh + 1) * HEAD_DIM]
                col = c * QKV_CHUNK + hh * HEAD_DIM
                if col < 2 * ATTN_WIDTH:
                    up = pltpu.roll(t, HEAD_DIM - ROPE_DIM // 2, 1)
                    down = pltpu.roll(t, ROPE_DIM // 2, 1)
                    t = t * cosv + jnp.where(first_half, up, down) * sinv
                cs = slice(col, col + HEAD_DIM)
                if d == 1:
                    o_ref[:, cs] = t.astype(BF16)
                else:
                    for r, part in enumerate(_deinterleave(t, stage, d)):
                        o_ref[0, r, :, cs] = part.astype(BF16)


def _cast_slab_spec(a, nsteps):
    rows, cols = a.shape
    assert rows % (nsteps * BF16_ROWS) == 0, (a.shape, nsteps)
    return pl.BlockSpec((rows // nsteps, cols), lambda i: (i, 0))


def _proj_qkv(x2d, g, ws, b, cos_t, sin_t, batch, cast_a):
    T, D = x2d.shape
    S = T // batch
    nseq = S // PROJ_TILE
    nsteps = T // PROJ_TILE
    gw = 3 * ATTN_WIDTH
    d1, d2 = ATTN_GROUPS[1][1], ATTN_GROUPS[2][1]
    full = lambda a: pl.BlockSpec(a.shape, lambda i: (0,) * a.ndim)
    dil_spec = lambda d: pl.BlockSpec((1, d, PROJ_TILE // d, gw),
                                      lambda i: (i // nseq, 0, i % nseq, 0))
    return pl.pallas_call(
        _proj_qkv_kernel,
        out_shape=(jax.ShapeDtypeStruct((T, D), BF16),
                   jax.ShapeDtypeStruct((T, gw), BF16),
                   jax.ShapeDtypeStruct((batch, d1, S // d1, gw), BF16),
                   jax.ShapeDtypeStruct((batch, d2, S // d2, gw), BF16),
                   jax.ShapeDtypeStruct(cast_a.shape, BF16)),
        grid=(nsteps,),
        in_specs=[
            pl.BlockSpec((PROJ_TILE, D), lambda i: (i, 0)),
            full(g), full(ws[0]), full(ws[1]), full(ws[2]), full(b),
            pl.BlockSpec((PROJ_TILE, HEAD_DIM), lambda i: (i % nseq, 0)),
            pl.BlockSpec((PROJ_TILE, HEAD_DIM), lambda i: (i % nseq, 0)),
            _cast_slab_spec(cast_a, nsteps),
        ],
        out_specs=(pl.BlockSpec((PROJ_TILE, D), lambda i: (i, 0)),
                   pl.BlockSpec((PROJ_TILE, gw), lambda i: (i, 0)), dil_spec(d1), dil_spec(d2),
                   _cast_slab_spec(cast_a, nsteps)),
        scratch_shapes=[pltpu.VMEM((PROJ_TILE // d2 * _pitch(d2), LANES), F32)],
        compiler_params=_cparams(("parallel",)),
        name="proj_qkv",
    )(x2d, g, ws[0], ws[1], ws[2], b, cos_t, sin_t, cast_a)


def _rope_tables(seq):
    pos = jnp.arange(seq, dtype=F32)
    inv = ROPE_THETA ** (-jnp.arange(0, ROPE_DIM, 2, dtype=F32) / ROPE_DIM)
    ang = pos[:, None] * inv[None, :]
    cos, sin = jnp.cos(ang), jnp.sin(ang)
    pad = HEAD_DIM - ROPE_DIM
    cos_t = jnp.concatenate([cos, cos, jnp.ones((seq, pad), F32)], axis=1)
    sin_t = jnp.concatenate([-sin, sin, jnp.zeros((seq, pad), F32)], axis=1)
    return cos_t, sin_t


def _sigmoid(x):
    return 0.5 * jnp.tanh(0.5 * x) + 0.5


def _rglru_kernel(xr_ref, gr_ref, cw_ref, cb_ref, gw_ref, gb_ref, lam_ref, o_ref,
                  xpad, af, uf, ab, ub, cf, cbk, *, seq):
    nseg = seq // SCAN_SEG
    nchunk = seq // SCAN_CHUNK
    seg_per_chunk = SCAN_CHUNK // SCAN_SEG
    pitch = nseg + 4

    cw = cw_ref[...]
    cbias = cb_ref[...]
    gbias = gb_ref[...]
    z = -lam_ref[...]
    sp = jnp.maximum(z, 0.0) + jnp.log1p(jnp.exp(-jnp.abs(z)))
    half_neg_c_sp = (-0.5 * LRU_C) * sp

    zero_rows = jnp.zeros((SUBLANES, LANES), F32)
    xpad[0:SUBLANES, :] = zero_rows
    xpad[seq + SUBLANES:seq + 2 * SUBLANES, :] = zero_rows

    def copy_in(c, carry):
        t0 = pl.multiple_of(c * SCAN_CHUNK, SCAN_CHUNK)
        xpad[pl.ds(t0 + SUBLANES, SCAN_CHUNK), :] = xr_ref[0, pl.ds(t0, SCAN_CHUNK), :]
        return carry

    lax.fori_loop(0, nchunk, copy_in, 0)

    def stage1(c, carry):
        t0 = c * SCAN_CHUNK
        xc = cbias
        for tap in range(CONV_WIDTH):
            lo = t0 + SUBLANES - CONV_WIDTH // 2 + tap
            xc = xc + cw[tap:tap + 1, :] * xpad[pl.ds(lo, SCAN_CHUNK), :]
        th = jnp.tanh(jnp.dot(xc.astype(BF16), gw_ref[...], preferred_element_type=F32) + gbias)
        xc_half = 0.5 * xc
        for d, (a_s, u_s) in enumerate(((af, uf), (ab, ub))):
            t_r = th[:, (2 * d) * LANES:(2 * d + 1) * LANES]
            t_i = th[:, (2 * d + 1) * LANES:(2 * d + 2) * LANES]
            half_k = half_neg_c_sp[d:d + 1, :]
            log_a = t_r * half_k + half_k
            a = jnp.exp(log_a)
            v = jnp.tanh(log_a) * (-1.0 - a * a)
            u = (xc_half * (v * lax.rsqrt(jnp.maximum(v, F32_MIN_NORMAL)))) * (t_i + 1.0)
            for k in range(seg_per_chunk):
                seg = c * seg_per_chunk + k
                a_s[pl.ds(seg, SCAN_SEG, stride=pitch), :] = a[k * SCAN_SEG:(k + 1) * SCAN_SEG, :]
                u_s[pl.ds(seg, SCAN_SEG, stride=pitch), :] = u[k * SCAN_SEG:(k + 1) * SCAN_SEG, :]
        return carry

    lax.fori_loop(0, nchunk, stage1, 0, unroll=4)

    def scan_dir(a_s, u_s, reverse):
        def body(it, carry):
            i = (SCAN_SEG - 1 - it) if reverse else it
            h, cum = carry
            rows = pl.ds(i * pitch, nseg)
            a = a_s[rows, :]
            u = u_s[rows, :]
            h = a * h + u
            cum = a * cum
            u_s[rows, :] = h
            a_s[rows, :] = cum
            return h, cum

        return lax.fori_loop(0, SCAN_SEG, body,
                             (jnp.zeros((nseg, LANES), F32), jnp.ones((nseg, LANES), F32)))

    end_h, end_a = scan_dir(af, uf, False)
    beg_h, beg_a = scan_dir(ab, ub, True)

    c = jnp.zeros((1, LANES), F32)
    for s in range(nseg):
        cf[s:s + 1, :] = c
        c = end_h[s:s + 1, :] + end_a[s:s + 1, :] * c
    c = jnp.zeros((1, LANES), F32)
    for s in range(nseg - 1, -1, -1):
        cbk[s:s + 1, :] = c
        c = beg_h[s:s + 1, :] + beg_a[s:s + 1, :] * c

    def finish(s, carry):
        t0 = pl.multiple_of(s * SCAN_SEG, SCAN_SEG)
        rows = pl.ds(s, SCAN_SEG, stride=pitch)
        hf = uf[rows, :] + af[rows, :] * cf[pl.ds(s, 1), :]
        hb = ub[rows, :] + ab[rows, :] * cbk[pl.ds(s, 1), :]
        g = jax.nn.gelu(gr_ref[0, pl.ds(t0, SCAN_SEG), :])
        o_ref[0, pl.ds(t0, SCAN_SEG), :] = (g * (hf + hb)).astype(BF16)
        return carry

    lax.fori_loop(0, nseg, finish, 0)


def _rglru(xrgr, conv_w, conv_b, gate_w, gate_b, lam):
    B, S, _ = xrgr.shape
    nblk = D_RNN // RNN_BW
    nseg = S // SCAN_SEG
    scr = pltpu.VMEM((SCAN_SEG * (nseg + 4), LANES), F32)
    return pl.pallas_call(
        functools.partial(_rglru_kernel, seq=S),
        out_shape=jax.ShapeDtypeStruct((B, S, D_RNN), BF16),
        grid=(B, nblk),
        in_specs=[
            pl.BlockSpec((1, S, RNN_BW), lambda b, c: (b, 0, c)),
            pl.BlockSpec((1, S, RNN_BW), lambda b, c: (b, 0, nblk + c)),
            pl.BlockSpec((CONV_WIDTH, RNN_BW), lambda b, c: (0, c)),
            pl.BlockSpec((1, RNN_BW), lambda b, c: (0, c)),
            pl.BlockSpec((None, RNN_BW, 4 * RNN_BW), lambda b, c: (c, 0, 0)),
            pl.BlockSpec((None, 1, 4 * RNN_BW), lambda b, c: (c, 0, 0)),
            pl.BlockSpec((None, 2, RNN_BW), lambda b, c: (c, 0, 0)),
        ],
        out_specs=pl.BlockSpec((1, S, RNN_BW), lambda b, c: (b, 0, c)),
        scratch_shapes=[pltpu.VMEM((S + 2 * SUBLANES, LANES), F32), scr, scr, scr, scr,
                        pltpu.VMEM((nseg, LANES), F32), pltpu.VMEM((nseg, LANES), F32)],
        compiler_params=_cparams(("parallel", "parallel")),
        name="rglru",
    )(xrgr, xrgr, conv_w, conv_b, gate_w, gate_b, lam)


def _attn_kernel(q_ref, k_ref, kp_ref, kn_ref, v_ref, vp_ref, vn_ref, o_ref, st_ref,
                 kext, vext, *, qblk, length):
    j = pl.program_id(2)
    nkey = ATT_SUB + 2 * RADIUS
    qi = lax.broadcasted_iota(jnp.int32, (ATT_SUB, nkey), 0)
    ki = lax.broadcasted_iota(jnp.int32, (ATT_SUB, nkey), 1)
    band_bias = jnp.where(jnp.abs(ki - RADIUS - qi) <= RADIUS, 0.0, NEG_INF)
    krow = lax.broadcasted_iota(jnp.int32, (1, nkey), 1)
    lane = lax.broadcasted_iota(jnp.int32, (ATT_SUB, LANES), 1)
    scale2 = HEAD_DIM ** -0.5 * LOG2_E

    for rr in range(q_ref.shape[1]):
        kx, vx = kext.at[rr], vext.at[rr]
        kx[0:RADIUS, :] = kp_ref[0, rr]
        kx[RADIUS:RADIUS + qblk, :] = k_ref[0, rr]
        kx[RADIUS + qblk:RADIUS + qblk + RADIUS, :] = kn_ref[0, rr]
        for h in range(HEADS):
            src = slice(h * HEAD_DIM, (h + 1) * HEAD_DIM)
            dst = slice(2 * h * HEAD_DIM, (2 * h + 1) * HEAD_DIM)
            vx[0:RADIUS, dst] = vp_ref[0, rr, :, src]
            vx[RADIUS:RADIUS + qblk, dst] = v_ref[0, rr, :, src]
            vx[RADIUS + qblk:RADIUS + qblk + RADIUS, dst] = vn_ref[0, rr, :, src]
            vx[:, (2 * h + 1) * HEAD_DIM:(2 * h + 2) * HEAD_DIM] = jnp.ones(
                (qblk + 2 * RADIUS, HEAD_DIM), BF16)

        for i in range(qblk // ATT_SUB):
            q0 = i * ATT_SUB
            kglob = j * qblk + q0 - RADIUS + krow
            edge_bias = jnp.where((kglob >= 0) & (kglob < length), 0.0, NEG_INF)
            bias = band_bias + edge_bias
            st_tile = jnp.ones((ATT_SUB, LANES), F32)
            for h in range(HEADS):
                hs = slice(h * HEAD_DIM, (h + 1) * HEAD_DIM)
                q = q_ref[0, rr, q0:q0 + ATT_SUB, hs]
                ks = kx[q0:q0 + nkey, hs]
                vs = vx[q0:q0 + nkey, 2 * h * HEAD_DIM:(2 * h + 2) * HEAD_DIM]
                s = lax.dot_general(q, ks, (((1,), (1,)), ((), ())),
                                    preferred_element_type=F32) * scale2 + bias
                m = jnp.max(s, axis=-1, keepdims=True)
                p = jnp.exp2(s - m)
                pv = jnp.dot(p.astype(BF16), vs, preferred_element_type=F32)
                l = pv[:, HEAD_DIM:2 * HEAD_DIM]
                o_ref[0, rr, q0:q0 + ATT_SUB, hs] = pv[:, 0:HEAD_DIM]
                st_tile = jnp.where(lane == h, m * LN_2,
                                    jnp.where(lane == HEADS + h, l, st_tile))
            st_ref[0, rr, q0:q0 + ATT_SUB, :] = st_tile


def _attention_group(qkv_g):
    B, d, L, _ = qkv_g.shape
    qblk = min(L, ATT_QBLK)
    nblk = L // qblk
    rpb = min(d, max(1, ATT_QBLK // L))
    halo_per_blk = qblk // RADIUS
    last_halo = L // RADIUS - 1

    main = lambda part: pl.BlockSpec((1, rpb, qblk, ATTN_WIDTH), lambda b, r, j: (b, r, j, part))
    prev = lambda part: pl.BlockSpec(
        (1, rpb, RADIUS, ATTN_WIDTH),
        lambda b, r, j: (b, r, jnp.maximum(j * halo_per_blk - 1, 0), part))
    nxt = lambda part: pl.BlockSpec(
        (1, rpb, RADIUS, ATTN_WIDTH),
        lambda b, r, j: (b, r, jnp.minimum((j + 1) * halo_per_blk, last_halo), part))
    return pl.pallas_call(
        functools.partial(_attn_kernel, qblk=qblk, length=L),
        out_shape=(jax.ShapeDtypeStruct((B, d, L, ATTN_WIDTH), F32),
                   jax.ShapeDtypeStruct((B, d, L, LANES), F32)),
        grid=(B, d // rpb, nblk),
        in_specs=[main(0), main(1), prev(1), nxt(1), main(2), prev(2), nxt(2)],
        out_specs=(pl.BlockSpec((1, rpb, qblk, ATTN_WIDTH), lambda b, r, j: (b, r, j, 0)),
                   pl.BlockSpec((1, rpb, qblk, LANES), lambda b, r, j: (b, r, j, 0))),
        scratch_shapes=[pltpu.VMEM((rpb, qblk + 2 * RADIUS, ATTN_WIDTH), BF16),
                        pltpu.VMEM((rpb, qblk + 2 * RADIUS, 2 * ATTN_WIDTH), BF16)],
        compiler_params=_cparams(("parallel", "parallel", "parallel")),
        name=f"attn_d{d}",
    )(qkv_g, qkv_g, qkv_g, qkv_g, qkv_g, qkv_g, qkv_g)


def _merge_kernel(x_ref, yr_ref, o0_ref, o1_ref, o2_ref, l0_ref, l1_ref, l2_ref,
                  gmix_ref, wg_ref, bg_ref, prnn_ref, pattn_ref, wout_ref,
                  gffn_ref, wrt_ref, brt_ref, cb_ref, cc_ref,
                  x2_ref, h2_ref, afft_ref, cbo_ref, cco_ref, yattn, stage):
    cbo_ref[...] = cb_ref[...].astype(BF16)
    cco_ref[...] = cc_ref[...].astype(BF16)
    d = x_ref.shape[1]
    x = x_ref[...]
    hb = _rms(x, gmix_ref[...]).astype(BF16)
    branch_a = jnp.dot(yr_ref[...], prnn_ref[...], preferred_element_type=F32)
    ga = jnp.dot(hb, wg_ref[:, 0:d], preferred_element_type=F32) + bg_ref[:, 0:d]
    gb = jnp.dot(hb, wg_ref[:, d:2 * d], preferred_element_type=F32) + bg_ref[:, d:2 * d]

    def natural(ref, g, cs):
        dil = ATTN_GROUPS[g][1]
        if dil == 1:
            return ref[:, cs]
        return _interleave([ref[0, r, :, cs] for r in range(dil)], stage, dil)

    o_refs = (o0_ref, o1_ref, o2_ref)
    all_lanes = slice(0, LANES)
    stats = [natural(r, g, all_lanes) for g, r in enumerate((l0_ref, l1_ref, l2_ref))]
    head_lane = lax.broadcasted_iota(jnp.int32, stats[0].shape, 1) < HEADS
    sums = [jnp.where(head_lane, pltpu.roll(st, LANES - HEADS, 1), 1.0) for st in stats]
    lses = [st + jnp.log(l) for st, l in zip(stats, sums)]
    m = jnp.maximum(jnp.maximum(lses[0], lses[1]), lses[2])
    ws = [jnp.exp(l - m) for l in lses]
    den = ws[0] + ws[1] + ws[2]
    coef = [w / (den * l) for w, l in zip(ws, sums)]
    for h in range(HEADS):
        hs = slice(h * HEAD_DIM, (h + 1) * HEAD_DIM)
        acc = coef[0][:, h:h + 1] * natural(o_refs[0], 0, hs)
        acc = acc + coef[1][:, h:h + 1] * natural(o_refs[1], 1, hs)
        acc = acc + coef[2][:, h:h + 1] * natural(o_refs[2], 2, hs)
        yattn[:, hs] = acc.astype(BF16)

    branch_b = jnp.dot(yattn[...], pattn_ref[...], preferred_element_type=F32)
    merged = _sigmoid(ga) * branch_a + _sigmoid(gb) * branch_b
    x2 = x + jnp.dot(merged.astype(BF16), wout_ref[...], preferred_element_type=F32)
    x2_ref[...] = x2

    h2 = _rms(x2, gffn_ref[...]).astype(BF16)
    h2_ref[...] = h2
    logits_t = lax.dot_general(wrt_ref[...], h2, (((1,), (1,)), ((), ())),
                               preferred_element_type=F32) + brt_ref[...]
    mxt = jnp.max(logits_t, axis=0, keepdims=True)
    et = jnp.exp(logits_t - mxt)
    afft_ref[0] = et / jnp.sum(et, axis=0, keepdims=True)


def _merge(x2d, y_rnn, outs, lses, gmix, wg, bg, p_rnn, p_attn, w_out, gffn, wr, br, batch,
           cast_b, cast_c):
    T, D = x2d.shape
    S = T // batch
    nseq = S // ROW_TILE
    nsteps = T // ROW_TILE
    E = wr.shape[1]
    row = lambda w: pl.BlockSpec((ROW_TILE, w), lambda i: (i, 0))
    full = lambda a: pl.BlockSpec(a.shape, lambda i: (0,) * a.ndim)

    def attn_spec(g, w):
        dil = ATTN_GROUPS[g][1]
        if dil == 1:
            return row(w)
        return pl.BlockSpec((1, dil, ROW_TILE // dil, w), lambda i: (i // nseq, 0, i % nseq, 0))

    wrt = wr.T
    brt = br.reshape(E, 1)
    consts = [gmix, wg, bg, p_rnn, p_attn, w_out, gffn, wrt, brt]
    args = [x2d, y_rnn, *outs, *lses, *consts, cast_b, cast_c]
    in_specs = ([row(D), row(D_RNN)] + [attn_spec(g, ATTN_WIDTH) for g in range(N_GROUPS)]
                + [attn_spec(g, LANES) for g in range(N_GROUPS)] + [full(a) for a in consts]
                + [_cast_slab_spec(cast_b, nsteps), _cast_slab_spec(cast_c, nsteps)])
    dmax = max(dil for _, dil in ATTN_GROUPS)
    return pl.pallas_call(
        _merge_kernel,
        out_shape=(jax.ShapeDtypeStruct((T, D), F32),
                   jax.ShapeDtypeStruct((T, D), BF16),
                   jax.ShapeDtypeStruct((batch, E, S), F32),
                   jax.ShapeDtypeStruct(cast_b.shape, BF16),
                   jax.ShapeDtypeStruct(cast_c.shape, BF16)),
        grid=(nsteps,),
        in_specs=in_specs,
        out_specs=(row(D), row(D),
                   pl.BlockSpec((1, E, ROW_TILE), lambda i: (i // nseq, 0, i % nseq)),
                   _cast_slab_spec(cast_b, nsteps), _cast_slab_spec(cast_c, nsteps)),
        scratch_shapes=[pltpu.VMEM((ROW_TILE, ATTN_WIDTH), BF16),
                        pltpu.VMEM((ROW_TILE // dmax * _pitch(dmax), LANES), F32)],
        compiler_params=_cparams(("parallel",)),
        name="merge",
    )(*args)


def _route_kernel(aff_ref, pos_ref, off_ref, *, cap):
    a = aff_ref[0]
    E, R, _ = a.shape
    bits = pltpu.bitcast(a, jnp.int32)

    def count(mask):
        c = jnp.sum(mask.astype(F32), axis=1, keepdims=True)
        return jnp.sum(c, axis=2, keepdims=True)

    def search(_, carry):
        lo, hi = carry
        q = (hi - lo + 3) >> 2
        m1 = jnp.minimum(lo + q, hi)
        m2 = jnp.minimum(m1 + q, hi)
        m3 = jnp.minimum(m2 + q, hi)
        ok1 = count(bits >= m1) >= cap
        ok2 = count(bits >= m2) >= cap
        ok3 = count(bits >= m3) >= cap
        new_lo = jnp.where(ok3, m3, jnp.where(ok2, m2, jnp.where(ok1, m1, lo)))
        new_hi = jnp.where(ok3, hi, jnp.where(ok2, m3 - 1, jnp.where(ok1, m2 - 1, m1 - 1)))
        return new_lo, new_hi

    lo0 = jnp.zeros((E, 1, 1), jnp.int32)
    hi0 = jnp.full((E, 1, 1), 0x7F7FFFFF, jnp.int32)
    thr, _ = lax.fori_loop(0, SEARCH_STEPS, search, (lo0, hi0))

    gt = bits > thr
    eq = bits == thr
    need = cap - count(gt)

    ki = lax.broadcasted_iota(jnp.int32, (LANES, LANES), 0)
    li = lax.broadcasted_iota(jnp.int32, (LANES, LANES), 1)
    upper = (ki <= li).astype(BF16)
    allones = jnp.ones((LANES, LANES), BF16)
    ri = lax.broadcasted_iota(jnp.int32, (R, R), 0)
    ci = lax.broadcasted_iota(jnp.int32, (R, R), 1)
    strict = (ci < ri).astype(BF16)

    def prefix(mask):
        mb = mask.astype(BF16).reshape(E * R, LANES)
        within = jnp.dot(mb, upper, preferred_element_type=F32)
        totals = jnp.dot(mb, allones, preferred_element_type=F32).astype(BF16)
        offs = []
        for e in range(E):
            offs.append(jnp.dot(strict, totals[e * R:(e + 1) * R, :],
                                preferred_element_type=F32))
        rowoff = jnp.concatenate(offs, axis=0)
        excl = within - mb.astype(F32) + rowoff
        return excl.reshape(E, R, LANES), totals

    eq_rank, _ = prefix(eq)
    sel = gt | (eq & (eq_rank < need.astype(F32)))
    slot, totals = prefix(sel)
    pos_ref[0] = jnp.where(sel, slot.astype(jnp.int32), -1)

    nblk = off_ref.shape[2]
    bi = lax.broadcasted_iota(jnp.int32, (nblk, R), 0)
    rj = lax.broadcasted_iota(jnp.int32, (nblk, R), 1)
    before_blk = (rj < bi * (R // nblk)).astype(BF16)
    blk = [jnp.dot(before_blk, totals[e * R:(e + 1) * R, :], preferred_element_type=F32)
           for e in range(E)]
    off_ref[0] = jnp.concatenate(blk, axis=0).reshape(E, nblk, LANES).astype(jnp.int32)


def _route(aff_t, cap):
    B, E, S = aff_t.shape
    R = S // LANES
    ntb = S // TOK_BLK
    a4 = aff_t.reshape(B, E, R, LANES)
    blk = pl.BlockSpec((1, E, R, LANES), lambda b: (b, 0, 0, 0))
    pos, off = pl.pallas_call(
        functools.partial(_route_kernel, cap=cap),
        out_shape=(jax.ShapeDtypeStruct((B, E, R, LANES), jnp.int32),
                   jax.ShapeDtypeStruct((B, E, ntb, LANES), jnp.int32)),
        grid=(B,),
        in_specs=[blk],
        out_specs=(blk, pl.BlockSpec((1, E, ntb, LANES), lambda b: (b, 0, 0, 0))),
        compiler_params=_cparams(("parallel",)),
        name="route",
    )(a4)
    return pos, off


def _slot_windows(off_ref, base, n_exp, ntb, tb):
    starts = []
    rounds = jnp.int32(0)
    for e in range(n_exp):
        idx = (base + e) * (ntb + 1) + tb
        start = (off_ref[idx] >> 4) << 4
        starts.append(start)
        rounds = jnp.maximum(rounds, (off_ref[idx + 1] - start + SLOT_WIN - 1) >> SLOT_SHIFT)
    return starts, rounds


def _gather_kernel(off_ref, pos_ref, aff_ref, h_ref, o_ref, g_ref, *, cap, ntb, n_exp, nsub):
    b = pl.program_id(0)
    grp = pl.program_id(1)
    t = pl.program_id(2)

    @pl.when(t == 0)
    def _():
        o_ref[...] = jnp.zeros_like(o_ref)
        g_ref[...] = jnp.zeros_like(g_ref)

    base = (b * pl.num_programs(1) + grp) * n_exp
    ji = lax.broadcasted_iota(jnp.int32, (SLOT_WIN, TOK_BLK), 0)
    for sb in range(nsub):
        toks = slice(sb * TOK_BLK, (sb + 1) * TOK_BLK)
        starts, rounds = _slot_windows(off_ref, base, n_exp, ntb, t * nsub + sb)

        def one_round(k, carry, toks=toks, starts=starts):
            pieces, rows = [], []
            for e in range(n_exp):
                first = starts[e] + k * SLOT_WIN
                row0 = pl.multiple_of(jnp.minimum(first, cap - SLOT_WIN), BF16_ROWS)
                slot = ji + row0
                hit = (slot == pos_ref[0, 0, e:e + 1, toks]) & (slot >= first)
                pieces.append(jnp.where(hit, 1.0, 0.0).astype(BF16))
                rows.append(row0)
                gate = jnp.sum(jnp.where(hit, aff_ref[0, 0, e:e + 1, toks], 0.0),
                               axis=1, keepdims=True)
                g_ref[0, e, pl.ds(row0, SLOT_WIN), :] += jnp.broadcast_to(gate, (SLOT_WIN, LANES))
            onehot = jnp.concatenate(pieces, axis=0)
            g = jnp.dot(onehot, h_ref[0, toks, :], preferred_element_type=F32)
            for e in range(n_exp):
                o_ref[0, e, pl.ds(rows[e], SLOT_WIN), :] += (
                    g[e * SLOT_WIN:(e + 1) * SLOT_WIN, :].astype(BF16))
            return carry

        one_round(0, 0)
        lax.fori_loop(1, rounds, one_round, 0)


def _gather(offs, pos, aff_t, h2, cap):
    B, E, S = pos.shape
    D = h2.shape[-1]
    ntb = S // TOK_BLK
    n_exp = E // EXPERT_SPLIT
    nsub = GATHER_TOK // TOK_BLK
    pos4 = pos.reshape(B, EXPERT_SPLIT, n_exp, S)
    aff4 = aff_t.reshape(B, EXPERT_SPLIT, n_exp, S)
    tok_rows = pl.BlockSpec((1, 1, n_exp, GATHER_TOK), lambda b, g, t, off: (b, g, 0, t))
    return pl.pallas_call(
        functools.partial(_gather_kernel, cap=cap, ntb=ntb, n_exp=n_exp, nsub=nsub),
        out_shape=(jax.ShapeDtypeStruct((B, E, cap, D), BF16),
                   jax.ShapeDtypeStruct((B, E, cap, LANES), F32)),
        grid_spec=pltpu.PrefetchScalarGridSpec(
            num_scalar_prefetch=1,
            grid=(B, EXPERT_SPLIT, S // GATHER_TOK),
            in_specs=[
                tok_rows, tok_rows,
                pl.BlockSpec((1, GATHER_TOK, D), lambda b, g, t, off: (b, t, 0)),
            ],
            out_specs=(pl.BlockSpec((1, n_exp, cap, D), lambda b, g, t, off: (b, g, 0, 0)),
                       pl.BlockSpec((1, n_exp, cap, LANES), lambda b, g, t, off: (b, g, 0, 0))),
        ),
        compiler_params=_cparams(("parallel", "parallel", "arbitrary")),
        name="gather",
    )(offs, pos4, aff4, h2)


def _ffn_kernel(x_ref, gate_ref, wg_ref, wu_ref, wd_ref, o_ref, acc, *, f_chunk):
    x = x_ref[0, 0]
    gate = gate_ref[0, 0, :, 0:1]
    nf = wg_ref.shape[2] // f_chunk
    for f in range(nf):
        fs = slice(f * f_chunk, (f + 1) * f_chunk)
        g = jnp.dot(x, wg_ref[0, :, fs], preferred_element_type=F32)
        u = jnp.dot(x, wu_ref[0, :, fs], preferred_element_type=F32)
        hid = (g * _sigmoid(g) * u).astype(BF16)
        part = jnp.dot(hid, wd_ref[0, fs, :], preferred_element_type=F32)
        if f == 0:
            acc[...] = part
        else:
            acc[...] += part
    o_ref[0, 0] = (acc[...] * gate).astype(BF16)


def _ffn(xg, gates, w_gate, w_up, w_down):
    B, E, C, D = xg.shape
    F = w_gate.shape[2]
    assert F % FFN_CHUNK == 0, (F, FFN_CHUNK)
    return pl.pallas_call(
        functools.partial(_ffn_kernel, f_chunk=FFN_CHUNK),
        out_shape=jax.ShapeDtypeStruct((B, E, C, D), BF16),
        grid=(E, B),
        in_specs=[
            pl.BlockSpec((1, 1, C, D), lambda e, b: (b, e, 0, 0)),
            pl.BlockSpec((1, 1, C, LANES), lambda e, b: (b, e, 0, 0)),
            pl.BlockSpec((1, D, F), lambda e, b: (e, 0, 0)),
            pl.BlockSpec((1, D, F), lambda e, b: (e, 0, 0)),
            pl.BlockSpec((1, F, D), lambda e, b: (e, 0, 0)),
        ],
        out_specs=pl.BlockSpec((1, 1, C, D), lambda e, b: (b, e, 0, 0)),
        scratch_shapes=[pltpu.VMEM((C, D), F32)],
        compiler_params=_cparams(("arbitrary", "arbitrary")),
        name="ffn",
    )(xg, gates, w_gate, w_up, w_down)


def _combine_kernel(off_ref, x_ref, pos_ref, eo_ref, g_ref, o_ref, wstack, acc, *, cap, ntb):
    b = pl.program_id(0)
    t = pl.program_id(1)
    E = pos_ref.shape[1]
    nsub = pos_ref.shape[2] // TOK_BLK
    ji = lax.broadcasted_iota(jnp.int32, (SLOT_WIN, TOK_BLK), 0)

    for sb in range(nsub):
        toks = slice(sb * TOK_BLK, (sb + 1) * TOK_BLK)
        starts, rounds = _slot_windows(off_ref, b * E, E, ntb, t * nsub + sb)

        def one_round(k, carry, toks=toks, starts=starts, wst=wstack.at[sb], first_round=False):
            pieces = []
            for e in range(E):
                first = starts[e] + k * SLOT_WIN
                row0 = pl.multiple_of(jnp.minimum(first, cap - SLOT_WIN), BF16_ROWS)
                wst[e * SLOT_WIN:(e + 1) * SLOT_WIN, :] = eo_ref[0, e, pl.ds(row0, SLOT_WIN), :]
                slot = ji + row0
                hit = (slot == pos_ref[0, e:e + 1, toks]) & (slot >= first)
                pieces.append(jnp.where(hit, 1.0, 0.0).astype(BF16))
            onehot_t = jnp.concatenate(pieces, axis=0)
            y = lax.dot_general(onehot_t, wst[...], (((0,), (0,)), ((), ())),
                                preferred_element_type=F32)
            acc[toks, :] = y + (x_ref[0, toks, :] if first_round else acc[toks, :])
            return carry

        one_round(0, 0, first_round=True)
        lax.fori_loop(1, rounds, one_round, 0)
    o_ref[0] = _rms(acc[...], g_ref[...])


def _combine(offs, x2, pos, eo, g_final, cap):
    B, S, D = x2.shape
    E = pos.shape[1]
    ntb = S // TOK_BLK
    return pl.pallas_call(
        functools.partial(_combine_kernel, cap=cap, ntb=ntb),
        out_shape=jax.ShapeDtypeStruct((B, S, D), F32),
        grid_spec=pltpu.PrefetchScalarGridSpec(
            num_scalar_prefetch=1,
            grid=(B, S // COMBINE_TOK),
            in_specs=[
                pl.BlockSpec((1, COMBINE_TOK, D), lambda b, t, off: (b, t, 0)),
                pl.BlockSpec((1, E, COMBINE_TOK), lambda b, t, off: (b, 0, t)),
                pl.BlockSpec((1, E, cap, D), lambda b, t, off: (b, 0, 0, 0),
                             pipeline_mode=pl.Buffered(1)),
                pl.BlockSpec((1, D), lambda b, t, off: (0, 0)),
            ],
            out_specs=pl.BlockSpec((1, COMBINE_TOK, D), lambda b, t, off: (b, t, 0)),
            scratch_shapes=[pltpu.VMEM((COMBINE_TOK // TOK_BLK, E * SLOT_WIN, D), BF16),
                            pltpu.VMEM((COMBINE_TOK, D), F32)],
        ),
        compiler_params=_cparams(("parallel", "arbitrary")),
        name="combine",
    )(offs, x2, pos, eo, g_final)


def _layer(x, norm_mix, w_in, b_in, conv_w, conv_b, rg_w, rg_b, rg_lambda, p_rnn, p_attn,
           w_out, norm_ffn, w_router, b_router, w_gate, w_up, w_down, norm_out, cos_t, sin_t):
    B, S, D = x.shape
    T = B * S
    x2d = x.reshape(T, D)
    n_rnn = 2 * D_RNN
    n_qkv = 3 * N_GROUPS * ATTN_WIDTH
    gmix = norm_mix.reshape(1, D)
    w_in_b = w_in.astype(BF16)
    b_in2 = b_in.reshape(1, -1)

    def group_cols(a, g):
        part = lambda p: a[:, n_rnn + (p * N_GROUPS + g) * ATTN_WIDTH:
                           n_rnn + (p * N_GROUPS + g + 1) * ATTN_WIDTH]
        return jnp.concatenate([part(0), part(1), part(2)], axis=1)

    ws = [group_cols(w_in_b, g) for g in range(N_GROUPS)]
    b_qkv = jnp.concatenate([group_cols(b_in2, g) for g in range(N_GROUPS)], axis=1)
    n_e, d_in, d_ff = w_gate.shape
    h, *qkv, w_gate_b = _proj_qkv(x2d, gmix, ws, b_qkv, cos_t, sin_t, B,
                                  w_gate.reshape(n_e * d_in, d_ff))
    xrgr = _proj_f32(h, w_in_b[:, :n_rnn], b_in2[:, :n_rnn])

    nblk = D_RNN // RNN_BW
    gate_w = (0.5 * jnp.transpose(rg_w, (2, 3, 0, 1, 4))).reshape(nblk, RNN_BW, 4 * RNN_BW).astype(BF16)
    gate_b = 0.5 * jnp.transpose(rg_b.reshape(2, 2, nblk, RNN_BW), (2, 0, 1, 3)).reshape(nblk, 1, 4 * RNN_BW)
    lam = jnp.transpose(rg_lambda.reshape(2, nblk, RNN_BW), (1, 0, 2))
    y_rnn = _rglru(xrgr.reshape(B, S, n_rnn), conv_w, conv_b.reshape(1, D_RNN), gate_w, gate_b, lam)

    outs, lses = [], []
    for g, (_, dilation) in enumerate(ATTN_GROUPS):
        qkv_g = qkv[g].reshape(B, 1, S, 3 * ATTN_WIDTH) if dilation == 1 else qkv[g]
        o, lse = _attention_group(qkv_g)
        if dilation == 1:
            o, lse = o.reshape(T, ATTN_WIDTH), lse.reshape(T, LANES)
        outs.append(o)
        lses.append(lse)

    x2, h2, aff_t, w_up_b, w_down_b = _merge(
        x2d, y_rnn.reshape(T, D_RNN), outs, lses, gmix,
        w_in_b[:, n_rnn + n_qkv:], b_in2[:, n_rnn + n_qkv:],
        p_rnn.astype(BF16), p_attn.astype(BF16), w_out.astype(BF16),
        norm_ffn.reshape(1, D), w_router.astype(BF16), b_router.reshape(1, -1), B,
        w_up.reshape(n_e * d_in, d_ff), w_down.reshape(n_e * d_ff, d_in))

    E = w_router.shape[1]
    cap = CAPACITY_FACTOR * S // E
    pos4, off4 = _route(aff_t, cap)
    pos = pos4.reshape(B, E, S)
    blk_off = off4[:, :, :, 0]
    offs = jnp.concatenate([blk_off, jnp.full((B, E, 1), cap, jnp.int32)], axis=2).reshape(-1)
    xg, gates = _gather(offs, pos, aff_t, h2.reshape(B, S, D), cap)
    eo = _ffn(xg, gates, w_gate_b.reshape(n_e, d_in, d_ff), w_up_b.reshape(n_e, d_in, d_ff),
              w_down_b.reshape(n_e, d_ff, d_in))
    return _combine(offs, x2.reshape(B, S, D), pos, eo, norm_out.reshape(1, D), cap)


def kernel(x, norm_mix, w_in, b_in, conv_w, conv_b, rg_w, rg_b, rg_lambda, p_rnn, p_attn, w_out,
           norm_ffn, w_router, b_router, w_gate, w_up, w_down, norm_final):
    depth = w_in.shape[0]
    assert depth == 1, "final rmsnorm is fused into the single layer's combine step"
    cos_t, sin_t = _rope_tables(x.shape[1])
    return _layer(x, norm_mix[0], w_in[0], b_in[0], conv_w[0], conv_b[0], rg_w[0], rg_b[0],
                  rg_lambda[0], p_rnn[0], p_attn[0], w_out[0], norm_ffn[0], w_router[0],
                  b_router[0], w_gate[0], w_up[0], w_down[0], norm_final, cos_t, sin_t)
```

```python
import functools

import jax
import jax.numpy as jnp
from jax import lax
from jax.experimental import pallas as pl
from jax.experimental.pallas import tpu as pltpu

F32 = jnp.float32
BF16 = jnp.bfloat16

D_RNN = 1024
RNN_BW = 128
CONV_WIDTH = 4
LRU_C = 8.0
HEAD_DIM = 128
HEADS = 4
ATTN_WIDTH = HEADS * HEAD_DIM
ATTN_GROUPS = ((128, 1), (512, 4), (2048, 16))
N_GROUPS = 3
ROPE_THETA = 500000.0
ROPE_DIM = HEAD_DIM // 4
RADIUS = 64
NEG_INF = -1e30
CAPACITY_FACTOR = 2
RMS_EPS = 1e-6

LANES = 128
SUBLANES = 8
BF16_ROWS = 16
VMEM_LIMIT = 56 * 1024 * 1024
F32_MIN_NORMAL = 1.1754944e-38
LOG2_E = 1.4426950408889634
LN_2 = 0.6931471805599453

PROJ_TILE = 1024
QKV_CHUNK = 512
ROW_TILE = 512
SCAN_SEG = 128
SCAN_CHUNK = 512
ATT_QBLK = 2048
ATT_SUB = 128
TOK_BLK = 256
GATHER_TOK = 2048
COMBINE_TOK = 512
SLOT_SHIFT = 6
SLOT_WIN = 1 << SLOT_SHIFT
EXPERT_SPLIT = 2
FFN_CHUNK = 512
SEARCH_STEPS = 17


def _cparams(sem):
    return pltpu.CompilerParams(dimension_semantics=sem, vmem_limit_bytes=VMEM_LIMIT)


def _rms(x, g):
    ms = jnp.mean(x * x, axis=-1, keepdims=True)
    return x * lax.rsqrt(ms + RMS_EPS) * g


def _pitch(d):
    return d if d <= 4 else d + 4


def _deinterleave(t, stage, d):
    n = t.shape[0]
    p = _pitch(d)
    if p == d:
        stage[0:n, :] = t
    else:
        for j in range(n // d):
            stage[j * p:j * p + d, :] = t[j * d:(j + 1) * d, :]
    return [stage[pl.ds(r, n // d, stride=p), :] for r in range(d)]


def _interleave(parts, stage, d):
    m = parts[0].shape[0]
    p = _pitch(d)
    for r in range(d):
        stage[pl.ds(r, m, stride=p), :] = parts[r]
    if p == d:
        return stage[0:m * d, :]
    return jnp.concatenate([stage[j * p:j * p + d, :] for j in range(m)], axis=0)


def _proj_f32_kernel(h_ref, w_ref, b_ref, o_ref, *, n_chunk):
    hb = h_ref[...]
    for c in range(o_ref.shape[1] // n_chunk):
        sl = slice(c * n_chunk, (c + 1) * n_chunk)
        o_ref[:, sl] = jnp.dot(hb, w_ref[:, sl], preferred_element_type=F32) + b_ref[:, sl]


def _proj_f32(h, w, b):
    T, D = h.shape
    N = w.shape[1]
    return pl.pallas_call(
        functools.partial(_proj_f32_kernel, n_chunk=512),
        out_shape=jax.ShapeDtypeStruct((T, N), F32),
        grid=(T // PROJ_TILE,),
        in_specs=[
            pl.BlockSpec((PROJ_TILE, D), lambda i: (i, 0)),
            pl.BlockSpec((D, N), lambda i: (0, 0)),
            pl.BlockSpec((1, N), lambda i: (0, 0)),
        ],
        out_specs=pl.BlockSpec((PROJ_TILE, N), lambda i: (i, 0)),
        compiler_params=_cparams(("parallel",)),
        name="proj_rnn",
    )(h, w, b)


def _proj_qkv_kernel(x_ref, g_ref, w0_ref, w1_ref, w2_ref, b_ref, cos_ref, sin_ref,
                     ca_ref, h_ref, o0_ref, o1_ref, o2_ref, cao_ref, stage):
    cao_ref[...] = ca_ref[...].astype(BF16)
    hb =_rms(x_ref[...], g_ref[...]).astype(BF16)
    h_ref[...] = hb
    cosv = cos_ref[...]
    sinv = sin_ref[...]
    lane = lax.broadcasted_iota(jnp.int32, (1, HEAD_DIM), 1)
    first_half = lane < (ROPE_DIM // 2)
    gw = 3 * ATTN_WIDTH
    for g, (w_ref, o_ref) in enumerate(((w0_ref, o0_ref), (w1_ref, o1_ref), (w2_ref, o2_ref))):
        d = ATTN_GROUPS[g][1]
        for c in range(gw // QKV_CHUNK):
            sl = slice(c * QKV_CHUNK, (c + 1) * QKV_CHUNK)
            t2 = jnp.dot(hb, w_ref[:, sl], preferred_element_type=F32)
            t2 = t2 + b_ref[:, g * gw + c * QKV_CHUNK:g * gw + (c + 1) * QKV_CHUNK]
            for hh in range(QKV_CHUNK // HEAD_DIM):
                t = t2[:, hh * HEAD_DIM:(hh + 1) * HEAD_DIM]
                col = c * QKV_CHUNK + hh * HEAD_DIM
                if col < 2 * ATTN_WIDTH:
                    up = pltpu.roll(t, HEAD_DIM - ROPE_DIM // 2, 1)
                    down = pltpu.roll(t, ROPE_DIM // 2, 1)
                    t = t * cosv + jnp.where(first_half, up, down) * sinv
                cs = slice(col, col + HEAD_DIM)
                if d == 1:
                    o_ref[:, cs] = t.astype(BF16)
                else:
                    for r, part in enumerate(_deinterleave(t, stage, d)):
                        o_ref[0, r, :, cs] = part.astype(BF16)


def _cast_slab_spec(a, nsteps):
    rows, cols = a.shape
    assert rows % (nsteps * BF16_ROWS) == 0, (a.shape, nsteps)
    return pl.BlockSpec((rows // nsteps, cols), lambda i: (i, 0))


def _proj_qkv(x2d, g, ws, b, cos_t, sin_t, batch, cast_a):
    T, D = x2d.shape
    S = T // batch
    nseq = S // PROJ_TILE
    nsteps = T // PROJ_TILE
    gw = 3 * ATTN_WIDTH
    d1, d2 = ATTN_GROUPS[1][1], ATTN_GROUPS[2][1]
    full = lambda a: pl.BlockSpec(a.shape, lambda i: (0,) * a.ndim)
    dil_spec = lambda d: pl.BlockSpec((1, d, PROJ_TILE // d, gw),
                                      lambda i: (i // nseq, 0, i % nseq, 0))
    return pl.pallas_call(
        _proj_qkv_kernel,
        out_shape=(jax.ShapeDtypeStruct((T, D), BF16),
                   jax.ShapeDtypeStruct((T, gw), BF16),
                   jax.ShapeDtypeStruct((batch, d1, S // d1, gw), BF16),
                   jax.ShapeDtypeStruct((batch, d2, S // d2, gw), BF16),
                   jax.ShapeDtypeStruct(cast_a.shape, BF16)),
        grid=(nsteps,),
        in_specs=[
            pl.BlockSpec((PROJ_TILE, D), lambda i: (i, 0)),
            full(g), full(ws[0]), full(ws[1]), full(ws[2]), full(b),
            pl.BlockSpec((PROJ_TILE, HEAD_DIM), lambda i: (i % nseq, 0)),
            pl.BlockSpec((PROJ_TILE, HEAD_DIM), lambda i: (i % nseq, 0)),
            _cast_slab_spec(cast_a, nsteps),
        ],
        out_specs=(pl.BlockSpec((PROJ_TILE, D), lambda i: (i, 0)),
                   pl.BlockSpec((PROJ_TILE, gw), lambda i: (i, 0)), dil_spec(d1), dil_spec(d2),
                   _cast_slab_spec(cast_a, nsteps)),
        scratch_shapes=[pltpu.VMEM((PROJ_TILE // d2 * _pitch(d2), LANES), F32)],
        compiler_params=_cparams(("parallel",)),
        name="proj_qkv",
    )(x2d, g, ws[0], ws[1], ws[2], b, cos_t, sin_t, cast_a)


def _rope_tables(seq):
    pos = jnp.arange(seq, dtype=F32)
    inv = ROPE_THETA ** (-jnp.arange(0, ROPE_DIM, 2, dtype=F32) / ROPE_DIM)
    ang = pos[:, None] * inv[None, :]
    cos, sin = jnp.cos(ang), jnp.sin(ang)
    pad = HEAD_DIM - ROPE_DIM
    cos_t = jnp.concatenate([cos, cos, jnp.ones((seq, pad), F32)], axis=1)
    sin_t = jnp.concatenate([-sin, sin, jnp.zeros((seq, pad), F32)], axis=1)
    return cos_t, sin_t


def _sigmoid(x):
    return 0.5 * jnp.tanh(0.5 * x) + 0.5


def _rglru_kernel(xr_ref, gr_ref, cw_ref, cb_ref, gw_ref, gb_ref, lam_ref, o_ref,
                  xpad, af, uf, ab, ub, cf, cbk, *, seq):
    nseg = seq // SCAN_SEG
    nchunk = seq // SCAN_CHUNK
    seg_per_chunk = SCAN_CHUNK // SCAN_SEG
    pitch = nseg + 4

    cw = cw_ref[...]
    cbias = cb_ref[...]
    gbias = gb_ref[...]
    z = -lam_ref[...]
    sp = jnp.maximum(z, 0.0) + jnp.log1p(jnp.exp(-jnp.abs(z)))
    half_neg_c_sp = (-0.5 * LRU_C) * sp

    zero_rows = jnp.zeros((SUBLANES, LANES), F32)
    xpad[0:SUBLANES, :] = zero_rows
    xpad[seq + SUBLANES:seq + 2 * SUBLANES, :] = zero_rows

    def copy_in(c, carry):
        t0 = pl.multiple_of(c * SCAN_CHUNK, SCAN_CHUNK)
        xpad[pl.ds(t0 + SUBLANES, SCAN_CHUNK), :] = xr_ref[0, pl.ds(t0, SCAN_CHUNK), :]
        return carry

    lax.fori_loop(0, nchunk, copy_in, 0)

    def stage1(c, carry):
        t0 = c * SCAN_CHUNK
        xc = cbias
        for tap in range(CONV_WIDTH):
            lo = t0 + SUBLANES - CONV_WIDTH // 2 + tap
            xc = xc + cw[tap:tap + 1, :] * xpad[pl.ds(lo, SCAN_CHUNK), :]
        th = jnp.tanh(jnp.dot(xc.astype(BF16), gw_ref[...], preferred_element_type=F32) + gbias)
        xc_half = 0.5 * xc
        for d, (a_s, u_s) in enumerate(((af, uf), (ab, ub))):
            t_r = th[:, (2 * d) * LANES:(2 * d + 1) * LANES]
            t_i = th[:, (2 * d + 1) * LANES:(2 * d + 2) * LANES]
            half_k = half_neg_c_sp[d:d + 1, :]
            log_a = t_r * half_k + half_k
            a = jnp.exp(log_a)
            v = jnp.tanh(log_a) * (-1.0 - a * a)
            u = (xc_half * (v * lax.rsqrt(jnp.maximum(v, F32_MIN_NORMAL)))) * (t_i + 1.0)
            for k in range(seg_per_chunk):
                seg = c * seg_per_chunk + k
                a_s[pl.ds(seg, SCAN_SEG, stride=pitch), :] = a[k * SCAN_SEG:(k + 1) * SCAN_SEG, :]
                u_s[pl.ds(seg, SCAN_SEG, stride=pitch), :] = u[k * SCAN_SEG:(k + 1) * SCAN_SEG, :]
        return carry

    lax.fori_loop(0, nchunk, stage1, 0, unroll=4)

    def scan_dir(a_s, u_s, reverse):
        def body(it, carry):
            i = (SCAN_SEG - 1 - it) if reverse else it
            h, cum = carry
            rows = pl.ds(i * pitch, nseg)
            a = a_s[rows, :]
            u = u_s[rows, :]
            h = a * h + u
            cum = a * cum
            u_s[rows, :] = h
            a_s[rows, :] = cum
            return h, cum

        return lax.fori_loop(0, SCAN_SEG, body,
                             (jnp.zeros((nseg, LANES), F32), jnp.ones((nseg, LANES), F32)))

    end_h, end_a = scan_dir(af, uf, False)
    beg_h, beg_a = scan_dir(ab, ub, True)

    c = jnp.zeros((1, LANES), F32)
    for s in range(nseg):
        cf[s:s + 1, :] = c
        c = end_h[s:s + 1, :] + end_a[s:s + 1, :] * c
    c = jnp.zeros((1, LANES), F32)
    for s in range(nseg - 1, -1, -1):
        cbk[s:s + 1, :] = c
        c = beg_h[s:s + 1, :] + beg_a[s:s + 1, :] * c

    def finish(s, carry):
        t0 = pl.multiple_of(s * SCAN_SEG, SCAN_SEG)
        rows = pl.ds(s, SCAN_SEG, stride=pitch)
        hf = uf[rows, :] + af[rows, :] * cf[pl.ds(s, 1), :]
        hb = ub[rows, :] + ab[rows, :] * cbk[pl.ds(s, 1), :]
        g = jax.nn.gelu(gr_ref[0, pl.ds(t0, SCAN_SEG), :])
        o_ref[0, pl.ds(t0, SCAN_SEG), :] = (g * (hf + hb)).astype(BF16)
        return carry

    lax.fori_loop(0, nseg, finish, 0)


def _rglru(xrgr, conv_w, conv_b, gate_w, gate_b, lam):
    B, S, _ = xrgr.shape
    nblk = D_RNN // RNN_BW
    nseg = S // SCAN_SEG
    scr = pltpu.VMEM((SCAN_SEG * (nseg + 4), LANES), F32)
    return pl.pallas_call(
        functools.partial(_rglru_kernel, seq=S),
        out_shape=jax.ShapeDtypeStruct((B, S, D_RNN), BF16),
        grid=(B, nblk),
        in_specs=[
            pl.BlockSpec((1, S, RNN_BW), lambda b, c: (b, 0, c)),
            pl.BlockSpec((1, S, RNN_BW), lambda b, c: (b, 0, nblk + c)),
            pl.BlockSpec((CONV_WIDTH, RNN_BW), lambda b, c: (0, c)),
            pl.BlockSpec((1, RNN_BW), lambda b, c: (0, c)),
            pl.BlockSpec((None, RNN_BW, 4 * RNN_BW), lambda b, c: (c, 0, 0)),
            pl.BlockSpec((None, 1, 4 * RNN_BW), lambda b, c: (c, 0, 0)),
            pl.BlockSpec((None, 2, RNN_BW), lambda b, c: (c, 0, 0)),
        ],
        out_specs=pl.BlockSpec((1, S, RNN_BW), lambda b, c: (b, 0, c)),
        scratch_shapes=[pltpu.VMEM((S + 2 * SUBLANES, LANES), F32), scr, scr, scr, scr,
                        pltpu.VMEM((nseg, LANES), F32), pltpu.VMEM((nseg, LANES), F32)],
        compiler_params=_cparams(("parallel", "parallel")),
        name="rglru",
    )(xrgr, xrgr, conv_w, conv_b, gate_w, gate_b, lam)


def _attn_kernel(q_ref, k_ref, kp_ref, kn_ref, v_ref, vp_ref, vn_ref, o_ref, st_ref,
                 kext, vext, *, qblk, length):
    j = pl.program_id(2)
    nkey = ATT_SUB + 2 * RADIUS
    qi = lax.broadcasted_iota(jnp.int32, (ATT_SUB, nkey), 0)
    ki = lax.broadcasted_iota(jnp.int32, (ATT_SUB, nkey), 1)
    band_bias = jnp.where(jnp.abs(ki - RADIUS - qi) <= RADIUS, 0.0, NEG_INF)
    krow = lax.broadcasted_iota(jnp.int32, (1, nkey), 1)
    lane = lax.broadcasted_iota(jnp.int32, (ATT_SUB, LANES), 1)
    scale2 = HEAD_DIM ** -0.5 * LOG2_E

    for rr in range(q_ref.shape[1]):
        kx, vx = kext.at[rr], vext.at[rr]
        kx[0:RADIUS, :] = kp_ref[0, rr]
        kx[RADIUS:RADIUS + qblk, :] = k_ref[0, rr]
        kx[RADIUS + qblk:RADIUS + qblk + RADIUS, :] = kn_ref[0, rr]
        for h in range(HEADS):
            src = slice(h * HEAD_DIM, (h + 1) * HEAD_DIM)
            dst = slice(2 * h * HEAD_DIM, (2 * h + 1) * HEAD_DIM)
            vx[0:RADIUS, dst] = vp_ref[0, rr, :, src]
            vx[RADIUS:RADIUS + qblk, dst] = v_ref[0, rr, :, src]
            vx[RADIUS + qblk:RADIUS + qblk + RADIUS, dst] = vn_ref[0, rr, :, src]
            vx[:, (2 * h + 1) * HEAD_DIM:(2 * h + 2) * HEAD_DIM] = jnp.ones(
                (qblk + 2 * RADIUS, HEAD_DIM), BF16)

        for i in range(qblk // ATT_SUB):
            q0 = i * ATT_SUB
            kglob = j * qblk + q0 - RADIUS + krow
            edge_bias = jnp.where((kglob >= 0) & (kglob < length), 0.0, NEG_INF)
            bias = band_bias + edge_bias
            st_tile = jnp.ones((ATT_SUB, LANES), F32)
            for h in range(HEADS):
                hs = slice(h * HEAD_DIM, (h + 1) * HEAD_DIM)
                q = q_ref[0, rr, q0:q0 + ATT_SUB, hs]
                ks = kx[q0:q0 + nkey, hs]
                vs = vx[q0:q0 + nkey, 2 * h * HEAD_DIM:(2 * h + 2) * HEAD_DIM]
                s = lax.dot_general(q, ks, (((1,), (1,)), ((), ())),
                                    preferred_element_type=F32) * scale2 + bias
                m = jnp.max(s, axis=-1, keepdims=True)
                p = jnp.exp2(s - m)
                pv = jnp.dot(p.astype(BF16), vs, preferred_element_type=F32)
                l = pv[:, HEAD_DIM:2 * HEAD_DIM]
                o_ref[0, rr, q0:q0 + ATT_SUB, hs] = pv[:, 0:HEAD_DIM]
                st_tile = jnp.where(lane == h, m * LN_2,
                                    jnp.where(lane == HEADS + h, l, st_tile))
            st_ref[0, rr, q0:q0 + ATT_SUB, :] = st_tile


def _attention_group(qkv_g):
    B, d, L, _ = qkv_g.shape
    qblk = min(L, ATT_QBLK)
    nblk = L // qblk
    rpb = min(d, max(1, ATT_QBLK // L))
    halo_per_blk = qblk // RADIUS
    last_halo = L // RADIUS - 1

    main = lambda part: pl.BlockSpec((1, rpb, qblk, ATTN_WIDTH), lambda b, r, j: (b, r, j, part))
    prev = lambda part: pl.BlockSpec(
        (1, rpb, RADIUS, ATTN_WIDTH),
        lambda b, r, j: (b, r, jnp.maximum(j * halo_per_blk - 1, 0), part))
    nxt = lambda part: pl.BlockSpec(
        (1, rpb, RADIUS, ATTN_WIDTH),
        lambda b, r, j: (b, r, jnp.minimum((j + 1) * halo_per_blk, last_halo), part))
    return pl.pallas_call(
        functools.partial(_attn_kernel, qblk=qblk, length=L),
        out_shape=(jax.ShapeDtypeStruct((B, d, L, ATTN_WIDTH), F32),
                   jax.ShapeDtypeStruct((B, d, L, LANES), F32)),
        grid=(B, d // rpb, nblk),
        in_specs=[main(0), main(1), prev(1), nxt(1), main(2), prev(2), nxt(2)],
        out_specs=(pl.BlockSpec((1, rpb, qblk, ATTN_WIDTH), lambda b, r, j: (b, r, j, 0)),
                   pl.BlockSpec((1, rpb, qblk, LANES), lambda b, r, j: (b, r, j, 0))),
        scratch_shapes=[pltpu.VMEM((rpb, qblk + 2 * RADIUS, ATTN_WIDTH), BF16),
                        pltpu.VMEM((rpb, qblk + 2 * RADIUS, 2 * ATTN_WIDTH), BF16)],
        compiler_params=_cparams(("parallel", "parallel", "parallel")),
        name=f"attn_d{d}",
    )(qkv_g, qkv_g, qkv_g, qkv_g, qkv_g, qkv_g, qkv_g)


def _merge_kernel(x_ref, yr_ref, o0_ref, o1_ref, o2_ref, l0_ref, l1_ref, l2_ref,
                  gmix_ref, wg_ref, bg_ref, prnn_ref, pattn_ref, wout_ref,
                  gffn_ref, wrt_ref, brt_ref, cb_ref, cc_ref,
                  x2_ref, h2_ref, afft_ref, cbo_ref, cco_ref, yattn, stage):
    cbo_ref[...] = cb_ref[...].astype(BF16)
    cco_ref[...] = cc_ref[...].astype(BF16)
    d = x_ref.shape[1]
    x = x_ref[...]
    hb = _rms(x, gmix_ref[...]).astype(BF16)
    branch_a = jnp.dot(yr_ref[...], prnn_ref[...], preferred_element_type=F32)
    ga = jnp.dot(hb, wg_ref[:, 0:d], preferred_element_type=F32) + bg_ref[:, 0:d]
    gb = jnp.dot(hb, wg_ref[:, d:2 * d], preferred_element_type=F32) + bg_ref[:, d:2 * d]

    def natural(ref, g, cs):
        dil = ATTN_GROUPS[g][1]
        if dil == 1:
            return ref[:, cs]
        return _interleave([ref[0, r, :, cs] for r in range(dil)], stage, dil)

    o_refs = (o0_ref, o1_ref, o2_ref)
    all_lanes = slice(0, LANES)
    stats = [natural(r, g, all_lanes) for g, r in enumerate((l0_ref, l1_ref, l2_ref))]
    head_lane = lax.broadcasted_iota(jnp.int32, stats[0].shape, 1) < HEADS
    sums = [jnp.where(head_lane, pltpu.roll(st, LANES - HEADS, 1), 1.0) for st in stats]
    lses = [st + jnp.log(l) for st, l in zip(stats, sums)]
    m = jnp.maximum(jnp.maximum(lses[0], lses[1]), lses[2])
    ws = [jnp.exp(l - m) for l in lses]
    den = ws[0] + ws[1] + ws[2]
    coef = [w / (den * l) for w, l in zip(ws, sums)]
    for h in range(HEADS):
        hs = slice(h * HEAD_DIM, (h + 1) * HEAD_DIM)
        acc = coef[0][:, h:h + 1] * natural(o_refs[0], 0, hs)
        acc = acc + coef[1][:, h:h + 1] * natural(o_refs[1], 1, hs)
        acc = acc + coef[2][:, h:h + 1] * natural(o_refs[2], 2, hs)
        yattn[:, hs] = acc.astype(BF16)

    branch_b = jnp.dot(yattn[...], pattn_ref[...], preferred_element_type=F32)
    merged2 = (jnp.tanh(ga) + 1.0) * branch_a + (jnp.tanh(gb) + 1.0) * branch_b
    x2 = x + jnp.dot(merged2.astype(BF16), wout_ref[...], preferred_element_type=F32)
    x2_ref[...] = x2

    h2 = _rms(x2, gffn_ref[...]).astype(BF16)
    h2_ref[...] = h2
    logits_t = lax.dot_general(wrt_ref[...], h2, (((1,), (1,)), ((), ())),
                               preferred_element_type=F32) + brt_ref[...]
    mxt = jnp.max(logits_t, axis=0, keepdims=True)
    et = jnp.exp(logits_t - mxt)
    afft_ref[0] = et / jnp.sum(et, axis=0, keepdims=True)


def _merge(x2d, y_rnn, outs, lses, gmix, wg, bg, p_rnn, p_attn, w_out, gffn, wr, br, batch,
           cast_b, cast_c):
    T, D = x2d.shape
    S = T // batch
    nseq = S // ROW_TILE
    nsteps = T // ROW_TILE
    E = wr.shape[1]
    row = lambda w: pl.BlockSpec((ROW_TILE, w), lambda i: (i, 0))
    full = lambda a: pl.BlockSpec(a.shape, lambda i: (0,) * a.ndim)

    def attn_spec(g, w):
        dil = ATTN_GROUPS[g][1]
        if dil == 1:
            return row(w)
        return pl.BlockSpec((1, dil, ROW_TILE // dil, w), lambda i: (i // nseq, 0, i % nseq, 0))

    wrt = wr.T
    brt = br.reshape(E, 1)
    consts = [gmix, wg, bg, p_rnn, p_attn, w_out, gffn, wrt, brt]
    args = [x2d, y_rnn, *outs, *lses, *consts, cast_b, cast_c]
    in_specs = ([row(D), row(D_RNN)] + [attn_spec(g, ATTN_WIDTH) for g in range(N_GROUPS)]
                + [attn_spec(g, LANES) for g in range(N_GROUPS)] + [full(a) for a in consts]
                + [_cast_slab_spec(cast_b, nsteps), _cast_slab_spec(cast_c, nsteps)])
    dmax = max(dil for _, dil in ATTN_GROUPS)
    return pl.pallas_call(
        _merge_kernel,
        out_shape=(jax.ShapeDtypeStruct((T, D), F32),
                   jax.ShapeDtypeStruct((T, D), BF16),
                   jax.ShapeDtypeStruct((batch, E, S), F32),
                   jax.ShapeDtypeStruct(cast_b.shape, BF16),
                   jax.ShapeDtypeStruct(cast_c.shape, BF16)),
        grid=(nsteps,),
        in_specs=in_specs,
        out_specs=(row(D), row(D),
                   pl.BlockSpec((1, E, ROW_TILE), lambda i: (i // nseq, 0, i % nseq)),
                   _cast_slab_spec(cast_b, nsteps), _cast_slab_spec(cast_c, nsteps)),
        scratch_shapes=[pltpu.VMEM((ROW_TILE, ATTN_WIDTH), BF16),
                        pltpu.VMEM((ROW_TILE // dmax * _pitch(dmax), LANES), F32)],
        compiler_params=_cparams(("parallel",)),
        name="merge",
    )(*args)


def _route_kernel(aff_ref, pos_ref, off_ref, *, cap):
    a = aff_ref[0]
    E, R, _ = a.shape
    bits = pltpu.bitcast(a, jnp.int32)

    def count(mask):
        c = jnp.sum(mask.astype(F32), axis=1, keepdims=True)
        return jnp.sum(c, axis=2, keepdims=True)

    def search(_, carry):
        lo, hi = carry
        q = (hi - lo + 3) >> 2
        m1 = jnp.minimum(lo + q, hi)
        m2 = jnp.minimum(m1 + q, hi)
        m3 = jnp.minimum(m2 + q, hi)
        ok1 = count(bits >= m1) >= cap
        ok2 = count(bits >= m2) >= cap
        ok3 = count(bits >= m3) >= cap
        new_lo = jnp.where(ok3, m3, jnp.where(ok2, m2, jnp.where(ok1, m1, lo)))
        new_hi = jnp.where(ok3, hi, jnp.where(ok2, m3 - 1, jnp.where(ok1, m2 - 1, m1 - 1)))
        return new_lo, new_hi

    lo0 = jnp.zeros((E, 1, 1), jnp.int32)
    hi0 = jnp.full((E, 1, 1), 0x7F7FFFFF, jnp.int32)
    thr, _ = lax.fori_loop(0, SEARCH_STEPS, search, (lo0, hi0))

    gt = bits > thr
    eq = bits == thr
    need = cap - count(gt)

    ki = lax.broadcasted_iota(jnp.int32, (LANES, LANES), 0)
    li = lax.broadcasted_iota(jnp.int32, (LANES, LANES), 1)
    upper = (ki <= li).astype(BF16)
    allones = jnp.ones((LANES, LANES), BF16)
    ri = lax.broadcasted_iota(jnp.int32, (R, R), 0)
    ci = lax.broadcasted_iota(jnp.int32, (R, R), 1)
    strict = (ci < ri).astype(BF16)

    def prefix(mask):
        mb = mask.astype(BF16).reshape(E * R, LANES)
        within = jnp.dot(mb, upper, preferred_element_type=F32)
        totals = jnp.dot(mb, allones, preferred_element_type=F32).astype(BF16)
        offs = []
        for e in range(E):
            offs.append(jnp.dot(strict, totals[e * R:(e + 1) * R, :],
                                preferred_element_type=F32))
        rowoff = jnp.concatenate(offs, axis=0)
        excl = within - mb.astype(F32) + rowoff
        return excl.reshape(E, R, LANES), totals

    eq_rank, _ = prefix(eq)
    sel = gt | (eq & (eq_rank < need.astype(F32)))
    slot, totals = prefix(sel)
    pos_ref[0] = jnp.where(sel, slot.astype(jnp.int32), -1)

    nblk = off_ref.shape[2]
    bi = lax.broadcasted_iota(jnp.int32, (nblk, R), 0)
    rj = lax.broadcasted_iota(jnp.int32, (nblk, R), 1)
    before_blk = (rj < bi * (R // nblk)).astype(BF16)
    blk = [jnp.dot(before_blk, totals[e * R:(e + 1) * R, :], preferred_element_type=F32)
           for e in range(E)]
    off_ref[0] = jnp.concatenate(blk, axis=0).reshape(E, nblk, LANES).astype(jnp.int32)


def _route(aff_t, cap):
    B, E, S = aff_t.shape
    R = S // LANES
    ntb = S // TOK_BLK
    a4 = aff_t.reshape(B, E, R, LANES)
    blk = pl.BlockSpec((1, E, R, LANES), lambda b: (b, 0, 0, 0))
    pos, off = pl.pallas_call(
        functools.partial(_route_kernel, cap=cap),
        out_shape=(jax.ShapeDtypeStruct((B, E, R, LANES), jnp.int32),
                   jax.ShapeDtypeStruct((B, E, ntb, LANES), jnp.int32)),
        grid=(B,),
        in_specs=[blk],
        out_specs=(blk, pl.BlockSpec((1, E, ntb, LANES), lambda b: (b, 0, 0, 0))),
        compiler_params=_cparams(("parallel",)),
        name="route",
    )(a4)
    return pos, off


def _slot_windows(off_ref, base, n_exp, ntb, tb):
    starts = []
    rounds = jnp.int32(0)
    for e in range(n_exp):
        idx = (base + e) * (ntb + 1) + tb
        start = (off_ref[idx] >> 4) << 4
        starts.append(start)
        rounds = jnp.maximum(rounds, (off_ref[idx + 1] - start + SLOT_WIN - 1) >> SLOT_SHIFT)
    return starts, rounds


def _gather_kernel(off_ref, pos_ref, aff_ref, h_ref, o_ref, g_ref, *, cap, ntb, n_exp, nsub):
    b = pl.program_id(0)
    grp = pl.program_id(1)
    t = pl.program_id(2)

    @pl.when(t == 0)
    def _():
        o_ref[...] = jnp.zeros_like(o_ref)
        g_ref[...] = jnp.zeros_like(g_ref)

    base = (b * pl.num_programs(1) + grp) * n_exp
    ji = lax.broadcasted_iota(jnp.int32, (SLOT_WIN, TOK_BLK), 0)
    for sb in range(nsub):
        toks = slice(sb * TOK_BLK, (sb + 1) * TOK_BLK)
        starts, rounds = _slot_windows(off_ref, base, n_exp, ntb, t * nsub + sb)

        def one_round(k, carry, toks=toks, starts=starts):
            pieces, rows = [], []
            for e in range(n_exp):
                first = starts[e] + k * SLOT_WIN
                row0 = pl.multiple_of(jnp.minimum(first, cap - SLOT_WIN), BF16_ROWS)
                slot = ji + row0
                hit = (slot == pos_ref[0, 0, e:e + 1, toks]) & (slot >= first)
                pieces.append(jnp.where(hit, 1.0, 0.0).astype(BF16))
                rows.append(row0)
                gate = jnp.sum(jnp.where(hit, aff_ref[0, 0, e:e + 1, toks], 0.0),
                               axis=1, keepdims=True)
                g_ref[0, e, pl.ds(row0, SLOT_WIN), :] += jnp.broadcast_to(gate, (SLOT_WIN, LANES))
            onehot = jnp.concatenate(pieces, axis=0)
            g = jnp.dot(onehot, h_ref[0, toks, :], preferred_element_type=F32)
            for e in range(n_exp):
                o_ref[0, e, pl.ds(rows[e], SLOT_WIN), :] += (
                    g[e * SLOT_WIN:(e + 1) * SLOT_WIN, :].astype(BF16))
            return carry

        one_round(0, 0)
        lax.fori_loop(1, rounds, one_round, 0)


def _gather(offs, pos, aff_t, h2, cap):
    B, E, S = pos.shape
    D = h2.shape[-1]
    ntb = S // TOK_BLK
    n_exp = E // EXPERT_SPLIT
    nsub = GATHER_TOK // TOK_BLK
    pos4 = pos.reshape(B, EXPERT_SPLIT, n_exp, S)
    aff4 = aff_t.reshape(B, EXPERT_SPLIT, n_exp, S)
    tok_rows = pl.BlockSpec((1, 1, n_exp, GATHER_TOK), lambda b, g, t, off: (b, g, 0, t))
    return pl.pallas_call(
        functools.partial(_gather_kernel, cap=cap, ntb=ntb, n_exp=n_exp, nsub=nsub),
        out_shape=(jax.ShapeDtypeStruct((B, E, cap, D), BF16),
                   jax.ShapeDtypeStruct((B, E, cap, LANES), F32)),
        grid_spec=pltpu.PrefetchScalarGridSpec(
            num_scalar_prefetch=1,
            grid=(B, EXPERT_SPLIT, S // GATHER_TOK),
            in_specs=[
                tok_rows, tok_rows,
                pl.BlockSpec((1, GATHER_TOK, D), lambda b, g, t, off: (b, t, 0)),
            ],
            out_specs=(pl.BlockSpec((1, n_exp, cap, D), lambda b, g, t, off: (b, g, 0, 0)),
                       pl.BlockSpec((1, n_exp, cap, LANES), lambda b, g, t, off: (b, g, 0, 0))),
        ),
        compiler_params=_cparams(("parallel", "parallel", "arbitrary")),
        name="gather",
    )(offs, pos4, aff4, h2)


def _ffn_kernel(x_ref, gate_ref, wg_ref, wu_ref, wd_ref, o_ref, acc, *, f_chunk):
    x = x_ref[0, 0]
    gate = gate_ref[0, 0, :, 0:1]
    nf = wg_ref.shape[2] // f_chunk
    for f in range(nf):
        fs = slice(f * f_chunk, (f + 1) * f_chunk)
        g = jnp.dot(x, wg_ref[0, :, fs], preferred_element_type=F32)
        u = jnp.dot(x, wu_ref[0, :, fs], preferred_element_type=F32)
        hid = (g * _sigmoid(g) * u).astype(BF16)
        part = jnp.dot(hid, wd_ref[0, fs, :], preferred_element_type=F32)
        if f == 0:
            acc[...] = part
        else:
            acc[...] += part
    o_ref[0, 0] = (acc[...] * gate).astype(BF16)


def _ffn(xg, gates, w_gate, w_up, w_down):
    B, E, C, D = xg.shape
    F = w_gate.shape[2]
    assert F % FFN_CHUNK == 0, (F, FFN_CHUNK)
    return pl.pallas_call(
        functools.partial(_ffn_kernel, f_chunk=FFN_CHUNK),
        out_shape=jax.ShapeDtypeStruct((B, E, C, D), BF16),
        grid=(E, B),
        in_specs=[
            pl.BlockSpec((1, 1, C, D), lambda e, b: (b, e, 0, 0)),
            pl.BlockSpec((1, 1, C, LANES), lambda e, b: (b, e, 0, 0)),
            pl.BlockSpec((1, D, F), lambda e, b: (e, 0, 0)),
            pl.BlockSpec((1, D, F), lambda e, b: (e, 0, 0)),
            pl.BlockSpec((1, F, D), lambda e, b: (e, 0, 0)),
        ],
        out_specs=pl.BlockSpec((1, 1, C, D), lambda e, b: (b, e, 0, 0)),
        scratch_shapes=[pltpu.VMEM((C, D), F32)],
        compiler_params=_cparams(("arbitrary", "arbitrary")),
        name="ffn",
    )(xg, gates, w_gate, w_up, w_down)


def _combine_kernel(off_ref, x_ref, pos_ref, eo_ref, g_ref, o_ref, wstack, acc, *, cap, ntb):
    b = pl.program_id(0)
    t = pl.program_id(1)
    E = pos_ref.shape[1]
    nsub = pos_ref.shape[2] // TOK_BLK
    ji = lax.broadcasted_iota(jnp.int32, (SLOT_WIN, TOK_BLK), 0)

    for sb in range(nsub):
        toks = slice(sb * TOK_BLK, (sb + 1) * TOK_BLK)
        starts, rounds = _slot_windows(off_ref, b * E, E, ntb, t * nsub + sb)

        def one_round(k, carry, toks=toks, starts=starts, wst=wstack.at[sb], first_round=False):
            pieces = []
            for e in range(E):
                first = starts[e] + k * SLOT_WIN
                row0 = pl.multiple_of(jnp.minimum(first, cap - SLOT_WIN), BF16_ROWS)
                wst[e * SLOT_WIN:(e + 1) * SLOT_WIN, :] = eo_ref[0, e, pl.ds(row0, SLOT_WIN), :]
                slot = ji + row0
                hit = (slot == pos_ref[0, e:e + 1, toks]) & (slot >= first)
                pieces.append(jnp.where(hit, 1.0, 0.0).astype(BF16))
            onehot_t = jnp.concatenate(pieces, axis=0)
            y = lax.dot_general(onehot_t, wst[...], (((0,), (0,)), ((), ())),
                                preferred_element_type=F32)
            acc[toks, :] = y + (x_ref[0, toks, :] if first_round else acc[toks, :])
            return carry

        one_round(0, 0, first_round=True)
        lax.fori_loop(1, rounds, one_round, 0)
    o_ref[0] = _rms(acc[...], g_ref[...])


def _combine(offs, x2, pos, eo, g_final, cap):
    B, S, D = x2.shape
    E = pos.shape[1]
    ntb = S // TOK_BLK
    return pl.pallas_call(
        functools.partial(_combine_kernel, cap=cap, ntb=ntb),
        out_shape=jax.ShapeDtypeStruct((B, S, D), F32),
        grid_spec=pltpu.PrefetchScalarGridSpec(
            num_scalar_prefetch=1,
            grid=(B, S // COMBINE_TOK),
            in_specs=[
                pl.BlockSpec((1, COMBINE_TOK, D), lambda b, t, off: (b, t, 0)),
                pl.BlockSpec((1, E, COMBINE_TOK), lambda b, t, off: (b, 0, t)),
                pl.BlockSpec((1, E, cap, D), lambda b, t, off: (b, 0, 0, 0),
                             pipeline_mode=pl.Buffered(1)),
                pl.BlockSpec((1, D), lambda b, t, off: (0, 0)),
            ],
            out_specs=pl.BlockSpec((1, COMBINE_TOK, D), lambda b, t, off: (b, t, 0)),
            scratch_shapes=[pltpu.VMEM((COMBINE_TOK // TOK_BLK, E * SLOT_WIN, D), BF16),
                            pltpu.VMEM((COMBINE_TOK, D), F32)],
        ),
        compiler_params=_cparams(("parallel", "arbitrary")),
        name="combine",
    )(offs, x2, pos, eo, g_final)


def _layer(x, norm_mix, w_in, b_in, conv_w, conv_b, rg_w, rg_b, rg_lambda, p_rnn, p_attn,
           w_out, norm_ffn, w_router, b_router, w_gate, w_up, w_down, norm_out, cos_t, sin_t):
    B, S, D = x.shape
    T = B * S
    x2d = x.reshape(T, D)
    n_rnn = 2 * D_RNN
    n_qkv = 3 * N_GROUPS * ATTN_WIDTH
    gmix = norm_mix.reshape(1, D)
    w_in_b = w_in.astype(BF16)
    b_in2 = b_in.reshape(1, -1)

    def group_cols(a, g):
        part = lambda p: a[:, n_rnn + (p * N_GROUPS + g) * ATTN_WIDTH:
                           n_rnn + (p * N_GROUPS + g + 1) * ATTN_WIDTH]
        return jnp.concatenate([part(0), part(1), part(2)], axis=1)

    ws = [group_cols(w_in_b, g) for g in range(N_GROUPS)]
    b_qkv = jnp.concatenate([group_cols(b_in2, g) for g in range(N_GROUPS)], axis=1)
    n_e, d_in, d_ff = w_gate.shape
    h, *qkv, w_gate_b = _proj_qkv(x2d, gmix, ws, b_qkv, cos_t, sin_t, B,
                                  w_gate.reshape(n_e * d_in, d_ff))
    xrgr = _proj_f32(h, w_in_b[:, :n_rnn], b_in2[:, :n_rnn])

    nblk = D_RNN // RNN_BW
    gate_w = (0.5 * jnp.transpose(rg_w, (2, 3, 0, 1, 4))).reshape(nblk, RNN_BW, 4 * RNN_BW).astype(BF16)
    gate_b = 0.5 * jnp.transpose(rg_b.reshape(2, 2, nblk, RNN_BW), (2, 0, 1, 3)).reshape(nblk, 1, 4 * RNN_BW)
    lam = jnp.transpose(rg_lambda.reshape(2, nblk, RNN_BW), (1, 0, 2))
    y_rnn = _rglru(xrgr.reshape(B, S, n_rnn), conv_w, conv_b.reshape(1, D_RNN), gate_w, gate_b, lam)

    outs, lses = [], []
    for g, (_, dilation) in enumerate(ATTN_GROUPS):
        qkv_g = qkv[g].reshape(B, 1, S, 3 * ATTN_WIDTH) if dilation == 1 else qkv[g]
        o, lse = _attention_group(qkv_g)
        if dilation == 1:
            o, lse = o.reshape(T, ATTN_WIDTH), lse.reshape(T, LANES)
        outs.append(o)
        lses.append(lse)

    x2, h2, aff_t, w_up_b, w_down_b = _merge(
        x2d, y_rnn.reshape(T, D_RNN), outs, lses, gmix,
        0.5 * w_in_b[:, n_rnn + n_qkv:], 0.5 * b_in2[:, n_rnn + n_qkv:],
        p_rnn.astype(BF16), p_attn.astype(BF16), (0.5 * w_out).astype(BF16),
        norm_ffn.reshape(1, D), w_router.astype(BF16), b_router.reshape(1, -1), B,
        w_up.reshape(n_e * d_in, d_ff), w_down.reshape(n_e * d_ff, d_in))

    E = w_router.shape[1]
    cap = CAPACITY_FACTOR * S // E
    pos4, off4 = _route(aff_t, cap)
    pos = pos4.reshape(B, E, S)
    blk_off = off4[:, :, :, 0]
    offs = jnp.concatenate([blk_off, jnp.full((B, E, 1), cap, jnp.int32)], axis=2).reshape(-1)
    xg, gates = _gather(offs, pos, aff_t, h2.reshape(B, S, D), cap)
    eo = _ffn(xg, gates, w_gate_b.reshape(n_e, d_in, d_ff), w_up_b.reshape(n_e, d_in, d_ff),
              w_down_b.reshape(n_e, d_ff, d_in))
    return _combine(offs, x2.reshape(B, S, D), pos, eo, norm_out.reshape(1, D), cap)


def kernel(x, norm_mix, w_in, b_in, conv_w, conv_b, rg_w, rg_b, rg_lambda, p_rnn, p_attn, w_out,
           norm_ffn, w_router, b_router, w_gate, w_up, w_down, norm_final):
    depth = w_in.shape[0]
    assert depth == 1, "final rmsnorm is fused into the single layer's combine step"
    cos_t, sin_t = _rope_tables(x.shape[1])
    return _layer(x, norm_mix[0], w_in[0], b_in[0], conv_w[0], conv_b[0], rg_w[0], rg_b[0],
                  rg_lambda[0], p_rnn[0], p_attn[0], w_out[0], norm_ffn[0], w_router[0],
                  b_router[0], w_gate[0], w_up[0], w_down[0], norm_final, cos_t, sin_t)
```

```python
import functools

import jax
import jax.numpy as jnp
from jax import lax
from jax.experimental import pallas as pl
from jax.experimental.pallas import tpu as pltpu

F32 = jnp.float32
BF16 = jnp.bfloat16

D_RNN = 1024
RNN_BW = 128
CONV_WIDTH = 4
LRU_C = 8.0
HEAD_DIM = 128
HEADS = 4
ATTN_WIDTH = HEADS * HEAD_DIM
ATTN_GROUPS = ((128, 1), (512, 4), (2048, 16))
N_GROUPS = 3
ROPE_THETA = 500000.0
ROPE_DIM = HEAD_DIM // 4
RADIUS = 64
NEG_INF = -1e30
CAPACITY_FACTOR = 2
RMS_EPS = 1e-6

LANES = 128
SUBLANES = 8
BF16_ROWS = 16
VMEM_LIMIT = 56 * 1024 * 1024
F32_MIN_NORMAL = 1.1754944e-38
GELU_C0 = 0.7978845608028654
GELU_C1 = 0.044715 * GELU_C0
LOG2_E = 1.4426950408889634
LN_2 = 0.6931471805599453

PROJ_TILE = 1024
RNN_PROJ_TILE = 2048
QKV_CHUNK = 512
ROW_TILE = 512
SCAN_SEG = 128
SCAN_CHUNK = 512
ATT_QBLK = 2048
ATT_SUB = 128
TOK_BLK = 256
GATHER_TOK = 2048
COMBINE_TOK = 512
SLOT_SHIFT = 6
SLOT_WIN = 1 << SLOT_SHIFT
EXPERT_SPLIT = 2
FFN_CHUNK = 512
SEARCH_STEPS = 17


def _cparams(sem):
    return pltpu.CompilerParams(dimension_semantics=sem, vmem_limit_bytes=VMEM_LIMIT)


def _rms(x, g):
    ms = jnp.mean(x * x, axis=-1, keepdims=True)
    return x * lax.rsqrt(ms + RMS_EPS) * g


def _pitch(d):
    return d if d <= 4 else d + 4


def _deinterleave(t, stage, d):
    n = t.shape[0]
    p = _pitch(d)
    if p == d:
        stage[0:n, :] = t
    else:
        for j in range(n // d):
            stage[j * p:j * p + d, :] = t[j * d:(j + 1) * d, :]
    return [stage[pl.ds(r, n // d, stride=p), :] for r in range(d)]


def _interleave(parts, stage, d):
    m = parts[0].shape[0]
    p = _pitch(d)
    for r in range(d):
        stage[pl.ds(r, m, stride=p), :] = parts[r]
    if p == d:
        return stage[0:m * d, :]
    return jnp.concatenate([stage[j * p:j * p + d, :] for j in range(m)], axis=0)


def _proj_f32_kernel(h_ref, w_ref, b_ref, o_ref, *, n_chunk):
    hb = h_ref[...]
    for c in range(o_ref.shape[1] // n_chunk):
        sl = slice(c * n_chunk, (c + 1) * n_chunk)
        o_ref[:, sl] = jnp.dot(hb, w_ref[:, sl], preferred_element_type=F32) + b_ref[:, sl]


def _proj_f32(h, w, b):
    T, D = h.shape
    N = w.shape[1]
    return pl.pallas_call(
        functools.partial(_proj_f32_kernel, n_chunk=512),
        out_shape=jax.ShapeDtypeStruct((T, N), F32),
        grid=(T // RNN_PROJ_TILE,),
        in_specs=[
            pl.BlockSpec((RNN_PROJ_TILE, D), lambda i: (i, 0)),
            pl.BlockSpec((D, N), lambda i: (0, 0)),
            pl.BlockSpec((1, N), lambda i: (0, 0)),
        ],
        out_specs=pl.BlockSpec((RNN_PROJ_TILE, N), lambda i: (i, 0)),
        compiler_params=_cparams(("parallel",)),
        name="proj_rnn",
    )(h, w, b)


def _proj_qkv_kernel(x_ref, g_ref, w0_ref, w1_ref, w2_ref, b_ref, cos_ref, sin_ref,
                     ca_ref, h_ref, o0_ref, o1_ref, o2_ref, cao_ref, stage):
    cao_ref[...] = ca_ref[...].astype(BF16)
    hb =_rms(x_ref[...], g_ref[...]).astype(BF16)
    h_ref[...] = hb
    cosv = cos_ref[...]
    sinv = sin_ref[...]
    lane = lax.broadcasted_iota(jnp.int32, (1, HEAD_DIM), 1)
    first_half = lane < (ROPE_DIM // 2)
    gw = 3 * ATTN_WIDTH
    for g, (w_ref, o_ref) in enumerate(((w0_ref, o0_ref), (w1_ref, o1_ref), (w2_ref, o2_ref))):
        d = ATTN_GROUPS[g][1]
        for c in range(gw // QKV_CHUNK):
            sl = slice(c * QKV_CHUNK, (c + 1) * QKV_CHUNK)
            t2 = jnp.dot(hb, w_ref[:, sl], preferred_element_type=F32)
            t2 = t2 + b_ref[:, g * gw + c * QKV_CHUNK:g * gw + (c + 1) * QKV_CHUNK]
            for hh in range(QKV_CHUNK // HEAD_DIM):
                t = t2[:, hh * HEAD_DIM:(hh + 1) * HEAD_DIM]
                col = c * QKV_CHUNK + hh * HEAD_DIM
                if col < 2 * ATTN_WIDTH:
                    up = pltpu.roll(t, HEAD_DIM - ROPE_DIM // 2, 1)
                    down = pltpu.roll(t, ROPE_DIM // 2, 1)
                    t = t * cosv + jnp.where(first_half, up, down) * sinv
                cs = slice(col, col + HEAD_DIM)
                if d == 1:
                    o_ref[:, cs] = t.astype(BF16)
                else:
                    for r, part in enumerate(_deinterleave(t, stage, d)):
                        o_ref[0, r, :, cs] = part.astype(BF16)


def _cast_slab_spec(a, nsteps):
    rows, cols = a.shape
    assert rows % (nsteps * BF16_ROWS) == 0, (a.shape, nsteps)
    return pl.BlockSpec((rows // nsteps, cols), lambda i: (i, 0))


def _proj_qkv(x2d, g, ws, b, cos_t, sin_t, batch, cast_a):
    T, D = x2d.shape
    S = T // batch
    nseq = S // PROJ_TILE
    nsteps = T // PROJ_TILE
    gw = 3 * ATTN_WIDTH
    d1, d2 = ATTN_GROUPS[1][1], ATTN_GROUPS[2][1]
    full = lambda a: pl.BlockSpec(a.shape, lambda i: (0,) * a.ndim)
    dil_spec = lambda d: pl.BlockSpec((1, d, PROJ_TILE // d, gw),
                                      lambda i: (i // nseq, 0, i % nseq, 0))
    return pl.pallas_call(
        _proj_qkv_kernel,
        out_shape=(jax.ShapeDtypeStruct((T, D), BF16),
                   jax.ShapeDtypeStruct((T, gw), BF16),
                   jax.ShapeDtypeStruct((batch, d1, S // d1, gw), BF16),
                   jax.ShapeDtypeStruct((batch, d2, S // d2, gw), BF16),
                   jax.ShapeDtypeStruct(cast_a.shape, BF16)),
        grid=(nsteps,),
        in_specs=[
            pl.BlockSpec((PROJ_TILE, D), lambda i: (i, 0)),
            full(g), full(ws[0]), full(ws[1]), full(ws[2]), full(b),
            pl.BlockSpec((PROJ_TILE, HEAD_DIM), lambda i: (i % nseq, 0)),
            pl.BlockSpec((PROJ_TILE, HEAD_DIM), lambda i: (i % nseq, 0)),
            _cast_slab_spec(cast_a, nsteps),
        ],
        out_specs=(pl.BlockSpec((PROJ_TILE, D), lambda i: (i, 0)),
                   pl.BlockSpec((PROJ_TILE, gw), lambda i: (i, 0)), dil_spec(d1), dil_spec(d2),
                   _cast_slab_spec(cast_a, nsteps)),
        scratch_shapes=[pltpu.VMEM((PROJ_TILE // d2 * _pitch(d2), LANES), F32)],
        compiler_params=_cparams(("parallel",)),
        name="proj_qkv",
    )(x2d, g, ws[0], ws[1], ws[2], b, cos_t, sin_t, cast_a)


def _rope_tables(seq):
    pos = jnp.arange(seq, dtype=F32)
    inv = ROPE_THETA ** (-jnp.arange(0, ROPE_DIM, 2, dtype=F32) / ROPE_DIM)
    ang = pos[:, None] * inv[None, :]
    cos, sin = jnp.cos(ang), jnp.sin(ang)
    pad = HEAD_DIM - ROPE_DIM
    cos_t = jnp.concatenate([cos, cos, jnp.ones((seq, pad), F32)], axis=1)
    sin_t = jnp.concatenate([-sin, sin, jnp.zeros((seq, pad), F32)], axis=1)
    return cos_t, sin_t


def _sigmoid(x):
    return 0.5 * jnp.tanh(0.5 * x) + 0.5


def _gelu_tanh(x):
    half = 0.5 * x
    inner = x * (GELU_C0 + GELU_C1 * (x * x))
    return half * jnp.tanh(inner) + half


def _rglru_kernel(xr_ref, gr_ref, cw_ref, cb_ref, gw_ref, gb_ref, lam_ref, o_ref,
                  xpad, af, uf, ab, ub, cf, cbk, *, seq):
    nseg = seq // SCAN_SEG
    nchunk = seq // SCAN_CHUNK
    seg_per_chunk = SCAN_CHUNK // SCAN_SEG
    pitch = nseg + 4

    cw = cw_ref[...]
    cbias = cb_ref[...]
    gbias = gb_ref[...]
    z = -lam_ref[...]
    sp = jnp.maximum(z, 0.0) + jnp.log1p(jnp.exp(-jnp.abs(z)))
    half_neg_c_sp = (-0.5 * LRU_C) * sp

    zero_rows = jnp.zeros((SUBLANES, LANES), F32)
    xpad[0:SUBLANES, :] = zero_rows
    xpad[seq + SUBLANES:seq + 2 * SUBLANES, :] = zero_rows

    def copy_in(c, carry):
        t0 = pl.multiple_of(c * SCAN_CHUNK, SCAN_CHUNK)
        xpad[pl.ds(t0 + SUBLANES, SCAN_CHUNK), :] = xr_ref[0, pl.ds(t0, SCAN_CHUNK), :]
        return carry

    lax.fori_loop(0, nchunk, copy_in, 0)

    def stage1(c, carry):
        t0 = c * SCAN_CHUNK
        xc = cbias
        for tap in range(CONV_WIDTH):
            lo = t0 + SUBLANES - CONV_WIDTH // 2 + tap
            xc = xc + cw[tap:tap + 1, :] * xpad[pl.ds(lo, SCAN_CHUNK), :]
        th = jnp.tanh(jnp.dot(xc.astype(BF16), gw_ref[...], preferred_element_type=F32) + gbias)
        xc_half = 0.5 * xc
        for d, (a_s, u_s) in enumerate(((af, uf), (ab, ub))):
            t_r = th[:, (2 * d) * LANES:(2 * d + 1) * LANES]
            t_i = th[:, (2 * d + 1) * LANES:(2 * d + 2) * LANES]
            half_k = half_neg_c_sp[d:d + 1, :]
            log_a = t_r * half_k + half_k
            a = jnp.exp(log_a)
            v = jnp.tanh(log_a) * (-1.0 - a * a)
            u = (xc_half * (v * lax.rsqrt(jnp.maximum(v, F32_MIN_NORMAL)))) * (t_i + 1.0)
            for k in range(seg_per_chunk):
                seg = c * seg_per_chunk + k
                a_s[pl.ds(seg, SCAN_SEG, stride=pitch), :] = a[k * SCAN_SEG:(k + 1) * SCAN_SEG, :]
                u_s[pl.ds(seg, SCAN_SEG, stride=pitch), :] = u[k * SCAN_SEG:(k + 1) * SCAN_SEG, :]
        return carry

    lax.fori_loop(0, nchunk, stage1, 0, unroll=4)

    def scan_dir(a_s, u_s, reverse):
        def body(it, carry):
            i = (SCAN_SEG - 1 - it) if reverse else it
            h, cum = carry
            rows = pl.ds(i * pitch, nseg)
            a = a_s[rows, :]
            u = u_s[rows, :]
            h = a * h + u
            cum = a * cum
            u_s[rows, :] = h
            a_s[rows, :] = cum
            return h, cum

        return lax.fori_loop(0, SCAN_SEG, body,
                             (jnp.zeros((nseg, LANES), F32), jnp.ones((nseg, LANES), F32)))

    end_h, end_a = scan_dir(af, uf, False)
    beg_h, beg_a = scan_dir(ab, ub, True)

    c = jnp.zeros((1, LANES), F32)
    for s in range(nseg):
        cf[s:s + 1, :] = c
        c = end_h[s:s + 1, :] + end_a[s:s + 1, :] * c
    c = jnp.zeros((1, LANES), F32)
    for s in range(nseg - 1, -1, -1):
        cbk[s:s + 1, :] = c
        c = beg_h[s:s + 1, :] + beg_a[s:s + 1, :] * c

    def finish(s, carry):
        t0 = pl.multiple_of(s * SCAN_SEG, SCAN_SEG)
        rows = pl.ds(s, SCAN_SEG, stride=pitch)
        hf = uf[rows, :] + af[rows, :] * cf[pl.ds(s, 1), :]
        hb = ub[rows, :] + ab[rows, :] * cbk[pl.ds(s, 1), :]
        o_ref[0, pl.ds(t0, SCAN_SEG), :] = (
            _gelu_tanh(gr_ref[0, pl.ds(t0, SCAN_SEG), :]) * (hf + hb)).astype(BF16)
        return carry

    lax.fori_loop(0, nseg, finish, 0)


def _rglru(xrgr, conv_w, conv_b, gate_w, gate_b, lam):
    B, S, _ = xrgr.shape
    nblk = D_RNN // RNN_BW
    nseg = S // SCAN_SEG
    scr = pltpu.VMEM((SCAN_SEG * (nseg + 4), LANES), F32)
    return pl.pallas_call(
        functools.partial(_rglru_kernel, seq=S),
        out_shape=jax.ShapeDtypeStruct((B, S, D_RNN), BF16),
        grid=(B, nblk),
        in_specs=[
            pl.BlockSpec((1, S, RNN_BW), lambda b, c: (b, 0, c)),
            pl.BlockSpec((1, S, RNN_BW), lambda b, c: (b, 0, nblk + c)),
            pl.BlockSpec((CONV_WIDTH, RNN_BW), lambda b, c: (0, c)),
            pl.BlockSpec((1, RNN_BW), lambda b, c: (0, c)),
            pl.BlockSpec((None, RNN_BW, 4 * RNN_BW), lambda b, c: (c, 0, 0)),
            pl.BlockSpec((None, 1, 4 * RNN_BW), lambda b, c: (c, 0, 0)),
            pl.BlockSpec((None, 2, RNN_BW), lambda b, c: (c, 0, 0)),
        ],
        out_specs=pl.BlockSpec((1, S, RNN_BW), lambda b, c: (b, 0, c)),
        scratch_shapes=[pltpu.VMEM((S + 2 * SUBLANES, LANES), F32), scr, scr, scr, scr,
                        pltpu.VMEM((nseg, LANES), F32), pltpu.VMEM((nseg, LANES), F32)],
        compiler_params=_cparams(("parallel", "parallel")),
        name="rglru",
    )(xrgr, xrgr, conv_w, conv_b, gate_w, gate_b, lam)


def _attn_kernel(q_ref, k_ref, kp_ref, kn_ref, v_ref, vp_ref, vn_ref, o_ref, st_ref,
                 kext, vext, *, qblk, length):
    j = pl.program_id(2)
    nkey = ATT_SUB + 2 * RADIUS
    qi = lax.broadcasted_iota(jnp.int32, (ATT_SUB, nkey), 0)
    ki = lax.broadcasted_iota(jnp.int32, (ATT_SUB, nkey), 1)
    band_bias = jnp.where(jnp.abs(ki - RADIUS - qi) <= RADIUS, 0.0, NEG_INF)
    krow = lax.broadcasted_iota(jnp.int32, (1, nkey), 1)
    lane = lax.broadcasted_iota(jnp.int32, (ATT_SUB, LANES), 1)
    scale2 = HEAD_DIM ** -0.5 * LOG2_E

    for rr in range(q_ref.shape[1]):
        kx, vx = kext.at[rr], vext.at[rr]
        kx[0:RADIUS, :] = kp_ref[0, rr]
        kx[RADIUS:RADIUS + qblk, :] = k_ref[0, rr]
        kx[RADIUS + qblk:RADIUS + qblk + RADIUS, :] = kn_ref[0, rr]
        for h in range(HEADS):
            src = slice(h * HEAD_DIM, (h + 1) * HEAD_DIM)
            dst = slice(2 * h * HEAD_DIM, (2 * h + 1) * HEAD_DIM)
            vx[0:RADIUS, dst] = vp_ref[0, rr, :, src]
            vx[RADIUS:RADIUS + qblk, dst] = v_ref[0, rr, :, src]
            vx[RADIUS + qblk:RADIUS + qblk + RADIUS, dst] = vn_ref[0, rr, :, src]
            vx[:, (2 * h + 1) * HEAD_DIM:(2 * h + 2) * HEAD_DIM] = jnp.ones(
                (qblk + 2 * RADIUS, HEAD_DIM), BF16)

        for i in range(qblk // ATT_SUB):
            q0 = i * ATT_SUB
            kglob = j * qblk + q0 - RADIUS + krow
            edge_bias = jnp.where((kglob >= 0) & (kglob < length), 0.0, NEG_INF)
            bias = band_bias + edge_bias
            st_tile = jnp.ones((ATT_SUB, LANES), F32)
            for h in range(HEADS):
                hs = slice(h * HEAD_DIM, (h + 1) * HEAD_DIM)
                q = q_ref[0, rr, q0:q0 + ATT_SUB, hs]
                ks = kx[q0:q0 + nkey, hs]
                vs = vx[q0:q0 + nkey, 2 * h * HEAD_DIM:(2 * h + 2) * HEAD_DIM]
                s = lax.dot_general(q, ks, (((1,), (1,)), ((), ())),
                                    preferred_element_type=F32) * scale2 + bias
                m = jnp.max(s, axis=-1, keepdims=True)
                p = jnp.exp2(s - m)
                pv = jnp.dot(p.astype(BF16), vs, preferred_element_type=F32)
                l = pv[:, HEAD_DIM:2 * HEAD_DIM]
                o_ref[0, rr, q0:q0 + ATT_SUB, hs] = pv[:, 0:HEAD_DIM]
                st_tile = jnp.where(lane == h, m * LN_2,
                                    jnp.where(lane == HEADS + h, l, st_tile))
            st_ref[0, rr, q0:q0 + ATT_SUB, :] = st_tile


def _attention_group(qkv_g):
    B, d, L, _ = qkv_g.shape
    qblk = min(L, ATT_QBLK)
    nblk = L // qblk
    rpb = min(d, max(1, ATT_QBLK // L))
    halo_per_blk = qblk // RADIUS
    last_halo = L // RADIUS - 1

    main = lambda part: pl.BlockSpec((1, rpb, qblk, ATTN_WIDTH), lambda b, r, j: (b, r, j, part))
    prev = lambda part: pl.BlockSpec(
        (1, rpb, RADIUS, ATTN_WIDTH),
        lambda b, r, j: (b, r, jnp.maximum(j * halo_per_blk - 1, 0), part))
    nxt = lambda part: pl.BlockSpec(
        (1, rpb, RADIUS, ATTN_WIDTH),
        lambda b, r, j: (b, r, jnp.minimum((j + 1) * halo_per_blk, last_halo), part))
    return pl.pallas_call(
        functools.partial(_attn_kernel, qblk=qblk, length=L),
        out_shape=(jax.ShapeDtypeStruct((B, d, L, ATTN_WIDTH), F32),
                   jax.ShapeDtypeStruct((B, d, L, LANES), F32)),
        grid=(B, d // rpb, nblk),
        in_specs=[main(0), main(1), prev(1), nxt(1), main(2), prev(2), nxt(2)],
        out_specs=(pl.BlockSpec((1, rpb, qblk, ATTN_WIDTH), lambda b, r, j: (b, r, j, 0)),
                   pl.BlockSpec((1, rpb, qblk, LANES), lambda b, r, j: (b, r, j, 0))),
        scratch_shapes=[pltpu.VMEM((rpb, qblk + 2 * RADIUS, ATTN_WIDTH), BF16),
                        pltpu.VMEM((rpb, qblk + 2 * RADIUS, 2 * ATTN_WIDTH), BF16)],
        compiler_params=_cparams(("parallel", "parallel", "parallel")),
        name=f"attn_d{d}",
    )(qkv_g, qkv_g, qkv_g, qkv_g, qkv_g, qkv_g, qkv_g)


def _merge_kernel(x_ref, yr_ref, o0_ref, o1_ref, o2_ref, l0_ref, l1_ref, l2_ref,
                  gmix_ref, wg_ref, bg_ref, prnn_ref, pattn_ref, wout_ref,
                  gffn_ref, wrt_ref, brt_ref, cb_ref, cc_ref,
                  x2_ref, h2_ref, afft_ref, cbo_ref, cco_ref, yattn, stage):
    cbo_ref[...] = cb_ref[...].astype(BF16)
    cco_ref[...] = cc_ref[...].astype(BF16)
    d = x_ref.shape[1]
    x = x_ref[...]
    hb = _rms(x, gmix_ref[...]).astype(BF16)
    branch_a = jnp.dot(yr_ref[...], prnn_ref[...], preferred_element_type=F32)
    ga = jnp.dot(hb, wg_ref[:, 0:d], preferred_element_type=F32) + bg_ref[:, 0:d]
    gb = jnp.dot(hb, wg_ref[:, d:2 * d], preferred_element_type=F32) + bg_ref[:, d:2 * d]

    def natural(ref, g, cs):
        dil = ATTN_GROUPS[g][1]
        if dil == 1:
            return ref[:, cs]
        return _interleave([ref[0, r, :, cs] for r in range(dil)], stage, dil)

    o_refs = (o0_ref, o1_ref, o2_ref)
    all_lanes = slice(0, LANES)
    stats = [natural(r, g, all_lanes) for g, r in enumerate((l0_ref, l1_ref, l2_ref))]
    head_lane = lax.broadcasted_iota(jnp.int32, stats[0].shape, 1) < HEADS
    sums = [jnp.where(head_lane, pltpu.roll(st, LANES - HEADS, 1), 1.0) for st in stats]
    lses = [st + jnp.log(l) for st, l in zip(stats, sums)]
    m = jnp.maximum(jnp.maximum(lses[0], lses[1]), lses[2])
    ws = [jnp.exp(l - m) for l in lses]
    den = ws[0] + ws[1] + ws[2]
    coef = [w / (den * l) for w, l in zip(ws, sums)]
    for h in range(HEADS):
        hs = slice(h * HEAD_DIM, (h + 1) * HEAD_DIM)
        acc = coef[0][:, h:h + 1] * natural(o_refs[0], 0, hs)
        acc = acc + coef[1][:, h:h + 1] * natural(o_refs[1], 1, hs)
        acc = acc + coef[2][:, h:h + 1] * natural(o_refs[2], 2, hs)
        yattn[:, hs] = acc.astype(BF16)

    branch_b = jnp.dot(yattn[...], pattn_ref[...], preferred_element_type=F32)
    merged2 = (jnp.tanh(ga) + 1.0) * branch_a + (jnp.tanh(gb) + 1.0) * branch_b
    x2 = x + jnp.dot(merged2.astype(BF16), wout_ref[...], preferred_element_type=F32)
    x2_ref[...] = x2

    h2 = _rms(x2, gffn_ref[...]).astype(BF16)
    h2_ref[...] = h2
    logits_t = lax.dot_general(wrt_ref[...], h2, (((1,), (1,)), ((), ())),
                               preferred_element_type=F32) + brt_ref[...]
    mxt = jnp.max(logits_t, axis=0, keepdims=True)
    et = jnp.exp(logits_t - mxt)
    afft_ref[0] = et / jnp.sum(et, axis=0, keepdims=True)


def _merge(x2d, y_rnn, outs, lses, gmix, wg, bg, p_rnn, p_attn, w_out, gffn, wr, br, batch,
           cast_b, cast_c):
    T, D = x2d.shape
    S = T // batch
    nseq = S // ROW_TILE
    nsteps = T // ROW_TILE
    E = wr.shape[1]
    row = lambda w: pl.BlockSpec((ROW_TILE, w), lambda i: (i, 0))
    full = lambda a: pl.BlockSpec(a.shape, lambda i: (0,) * a.ndim)

    def attn_spec(g, w):
        dil = ATTN_GROUPS[g][1]
        if dil == 1:
            return row(w)
        return pl.BlockSpec((1, dil, ROW_TILE // dil, w), lambda i: (i // nseq, 0, i % nseq, 0))

    wrt = wr.T
    brt = br.reshape(E, 1)
    consts = [gmix, wg, bg, p_rnn, p_attn, w_out, gffn, wrt, brt]
    args = [x2d, y_rnn, *outs, *lses, *consts, cast_b, cast_c]
    in_specs = ([row(D), row(D_RNN)] + [attn_spec(g, ATTN_WIDTH) for g in range(N_GROUPS)]
                + [attn_spec(g, LANES) for g in range(N_GROUPS)] + [full(a) for a in consts]
                + [_cast_slab_spec(cast_b, nsteps), _cast_slab_spec(cast_c, nsteps)])
    dmax = max(dil for _, dil in ATTN_GROUPS)
    return pl.pallas_call(
        _merge_kernel,
        out_shape=(jax.ShapeDtypeStruct((T, D), F32),
                   jax.ShapeDtypeStruct((T, D), BF16),
                   jax.ShapeDtypeStruct((batch, E, S), F32),
                   jax.ShapeDtypeStruct(cast_b.shape, BF16),
                   jax.ShapeDtypeStruct(cast_c.shape, BF16)),
        grid=(nsteps,),
        in_specs=in_specs,
        out_specs=(row(D), row(D),
                   pl.BlockSpec((1, E, ROW_TILE), lambda i: (i // nseq, 0, i % nseq)),
                   _cast_slab_spec(cast_b, nsteps), _cast_slab_spec(cast_c, nsteps)),
        scratch_shapes=[pltpu.VMEM((ROW_TILE, ATTN_WIDTH), BF16),
                        pltpu.VMEM((ROW_TILE // dmax * _pitch(dmax), LANES), F32)],
        compiler_params=_cparams(("parallel",)),
        name="merge",
    )(*args)


def _route_kernel(aff_ref, pos_ref, off_ref, *, cap):
    a = aff_ref[0]
    E, R, _ = a.shape
    bits = pltpu.bitcast(a, jnp.int32)

    def count(mask):
        c = jnp.sum(mask.astype(F32), axis=1, keepdims=True)
        return jnp.sum(c, axis=2, keepdims=True)

    def search(_, carry):
        lo, hi = carry
        q = (hi - lo + 3) >> 2
        m1 = jnp.minimum(lo + q, hi)
        m2 = jnp.minimum(m1 + q, hi)
        m3 = jnp.minimum(m2 + q, hi)
        ok1 = count(bits >= m1) >= cap
        ok2 = count(bits >= m2) >= cap
        ok3 = count(bits >= m3) >= cap
        new_lo = jnp.where(ok3, m3, jnp.where(ok2, m2, jnp.where(ok1, m1, lo)))
        new_hi = jnp.where(ok3, hi, jnp.where(ok2, m3 - 1, jnp.where(ok1, m2 - 1, m1 - 1)))
        return new_lo, new_hi

    lo0 = jnp.zeros((E, 1, 1), jnp.int32)
    hi0 = jnp.full((E, 1, 1), 0x7F7FFFFF, jnp.int32)
    thr, _ = lax.fori_loop(0, SEARCH_STEPS, search, (lo0, hi0))

    gt = bits > thr
    eq = bits == thr
    need = cap - count(gt)

    ki = lax.broadcasted_iota(jnp.int32, (LANES, LANES), 0)
    li = lax.broadcasted_iota(jnp.int32, (LANES, LANES), 1)
    upper = (ki <= li).astype(BF16)
    allones = jnp.ones((LANES, LANES), BF16)
    ri = lax.broadcasted_iota(jnp.int32, (R, R), 0)
    ci = lax.broadcasted_iota(jnp.int32, (R, R), 1)
    strict = (ci < ri).astype(BF16)

    def prefix(mask):
        mb = mask.astype(BF16).reshape(E * R, LANES)
        within = jnp.dot(mb, upper, preferred_element_type=F32)
        totals = jnp.dot(mb, allones, preferred_element_type=F32).astype(BF16)
        offs = []
        for e in range(E):
            offs.append(jnp.dot(strict, totals[e * R:(e + 1) * R, :],
                                preferred_element_type=F32))
        rowoff = jnp.concatenate(offs, axis=0)
        excl = within - mb.astype(F32) + rowoff
        return excl.reshape(E, R, LANES), totals

    eq_rank, _ = prefix(eq)
    sel = gt | (eq & (eq_rank < need.astype(F32)))
    slot, totals = prefix(sel)
    pos_ref[0] = jnp.where(sel, slot.astype(jnp.int32), -1)

    nblk = off_ref.shape[2]
    bi = lax.broadcasted_iota(jnp.int32, (nblk, R), 0)
    rj = lax.broadcasted_iota(jnp.int32, (nblk, R), 1)
    before_blk = (rj < bi * (R // nblk)).astype(BF16)
    blk = [jnp.dot(before_blk, totals[e * R:(e + 1) * R, :], preferred_element_type=F32)
           for e in range(E)]
    off_ref[0] = jnp.concatenate(blk, axis=0).reshape(E, nblk, LANES).astype(jnp.int32)


def _route(aff_t, cap):
    B, E, S = aff_t.shape
    R = S // LANES
    ntb = S // TOK_BLK
    a4 = aff_t.reshape(B, E, R, LANES)
    blk = pl.BlockSpec((1, E, R, LANES), lambda b: (b, 0, 0, 0))
    pos, off = pl.pallas_call(
        functools.partial(_route_kernel, cap=cap),
        out_shape=(jax.ShapeDtypeStruct((B, E, R, LANES), jnp.int32),
                   jax.ShapeDtypeStruct((B, E, ntb, LANES), jnp.int32)),
        grid=(B,),
        in_specs=[blk],
        out_specs=(blk, pl.BlockSpec((1, E, ntb, LANES), lambda b: (b, 0, 0, 0))),
        compiler_params=_cparams(("parallel",)),
        name="route",
    )(a4)
    return pos, off


def _slot_windows(off_ref, base, n_exp, ntb, tb):
    starts = []
    rounds = jnp.int32(0)
    for e in range(n_exp):
        idx = (base + e) * (ntb + 1) + tb
        start = (off_ref[idx] >> 4) << 4
        starts.append(start)
        rounds = jnp.maximum(rounds, (off_ref[idx + 1] - start + SLOT_WIN - 1) >> SLOT_SHIFT)
    return starts, rounds


def _gather_kernel(off_ref, pos_ref, aff_ref, h_ref, o_ref, g_ref, *, cap, ntb, n_exp, nsub):
    b = pl.program_id(0)
    grp = pl.program_id(1)
    t = pl.program_id(2)

    @pl.when(t == 0)
    def _():
        o_ref[...] = jnp.zeros_like(o_ref)
        g_ref[...] = jnp.zeros_like(g_ref)

    base = (b * pl.num_programs(1) + grp) * n_exp
    ji = lax.broadcasted_iota(jnp.int32, (SLOT_WIN, TOK_BLK), 0)
    for sb in range(nsub):
        toks = slice(sb * TOK_BLK, (sb + 1) * TOK_BLK)
        starts, rounds = _slot_windows(off_ref, base, n_exp, ntb, t * nsub + sb)

        def one_round(k, carry, toks=toks, starts=starts):
            pieces, rows = [], []
            for e in range(n_exp):
                first = starts[e] + k * SLOT_WIN
                row0 = pl.multiple_of(jnp.minimum(first, cap - SLOT_WIN), BF16_ROWS)
                slot = ji + row0
                hit = (slot == pos_ref[0, 0, e:e + 1, toks]) & (slot >= first)
                pieces.append(jnp.where(hit, 1.0, 0.0).astype(BF16))
                rows.append(row0)
                gate = jnp.sum(jnp.where(hit, aff_ref[0, 0, e:e + 1, toks], 0.0),
                               axis=1, keepdims=True)
                g_ref[0, e, pl.ds(row0, SLOT_WIN), :] += jnp.broadcast_to(gate, (SLOT_WIN, LANES))
            onehot = jnp.concatenate(pieces, axis=0)
            g = jnp.dot(onehot, h_ref[0, toks, :], preferred_element_type=F32)
            for e in range(n_exp):
                o_ref[0, e, pl.ds(rows[e], SLOT_WIN), :] += (
                    g[e * SLOT_WIN:(e + 1) * SLOT_WIN, :].astype(BF16))
            return carry

        one_round(0, 0)
        lax.fori_loop(1, rounds, one_round, 0)


def _gather(offs, pos, aff_t, h2, cap):
    B, E, S = pos.shape
    D = h2.shape[-1]
    ntb = S // TOK_BLK
    n_exp = E // EXPERT_SPLIT
    nsub = GATHER_TOK // TOK_BLK
    pos4 = pos.reshape(B, EXPERT_SPLIT, n_exp, S)
    aff4 = aff_t.reshape(B, EXPERT_SPLIT, n_exp, S)
    tok_rows = pl.BlockSpec((1, 1, n_exp, GATHER_TOK), lambda b, g, t, off: (b, g, 0, t))
    return pl.pallas_call(
        functools.partial(_gather_kernel, cap=cap, ntb=ntb, n_exp=n_exp, nsub=nsub),
        out_shape=(jax.ShapeDtypeStruct((B, E, cap, D), BF16),
                   jax.ShapeDtypeStruct((B, E, cap, LANES), F32)),
        grid_spec=pltpu.PrefetchScalarGridSpec(
            num_scalar_prefetch=1,
            grid=(B, EXPERT_SPLIT, S // GATHER_TOK),
            in_specs=[
                tok_rows, tok_rows,
                pl.BlockSpec((1, GATHER_TOK, D), lambda b, g, t, off: (b, t, 0)),
            ],
            out_specs=(pl.BlockSpec((1, n_exp, cap, D), lambda b, g, t, off: (b, g, 0, 0)),
                       pl.BlockSpec((1, n_exp, cap, LANES), lambda b, g, t, off: (b, g, 0, 0))),
        ),
        compiler_params=_cparams(("parallel", "parallel", "arbitrary")),
        name="gather",
    )(offs, pos4, aff4, h2)


def _ffn_kernel(x_ref, gate_ref, wg_ref, wu_ref, wd_ref, o_ref, acc, *, f_chunk):
    x = x_ref[0, 0]
    gate = gate_ref[0, 0, :, 0:1]
    nf = wg_ref.shape[2] // f_chunk
    for f in range(nf):
        fs = slice(f * f_chunk, (f + 1) * f_chunk)
        g = jnp.dot(x, wg_ref[0, :, fs], preferred_element_type=F32)
        u = jnp.dot(x, wu_ref[0, :, fs], preferred_element_type=F32)
        hid = (g * _sigmoid(g) * u).astype(BF16)
        part = jnp.dot(hid, wd_ref[0, fs, :], preferred_element_type=F32)
        if f == 0:
            acc[...] = part
        else:
            acc[...] += part
    o_ref[0, 0] = (acc[...] * gate).astype(BF16)


def _ffn(xg, gates, w_gate, w_up, w_down):
    B, E, C, D = xg.shape
    F = w_gate.shape[2]
    assert F % FFN_CHUNK == 0, (F, FFN_CHUNK)
    return pl.pallas_call(
        functools.partial(_ffn_kernel, f_chunk=FFN_CHUNK),
        out_shape=jax.ShapeDtypeStruct((B, E, C, D), BF16),
        grid=(E, B),
        in_specs=[
            pl.BlockSpec((1, 1, C, D), lambda e, b: (b, e, 0, 0)),
            pl.BlockSpec((1, 1, C, LANES), lambda e, b: (b, e, 0, 0)),
            pl.BlockSpec((1, D, F), lambda e, b: (e, 0, 0)),
            pl.BlockSpec((1, D, F), lambda e, b: (e, 0, 0)),
            pl.BlockSpec((1, F, D), lambda e, b: (e, 0, 0)),
        ],
        out_specs=pl.BlockSpec((1, 1, C, D), lambda e, b: (b, e, 0, 0)),
        scratch_shapes=[pltpu.VMEM((C, D), F32)],
        compiler_params=_cparams(("arbitrary", "arbitrary")),
        name="ffn",
    )(xg, gates, w_gate, w_up, w_down)


def _combine_kernel(off_ref, x_ref, pos_ref, eo_ref, g_ref, o_ref, wstack, acc, *, cap, ntb):
    b = pl.program_id(0)
    t = pl.program_id(1)
    E = pos_ref.shape[1]
    nsub = pos_ref.shape[2] // TOK_BLK
    ji = lax.broadcasted_iota(jnp.int32, (SLOT_WIN, TOK_BLK), 0)

    for sb in range(nsub):
        toks = slice(sb * TOK_BLK, (sb + 1) * TOK_BLK)
        starts, rounds = _slot_windows(off_ref, b * E, E, ntb, t * nsub + sb)

        def one_round(k, carry, toks=toks, starts=starts, wst=wstack.at[sb], first_round=False):
            pieces = []
            for e in range(E):
                first = starts[e] + k * SLOT_WIN
                row0 = pl.multiple_of(jnp.minimum(first, cap - SLOT_WIN), BF16_ROWS)
                wst[e * SLOT_WIN:(e + 1) * SLOT_WIN, :] = eo_ref[0, e, pl.ds(row0, SLOT_WIN), :]
                slot = ji + row0
                hit = (slot == pos_ref[0, e:e + 1, toks]) & (slot >= first)
                pieces.append(jnp.where(hit, 1.0, 0.0).astype(BF16))
            onehot_t = jnp.concatenate(pieces, axis=0)
            y = lax.dot_general(onehot_t, wst[...], (((0,), (0,)), ((), ())),
                                preferred_element_type=F32)
            acc[toks, :] = y + (x_ref[0, toks, :] if first_round else acc[toks, :])
            return carry

        one_round(0, 0, first_round=True)
        lax.fori_loop(1, rounds, one_round, 0)
    o_ref[0] = _rms(acc[...], g_ref[...])


def _combine(offs, x2, pos, eo, g_final, cap):
    B, S, D = x2.shape
    E = pos.shape[1]
    ntb = S // TOK_BLK
    return pl.pallas_call(
        functools.partial(_combine_kernel, cap=cap, ntb=ntb),
        out_shape=jax.ShapeDtypeStruct((B, S, D), F32),
        grid_spec=pltpu.PrefetchScalarGridSpec(
            num_scalar_prefetch=1,
            grid=(B, S // COMBINE_TOK),
            in_specs=[
                pl.BlockSpec((1, COMBINE_TOK, D), lambda b, t, off: (b, t, 0)),
                pl.BlockSpec((1, E, COMBINE_TOK), lambda b, t, off: (b, 0, t)),
                pl.BlockSpec((1, E, cap, D), lambda b, t, off: (b, 0, 0, 0),
                             pipeline_mode=pl.Buffered(1)),
                pl.BlockSpec((1, D), lambda b, t, off: (0, 0)),
            ],
            out_specs=pl.BlockSpec((1, COMBINE_TOK, D), lambda b, t, off: (b, t, 0)),
            scratch_shapes=[pltpu.VMEM((COMBINE_TOK // TOK_BLK, E * SLOT_WIN, D), BF16),
                            pltpu.VMEM((COMBINE_TOK, D), F32)],
        ),
        compiler_params=_cparams(("parallel", "arbitrary")),
        name="combine",
    )(offs, x2, pos, eo, g_final)


def _layer(x, norm_mix, w_in, b_in, conv_w, conv_b, rg_w, rg_b, rg_lambda, p_rnn, p_attn,
           w_out, norm_ffn, w_router, b_router, w_gate, w_up, w_down, norm_out, cos_t, sin_t):
    B, S, D = x.shape
    T = B * S
    x2d = x.reshape(T, D)
    n_rnn = 2 * D_RNN
    n_qkv = 3 * N_GROUPS * ATTN_WIDTH
    gmix = norm_mix.reshape(1, D)
    w_in_b = w_in.astype(BF16)
    b_in2 = b_in.reshape(1, -1)

    def group_cols(a, g):
        part = lambda p: a[:, n_rnn + (p * N_GROUPS + g) * ATTN_WIDTH:
                           n_rnn + (p * N_GROUPS + g + 1) * ATTN_WIDTH]
        return jnp.concatenate([part(0), part(1), part(2)], axis=1)

    ws = [group_cols(w_in_b, g) for g in range(N_GROUPS)]
    b_qkv = jnp.concatenate([group_cols(b_in2, g) for g in range(N_GROUPS)], axis=1)
    n_e, d_in, d_ff = w_gate.shape
    h, *qkv, w_gate_b = _proj_qkv(x2d, gmix, ws, b_qkv, cos_t, sin_t, B,
                                  w_gate.reshape(n_e * d_in, d_ff))
    xrgr = _proj_f32(h, w_in_b[:, :n_rnn], b_in2[:, :n_rnn])

    nblk = D_RNN // RNN_BW
    gate_w = (0.5 * jnp.transpose(rg_w, (2, 3, 0, 1, 4))).reshape(nblk, RNN_BW, 4 * RNN_BW).astype(BF16)
    gate_b = 0.5 * jnp.transpose(rg_b.reshape(2, 2, nblk, RNN_BW), (2, 0, 1, 3)).reshape(nblk, 1, 4 * RNN_BW)
    lam = jnp.transpose(rg_lambda.reshape(2, nblk, RNN_BW), (1, 0, 2))
    y_rnn = _rglru(xrgr.reshape(B, S, n_rnn), conv_w, conv_b.reshape(1, D_RNN), gate_w, gate_b, lam)

    outs, lses = [], []
    for g, (_, dilation) in enumerate(ATTN_GROUPS):
        qkv_g = qkv[g].reshape(B, 1, S, 3 * ATTN_WIDTH) if dilation == 1 else qkv[g]
        o, lse = _attention_group(qkv_g)
        if dilation == 1:
            o, lse = o.reshape(T, ATTN_WIDTH), lse.reshape(T, LANES)
        outs.append(o)
        lses.append(lse)

    x2, h2, aff_t, w_up_b, w_down_b = _merge(
        x2d, y_rnn.reshape(T, D_RNN), outs, lses, gmix,
        0.5 * w_in_b[:, n_rnn + n_qkv:], 0.5 * b_in2[:, n_rnn + n_qkv:],
        p_rnn.astype(BF16), p_attn.astype(BF16), (0.5 * w_out).astype(BF16),
        norm_ffn.reshape(1, D), w_router.astype(BF16), b_router.reshape(1, -1), B,
        w_up.reshape(n_e * d_in, d_ff), w_down.reshape(n_e * d_ff, d_in))

    E = w_router.shape[1]
    cap = CAPACITY_FACTOR * S // E
    pos4, off4 = _route(aff_t, cap)
    pos = pos4.reshape(B, E, S)
    blk_off = off4[:, :, :, 0]
    offs = jnp.concatenate([blk_off, jnp.full((B, E, 1), cap, jnp.int32)], axis=2).reshape(-1)
    xg, gates = _gather(offs, pos, aff_t, h2.reshape(B, S, D), cap)
    eo = _ffn(xg, gates, w_gate_b.reshape(n_e, d_in, d_ff), w_up_b.reshape(n_e, d_in, d_ff),
              w_down_b.reshape(n_e, d_ff, d_in))
    return _combine(offs, x2.reshape(B, S, D), pos, eo, norm_out.reshape(1, D), cap)


def kernel(x, norm_mix, w_in, b_in, conv_w, conv_b, rg_w, rg_b, rg_lambda, p_rnn, p_attn, w_out,
           norm_ffn, w_router, b_router, w_gate, w_up, w_down, norm_final):
    depth = w_in.shape[0]
    assert depth == 1, "final rmsnorm is fused into the single layer's combine step"
    cos_t, sin_t = _rope_tables(x.shape[1])
    return _layer(x, norm_mix[0], w_in[0], b_in[0], conv_w[0], conv_b[0], rg_w[0], rg_b[0],
                  rg_lambda[0], p_rnn[0], p_attn[0], w_out[0], norm_ffn[0], w_router[0],
                  b_router[0], w_gate[0], w_up[0], w_down[0], norm_final, cos_t, sin_t)
```

```python
import functools

import jax
import jax.numpy as jnp
from jax import lax
from jax.experimental import pallas as pl
from jax.experimental.pallas import tpu as pltpu

F32 = jnp.float32
BF16 = jnp.bfloat16

D_RNN = 1024
RNN_BW = 128
CONV_WIDTH = 4
LRU_C = 8.0
HEAD_DIM = 128
HEADS = 4
ATTN_WIDTH = HEADS * HEAD_DIM
ATTN_GROUPS = ((128, 1), (512, 4), (2048, 16))
N_GROUPS = 3
ROPE_THETA = 500000.0
ROPE_DIM = HEAD_DIM // 4
RADIUS = 64
NEG_INF = -1e30
CAPACITY_FACTOR = 2
RMS_EPS = 1e-6

LANES = 128
SUBLANES = 8
BF16_ROWS = 16
VMEM_LIMIT = 56 * 1024 * 1024
F32_MIN_NORMAL = 1.1754944e-38
GELU_C0 = 0.7978845608028654
GELU_C1 = 0.044715 * GELU_C0
LOG2_E = 1.4426950408889634
LN_2 = 0.6931471805599453

PROJ_TILE = 1024
RNN_PROJ_TILE = 2048
QKV_CHUNK = 512
ROW_TILE = 512
SCAN_SEG = 128
SCAN_CHUNK = 512
ATT_QBLK = 2048
ATT_SUB = 128
TOK_BLK = 256
GATHER_TOK = 2048
COMBINE_TOK = 1024
SLOT_SHIFT = 6
SLOT_WIN = 1 << SLOT_SHIFT
EXPERT_SPLIT = 2
FFN_CHUNK = 512
SEARCH_STEPS = 17


def _cparams(sem):
    return pltpu.CompilerParams(dimension_semantics=sem, vmem_limit_bytes=VMEM_LIMIT)


def _rms(x, g):
    ms = jnp.mean(x * x, axis=-1, keepdims=True)
    return x * lax.rsqrt(ms + RMS_EPS) * g


def _pitch(d):
    return d if d <= 4 else d + 4


def _deinterleave(t, stage, d):
    n = t.shape[0]
    p = _pitch(d)
    if p == d:
        stage[0:n, :] = t
    else:
        for j in range(n // d):
            stage[j * p:j * p + d, :] = t[j * d:(j + 1) * d, :]
    return [stage[pl.ds(r, n // d, stride=p), :] for r in range(d)]


def _interleave(parts, stage, d):
    m = parts[0].shape[0]
    p = _pitch(d)
    for r in range(d):
        stage[pl.ds(r, m, stride=p), :] = parts[r]
    if p == d:
        return stage[0:m * d, :]
    return jnp.concatenate([stage[j * p:j * p + d, :] for j in range(m)], axis=0)


def _proj_f32_kernel(h_ref, w_ref, b_ref, o_ref, *, n_chunk):
    hb = h_ref[...]
    for c in range(o_ref.shape[1] // n_chunk):
        sl = slice(c * n_chunk, (c + 1) * n_chunk)
        o_ref[:, sl] = jnp.dot(hb, w_ref[:, sl], preferred_element_type=F32) + b_ref[:, sl]


def _proj_f32(h, w, b):
    T, D = h.shape
    N = w.shape[1]
    return pl.pallas_call(
        functools.partial(_proj_f32_kernel, n_chunk=512),
        out_shape=jax.ShapeDtypeStruct((T, N), F32),
        grid=(T // RNN_PROJ_TILE,),
        in_specs=[
            pl.BlockSpec((RNN_PROJ_TILE, D), lambda i: (i, 0)),
            pl.BlockSpec((D, N), lambda i: (0, 0)),
            pl.BlockSpec((1, N), lambda i: (0, 0)),
        ],
        out_specs=pl.BlockSpec((RNN_PROJ_TILE, N), lambda i: (i, 0)),
        compiler_params=_cparams(("parallel",)),
        name="proj_rnn",
    )(h, w, b)


def _proj_qkv_kernel(x_ref, g_ref, w0_ref, w1_ref, w2_ref, b_ref, cos_ref, sin_ref,
                     ca_ref, h_ref, o0_ref, o1_ref, o2_ref, cao_ref, stage):
    cao_ref[...] = ca_ref[...].astype(BF16)
    hb =_rms(x_ref[...], g_ref[...]).astype(BF16)
    h_ref[...] = hb
    cosv = cos_ref[...]
    sinv = sin_ref[...]
    lane = lax.broadcasted_iota(jnp.int32, (1, HEAD_DIM), 1)
    first_half = lane < (ROPE_DIM // 2)
    gw = 3 * ATTN_WIDTH
    for g, (w_ref, o_ref) in enumerate(((w0_ref, o0_ref), (w1_ref, o1_ref), (w2_ref, o2_ref))):
        d = ATTN_GROUPS[g][1]
        for c in range(gw // QKV_CHUNK):
            sl = slice(c * QKV_CHUNK, (c + 1) * QKV_CHUNK)
            t2 = jnp.dot(hb, w_ref[:, sl], preferred_element_type=F32)
            t2 = t2 + b_ref[:, g * gw + c * QKV_CHUNK:g * gw + (c + 1) * QKV_CHUNK]
            for hh in range(QKV_CHUNK // HEAD_DIM):
                t = t2[:, hh * HEAD_DIM:(hh + 1) * HEAD_DIM]
                col = c * QKV_CHUNK + hh * HEAD_DIM
                if col < 2 * ATTN_WIDTH:
                    up = pltpu.roll(t, HEAD_DIM - ROPE_DIM // 2, 1)
                    down = pltpu.roll(t, ROPE_DIM // 2, 1)
                    t = t * cosv + jnp.where(first_half, up, down) * sinv
                cs = slice(col, col + HEAD_DIM)
                if d == 1:
                    o_ref[:, cs] = t.astype(BF16)
                else:
                    for r, part in enumerate(_deinterleave(t, stage, d)):
                        o_ref[0, r, :, cs] = part.astype(BF16)


def _cast_slab_spec(a, nsteps):
    rows, cols = a.shape
    assert rows % (nsteps * BF16_ROWS) == 0, (a.shape, nsteps)
    return pl.BlockSpec((rows // nsteps, cols), lambda i: (i, 0))


def _proj_qkv(x2d, g, ws, b, cos_t, sin_t, batch, cast_a):
    T, D = x2d.shape
    S = T // batch
    nseq = S // PROJ_TILE
    nsteps = T // PROJ_TILE
    gw = 3 * ATTN_WIDTH
    d1, d2 = ATTN_GROUPS[1][1], ATTN_GROUPS[2][1]
    full = lambda a: pl.BlockSpec(a.shape, lambda i: (0,) * a.ndim)
    dil_spec = lambda d: pl.BlockSpec((1, d, PROJ_TILE // d, gw),
                                      lambda i: (i // nseq, 0, i % nseq, 0))
    return pl.pallas_call(
        _proj_qkv_kernel,
        out_shape=(jax.ShapeDtypeStruct((T, D), BF16),
                   jax.ShapeDtypeStruct((T, gw), BF16),
                   jax.ShapeDtypeStruct((batch, d1, S // d1, gw), BF16),
                   jax.ShapeDtypeStruct((batch, d2, S // d2, gw), BF16),
                   jax.ShapeDtypeStruct(cast_a.shape, BF16)),
        grid=(nsteps,),
        in_specs=[
            pl.BlockSpec((PROJ_TILE, D), lambda i: (i, 0)),
            full(g), full(ws[0]), full(ws[1]), full(ws[2]), full(b),
            pl.BlockSpec((PROJ_TILE, HEAD_DIM), lambda i: (i % nseq, 0)),
            pl.BlockSpec((PROJ_TILE, HEAD_DIM), lambda i: (i % nseq, 0)),
            _cast_slab_spec(cast_a, nsteps),
        ],
        out_specs=(pl.BlockSpec((PROJ_TILE, D), lambda i: (i, 0)),
                   pl.BlockSpec((PROJ_TILE, gw), lambda i: (i, 0)), dil_spec(d1), dil_spec(d2),
                   _cast_slab_spec(cast_a, nsteps)),
        scratch_shapes=[pltpu.VMEM((PROJ_TILE // d2 * _pitch(d2), LANES), F32)],
        compiler_params=_cparams(("parallel",)),
        name="proj_qkv",
    )(x2d, g, ws[0], ws[1], ws[2], b, cos_t, sin_t, cast_a)


def _rope_tables(seq):
    pos = jnp.arange(seq, dtype=F32)
    inv = ROPE_THETA ** (-jnp.arange(0, ROPE_DIM, 2, dtype=F32) / ROPE_DIM)
    ang = pos[:, None] * inv[None, :]
    cos, sin = jnp.cos(ang), jnp.sin(ang)
    pad = HEAD_DIM - ROPE_DIM
    cos_t = jnp.concatenate([cos, cos, jnp.ones((seq, pad), F32)], axis=1)
    sin_t = jnp.concatenate([-sin, sin, jnp.zeros((seq, pad), F32)], axis=1)
    return cos_t, sin_t


def _sigmoid(x):
    return 0.5 * jnp.tanh(0.5 * x) + 0.5


def _gelu_tanh(x):
    half = 0.5 * x
    inner = x * (GELU_C0 + GELU_C1 * (x * x))
    return half * jnp.tanh(inner) + half


def _rglru_kernel(xr_ref, gr_ref, cw_ref, cb_ref, gw_ref, gb_ref, lam_ref, o_ref,
                  xpad, af, uf, ab, ub, cf, cbk, *, seq):
    nseg = seq // SCAN_SEG
    nchunk = seq // SCAN_CHUNK
    seg_per_chunk = SCAN_CHUNK // SCAN_SEG
    pitch = nseg + 4

    cw = cw_ref[...]
    cbias = cb_ref[...]
    gbias = gb_ref[...]
    z = -lam_ref[...]
    sp = jnp.maximum(z, 0.0) + jnp.log1p(jnp.exp(-jnp.abs(z)))
    half_neg_c_sp = (-0.5 * LRU_C) * sp

    zero_rows = jnp.zeros((SUBLANES, LANES), F32)
    xpad[0:SUBLANES, :] = zero_rows
    xpad[seq + SUBLANES:seq + 2 * SUBLANES, :] = zero_rows

    def copy_in(c, carry):
        t0 = pl.multiple_of(c * SCAN_CHUNK, SCAN_CHUNK)
        xpad[pl.ds(t0 + SUBLANES, SCAN_CHUNK), :] = xr_ref[0, pl.ds(t0, SCAN_CHUNK), :]
        return carry

    lax.fori_loop(0, nchunk, copy_in, 0)

    def stage1(c, carry):
        t0 = c * SCAN_CHUNK
        xc = cbias
        for tap in range(CONV_WIDTH):
            lo = t0 + SUBLANES - CONV_WIDTH // 2 + tap
            xc = xc + cw[tap:tap + 1, :] * xpad[pl.ds(lo, SCAN_CHUNK), :]
        th = jnp.tanh(jnp.dot(xc.astype(BF16), gw_ref[...], preferred_element_type=F32) + gbias)
        xc_half = 0.5 * xc
        for d, (a_s, u_s) in enumerate(((af, uf), (ab, ub))):
            t_r = th[:, (2 * d) * LANES:(2 * d + 1) * LANES]
            t_i = th[:, (2 * d + 1) * LANES:(2 * d + 2) * LANES]
            half_k = half_neg_c_sp[d:d + 1, :]
            log_a = t_r * half_k + half_k
            a = jnp.exp(log_a)
            v = jnp.tanh(log_a) * (-1.0 - a * a)
            u = (xc_half * (v * lax.rsqrt(jnp.maximum(v, F32_MIN_NORMAL)))) * (t_i + 1.0)
            for k in range(seg_per_chunk):
                seg = c * seg_per_chunk + k
                a_s[pl.ds(seg, SCAN_SEG, stride=pitch), :] = a[k * SCAN_SEG:(k + 1) * SCAN_SEG, :]
                u_s[pl.ds(seg, SCAN_SEG, stride=pitch), :] = u[k * SCAN_SEG:(k + 1) * SCAN_SEG, :]
        return carry

    lax.fori_loop(0, nchunk, stage1, 0, unroll=4)

    def scan_dir(a_s, u_s, reverse):
        def body(it, carry):
            i = (SCAN_SEG - 1 - it) if reverse else it
            h, cum = carry
            rows = pl.ds(i * pitch, nseg)
            a = a_s[rows, :]
            u = u_s[rows, :]
            h = a * h + u
            cum = a * cum
            u_s[rows, :] = h
            a_s[rows, :] = cum
            return h, cum

        return lax.fori_loop(0, SCAN_SEG, body,
                             (jnp.zeros((nseg, LANES), F32), jnp.ones((nseg, LANES), F32)))

    end_h, end_a = scan_dir(af, uf, False)
    beg_h, beg_a = scan_dir(ab, ub, True)

    c = jnp.zeros((1, LANES), F32)
    for s in range(nseg):
        cf[s:s + 1, :] = c
        c = end_h[s:s + 1, :] + end_a[s:s + 1, :] * c
    c = jnp.zeros((1, LANES), F32)
    for s in range(nseg - 1, -1, -1):
        cbk[s:s + 1, :] = c
        c = beg_h[s:s + 1, :] + beg_a[s:s + 1, :] * c

    def finish(s, carry):
        t0 = pl.multiple_of(s * SCAN_SEG, SCAN_SEG)
        rows = pl.ds(s, SCAN_SEG, stride=pitch)
        hf = uf[rows, :] + af[rows, :] * cf[pl.ds(s, 1), :]
        hb = ub[rows, :] + ab[rows, :] * cbk[pl.ds(s, 1), :]
        o_ref[0, pl.ds(t0, SCAN_SEG), :] = (
            _gelu_tanh(gr_ref[0, pl.ds(t0, SCAN_SEG), :]) * (hf + hb)).astype(BF16)
        return carry

    lax.fori_loop(0, nseg, finish, 0)


def _rglru(xrgr, conv_w, conv_b, gate_w, gate_b, lam):
    B, S, _ = xrgr.shape
    nblk = D_RNN // RNN_BW
    nseg = S // SCAN_SEG
    scr = pltpu.VMEM((SCAN_SEG * (nseg + 4), LANES), F32)
    return pl.pallas_call(
        functools.partial(_rglru_kernel, seq=S),
        out_shape=jax.ShapeDtypeStruct((B, S, D_RNN), BF16),
        grid=(B, nblk),
        in_specs=[
            pl.BlockSpec((1, S, RNN_BW), lambda b, c: (b, 0, c)),
            pl.BlockSpec((1, S, RNN_BW), lambda b, c: (b, 0, nblk + c)),
            pl.BlockSpec((CONV_WIDTH, RNN_BW), lambda b, c: (0, c)),
            pl.BlockSpec((1, RNN_BW), lambda b, c: (0, c)),
            pl.BlockSpec((None, RNN_BW, 4 * RNN_BW), lambda b, c: (c, 0, 0)),
            pl.BlockSpec((None, 1, 4 * RNN_BW), lambda b, c: (c, 0, 0)),
            pl.BlockSpec((None, 2, RNN_BW), lambda b, c: (c, 0, 0)),
        ],
        out_specs=pl.BlockSpec((1, S, RNN_BW), lambda b, c: (b, 0, c)),
        scratch_shapes=[pltpu.VMEM((S + 2 * SUBLANES, LANES), F32), scr, scr, scr, scr,
                        pltpu.VMEM((nseg, LANES), F32), pltpu.VMEM((nseg, LANES), F32)],
        compiler_params=_cparams(("parallel", "parallel")),
        name="rglru",
    )(xrgr, xrgr, conv_w, conv_b, gate_w, gate_b, lam)


def _attn_kernel(q_ref, k_ref, kp_ref, kn_ref, v_ref, vp_ref, vn_ref, o_ref, st_ref,
                 kext, vext, *, qblk, length):
    j = pl.program_id(2)
    nkey = ATT_SUB + 2 * RADIUS
    qi = lax.broadcasted_iota(jnp.int32, (ATT_SUB, nkey), 0)
    ki = lax.broadcasted_iota(jnp.int32, (ATT_SUB, nkey), 1)
    band_bias = jnp.where(jnp.abs(ki - RADIUS - qi) <= RADIUS, 0.0, NEG_INF)
    krow = lax.broadcasted_iota(jnp.int32, (1, nkey), 1)
    lane = lax.broadcasted_iota(jnp.int32, (ATT_SUB, LANES), 1)
    scale2 = HEAD_DIM ** -0.5 * LOG2_E

    for rr in range(q_ref.shape[1]):
        kx, vx = kext.at[rr], vext.at[rr]
        kx[0:RADIUS, :] = kp_ref[0, rr]
        kx[RADIUS:RADIUS + qblk, :] = k_ref[0, rr]
        kx[RADIUS + qblk:RADIUS + qblk + RADIUS, :] = kn_ref[0, rr]
        for h in range(HEADS):
            src = slice(h * HEAD_DIM, (h + 1) * HEAD_DIM)
            dst = slice(2 * h * HEAD_DIM, (2 * h + 1) * HEAD_DIM)
            vx[0:RADIUS, dst] = vp_ref[0, rr, :, src]
            vx[RADIUS:RADIUS + qblk, dst] = v_ref[0, rr, :, src]
            vx[RADIUS + qblk:RADIUS + qblk + RADIUS, dst] = vn_ref[0, rr, :, src]
            vx[:, (2 * h + 1) * HEAD_DIM:(2 * h + 2) * HEAD_DIM] = jnp.ones(
                (qblk + 2 * RADIUS, HEAD_DIM), BF16)

        for i in range(qblk // ATT_SUB):
            q0 = i * ATT_SUB
            kglob = j * qblk + q0 - RADIUS + krow
            edge_bias = jnp.where((kglob >= 0) & (kglob < length), 0.0, NEG_INF)
            bias = band_bias + edge_bias
            st_tile = jnp.ones((ATT_SUB, LANES), F32)
            for h in range(HEADS):
                hs = slice(h * HEAD_DIM, (h + 1) * HEAD_DIM)
                q = q_ref[0, rr, q0:q0 + ATT_SUB, hs]
                ks = kx[q0:q0 + nkey, hs]
                vs = vx[q0:q0 + nkey, 2 * h * HEAD_DIM:(2 * h + 2) * HEAD_DIM]
                s = lax.dot_general(q, ks, (((1,), (1,)), ((), ())),
                                    preferred_element_type=F32) * scale2 + bias
                m = jnp.max(s, axis=-1, keepdims=True)
                p = jnp.exp2(s - m)
                pv = jnp.dot(p.astype(BF16), vs, preferred_element_type=F32)
                l = pv[:, HEAD_DIM:2 * HEAD_DIM]
                o_ref[0, rr, q0:q0 + ATT_SUB, hs] = pv[:, 0:HEAD_DIM]
                st_tile = jnp.where(lane == h, m * LN_2,
                                    jnp.where(lane == HEADS + h, l, st_tile))
            st_ref[0, rr, q0:q0 + ATT_SUB, :] = st_tile


def _attention_group(qkv_g):
    B, d, L, _ = qkv_g.shape
    qblk = min(L, ATT_QBLK)
    nblk = L // qblk
    rpb = min(d, max(1, ATT_QBLK // L))
    halo_per_blk = qblk // RADIUS
    last_halo = L // RADIUS - 1

    main = lambda part: pl.BlockSpec((1, rpb, qblk, ATTN_WIDTH), lambda b, r, j: (b, r, j, part))
    prev = lambda part: pl.BlockSpec(
        (1, rpb, RADIUS, ATTN_WIDTH),
        lambda b, r, j: (b, r, jnp.maximum(j * halo_per_blk - 1, 0), part))
    nxt = lambda part: pl.BlockSpec(
        (1, rpb, RADIUS, ATTN_WIDTH),
        lambda b, r, j: (b, r, jnp.minimum((j + 1) * halo_per_blk, last_halo), part))
    return pl.pallas_call(
        functools.partial(_attn_kernel, qblk=qblk, length=L),
        out_shape=(jax.ShapeDtypeStruct((B, d, L, ATTN_WIDTH), F32),
                   jax.ShapeDtypeStruct((B, d, L, LANES), F32)),
        grid=(B, d // rpb, nblk),
        in_specs=[main(0), main(1), prev(1), nxt(1), main(2), prev(2), nxt(2)],
        out_specs=(pl.BlockSpec((1, rpb, qblk, ATTN_WIDTH), lambda b, r, j: (b, r, j, 0)),
                   pl.BlockSpec((1, rpb, qblk, LANES), lambda b, r, j: (b, r, j, 0))),
        scratch_shapes=[pltpu.VMEM((rpb, qblk + 2 * RADIUS, ATTN_WIDTH), BF16),
                        pltpu.VMEM((rpb, qblk + 2 * RADIUS, 2 * ATTN_WIDTH), BF16)],
        compiler_params=_cparams(("parallel", "parallel", "parallel")),
        name=f"attn_d{d}",
    )(qkv_g, qkv_g, qkv_g, qkv_g, qkv_g, qkv_g, qkv_g)


def _merge_kernel(x_ref, yr_ref, o0_ref, o1_ref, o2_ref, l0_ref, l1_ref, l2_ref,
                  gmix_ref, wg_ref, bg_ref, prnn_ref, pattn_ref, wout_ref,
                  gffn_ref, wrt_ref, brt_ref, cb_ref, cc_ref,
                  x2_ref, h2_ref, afft_ref, cbo_ref, cco_ref, yattn, stage):
    cbo_ref[...] = cb_ref[...].astype(BF16)
    cco_ref[...] = cc_ref[...].astype(BF16)
    d = x_ref.shape[1]
    x = x_ref[...]
    hb = _rms(x, gmix_ref[...]).astype(BF16)
    branch_a = jnp.dot(yr_ref[...], prnn_ref[...], preferred_element_type=F32)
    ga = jnp.dot(hb, wg_ref[:, 0:d], preferred_element_type=F32) + bg_ref[:, 0:d]
    gb = jnp.dot(hb, wg_ref[:, d:2 * d], preferred_element_type=F32) + bg_ref[:, d:2 * d]

    def natural(ref, g, cs):
        dil = ATTN_GROUPS[g][1]
        if dil == 1:
            return ref[:, cs]
        return _interleave([ref[0, r, :, cs] for r in range(dil)], stage, dil)

    o_refs = (o0_ref, o1_ref, o2_ref)
    all_lanes = slice(0, LANES)
    stats = [natural(r, g, all_lanes) for g, r in enumerate((l0_ref, l1_ref, l2_ref))]
    head_lane = lax.broadcasted_iota(jnp.int32, stats[0].shape, 1) < HEADS
    sums = [jnp.where(head_lane, pltpu.roll(st, LANES - HEADS, 1), 1.0) for st in stats]
    lses = [st + jnp.log(l) for st, l in zip(stats, sums)]
    m = jnp.maximum(jnp.maximum(lses[0], lses[1]), lses[2])
    ws = [jnp.exp(l - m) for l in lses]
    den = ws[0] + ws[1] + ws[2]
    coef = [w / (den * l) for w, l in zip(ws, sums)]
    for h in range(HEADS):
        hs = slice(h * HEAD_DIM, (h + 1) * HEAD_DIM)
        acc = coef[0][:, h:h + 1] * natural(o_refs[0], 0, hs)
        acc = acc + coef[1][:, h:h + 1] * natural(o_refs[1], 1, hs)
        acc = acc + coef[2][:, h:h + 1] * natural(o_refs[2], 2, hs)
        yattn[:, hs] = acc.astype(BF16)

    branch_b = jnp.dot(yattn[...], pattn_ref[...], preferred_element_type=F32)
    merged2 = (jnp.tanh(ga) + 1.0) * branch_a + (jnp.tanh(gb) + 1.0) * branch_b
    x2 = x + jnp.dot(merged2.astype(BF16), wout_ref[...], preferred_element_type=F32)
    x2_ref[...] = x2

    h2 = _rms(x2, gffn_ref[...]).astype(BF16)
    h2_ref[...] = h2
    logits_t = lax.dot_general(wrt_ref[...], h2, (((1,), (1,)), ((), ())),
                               preferred_element_type=F32) + brt_ref[...]
    mxt = jnp.max(logits_t, axis=0, keepdims=True)
    et = jnp.exp(logits_t - mxt)
    afft_ref[0] = et / jnp.sum(et, axis=0, keepdims=True)


def _merge(x2d, y_rnn, outs, lses, gmix, wg, bg, p_rnn, p_attn, w_out, gffn, wr, br, batch,
           cast_b, cast_c):
    T, D = x2d.shape
    S = T // batch
    nseq = S // ROW_TILE
    nsteps = T // ROW_TILE
    E = wr.shape[1]
    row = lambda w: pl.BlockSpec((ROW_TILE, w), lambda i: (i, 0))
    full = lambda a: pl.BlockSpec(a.shape, lambda i: (0,) * a.ndim)

    def attn_spec(g, w):
        dil = ATTN_GROUPS[g][1]
        if dil == 1:
            return row(w)
        return pl.BlockSpec((1, dil, ROW_TILE // dil, w), lambda i: (i // nseq, 0, i % nseq, 0))

    wrt = wr.T
    brt = br.reshape(E, 1)
    consts = [gmix, wg, bg, p_rnn, p_attn, w_out, gffn, wrt, brt]
    args = [x2d, y_rnn, *outs, *lses, *consts, cast_b, cast_c]
    in_specs = ([row(D), row(D_RNN)] + [attn_spec(g, ATTN_WIDTH) for g in range(N_GROUPS)]
                + [attn_spec(g, LANES) for g in range(N_GROUPS)] + [full(a) for a in consts]
                + [_cast_slab_spec(cast_b, nsteps), _cast_slab_spec(cast_c, nsteps)])
    dmax = max(dil for _, dil in ATTN_GROUPS)
    return pl.pallas_call(
        _merge_kernel,
        out_shape=(jax.ShapeDtypeStruct((T, D), F32),
                   jax.ShapeDtypeStruct((T, D), BF16),
                   jax.ShapeDtypeStruct((batch, E, S), F32),
                   jax.ShapeDtypeStruct(cast_b.shape, BF16),
                   jax.ShapeDtypeStruct(cast_c.shape, BF16)),
        grid=(nsteps,),
        in_specs=in_specs,
        out_specs=(row(D), row(D),
                   pl.BlockSpec((1, E, ROW_TILE), lambda i: (i // nseq, 0, i % nseq)),
                   _cast_slab_spec(cast_b, nsteps), _cast_slab_spec(cast_c, nsteps)),
        scratch_shapes=[pltpu.VMEM((ROW_TILE, ATTN_WIDTH), BF16),
                        pltpu.VMEM((ROW_TILE // dmax * _pitch(dmax), LANES), F32)],
        compiler_params=_cparams(("parallel",)),
        name="merge",
    )(*args)


def _route_kernel(aff_ref, pos_ref, off_ref, *, cap):
    a = aff_ref[0]
    E, R, _ = a.shape
    bits = pltpu.bitcast(a, jnp.int32)

    def count(mask):
        c = jnp.sum(mask.astype(F32), axis=1, keepdims=True)
        return jnp.sum(c, axis=2, keepdims=True)

    def search(_, carry):
        lo, hi = carry
        q = (hi - lo + 3) >> 2
        m1 = jnp.minimum(lo + q, hi)
        m2 = jnp.minimum(m1 + q, hi)
        m3 = jnp.minimum(m2 + q, hi)
        ok1 = count(bits >= m1) >= cap
        ok2 = count(bits >= m2) >= cap
        ok3 = count(bits >= m3) >= cap
        new_lo = jnp.where(ok3, m3, jnp.where(ok2, m2, jnp.where(ok1, m1, lo)))
        new_hi = jnp.where(ok3, hi, jnp.where(ok2, m3 - 1, jnp.where(ok1, m2 - 1, m1 - 1)))
        return new_lo, new_hi

    lo0 = jnp.zeros((E, 1, 1), jnp.int32)
    hi0 = jnp.full((E, 1, 1), 0x7F7FFFFF, jnp.int32)
    thr, _ = lax.fori_loop(0, SEARCH_STEPS, search, (lo0, hi0))

    gt = bits > thr
    eq = bits == thr
    need = cap - count(gt)

    ki = lax.broadcasted_iota(jnp.int32, (LANES, LANES), 0)
    li = lax.broadcasted_iota(jnp.int32, (LANES, LANES), 1)
    upper = (ki <= li).astype(BF16)
    allones = jnp.ones((LANES, LANES), BF16)
    ri = lax.broadcasted_iota(jnp.int32, (R, R), 0)
    ci = lax.broadcasted_iota(jnp.int32, (R, R), 1)
    strict = (ci < ri).astype(BF16)

    def prefix(mask):
        mb = mask.astype(BF16).reshape(E * R, LANES)
        within = jnp.dot(mb, upper, preferred_element_type=F32)
        totals = jnp.dot(mb, allones, preferred_element_type=F32).astype(BF16)
        offs = []
        for e in range(E):
            offs.append(jnp.dot(strict, totals[e * R:(e + 1) * R, :],
                                preferred_element_type=F32))
        rowoff = jnp.concatenate(offs, axis=0)
        excl = within - mb.astype(F32) + rowoff
        return excl.reshape(E, R, LANES), totals

    eq_rank, _ = prefix(eq)
    sel = gt | (eq & (eq_rank < need.astype(F32)))
    slot, totals = prefix(sel)
    pos_ref[0] = jnp.where(sel, slot.astype(jnp.int32), -1)

    nblk = off_ref.shape[2]
    bi = lax.broadcasted_iota(jnp.int32, (nblk, R), 0)
    rj = lax.broadcasted_iota(jnp.int32, (nblk, R), 1)
    before_blk = (rj < bi * (R // nblk)).astype(BF16)
    blk = [jnp.dot(before_blk, totals[e * R:(e + 1) * R, :], preferred_element_type=F32)
           for e in range(E)]
    off_ref[0] = jnp.concatenate(blk, axis=0).reshape(E, nblk, LANES).astype(jnp.int32)


def _route(aff_t, cap):
    B, E, S = aff_t.shape
    R = S // LANES
    ntb = S // TOK_BLK
    a4 = aff_t.reshape(B, E, R, LANES)
    blk = pl.BlockSpec((1, E, R, LANES), lambda b: (b, 0, 0, 0))
    pos, off = pl.pallas_call(
        functools.partial(_route_kernel, cap=cap),
        out_shape=(jax.ShapeDtypeStruct((B, E, R, LANES), jnp.int32),
                   jax.ShapeDtypeStruct((B, E, ntb, LANES), jnp.int32)),
        grid=(B,),
        in_specs=[blk],
        out_specs=(blk, pl.BlockSpec((1, E, ntb, LANES), lambda b: (b, 0, 0, 0))),
        compiler_params=_cparams(("parallel",)),
        name="route",
    )(a4)
    return pos, off


def _slot_windows(off_ref, base, n_exp, ntb, tb):
    starts = []
    rounds = jnp.int32(0)
    for e in range(n_exp):
        idx = (base + e) * (ntb + 1) + tb
        start = (off_ref[idx] >> 4) << 4
        starts.append(start)
        rounds = jnp.maximum(rounds, (off_ref[idx + 1] - start + SLOT_WIN - 1) >> SLOT_SHIFT)
    return starts, rounds


def _gather_kernel(off_ref, pos_ref, aff_ref, h_ref, o_ref, g_ref, *, cap, ntb, n_exp, nsub):
    b = pl.program_id(0)
    grp = pl.program_id(1)
    t = pl.program_id(2)

    @pl.when(t == 0)
    def _():
        o_ref[...] = jnp.zeros_like(o_ref)
        g_ref[...] = jnp.zeros_like(g_ref)

    base = (b * pl.num_programs(1) + grp) * n_exp
    ji = lax.broadcasted_iota(jnp.int32, (SLOT_WIN, TOK_BLK), 0)
    for sb in range(nsub):
        toks = slice(sb * TOK_BLK, (sb + 1) * TOK_BLK)
        starts, rounds = _slot_windows(off_ref, base, n_exp, ntb, t * nsub + sb)

        def one_round(k, carry, toks=toks, starts=starts):
            pieces, rows = [], []
            for e in range(n_exp):
                first = starts[e] + k * SLOT_WIN
                row0 = pl.multiple_of(jnp.minimum(first, cap - SLOT_WIN), BF16_ROWS)
                slot = ji + row0
                hit = (slot == pos_ref[0, 0, e:e + 1, toks]) & (slot >= first)
                pieces.append(jnp.where(hit, 1.0, 0.0).astype(BF16))
                rows.append(row0)
                gate = jnp.sum(jnp.where(hit, aff_ref[0, 0, e:e + 1, toks], 0.0),
                               axis=1, keepdims=True)
                g_ref[0, e, pl.ds(row0, SLOT_WIN), :] += jnp.broadcast_to(gate, (SLOT_WIN, LANES))
            onehot = jnp.concatenate(pieces, axis=0)
            g = jnp.dot(onehot, h_ref[0, toks, :], preferred_element_type=F32)
            for e in range(n_exp):
                o_ref[0, e, pl.ds(rows[e], SLOT_WIN), :] += (
                    g[e * SLOT_WIN:(e + 1) * SLOT_WIN, :].astype(BF16))
            return carry

        one_round(0, 0)
        lax.fori_loop(1, rounds, one_round, 0)


def _gather(offs, pos, aff_t, h2, cap):
    B, E, S = pos.shape
    D = h2.shape[-1]
    ntb = S // TOK_BLK
    n_exp = E // EXPERT_SPLIT
    nsub = GATHER_TOK // TOK_BLK
    pos4 = pos.reshape(B, EXPERT_SPLIT, n_exp, S)
    aff4 = aff_t.reshape(B, EXPERT_SPLIT, n_exp, S)
    tok_rows = pl.BlockSpec((1, 1, n_exp, GATHER_TOK), lambda b, g, t, off: (b, g, 0, t))
    return pl.pallas_call(
        functools.partial(_gather_kernel, cap=cap, ntb=ntb, n_exp=n_exp, nsub=nsub),
        out_shape=(jax.ShapeDtypeStruct((B, E, cap, D), BF16),
                   jax.ShapeDtypeStruct((B, E, cap, LANES), F32)),
        grid_spec=pltpu.PrefetchScalarGridSpec(
            num_scalar_prefetch=1,
            grid=(B, EXPERT_SPLIT, S // GATHER_TOK),
            in_specs=[
                tok_rows, tok_rows,
                pl.BlockSpec((1, GATHER_TOK, D), lambda b, g, t, off: (b, t, 0)),
            ],
            out_specs=(pl.BlockSpec((1, n_exp, cap, D), lambda b, g, t, off: (b, g, 0, 0)),
                       pl.BlockSpec((1, n_exp, cap, LANES), lambda b, g, t, off: (b, g, 0, 0))),
        ),
        compiler_params=_cparams(("parallel", "parallel", "arbitrary")),
        name="gather",
    )(offs, pos4, aff4, h2)


def _ffn_kernel(x_ref, gate_ref, wg_ref, wu_ref, wd_ref, o_ref, acc, *, f_chunk):
    x = x_ref[0, 0]
    gate = gate_ref[0, 0, :, 0:1]
    nf = wg_ref.shape[2] // f_chunk
    for f in range(nf):
        fs = slice(f * f_chunk, (f + 1) * f_chunk)
        g = jnp.dot(x, wg_ref[0, :, fs], preferred_element_type=F32)
        u = jnp.dot(x, wu_ref[0, :, fs], preferred_element_type=F32)
        hid = (g * _sigmoid(g) * u).astype(BF16)
        part = jnp.dot(hid, wd_ref[0, fs, :], preferred_element_type=F32)
        if f == 0:
            acc[...] = part
        else:
            acc[...] += part
    o_ref[0, 0] = (acc[...] * gate).astype(BF16)


def _ffn(xg, gates, w_gate, w_up, w_down):
    B, E, C, D = xg.shape
    F = w_gate.shape[2]
    assert F % FFN_CHUNK == 0, (F, FFN_CHUNK)
    return pl.pallas_call(
        functools.partial(_ffn_kernel, f_chunk=FFN_CHUNK),
        out_shape=jax.ShapeDtypeStruct((B, E, C, D), BF16),
        grid=(E, B),
        in_specs=[
            pl.BlockSpec((1, 1, C, D), lambda e, b: (b, e, 0, 0)),
            pl.BlockSpec((1, 1, C, LANES), lambda e, b: (b, e, 0, 0)),
            pl.BlockSpec((1, D, F), lambda e, b: (e, 0, 0)),
            pl.BlockSpec((1, D, F), lambda e, b: (e, 0, 0)),
            pl.BlockSpec((1, F, D), lambda e, b: (e, 0, 0)),
        ],
        out_specs=pl.BlockSpec((1, 1, C, D), lambda e, b: (b, e, 0, 0)),
        scratch_shapes=[pltpu.VMEM((C, D), F32)],
        compiler_params=_cparams(("arbitrary", "arbitrary")),
        name="ffn",
    )(xg, gates, w_gate, w_up, w_down)


def _combine_kernel(off_ref, x_ref, pos_ref, eo_hbm, g_ref, o_ref, wstack, acc, sem, *, cap, ntb):
    b = pl.program_id(0)
    t = pl.program_id(1)
    E = pos_ref.shape[1]
    nsub = pos_ref.shape[2] // TOK_BLK
    ji = lax.broadcasted_iota(jnp.int32, (SLOT_WIN, TOK_BLK), 0)

    def window_copy(sb, e, row0):
        return pltpu.make_async_copy(
            eo_hbm.at[b, e, pl.ds(row0, SLOT_WIN), :],
            wstack.at[sb, pl.ds(e * SLOT_WIN, SLOT_WIN), :], sem.at[sb, e])

    def round_windows(starts, k):
        out = []
        for e in range(E):
            first = starts[e] + k * SLOT_WIN
            out.append((first, pl.multiple_of(jnp.minimum(first, cap - SLOT_WIN), BF16_ROWS)))
        return out

    plans = []
    for sb in range(nsub):
        starts, rounds = _slot_windows(off_ref, b * E, E, ntb, t * nsub + sb)
        plans.append((starts, rounds))
        for e, (_, row0) in enumerate(round_windows(starts, 0)):
            window_copy(sb, e, row0).start()

    for sb in range(nsub):
        toks = slice(sb * TOK_BLK, (sb + 1) * TOK_BLK)
        starts, rounds = plans[sb]

        def one_round(k, carry, sb=sb, toks=toks, starts=starts, first_round=False):
            wins = round_windows(starts, k)
            if not first_round:
                for e, (_, row0) in enumerate(wins):
                    window_copy(sb, e, row0).start()
            pieces = []
            for e, (first, row0) in enumerate(wins):
                slot = ji + row0
                hit = (slot == pos_ref[0, e:e + 1, toks]) & (slot >= first)
                pieces.append(jnp.where(hit, 1.0, 0.0).astype(BF16))
            onehot_t = jnp.concatenate(pieces, axis=0)
            for e, (_, row0) in enumerate(wins):
                window_copy(sb, e, row0).wait()
            y = lax.dot_general(onehot_t, wstack[sb], (((0,), (0,)), ((), ())),
                                preferred_element_type=F32)
            acc[toks, :] = y + (x_ref[0, toks, :] if first_round else acc[toks, :])
            return carry

        one_round(0, 0, first_round=True)
        lax.fori_loop(1, rounds, one_round, 0)
    o_ref[0] = _rms(acc[...], g_ref[...])


def _combine(offs, x2, pos, eo, g_final, cap):
    B, S, D = x2.shape
    E = pos.shape[1]
    ntb = S // TOK_BLK
    return pl.pallas_call(
        functools.partial(_combine_kernel, cap=cap, ntb=ntb),
        out_shape=jax.ShapeDtypeStruct((B, S, D), F32),
        grid_spec=pltpu.PrefetchScalarGridSpec(
            num_scalar_prefetch=1,
            grid=(B, S // COMBINE_TOK),
            in_specs=[
                pl.BlockSpec((1, COMBINE_TOK, D), lambda b, t, off: (b, t, 0)),
                pl.BlockSpec((1, E, COMBINE_TOK), lambda b, t, off: (b, 0, t)),
                pl.BlockSpec(memory_space=pl.ANY),
                pl.BlockSpec((1, D), lambda b, t, off: (0, 0)),
            ],
            out_specs=pl.BlockSpec((1, COMBINE_TOK, D), lambda b, t, off: (b, t, 0)),
            scratch_shapes=[pltpu.VMEM((COMBINE_TOK // TOK_BLK, E * SLOT_WIN, D), BF16),
                            pltpu.VMEM((COMBINE_TOK, D), F32),
                            pltpu.SemaphoreType.DMA((COMBINE_TOK // TOK_BLK, E))],
        ),
        compiler_params=_cparams(("parallel", "arbitrary")),
        name="combine",
    )(offs, x2, pos, eo, g_final)


def _layer(x, norm_mix, w_in, b_in, conv_w, conv_b, rg_w, rg_b, rg_lambda, p_rnn, p_attn,
           w_out, norm_ffn, w_router, b_router, w_gate, w_up, w_down, norm_out, cos_t, sin_t):
    B, S, D = x.shape
    T = B * S
    x2d = x.reshape(T, D)
    n_rnn = 2 * D_RNN
    n_qkv = 3 * N_GROUPS * ATTN_WIDTH
    gmix = norm_mix.reshape(1, D)
    w_in_b = w_in.astype(BF16)
    b_in2 = b_in.reshape(1, -1)

    def group_cols(a, g):
        part = lambda p: a[:, n_rnn + (p * N_GROUPS + g) * ATTN_WIDTH:
                           n_rnn + (p * N_GROUPS + g + 1) * ATTN_WIDTH]
        return jnp.concatenate([part(0), part(1), part(2)], axis=1)

    ws = [group_cols(w_in_b, g) for g in range(N_GROUPS)]
    b_qkv = jnp.concatenate([group_cols(b_in2, g) for g in range(N_GROUPS)], axis=1)
    n_e, d_in, d_ff = w_gate.shape
    h, *qkv, w_gate_b = _proj_qkv(x2d, gmix, ws, b_qkv, cos_t, sin_t, B,
                                  w_gate.reshape(n_e * d_in, d_ff))
    xrgr = _proj_f32(h, w_in_b[:, :n_rnn], b_in2[:, :n_rnn])

    nblk = D_RNN // RNN_BW
    gate_w = (0.5 * jnp.transpose(rg_w, (2, 3, 0, 1, 4))).reshape(nblk, RNN_BW, 4 * RNN_BW).astype(BF16)
    gate_b = 0.5 * jnp.transpose(rg_b.reshape(2, 2, nblk, RNN_BW), (2, 0, 1, 3)).reshape(nblk, 1, 4 * RNN_BW)
    lam = jnp.transpose(rg_lambda.reshape(2, nblk, RNN_BW), (1, 0, 2))
    y_rnn = _rglru(xrgr.reshape(B, S, n_rnn), conv_w, conv_b.reshape(1, D_RNN), gate_w, gate_b, lam)

    outs, lses = [], []
    for g, (_, dilation) in enumerate(ATTN_GROUPS):
        qkv_g = qkv[g].reshape(B, 1, S, 3 * ATTN_WIDTH) if dilation == 1 else qkv[g]
        o, lse = _attention_group(qkv_g)
        if dilation == 1:
            o, lse = o.reshape(T, ATTN_WIDTH), lse.reshape(T, LANES)
        outs.append(o)
        lses.append(lse)

    x2, h2, aff_t, w_up_b, w_down_b = _merge(
        x2d, y_rnn.reshape(T, D_RNN), outs, lses, gmix,
        0.5 * w_in_b[:, n_rnn + n_qkv:], 0.5 * b_in2[:, n_rnn + n_qkv:],
        p_rnn.astype(BF16), p_attn.astype(BF16), (0.5 * w_out).astype(BF16),
        norm_ffn.reshape(1, D), w_router.astype(BF16), b_router.reshape(1, -1), B,
        w_up.reshape(n_e * d_in, d_ff), w_down.reshape(n_e * d_ff, d_in))

    E = w_router.shape[1]
    cap = CAPACITY_FACTOR * S // E
    pos4, off4 = _route(aff_t, cap)
    pos = pos4.reshape(B, E, S)
    blk_off = off4[:, :, :, 0]
    offs = jnp.concatenate([blk_off, jnp.full((B, E, 1), cap, jnp.int32)], axis=2).reshape(-1)
    xg, gates = _gather(offs, pos, aff_t, h2.reshape(B, S, D), cap)
    eo = _ffn(xg, gates, w_gate_b.reshape(n_e, d_in, d_ff), w_up_b.reshape(n_e, d_in, d_ff),
              w_down_b.reshape(n_e, d_ff, d_in))
    return _combine(offs, x2.reshape(B, S, D), pos, eo, norm_out.reshape(1, D), cap)


def kernel(x, norm_mix, w_in, b_in, conv_w, conv_b, rg_w, rg_b, rg_lambda, p_rnn, p_attn, w_out,
           norm_ffn, w_router, b_router, w_gate, w_up, w_down, norm_final):
    depth = w_in.shape[0]
    assert depth == 1, "final rmsnorm is fused into the single layer's combine step"
    cos_t, sin_t = _rope_tables(x.shape[1])
    return _layer(x, norm_mix[0], w_in[0], b_in[0], conv_w[0], conv_b[0], rg_w[0], rg_b[0],
                  rg_lambda[0], p_rnn[0], p_attn[0], w_out[0], norm_ffn[0], w_router[0],
                  b_router[0], w_gate[0], w_up[0], w_down[0], norm_final, cos_t, sin_t)
```
